```python
import math
import jax, jax.numpy as jnp
from jax import lax
import numpy as np

D_MODEL = 1024
BATCH = 4
SEQ = 4096
DEPTH = 2

N_A_LAYERS = DEPTH // 2
N_B_LAYERS = DEPTH - N_A_LAYERS
EPS = 1e-6

EXPAND = 2
CONV_D = EXPAND * D_MODEL
CONV_WIDTH = 3

N_HEADS = 16
HEAD_DIM = D_MODEL // N_HEADS
N_KV_GROUPS = 4
HEADS_PER_GROUP = N_HEADS // N_KV_GROUPS
ATTN_D = N_HEADS * HEAD_DIM
KV_D = N_KV_GROUPS * HEAD_DIM
N_BRANCH = 3
N_KV_STREAMS = 6
ROT_DIM = HEAD_DIM // 4
ROPE_THETA = 500000.0
CMP_BLOCK = 32
CMP_STRIDE = 16
CMP_R = CMP_BLOCK // CMP_STRIDE
CMP_HIDDEN = 4 * HEAD_DIM
SLC_BLOCK = 64
SLC_RATIO = SLC_BLOCK // CMP_STRIDE
N_SELECT = 16
N_LOCAL = 2
WINDOW = 512
Q_BLOCK = 64
FORCE_SCORE = 1e4

kernel_name = "yoco_shortconv_nsa_hybrid"


def rmsnorm(x, g):
    xf = x.astype(jnp.float32)
    y = xf * lax.rsqrt(jnp.mean(xf * xf, axis=-1, keepdims=True) + EPS)
    return (y * g.astype(jnp.float32)).astype(x.dtype)


def partial_rope(x):
    S = x.shape[1]
    pos = jnp.arange(S, dtype=jnp.float32)
    inv = ROPE_THETA ** (-jnp.arange(0, ROT_DIM, 2, dtype=jnp.float32) / ROT_DIM)
    ang = pos[:, None] * inv[None, :]
    cos = jnp.cos(ang)[None, :, None, :]
    sin = jnp.sin(ang)[None, :, None, :]
    xr = x[..., :ROT_DIM].astype(jnp.float32)
    x1, x2 = xr[..., :ROT_DIM // 2], xr[..., ROT_DIM // 2:]
    rot = jnp.concatenate([x1 * cos - x2 * sin, x2 * cos + x1 * sin], axis=-1).astype(x.dtype)
    return jnp.concatenate([rot, x[..., ROT_DIM:]], axis=-1)


def masked_softmax(s, mask, axis=-1):
    s = jnp.where(mask, s.astype(jnp.float32), -jnp.inf)
    m = jnp.max(s, axis=axis, keepdims=True)
    m = jnp.where(jnp.isfinite(m), m, 0.0)
    e = jnp.where(mask, jnp.exp(s - m), 0.0)
    d = jnp.sum(e, axis=axis, keepdims=True)
    return e / jnp.maximum(d, 1e-30)


def short_conv_layer(x, norm_g, w_in, conv_w, w_out):
    S = x.shape[1]
    h = rmsnorm(x, norm_g)
    proj = h @ w_in
    b, c, u, z = jnp.split(proj, 4, axis=-1)
    vp = jnp.pad(c * u, ((0, 0), (CONV_WIDTH - 1, 0), (0, 0)))
    conv = conv_w[0] * vp[:, 0:S]
    for k in range(1, CONV_WIDTH):
        conv = conv + conv_w[k] * vp[:, k:k + S]
    y = b * conv * jax.nn.silu(z)
    return x + y @ w_out


def compress_blocks(kv, pos_emb, w1, w2):
    B, S, G, D = kv.shape
    chunks = kv.reshape(B, S // CMP_STRIDE, CMP_STRIDE, G, D)
    n_cmp = S // CMP_STRIDE - CMP_R + 1
    blocks = jnp.concatenate([chunks[:, r:r + n_cmp] for r in range(CMP_R)], axis=2)
    blocks = blocks + pos_emb[None, None, :, None, :]
    flat = blocks.transpose(0, 1, 3, 2, 4).reshape(B, n_cmp, G, CMP_BLOCK * D)
    return jax.nn.gelu(flat @ w1) @ w2


def shared_kv(h, norm_g, w_kv, cmp_pos_k, cmp_w1_k, cmp_w2_k, cmp_pos_v, cmp_w1_v, cmp_w2_v):
    B, S, _ = h.shape
    kv = (rmsnorm(h, norm_g) @ w_kv).reshape(B, S, N_KV_STREAMS, N_KV_GROUPS, HEAD_DIM)
    k_c, v_c, k_s, v_s, k_w, v_w = [kv[:, :, i] for i in range(N_KV_STREAMS)]
    k_cmp = compress_blocks(k_c, cmp_pos_k, cmp_w1_k, cmp_w2_k)
    v_cmp = compress_blocks(v_c, cmp_pos_v, cmp_w1_v, cmp_w2_v)
    return (k_cmp, v_cmp, partial_rope(k_s), v_s, partial_rope(k_w), v_w)


def nsa_attention(q, q_rot, gates, k_cmp, v_cmp, k_slc, v_slc, k_win, v_win):
    B, S, H, D = q.shape
    G, I = N_KV_GROUPS, HEADS_PER_GROUP
    n_blk = S // Q_BLOCK
    n_cmp = k_cmp.shape[1]
    n_sb = S // SLC_BLOCK
    n_sel = min(N_SELECT, n_sb)
    scale = HEAD_DIM ** -0.5
    cmp_end = jnp.arange(n_cmp) * CMP_STRIDE + CMP_BLOCK - 1
    kb_slc = k_slc.reshape(B, n_sb, SLC_BLOCK, G, D).transpose(0, 3, 1, 2, 4)
    vb_slc = v_slc.reshape(B, n_sb, SLC_BLOCK, G, D).transpose(0, 3, 1, 2, 4)
    k_win_p = jnp.pad(k_win, ((0, 0), (WINDOW, 0), (0, 0), (0, 0)))
    v_win_p = jnp.pad(v_win, ((0, 0), (WINDOW, 0), (0, 0), (0, 0)))
    bi = jnp.arange(B)[:, None, None, None]
    gi = jnp.arange(G)[None, None, :, None]
    blk_ids = jnp.arange(n_sb)

    def to_blocks(a):
        return a.reshape(B, n_blk, Q_BLOCK, G, I, a.shape[-1]).transpose(1, 0, 2, 3, 4, 5)

    def block_fn(args):
        blk, qc, qr, g = args
        start = blk * Q_BLOCK
        t = start + jnp.arange(Q_BLOCK)

        s_c = jnp.einsum('bqgid,bngd->bqgin', qc, k_cmp) * scale
        m_c = (cmp_end[None, :] <= t[:, None])[None, :, None, None, :]
        p_c = masked_softmax(s_c, m_c)
        o_c = jnp.einsum('bqgin,bngd->bqgid', p_c.astype(v_cmp.dtype), v_cmp)

        p_grp = jnp.pad(p_c.sum(axis=3), ((0, 0), (0, 0), (0, 0), (CMP_R - 1, CMP_R - 1)))
        imp = 0.0
        for m in range(SLC_RATIO):
            for n in range(CMP_R):
                off = CMP_R - 1 + m - n
                imp = imp + p_grp[..., off:off + SLC_RATIO * (n_sb - 1) + 1:SLC_RATIO]
        cur = t // SLC_BLOCK
        valid = blk_ids[None, :] <= cur[:, None]
        forced = (blk_ids[None, :] == 0) | (valid & (blk_ids[None, :] > cur[:, None] - N_LOCAL))
        score = jnp.where(forced[None, :, None, :], FORCE_SCORE, imp)
        score = jnp.where(valid[None, :, None, :], score, -jnp.inf)
        top_s, idx = lax.top_k(score, n_sel)
        sel_ok = jnp.isfinite(top_s)

        kg = kb_slc[bi, gi, idx]
        vg = vb_slc[bi, gi, idx]
        s_s = jnp.einsum('bqgid,bqgnld->bqginl', qr, kg) * scale
        kpos = idx[..., None] * SLC_BLOCK + jnp.arange(SLC_BLOCK)
        m_s = sel_ok[..., None] & (kpos <= t[None, :, None, None, None])
        p_s = masked_softmax(s_s.reshape(B, Q_BLOCK, G, I, n_sel * SLC_BLOCK),
                             m_s.reshape(B, Q_BLOCK, G, 1, n_sel * SLC_BLOCK))
        p_s = p_s.reshape(B, Q_BLOCK, G, I, n_sel, SLC_BLOCK).astype(vg.dtype)
        o_s = jnp.einsum('bqginl,bqgnld->bqgid', p_s, vg)

        kw = lax.dynamic_slice_in_dim(k_win_p, start, WINDOW + Q_BLOCK, axis=1)
        vw = lax.dynamic_slice_in_dim(v_win_p, start, WINDOW + Q_BLOCK, axis=1)
        kpos_w = start - WINDOW + jnp.arange(WINDOW + Q_BLOCK)
        m_w = ((kpos_w[None, :] <= t[:, None]) & (kpos_w[None, :] > t[:, None] - WINDOW)
               & (kpos_w[None, :] >= 0))
        s_w = jnp.einsum('bqgid,bkgd->bqgik', qr, kw) * scale
        p_w = masked_softmax(s_w, m_w[None, :, None, None, :]).astype(vw.dtype)
        o_w = jnp.einsum('bqgik,bkgd->bqgid', p_w, vw)

        return g[..., 0:1] * o_c + g[..., 1:2] * o_s + g[..., 2:3] * o_w

    xs = (jnp.arange(n_blk), to_blocks(q), to_blocks(q_rot), to_blocks(gates))
    out = lax.map(block_fn, xs)
    return out.transpose(1, 0, 2, 3, 4, 5).reshape(B, S, H, D)


def nsa_layer(x, kvs, norm_g, w_in, w_out):
    B, S, _ = x.shape
    h = rmsnorm(x, norm_g)
    proj = h @ w_in
    q = proj[..., :ATTN_D].reshape(B, S, N_HEADS, HEAD_DIM)
    gates = jax.nn.sigmoid(proj[..., ATTN_D:ATTN_D + N_HEADS * N_BRANCH]
                           .reshape(B, S, N_HEADS, N_BRANCH))
    z = proj[..., ATTN_D + N_HEADS * N_BRANCH:]
    o = nsa_attention(q, partial_rope(q), gates, *kvs)
    y = o.reshape(B, S, ATTN_D) * jax.nn.silu(z)
    return x + y @ w_out


def setup_inputs(seed: int = 0) -> dict:
    key = jax.random.key(seed)
    ks = jax.random.split(key, 20)
    f = jnp.float32
    nrm = lambda k, shape, s: jax.random.normal(k, shape, f) * s
    gain = lambda k, shape: 1.0 + 0.01 * jax.random.normal(k, shape, f)
    b_in_cols = ATTN_D + N_HEADS * N_BRANCH + ATTN_D
    return {
        "x": jax.random.normal(ks[0], (BATCH, SEQ, D_MODEL), f),
        "a_norm": gain(ks[1], (N_A_LAYERS, D_MODEL)),
        "a_w_in": nrm(ks[2], (N_A_LAYERS, D_MODEL, 4 * CONV_D), D_MODEL ** -0.5),
        "a_conv_w": nrm(ks[3], (N_A_LAYERS, CONV_WIDTH, CONV_D), CONV_WIDTH ** -0.5),
        "a_w_out": nrm(ks[4], (N_A_LAYERS, CONV_D, D_MODEL), CONV_D ** -0.5),
        "kv_norm": gain(ks[5], (D_MODEL,)),
        "w_kv": nrm(ks[6], (D_MODEL, N_KV_STREAMS * KV_D), D_MODEL ** -0.5),
        "cmp_pos_k": nrm(ks[7], (CMP_BLOCK, HEAD_DIM), 0.1),
        "cmp_w1_k": nrm(ks[8], (CMP_BLOCK * HEAD_DIM, CMP_HIDDEN), (CMP_BLOCK * HEAD_DIM) ** -0.5),
        "cmp_w2_k": nrm(ks[9], (CMP_HIDDEN, HEAD_DIM), CMP_HIDDEN ** -0.5),
        "cmp_pos_v": nrm(ks[10], (CMP_BLOCK, HEAD_DIM), 0.1),
        "cmp_w1_v": nrm(ks[11], (CMP_BLOCK * HEAD_DIM, CMP_HIDDEN), (CMP_BLOCK * HEAD_DIM) ** -0.5),
        "cmp_w2_v": nrm(ks[12], (CMP_HIDDEN, HEAD_DIM), CMP_HIDDEN ** -0.5),
        "b_norm": gain(ks[13], (N_B_LAYERS, D_MODEL)),
        "b_w_in": nrm(ks[14], (N_B_LAYERS, D_MODEL, b_in_cols), D_MODEL ** -0.5),
        "b_w_out": nrm(ks[15], (N_B_LAYERS, ATTN_D, D_MODEL), ATTN_D ** -0.5),
        "final_norm": gain(ks[16], (D_MODEL,)),
    }


def reference(x, a_norm, a_w_in, a_conv_w, a_w_out, kv_norm, w_kv, cmp_pos_k, cmp_w1_k,
              cmp_w2_k, cmp_pos_v, cmp_w1_v, cmp_w2_v, b_norm, b_w_in, b_w_out, final_norm):
    h = x
    kvs = None
    for layer in range(DEPTH):
        if layer < N_A_LAYERS:
            h = short_conv_layer(h, a_norm[layer], a_w_in[layer], a_conv_w[layer], a_w_out[layer])
        else:
            if layer == N_A_LAYERS:
                kvs = shared_kv(h, kv_norm, w_kv, cmp_pos_k, cmp_w1_k, cmp_w2_k,
                                cmp_pos_v, cmp_w1_v, cmp_w2_v)
            j = layer - N_A_LAYERS
            h = nsa_layer(h, kvs, b_norm[j], b_w_in[j], b_w_out[j])
    return rmsnorm(h, final_norm)
```

```python
import functools

import jax
import jax.numpy as jnp
from jax import lax
from jax.experimental import pallas as pl
from jax.experimental.pallas import tpu as pltpu

EPS = 1e-6
CONV_WIDTH = 3
N_HEADS = 16
HEAD_DIM = 64
N_KV_GROUPS = 4
HEADS_PER_GROUP = N_HEADS // N_KV_GROUPS
N_BRANCH = 3
ROT_DIM = HEAD_DIM // 4
N_FREQ = ROT_DIM // 2
ROPE_THETA = 500000.0
CMP_BLOCK = 32
CMP_STRIDE = 16
SLC_BLOCK = 64
N_SELECT = 16
N_LOCAL = 2
WINDOW = 512
FORCE_SCORE = 1e4

LANES = 128
SUBLANES = 8
KEY_CHUNK = 256
GATE_ROWS = 16
MASK_BIAS = -1e30
VMEM_LIMIT = 56 * 1024 * 1024

BF16 = jnp.bfloat16
F32 = jnp.float32
NT_DIMS = (((1,), (1,)), ((), ()))


def _dot(a, b):
    return jnp.dot(a, b, preferred_element_type=F32)


def _dot_nt(a, b):
    return lax.dot_general(a, b, NT_DIMS, preferred_element_type=F32)


def _sigmoid(x):
    return 1.0 / (1.0 + jnp.exp(-x))


def _inv_rms(x):
    return lax.rsqrt(jnp.mean(x * x, axis=-1, keepdims=True) + EPS)


def _conv_layer_kernel(x_ref, g_ref, win_ref, cw_ref, wout_ref, o_ref, vbuf_ref, *,
                       tm, tiles_per_seq, conv_d, cchunk):
    @pl.when(pl.program_id(0) % tiles_per_seq == 0)
    def _():
        vbuf_ref[0:SUBLANES, :] = jnp.zeros((SUBLANES, conv_d), F32)

    x = x_ref[...]
    hn = (x * _inv_rms(x) * g_ref[...]).astype(BF16)
    acc = jnp.zeros(x.shape, F32)
    for cc in range(conv_d // cchunk):
        cs = cc * cchunk
        b = _dot(hn, win_ref[:, cs:cs + cchunk])
        c = _dot(hn, win_ref[:, conv_d + cs:conv_d + cs + cchunk])
        u = _dot(hn, win_ref[:, 2 * conv_d + cs:2 * conv_d + cs + cchunk])
        z = _dot(hn, win_ref[:, 3 * conv_d + cs:3 * conv_d + cs + cchunk])
        v = c * u
        vbuf_ref[SUBLANES:SUBLANES + tm, cs:cs + cchunk] = v
        v1 = vbuf_ref[SUBLANES - 1:SUBLANES - 1 + tm, cs:cs + cchunk]
        v2 = vbuf_ref[SUBLANES - 2:SUBLANES - 2 + tm, cs:cs + cchunk]
        conv = (cw_ref[0:1, cs:cs + cchunk] * v2 + cw_ref[1:2, cs:cs + cchunk] * v1
                + cw_ref[2:3, cs:cs + cchunk] * v)
        vbuf_ref[0:SUBLANES, cs:cs + cchunk] = v[tm - SUBLANES:tm, :]
        y = b * conv * (z * _sigmoid(z))
        acc = acc + _dot(y.astype(BF16), wout_ref[cs:cs + cchunk, :])
    o_ref[...] = x + acc


def _conv_layer(h, norm_g, w_in, conv_w, w_out, *, seq, tm=512, cchunk=512):
    t, d = h.shape
    conv_d = conv_w.shape[1]
    tm = min(tm, seq)
    const = lambda shape: pl.BlockSpec(shape, lambda i: (0,) * len(shape), pipeline_mode=pl.Buffered(1))
    return pl.pallas_call(
        functools.partial(_conv_layer_kernel, tm=tm, tiles_per_seq=seq // tm, conv_d=conv_d, cchunk=cchunk),
        out_shape=jax.ShapeDtypeStruct((t, d), F32),
        grid=(t // tm,),
        in_specs=[
            pl.BlockSpec((tm, d), lambda i: (i, 0)),
            const((1, d)),
            const((d, 4 * conv_d)),
            const((CONV_WIDTH, conv_d)),
            const((conv_d, d)),
        ],
        out_specs=pl.BlockSpec((tm, d), lambda i: (i, 0)),
        scratch_shapes=[pltpu.VMEM((SUBLANES + tm, conv_d), F32)],
        compiler_params=pltpu.CompilerParams(dimension_semantics=("arbitrary",), vmem_limit_bytes=VMEM_LIMIT),
        name="conv_layer",
    )(h, norm_g.reshape(1, d), w_in, conv_w, w_out)


def _nsa_proj_kernel(h_ref, bn_ref, kn_ref, waT_ref, wvT_ref, wk_ref, cosT_ref, sinT_ref,
                     cosf_ref, sina_ref, sinb_ref,
                     qT_ref, qrT_ref, szT_ref, gT_ref, kvc_ref, ks_ref, kw_ref, vsT_ref, vwT_ref, *, tm, d):
    attn_d = N_HEADS * HEAD_DIM
    kv_d = N_KV_GROUPS * HEAD_DIM
    h = h_ref[0]
    hr = h * _inv_rms(h)
    hq = (hr * bn_ref[...]).astype(BF16)
    hk = (hr * kn_ref[...]).astype(BF16)

    qT = _dot_nt(waT_ref[0:attn_d, :], hq) * (HEAD_DIM ** -0.5)
    cosT = cosT_ref[...]
    sinT = sinT_ref[...]
    for hd in range(N_HEADS):
        r0 = hd * HEAD_DIM
        blk = qT[r0:r0 + HEAD_DIM, :]
        x1 = blk[0:N_FREQ, :]
        x2 = blk[N_FREQ:ROT_DIM, :]
        rot = jnp.concatenate([x1 * cosT - x2 * sinT, x2 * cosT + x1 * sinT, blk[ROT_DIM:, :]], axis=0)
        qT_ref[0, r0:r0 + HEAD_DIM, :] = blk.astype(BF16)
        qrT_ref[0, r0:r0 + HEAD_DIM, :] = rot.astype(BF16)

    zT = _dot_nt(waT_ref[attn_d:2 * attn_d, :], hq)
    szT_ref[0] = zT * _sigmoid(zT)
    gT_ref[0] = _sigmoid(_dot_nt(waT_ref[2 * attn_d:2 * attn_d + N_KV_GROUPS * GATE_ROWS, :], hq))

    vT = _dot_nt(wvT_ref[...], hk)
    vsT_ref[0, 0] = vT[0:kv_d, :].astype(BF16)
    vwT_ref[0, 0] = vT[kv_d:2 * kv_d, :].astype(BF16)

    kk = _dot(hk, wk_ref[...])
    kvc_ref[0] = kk[:, 0:2 * kv_d]
    cosf = cosf_ref[...]
    sina = sina_ref[...]
    sinb = sinb_ref[...]
    row = lax.broadcasted_iota(jnp.int32, (tm, LANES), 0)
    lane = lax.broadcasted_iota(jnp.int32, (tm, LANES), 1)
    tok_blk = (pl.program_id(1) * tm + row) // SLC_BLOCK
    onehot = jnp.where((lane >= HEAD_DIM) & (lane - HEAD_DIM == tok_blk), 1.0, 0.0)
    for j in range(2 * N_KV_GROUPS):
        c0 = 2 * kv_d + j * LANES
        xb = kk[:, c0:c0 + LANES]
        rot = (xb * cosf + pltpu.roll(xb, LANES - N_FREQ, axis=1) * sina
               + pltpu.roll(xb, N_FREQ, axis=1) * sinb)
        if j < N_KV_GROUPS:
            ks_ref[0, :, j * LANES:(j + 1) * LANES] = (rot + onehot).astype(BF16)
        else:
            jj = j - N_KV_GROUPS
            kw_ref[0, :, jj * LANES:(jj + 1) * LANES] = rot.astype(BF16)


def _nsa_proj(h3, b_norm, kv_norm, waT, wvT, wk, cosT, sinT, cosf, sina, sinb, *, tm=KEY_CHUNK):
    bsz, seq, d = h3.shape
    attn_d = N_HEADS * HEAD_DIM
    kv_d = N_KV_GROUPS * HEAD_DIM
    kpad = N_KV_GROUPS * LANES
    nt = seq // tm
    const = lambda shape: pl.BlockSpec(shape, lambda b, i: (0,) * len(shape))
    fm = lambda rows: pl.BlockSpec((1, rows, tm), lambda b, i: (b, 0, i))
    tmj = lambda cols: pl.BlockSpec((1, tm, cols), lambda b, i: (b, i, 0))
    out_shape = [
        jax.ShapeDtypeStruct((bsz, attn_d, seq), BF16),
        jax.ShapeDtypeStruct((bsz, attn_d, seq), BF16),
        jax.ShapeDtypeStruct((bsz, attn_d, seq), F32),
        jax.ShapeDtypeStruct((bsz, N_KV_GROUPS * GATE_ROWS, seq), F32),
        jax.ShapeDtypeStruct((bsz, seq, 2 * kv_d), F32),
        jax.ShapeDtypeStruct((bsz, seq, kpad), BF16),
        jax.ShapeDtypeStruct((bsz, seq, kpad), BF16),
        jax.ShapeDtypeStruct((bsz, nt, kv_d, tm), BF16),
        jax.ShapeDtypeStruct((bsz, nt, kv_d, tm), BF16),
    ]
    out_specs = [
        fm(attn_d), fm(attn_d), fm(attn_d), fm(N_KV_GROUPS * GATE_ROWS),
        tmj(2 * kv_d), tmj(kpad), tmj(kpad),
        pl.BlockSpec((1, 1, kv_d, tm), lambda b, i: (b, i, 0, 0)),
        pl.BlockSpec((1, 1, kv_d, tm), lambda b, i: (b, i, 0, 0)),
    ]
    return pl.pallas_call(
        functools.partial(_nsa_proj_kernel, tm=tm, d=d),
        out_shape=out_shape,
        grid=(bsz, nt),
        in_specs=[
            tmj(d), const((1, d)), const((1, d)),
            const(waT.shape), const(wvT.shape), const(wk.shape),
            pl.BlockSpec((N_FREQ, tm), lambda b, i: (0, i)),
            pl.BlockSpec((N_FREQ, tm), lambda b, i: (0, i)),
            pl.BlockSpec((tm, LANES), lambda b, i: (i, 0)),
            pl.BlockSpec((tm, LANES), lambda b, i: (i, 0)),
            pl.BlockSpec((tm, LANES), lambda b, i: (i, 0)),
        ],
        out_specs=out_specs,
        compiler_params=pltpu.CompilerParams(dimension_semantics=("parallel", "parallel"),
                                             vmem_limit_bytes=VMEM_LIMIT),
        name="nsa_proj",
    )(h3, b_norm.reshape(1, d), kv_norm.reshape(1, d), waT, wvT, wk, cosT, sinT, cosf, sina, sinb)


def _gelu_tanh(x):
    return x * (0.5 * (1.0 + jnp.tanh(0.7978845608028654 * (x + 0.044715 * (x * x * x)))))


def _compress_kernel(x_ref, w1_ref, pos_ref, w2_ref, w2T_ref, kc_ref, cT_ref, *, nchunk, hidden):
    u = jnp.zeros((nchunk, 2 * hidden), F32)
    v = jnp.zeros((nchunk, 2 * hidden), F32)
    for l in range(CMP_STRIDE):
        xl = x_ref[0, pl.ds(l, nchunk, stride=CMP_STRIDE), :]
        u = u + _dot((xl + pos_ref[0, l:l + 1, :]).astype(BF16), w1_ref[0, l])
        l2 = CMP_STRIDE + l
        v = v + _dot((xl + pos_ref[0, l2:l2 + 1, :]).astype(BF16), w1_ref[0, l2])
    hid = u + pltpu.roll(v, nchunk - 1, axis=0)
    act = _gelu_tanh(hid).astype(BF16)
    rows_ok = lax.broadcasted_iota(jnp.int32, (nchunk, LANES), 0) < nchunk - 1
    cols_ok = lax.broadcasted_iota(jnp.int32, (HEAD_DIM, nchunk), 1) < nchunk - 1
    for p in range(2):
        a = act[:, p * hidden:(p + 1) * hidden]
        kc_ref[0, 0, p] = jnp.where(rows_ok, _dot(a, w2_ref[0]), 0.0).astype(BF16)
        cT_ref[0, 0, p] = jnp.where(cols_ok, _dot_nt(w2T_ref[0], a), 0.0).astype(BF16)


def _compress(kvc, w1bd, pos2, w2p, w2T):
    bsz, seq, _ = kvc.shape
    nchunk = seq // CMP_STRIDE
    hidden = w2T.shape[2]
    return pl.pallas_call(
        functools.partial(_compress_kernel, nchunk=nchunk, hidden=hidden),
        out_shape=[
            jax.ShapeDtypeStruct((2, bsz, N_KV_GROUPS, nchunk, LANES), BF16),
            jax.ShapeDtypeStruct((2, bsz, N_KV_GROUPS, HEAD_DIM, nchunk), BF16),
        ],
        grid=(2, bsz, N_KV_GROUPS // 2),
        in_specs=[
            pl.BlockSpec((1, seq, LANES), lambda s, b, p: (b, 0, 2 * s + p)),
            pl.BlockSpec((1, CMP_BLOCK, LANES, 2 * hidden), lambda s, b, p: (s, 0, 0, 0)),
            pl.BlockSpec((1, CMP_BLOCK, LANES), lambda s, b, p: (s, 0, 0)),
            pl.BlockSpec((1, hidden, LANES), lambda s, b, p: (s, 0, 0)),
            pl.BlockSpec((1, HEAD_DIM, hidden), lambda s, b, p: (s, 0, 0)),
        ],
        out_specs=[
            pl.BlockSpec((1, 1, 2, nchunk, LANES), lambda s, b, p: (s, b, p, 0, 0)),
            pl.BlockSpec((1, 1, 2, HEAD_DIM, nchunk), lambda s, b, p: (s, b, p, 0, 0)),
        ],
        compiler_params=pltpu.CompilerParams(dimension_semantics=("parallel", "parallel", "parallel"),
                                             vmem_limit_bytes=VMEM_LIMIT),
        name="compress",
    )(kvc, w1bd, pos2, w2p, w2T)


def _attn_kernel(q_ref, qr_ref, g_ref, sz_ref, kc_ref, vcT_ref, ks_ref, kw_ref, vsT_ref, vwT_ref, y_ref,
                 p_sc, sc_sc, oc_sc, *, tq, n_cmp):
    qi = pl.program_id(2)
    t0 = qi * tq
    neg_inf = -jnp.inf
    tvec = t0 + lax.broadcasted_iota(jnp.int32, (1, tq), 1)

    kc = kc_ref[0, 0, :, 0:HEAD_DIM]
    cmp_end = lax.broadcasted_iota(jnp.int32, (n_cmp, tq), 0) * CMP_STRIDE + (CMP_BLOCK - 1)
    cmask = cmp_end <= tvec
    p_grp = jnp.zeros((n_cmp, tq), F32)
    for hh in range(HEADS_PER_GROUP):
        r0 = hh * HEAD_DIM
        s = jnp.where(cmask, _dot(kc, q_ref[0, r0:r0 + HEAD_DIM, :]), neg_inf)
        m = jnp.max(s, axis=0, keepdims=True)
        m = jnp.where(m == neg_inf, 0.0, m)
        e = jnp.exp(s - m)
        den = jnp.sum(e, axis=0, keepdims=True)
        p = e * (1.0 / jnp.maximum(den, 1e-30))
        p_grp = p_grp + p
        oc_sc[r0:r0 + HEAD_DIM, :] = _dot(vcT_ref[0, 0], p.astype(BF16))

    n_sb = n_cmp // 4
    ratio = SLC_BLOCK // CMP_STRIDE
    imp_cols = []
    for c in range(tq // LANES):
        p_sc[c, 0:SUBLANES, :] = jnp.zeros((SUBLANES, LANES), F32)
        p_sc[c, SUBLANES:SUBLANES + n_cmp, :] = p_grp[:, c * LANES:(c + 1) * LANES]
        tap = lambda o: p_sc[c, pl.ds(SUBLANES + o, n_sb, stride=ratio), :]
        imp_cols.append(tap(-1) + 2.0 * (tap(0) + tap(1) + tap(2)) + tap(3))
    imp = jnp.concatenate(imp_cols, axis=1)

    jrow = lax.broadcasted_iota(jnp.int32, (n_sb, tq), 0)
    cur = tvec // SLC_BLOCK
    valid = jrow <= cur
    forced = (jrow == 0) | (valid & (jrow > cur - N_LOCAL))
    score = jnp.where(valid, jnp.where(forced, FORCE_SCORE, imp), neg_inf)
    sc_sc[...] = score

    def rank_body(jp, cnt):
        sb = sc_sc[pl.ds(jp, 1), :]
        before = (sb > score) | ((sb == score) & (jrow > jp))
        return cnt + jnp.where(before, 1, 0)

    n_valid = (t0 + tq) // SLC_BLOCK
    rank = lax.fori_loop(0, n_valid, rank_body, jnp.zeros((n_sb, tq), jnp.int32))
    sel_bias = jnp.where((rank < N_SELECT) & valid, 0.0, MASK_BIAS).astype(BF16)
    if n_sb < HEAD_DIM:
        sel_bias = jnp.concatenate([sel_bias, jnp.zeros((HEAD_DIM - n_sb, tq), BF16)], axis=0)

    krow = lax.broadcasted_iota(jnp.int32, (KEY_CHUNK, tq), 0)

    def softmax_first(s, vT):
        m = jnp.max(s, axis=0, keepdims=True)
        p = jnp.exp(s - m)
        return m, jnp.sum(p, axis=0, keepdims=True), _dot(vT, p.astype(BF16))

    def softmax_next(carry, s, vT):
        m, l, acc = carry
        mn = jnp.maximum(m, jnp.max(s, axis=0, keepdims=True))
        a = jnp.exp(m - mn)
        p = jnp.exp(s - mn)
        return mn, a * l + jnp.sum(p, axis=0, keepdims=True), a * acc + _dot(vT, p.astype(BF16))

    def key_chunk(ref, c):
        return ref[0, pl.ds(pl.multiple_of(c * KEY_CHUNK, KEY_CHUNK), KEY_CHUNK), :]

    for hh in range(HEADS_PER_GROUP):
        r0 = hh * HEAD_DIM
        qa = jnp.concatenate([qr_ref[0, r0:r0 + HEAD_DIM, :], sel_bias], axis=0)

        causal = (t0 + krow) <= tvec
        carry = softmax_first(jnp.where(causal, _dot(key_chunk(ks_ref, qi), qa), neg_inf), vsT_ref[0, qi])

        def sel_body(c, carry):
            return softmax_next(carry, _dot(key_chunk(ks_ref, c), qa), vsT_ref[0, c])

        m, l, acc = lax.fori_loop(0, qi, sel_body, carry)
        o_s = acc * (1.0 / l)

        carry = softmax_first(jnp.where(causal, _dot(key_chunk(kw_ref, qi), qa), neg_inf), vwT_ref[0, qi])
        for back in (1, 2):
            c = jnp.maximum(qi - back, 0)
            kpos = t0 - back * KEY_CHUNK + krow
            wmask = (kpos > tvec - WINDOW) & (kpos >= 0)
            carry = softmax_next(carry, jnp.where(wmask, _dot(key_chunk(kw_ref, c), qa), neg_inf), vwT_ref[0, c])
        m, l, acc = carry
        o_w = acc * (1.0 / l)

        g0 = g_ref[0, N_BRANCH * hh:N_BRANCH * hh + 1, :]
        g1 = g_ref[0, N_BRANCH * hh + 1:N_BRANCH * hh + 2, :]
        g2 = g_ref[0, N_BRANCH * hh + 2:N_BRANCH * hh + 3, :]
        o = g0 * oc_sc[r0:r0 + HEAD_DIM, :] + g1 * o_s + g2 * o_w
        y_ref[0, r0:r0 + HEAD_DIM, :] = (o * sz_ref[0, r0:r0 + HEAD_DIM, :]).astype(BF16)


def _nsa_attn(qT, qrT, gT, szT, kcmp, vcmpT, ks, kw, vsT, vwT, *, tq=KEY_CHUNK):
    bsz, attn_d, seq = qT.shape
    n_cmp = kcmp.shape[2]
    nchunks = seq // KEY_CHUNK
    gd = HEADS_PER_GROUP * HEAD_DIM
    qspec = pl.BlockSpec((1, gd, tq), lambda b, g, i: (b, g, i))
    kspec = pl.BlockSpec((1, seq, LANES), lambda b, g, i: (b, 0, g))
    vspec = pl.BlockSpec((1, nchunks, HEAD_DIM, KEY_CHUNK), lambda b, g, i: (b, 0, g, 0))
    return pl.pallas_call(
        functools.partial(_attn_kernel, tq=tq, n_cmp=n_cmp),
        out_shape=jax.ShapeDtypeStruct((bsz, attn_d, seq), BF16),
        grid=(bsz, N_KV_GROUPS, seq // tq),
        in_specs=[
            qspec, qspec,
            pl.BlockSpec((1, GATE_ROWS, tq), lambda b, g, i: (b, g, i)),
            qspec,
            pl.BlockSpec((1, 1, n_cmp, LANES), lambda b, g, i: (b, g, 0, 0)),
            pl.BlockSpec((1, 1, HEAD_DIM, n_cmp), lambda b, g, i: (b, g, 0, 0)),
            kspec, kspec, vspec, vspec,
        ],
        out_specs=qspec,
        scratch_shapes=[
            pltpu.VMEM((tq // LANES, SUBLANES + n_cmp, LANES), F32),
            pltpu.VMEM((n_cmp // 4, tq), F32),
            pltpu.VMEM((gd, tq), F32),
        ],
        compiler_params=pltpu.CompilerParams(dimension_semantics=("parallel", "parallel", "arbitrary"),
                                             vmem_limit_bytes=VMEM_LIMIT),
        name="nsa_attn",
    )(qT, qrT, gT, szT, kcmp, vcmpT, ks, kw, vsT, vwT)


def _nsa_out_kernel(y_ref, h_ref, woT_ref, fn_ref, o_ref):
    oT = _dot(woT_ref[...], y_ref[0])
    h2 = h_ref[0] + oT.T
    o_ref[0] = h2 * _inv_rms(h2) * fn_ref[...]


def _nsa_out(yT, h3, woT, final_norm, *, tm=KEY_CHUNK):
    bsz, seq, d = h3.shape
    attn_d = yT.shape[1]
    return pl.pallas_call(
        _nsa_out_kernel,
        out_shape=jax.ShapeDtypeStruct((bsz, seq, d), F32),
        grid=(bsz, seq // tm),
        in_specs=[
            pl.BlockSpec((1, attn_d, tm), lambda b, i: (b, 0, i)),
            pl.BlockSpec((1, tm, d), lambda b, i: (b, i, 0)),
            pl.BlockSpec((d, attn_d), lambda b, i: (0, 0)),
            pl.BlockSpec((1, d), lambda b, i: (0, 0)),
        ],
        out_specs=pl.BlockSpec((1, tm, d), lambda b, i: (b, i, 0)),
        compiler_params=pltpu.CompilerParams(dimension_semantics=("parallel", "parallel"),
                                             vmem_limit_bytes=VMEM_LIMIT),
        name="nsa_out",
    )(yT, h3, woT, final_norm.reshape(1, d))


def _rope_tables(seq):
    pos = jnp.arange(seq, dtype=F32)
    inv = ROPE_THETA ** (-jnp.arange(0, ROT_DIM, 2, dtype=F32) / ROT_DIM)
    ang = pos[:, None] * inv[None, :]
    cos, sin = jnp.cos(ang), jnp.sin(ang)
    z = lambda n: jnp.zeros((seq, n), F32)
    cosf = jnp.concatenate([cos, cos, jnp.ones((seq, LANES - ROT_DIM), F32)], axis=1)
    sina = jnp.concatenate([-sin, z(LANES - N_FREQ)], axis=1)
    sinb = jnp.concatenate([z(N_FREQ), sin, z(LANES - ROT_DIM)], axis=1)
    return cos.T, sin.T, cosf, sina, sinb


def _blockdiag2(w1):
    hidden = w1.shape[1]
    w = w1.reshape(CMP_BLOCK, HEAD_DIM, hidden)
    zero = jnp.zeros_like(w)
    top = jnp.concatenate([w, zero], axis=2)
    bot = jnp.concatenate([zero, w], axis=2)
    return jnp.concatenate([top, bot], axis=1)


def kernel(x, a_norm, a_w_in, a_conv_w, a_w_out, kv_norm, w_kv, cmp_pos_k, cmp_w1_k, cmp_w2_k,
           cmp_pos_v, cmp_w1_v, cmp_w2_v, b_norm, b_w_in, b_w_out, final_norm):
    bsz, seq, d = x.shape
    attn_d = N_HEADS * HEAD_DIM
    kv_d = N_KV_GROUPS * HEAD_DIM
    assert b_norm.shape[0] == 1, "one NSA layer reads the shared K/V side"
    assert seq % (2 * KEY_CHUNK) == 0

    h = x.reshape(bsz * seq, d)
    for layer in range(a_norm.shape[0]):
        h = _conv_layer(h, a_norm[layer], a_w_in[layer].astype(BF16), a_conv_w[layer],
                        a_w_out[layer].astype(BF16), seq=seq)
    h3 = h.reshape(bsz, seq, d)

    w_in = b_w_in[0]
    n_gate = N_HEADS * N_BRANCH
    wg = w_in[:, attn_d:attn_d + n_gate].reshape(d, N_KV_GROUPS, HEADS_PER_GROUP * N_BRANCH)
    wg = jnp.pad(wg, ((0, 0), (0, 0), (0, GATE_ROWS - HEADS_PER_GROUP * N_BRANCH)))
    waT = jnp.concatenate([w_in[:, :attn_d], w_in[:, attn_d + n_gate:], wg.reshape(d, -1)], axis=1).T.astype(BF16)
    wkv = w_kv.reshape(d, 2 * N_BRANCH, N_KV_GROUPS, HEAD_DIM)
    k_c, v_c, k_s, v_s, k_w, v_w = [wkv[:, i] for i in range(2 * N_BRANCH)]
    flat = lambda w: w.reshape(d, kv_d)
    pad_lanes = lambda w: jnp.pad(w, ((0, 0), (0, 0), (0, LANES - HEAD_DIM))).reshape(d, N_KV_GROUPS * LANES)
    wvT = jnp.concatenate([flat(v_s), flat(v_w)], axis=1).T.astype(BF16)
    wk = jnp.concatenate([flat(k_c), flat(v_c), pad_lanes(k_s), pad_lanes(k_w)], axis=1).astype(BF16)
    cosT, sinT, cosf, sina, sinb = _rope_tables(seq)

    qT, qrT, szT, gT, kvc, ks, kw, vsT, vwT = _nsa_proj(
        h3, b_norm[0], kv_norm, waT, wvT, wk, cosT, sinT, cosf, sina, sinb)

    w1bd = jnp.stack([_blockdiag2(cmp_w1_k), _blockdiag2(cmp_w1_v)]).astype(BF16)
    pos2 = jnp.stack([jnp.tile(cmp_pos_k, (1, 2)), jnp.tile(cmp_pos_v, (1, 2))])
    w2 = jnp.stack([cmp_w2_k, cmp_w2_v])
    w2p = jnp.pad(w2, ((0, 0), (0, 0), (0, LANES - HEAD_DIM))).astype(BF16)
    w2T = jnp.swapaxes(w2, 1, 2).astype(BF16)
    cmp_tm, cmp_fm = _compress(kvc, w1bd, pos2, w2p, w2T)

    yT = _nsa_attn(qT, qrT, gT, szT, cmp_tm[0], cmp_fm[1], ks, kw, vsT, vwT)
    return _nsa_out(yT, h3, b_w_out[0].T.astype(BF16), final_norm)
```

```python
import functools

import jax
import jax.numpy as jnp
from jax import lax
from jax.experimental import pallas as pl
from jax.experimental.pallas import tpu as pltpu

EPS = 1e-6
CONV_WIDTH = 3
N_HEADS = 16
HEAD_DIM = 64
N_KV_GROUPS = 4
HEADS_PER_GROUP = N_HEADS // N_KV_GROUPS
N_BRANCH = 3
ROT_DIM = HEAD_DIM // 4
N_FREQ = ROT_DIM // 2
ROPE_THETA = 500000.0
CMP_BLOCK = 32
CMP_STRIDE = 16
SLC_BLOCK = 64
N_SELECT = 16
N_LOCAL = 2
WINDOW = 512
FORCE_SCORE = 1e4

LANES = 128
SUBLANES = 8
KEY_CHUNK = 256
GATE_ROWS = 16
MASK_BIAS = -1e30
LOG2_E = 1.4426950408889634
VMEM_LIMIT = 56 * 1024 * 1024

BF16 = jnp.bfloat16
F32 = jnp.float32
NT_DIMS = (((1,), (1,)), ((), ()))


def _dot(a, b):
    return jnp.dot(a, b, preferred_element_type=F32)


def _dot_nt(a, b):
    return lax.dot_general(a, b, NT_DIMS, preferred_element_type=F32)


def _sigmoid(x):
    return 1.0 / (1.0 + jnp.exp(-x))


def _inv_rms(x):
    return lax.rsqrt(jnp.mean(x * x, axis=-1, keepdims=True) + EPS)


def _conv_layer_kernel(x_ref, g_ref, win_ref, cw_ref, wout_ref, o_ref, vbuf_ref, *,
                       tm, tiles_per_seq, conv_d, cchunk):
    @pl.when(pl.program_id(0) % tiles_per_seq == 0)
    def _():
        vbuf_ref[0:SUBLANES, :] = jnp.zeros((SUBLANES, conv_d), F32)

    x = x_ref[...]
    hn = (x * _inv_rms(x) * g_ref[...]).astype(BF16)
    acc = jnp.zeros(x.shape, F32)
    for cc in range(conv_d // cchunk):
        cs = cc * cchunk
        b = _dot(hn, win_ref[:, cs:cs + cchunk])
        c = _dot(hn, win_ref[:, conv_d + cs:conv_d + cs + cchunk])
        u = _dot(hn, win_ref[:, 2 * conv_d + cs:2 * conv_d + cs + cchunk])
        z = _dot(hn, win_ref[:, 3 * conv_d + cs:3 * conv_d + cs + cchunk])
        v = c * u
        vbuf_ref[SUBLANES:SUBLANES + tm, cs:cs + cchunk] = v
        v1 = vbuf_ref[SUBLANES - 1:SUBLANES - 1 + tm, cs:cs + cchunk]
        v2 = vbuf_ref[SUBLANES - 2:SUBLANES - 2 + tm, cs:cs + cchunk]
        conv = (cw_ref[0:1, cs:cs + cchunk] * v2 + cw_ref[1:2, cs:cs + cchunk] * v1
                + cw_ref[2:3, cs:cs + cchunk] * v)
        vbuf_ref[0:SUBLANES, cs:cs + cchunk] = v[tm - SUBLANES:tm, :]
        y = b * conv * (z * _sigmoid(z))
        acc = acc + _dot(y.astype(BF16), wout_ref[cs:cs + cchunk, :])
    o_ref[...] = x + acc


def _conv_layer(h, norm_g, w_in, conv_w, w_out, *, seq, tm=512, cchunk=512):
    t, d = h.shape
    conv_d = conv_w.shape[1]
    tm = min(tm, seq)
    const = lambda shape: pl.BlockSpec(shape, lambda i: (0,) * len(shape), pipeline_mode=pl.Buffered(1))
    return pl.pallas_call(
        functools.partial(_conv_layer_kernel, tm=tm, tiles_per_seq=seq // tm, conv_d=conv_d, cchunk=cchunk),
        out_shape=jax.ShapeDtypeStruct((t, d), F32),
        grid=(t // tm,),
        in_specs=[
            pl.BlockSpec((tm, d), lambda i: (i, 0)),
            const((1, d)),
            const((d, 4 * conv_d)),
            const((CONV_WIDTH, conv_d)),
            const((conv_d, d)),
        ],
        out_specs=pl.BlockSpec((tm, d), lambda i: (i, 0)),
        scratch_shapes=[pltpu.VMEM((SUBLANES + tm, conv_d), F32)],
        compiler_params=pltpu.CompilerParams(dimension_semantics=("arbitrary",), vmem_limit_bytes=VMEM_LIMIT),
        name="conv_layer",
    )(h, norm_g.reshape(1, d), w_in, conv_w, w_out)


def _nsa_proj_kernel(h_ref, bn_ref, kn_ref, waT_ref, wvT_ref, wk_ref, cosT_ref, sinT_ref,
                     cosf_ref, sina_ref, sinb_ref,
                     qT_ref, qrT_ref, szT_ref, gT_ref, kvc_ref, ks_ref, kw_ref, vsT_ref, vwT_ref, *, tm, d):
    attn_d = N_HEADS * HEAD_DIM
    kv_d = N_KV_GROUPS * HEAD_DIM
    h = h_ref[0]
    hr = h * _inv_rms(h)
    hq = (hr * bn_ref[...]).astype(BF16)
    hk = (hr * kn_ref[...]).astype(BF16)

    qT = _dot_nt(waT_ref[0:attn_d, :], hq) * (HEAD_DIM ** -0.5 * LOG2_E)
    cosT = cosT_ref[...]
    sinT = sinT_ref[...]
    for hd in range(N_HEADS):
        r0 = hd * HEAD_DIM
        blk = qT[r0:r0 + HEAD_DIM, :]
        x1 = blk[0:N_FREQ, :]
        x2 = blk[N_FREQ:ROT_DIM, :]
        rot = jnp.concatenate([x1 * cosT - x2 * sinT, x2 * cosT + x1 * sinT, blk[ROT_DIM:, :]], axis=0)
        qT_ref[0, r0:r0 + HEAD_DIM, :] = blk.astype(BF16)
        qrT_ref[0, r0:r0 + HEAD_DIM, :] = rot.astype(BF16)

    zT = _dot_nt(waT_ref[attn_d:2 * attn_d, :], hq)
    szT_ref[0] = zT * _sigmoid(zT)
    gT_ref[0] = _sigmoid(_dot_nt(waT_ref[2 * attn_d:2 * attn_d + N_KV_GROUPS * GATE_ROWS, :], hq))

    vT = _dot_nt(wvT_ref[...], hk)
    vsT_ref[0, 0] = vT[0:kv_d, :].astype(BF16)
    vwT_ref[0, 0] = vT[kv_d:2 * kv_d, :].astype(BF16)

    kk = _dot(hk, wk_ref[...])
    kvc_ref[0] = kk[:, 0:2 * kv_d]
    cosf = cosf_ref[...]
    sina = sina_ref[...]
    sinb = sinb_ref[...]
    row = lax.broadcasted_iota(jnp.int32, (tm, LANES), 0)
    lane = lax.broadcasted_iota(jnp.int32, (tm, LANES), 1)
    tok_blk = (pl.program_id(1) * tm + row) // SLC_BLOCK
    onehot = jnp.where((lane >= HEAD_DIM) & (lane - HEAD_DIM == tok_blk), 1.0, 0.0)
    for j in range(2 * N_KV_GROUPS):
        c0 = 2 * kv_d + j * LANES
        xb = kk[:, c0:c0 + LANES]
        rot = (xb * cosf + pltpu.roll(xb, LANES - N_FREQ, axis=1) * sina
               + pltpu.roll(xb, N_FREQ, axis=1) * sinb)
        if j < N_KV_GROUPS:
            ks_ref[0, :, j * LANES:(j + 1) * LANES] = (rot + onehot).astype(BF16)
        else:
            jj = j - N_KV_GROUPS
            kw_ref[0, :, jj * LANES:(jj + 1) * LANES] = rot.astype(BF16)


def _nsa_proj(h3, b_norm, kv_norm, waT, wvT, wk, cosT, sinT, cosf, sina, sinb, *, tm=KEY_CHUNK):
    bsz, seq, d = h3.shape
    attn_d = N_HEADS * HEAD_DIM
    kv_d = N_KV_GROUPS * HEAD_DIM
    kpad = N_KV_GROUPS * LANES
    nt = seq // tm
    const = lambda shape: pl.BlockSpec(shape, lambda b, i: (0,) * len(shape))
    fm = lambda rows: pl.BlockSpec((1, rows, tm), lambda b, i: (b, 0, i))
    tmj = lambda cols: pl.BlockSpec((1, tm, cols), lambda b, i: (b, i, 0))
    out_shape = [
        jax.ShapeDtypeStruct((bsz, attn_d, seq), BF16),
        jax.ShapeDtypeStruct((bsz, attn_d, seq), BF16),
        jax.ShapeDtypeStruct((bsz, attn_d, seq), F32),
        jax.ShapeDtypeStruct((bsz, N_KV_GROUPS * GATE_ROWS, seq), F32),
        jax.ShapeDtypeStruct((bsz, seq, 2 * kv_d), F32),
        jax.ShapeDtypeStruct((bsz, seq, kpad), BF16),
        jax.ShapeDtypeStruct((bsz, seq, kpad), BF16),
        jax.ShapeDtypeStruct((bsz, nt, kv_d, tm), BF16),
        jax.ShapeDtypeStruct((bsz, nt, kv_d, tm), BF16),
    ]
    out_specs = [
        fm(attn_d), fm(attn_d), fm(attn_d), fm(N_KV_GROUPS * GATE_ROWS),
        tmj(2 * kv_d), tmj(kpad), tmj(kpad),
        pl.BlockSpec((1, 1, kv_d, tm), lambda b, i: (b, i, 0, 0)),
        pl.BlockSpec((1, 1, kv_d, tm), lambda b, i: (b, i, 0, 0)),
    ]
    return pl.pallas_call(
        functools.partial(_nsa_proj_kernel, tm=tm, d=d),
        out_shape=out_shape,
        grid=(bsz, nt),
        in_specs=[
            tmj(d), const((1, d)), const((1, d)),
            const(waT.shape), const(wvT.shape), const(wk.shape),
            pl.BlockSpec((N_FREQ, tm), lambda b, i: (0, i)),
            pl.BlockSpec((N_FREQ, tm), lambda b, i: (0, i)),
            pl.BlockSpec((tm, LANES), lambda b, i: (i, 0)),
            pl.BlockSpec((tm, LANES), lambda b, i: (i, 0)),
            pl.BlockSpec((tm, LANES), lambda b, i: (i, 0)),
        ],
        out_specs=out_specs,
        compiler_params=pltpu.CompilerParams(dimension_semantics=("parallel", "parallel"),
                                             vmem_limit_bytes=VMEM_LIMIT),
        name="nsa_proj",
    )(h3, b_norm.reshape(1, d), kv_norm.reshape(1, d), waT, wvT, wk, cosT, sinT, cosf, sina, sinb)


def _gelu_tanh(x):
    return x * (0.5 * (1.0 + jnp.tanh(0.7978845608028654 * (x + 0.044715 * (x * x * x)))))


def _compress_kernel(x_ref, w1_ref, pos_ref, w2_ref, w2T_ref, kc_ref, cT_ref, *, nchunk, hidden):
    u = jnp.zeros((nchunk, 2 * hidden), F32)
    v = jnp.zeros((nchunk, 2 * hidden), F32)
    for l in range(CMP_STRIDE):
        xl = x_ref[0, pl.ds(l, nchunk, stride=CMP_STRIDE), :]
        u = u + _dot((xl + pos_ref[0, l:l + 1, :]).astype(BF16), w1_ref[0, l])
        l2 = CMP_STRIDE + l
        v = v + _dot((xl + pos_ref[0, l2:l2 + 1, :]).astype(BF16), w1_ref[0, l2])
    hid = u + pltpu.roll(v, nchunk - 1, axis=0)
    act = _gelu_tanh(hid).astype(BF16)
    rows_ok = lax.broadcasted_iota(jnp.int32, (nchunk, LANES), 0) < nchunk - 1
    cols_ok = lax.broadcasted_iota(jnp.int32, (HEAD_DIM, nchunk), 1) < nchunk - 1
    for p in range(2):
        a = act[:, p * hidden:(p + 1) * hidden]
        kc_ref[0, 0, p] = jnp.where(rows_ok, _dot(a, w2_ref[0]), 0.0).astype(BF16)
        cT_ref[0, 0, p] = jnp.where(cols_ok, _dot_nt(w2T_ref[0], a), 0.0).astype(BF16)


def _compress(kvc, w1bd, pos2, w2p, w2T):
    bsz, seq, _ = kvc.shape
    nchunk = seq // CMP_STRIDE
    hidden = w2T.shape[2]
    return pl.pallas_call(
        functools.partial(_compress_kernel, nchunk=nchunk, hidden=hidden),
        out_shape=[
            jax.ShapeDtypeStruct((2, bsz, N_KV_GROUPS, nchunk, LANES), BF16),
            jax.ShapeDtypeStruct((2, bsz, N_KV_GROUPS, HEAD_DIM, nchunk), BF16),
        ],
        grid=(2, bsz, N_KV_GROUPS // 2),
        in_specs=[
            pl.BlockSpec((1, seq, LANES), lambda s, b, p: (b, 0, 2 * s + p)),
            pl.BlockSpec((1, CMP_BLOCK, LANES, 2 * hidden), lambda s, b, p: (s, 0, 0, 0)),
            pl.BlockSpec((1, CMP_BLOCK, LANES), lambda s, b, p: (s, 0, 0)),
            pl.BlockSpec((1, hidden, LANES), lambda s, b, p: (s, 0, 0)),
            pl.BlockSpec((1, HEAD_DIM, hidden), lambda s, b, p: (s, 0, 0)),
        ],
        out_specs=[
            pl.BlockSpec((1, 1, 2, nchunk, LANES), lambda s, b, p: (s, b, p, 0, 0)),
            pl.BlockSpec((1, 1, 2, HEAD_DIM, nchunk), lambda s, b, p: (s, b, p, 0, 0)),
        ],
        compiler_params=pltpu.CompilerParams(dimension_semantics=("parallel", "parallel", "parallel"),
                                             vmem_limit_bytes=VMEM_LIMIT),
        name="compress",
    )(kvc, w1bd, pos2, w2p, w2T)


def _attn_kernel(q_ref, qr_ref, g_ref, sz_ref, kc_ref, vcT_ref, ks_ref, kw_ref, vsT_ref, vwT_ref, y_ref,
                 p_sc, sc_sc, oc_sc, qa_sc, s_sc, acc_sc, *, tq, n_cmp):
    qi = pl.program_id(2)
    t0 = qi * tq
    neg_inf = -jnp.inf
    tvec = t0 + lax.broadcasted_iota(jnp.int32, (1, tq), 1)

    kc = kc_ref[0, 0, :, 0:HEAD_DIM]
    cmp_end = lax.broadcasted_iota(jnp.int32, (n_cmp, tq), 0) * CMP_STRIDE + (CMP_BLOCK - 1)
    cmask = cmp_end <= tvec
    p_grp = jnp.zeros((n_cmp, tq), F32)
    for hh in range(HEADS_PER_GROUP):
        r0 = hh * HEAD_DIM
        s = jnp.where(cmask, _dot(kc, q_ref[0, r0:r0 + HEAD_DIM, :]), neg_inf)
        m = jnp.max(s, axis=0, keepdims=True)
        m = jnp.where(m == neg_inf, 0.0, m)
        e = jnp.exp2(s - m)
        den = jnp.sum(e, axis=0, keepdims=True)
        p = e * (1.0 / jnp.maximum(den, 1e-30))
        p_grp = p_grp + p
        oc_sc[r0:r0 + HEAD_DIM, :] = _dot(vcT_ref[0, 0], p.astype(BF16))

    n_sb = n_cmp // 4
    ratio = SLC_BLOCK // CMP_STRIDE
    imp_cols = []
    for c in range(tq // LANES):
        p_sc[c, 0:SUBLANES, :] = jnp.zeros((SUBLANES, LANES), F32)
        p_sc[c, SUBLANES:SUBLANES + n_cmp, :] = p_grp[:, c * LANES:(c + 1) * LANES]
        tap = lambda o: p_sc[c, pl.ds(SUBLANES + o, n_sb, stride=ratio), :]
        imp_cols.append(tap(-1) + 2.0 * (tap(0) + tap(1) + tap(2)) + tap(3))
    imp = jnp.concatenate(imp_cols, axis=1)

    jrow = lax.broadcasted_iota(jnp.int32, (n_sb, tq), 0)
    cur = tvec // SLC_BLOCK
    valid = jrow <= cur
    forced = (jrow == 0) | (valid & (jrow > cur - N_LOCAL))
    score = jnp.where(valid, jnp.where(forced, FORCE_SCORE, imp), neg_inf)
    sc_sc[...] = score

    def rank_body(jp, cnt):
        sb = sc_sc[pl.ds(jp, 1), :]
        before = (sb > score) | ((sb == score) & (jrow > jp))
        return cnt + jnp.where(before, 1, 0)

    n_valid = (t0 + tq) // SLC_BLOCK
    n_rank = jnp.where(n_valid > N_SELECT, n_valid, 0)
    rank = lax.fori_loop(0, n_rank, rank_body, jnp.zeros((n_sb, tq), jnp.int32))
    sel_bias = jnp.where((rank < N_SELECT) & valid, 0.0, MASK_BIAS).astype(BF16)
    if n_sb < HEAD_DIM:
        sel_bias = jnp.concatenate([sel_bias, jnp.zeros((HEAD_DIM - n_sb, tq), BF16)], axis=0)

    krow = lax.broadcasted_iota(jnp.int32, (KEY_CHUNK, tq), 0)

    heads = range(HEADS_PER_GROUP)

    def fold8(x, op):
        slabs = [x[r:r + SUBLANES, :] for r in range(0, x.shape[0], SUBLANES)]
        while len(slabs) > 1:
            slabs = [op(slabs[i], slabs[i + 1]) for i in range(0, len(slabs), 2)]
        return slabs[0]

    def key_chunk(ref, c):
        return ref[0, pl.ds(pl.multiple_of(c * KEY_CHUNK, KEY_CHUNK), KEY_CHUNK), :]

    for hh in heads:
        r0 = hh * HEAD_DIM
        qa_sc[hh] = jnp.concatenate([qr_ref[0, r0:r0 + HEAD_DIM, :], sel_bias], axis=0)
    causal = (t0 + krow) <= tvec

    def score_body(c, mparts):
        kch = key_chunk(ks_ref, c)
        out = []
        for hh in heads:
            s = _dot(kch, qa_sc[hh])
            s_sc[hh, c] = s
            out.append(jnp.maximum(mparts[hh], fold8(s, jnp.maximum)))
        return tuple(out)

    mparts = lax.fori_loop(0, qi, score_body,
                           tuple(jnp.full((SUBLANES, tq), neg_inf, F32) for _ in heads))
    kch = key_chunk(ks_ref, qi)
    m_sel = []
    for hh in heads:
        s = jnp.where(causal, _dot(kch, qa_sc[hh]), neg_inf)
        s_sc[hh, qi] = s
        m_sel.append(jnp.max(jnp.maximum(mparts[hh], fold8(s, jnp.maximum)), axis=0, keepdims=True))

    for hh in heads:
        acc_sc[hh] = jnp.zeros((HEAD_DIM, tq), F32)

    def pv_body(c, lparts):
        vT = vsT_ref[0, c]
        out = []
        for hh in heads:
            p = jnp.exp2(s_sc[hh, c] - m_sel[hh])
            acc_sc[hh] += _dot(vT, p.astype(BF16))
            out.append(lparts[hh] + fold8(p, jnp.add))
        return tuple(out)

    lparts = lax.fori_loop(0, qi + 1, pv_body, tuple(jnp.zeros((SUBLANES, tq), F32) for _ in heads))

    c1 = jnp.maximum(qi - 1, 0)
    c2 = jnp.maximum(qi - 2, 0)
    kw0, kw1, kw2 = key_chunk(kw_ref, qi), key_chunk(kw_ref, c1), key_chunk(kw_ref, c2)
    kpos1 = t0 - KEY_CHUNK + krow
    kpos2 = kpos1 - KEY_CHUNK
    exists1 = kpos1 >= 0
    in_window = (kpos2 > tvec - WINDOW) & (kpos2 >= 0)
    for hh in heads:
        r0 = hh * HEAD_DIM
        qa = qa_sc[hh]
        s0 = jnp.where(causal, _dot(kw0, qa), neg_inf)
        s1 = jnp.where(exists1, _dot(kw1, qa), neg_inf)
        s2 = jnp.where(in_window, _dot(kw2, qa), neg_inf)
        m8 = jnp.maximum(jnp.maximum(fold8(s0, jnp.maximum), fold8(s1, jnp.maximum)), fold8(s2, jnp.maximum))
        m = jnp.max(m8, axis=0, keepdims=True)
        p0, p1, p2 = jnp.exp2(s0 - m), jnp.exp2(s1 - m), jnp.exp2(s2 - m)
        l = jnp.sum(fold8(p0, jnp.add) + fold8(p1, jnp.add) + fold8(p2, jnp.add), axis=0, keepdims=True)
        acc = (_dot(vwT_ref[0, qi], p0.astype(BF16)) + _dot(vwT_ref[0, c1], p1.astype(BF16))
               + _dot(vwT_ref[0, c2], p2.astype(BF16)))
        o_w = acc * (1.0 / l)
        o_s = acc_sc[hh] * (1.0 / jnp.sum(lparts[hh], axis=0, keepdims=True))

        g0 = g_ref[0, N_BRANCH * hh:N_BRANCH * hh + 1, :]
        g1 = g_ref[0, N_BRANCH * hh + 1:N_BRANCH * hh + 2, :]
        g2 = g_ref[0, N_BRANCH * hh + 2:N_BRANCH * hh + 3, :]
        o = g0 * oc_sc[r0:r0 + HEAD_DIM, :] + g1 * o_s + g2 * o_w
        y_ref[0, r0:r0 + HEAD_DIM, :] = (o * sz_ref[0, r0:r0 + HEAD_DIM, :]).astype(BF16)


def _nsa_attn(qT, qrT, gT, szT, kcmp, vcmpT, ks, kw, vsT, vwT, *, tq=KEY_CHUNK):
    bsz, attn_d, seq = qT.shape
    n_cmp = kcmp.shape[2]
    nchunks = seq // KEY_CHUNK
    gd = HEADS_PER_GROUP * HEAD_DIM
    qspec = pl.BlockSpec((1, gd, tq), lambda b, g, i: (b, g, i))
    kspec = pl.BlockSpec((1, seq, LANES), lambda b, g, i: (b, 0, g))
    vspec = pl.BlockSpec((1, nchunks, HEAD_DIM, KEY_CHUNK), lambda b, g, i: (b, 0, g, 0))
    return pl.pallas_call(
        functools.partial(_attn_kernel, tq=tq, n_cmp=n_cmp),
        out_shape=jax.ShapeDtypeStruct((bsz, attn_d, seq), BF16),
        grid=(bsz, N_KV_GROUPS, seq // tq),
        in_specs=[
            qspec, qspec,
            pl.BlockSpec((1, GATE_ROWS, tq), lambda b, g, i: (b, g, i)),
            qspec,
            pl.BlockSpec((1, 1, n_cmp, LANES), lambda b, g, i: (b, g, 0, 0)),
            pl.BlockSpec((1, 1, HEAD_DIM, n_cmp), lambda b, g, i: (b, g, 0, 0)),
            kspec, kspec, vspec, vspec,
        ],
        out_specs=qspec,
        scratch_shapes=[
            pltpu.VMEM((tq // LANES, SUBLANES + n_cmp, LANES), F32),
            pltpu.VMEM((n_cmp // 4, tq), F32),
            pltpu.VMEM((gd, tq), F32),
            pltpu.VMEM((HEADS_PER_GROUP, 2 * HEAD_DIM, tq), BF16),
            pltpu.VMEM((HEADS_PER_GROUP, nchunks, KEY_CHUNK, tq), F32),
            pltpu.VMEM((HEADS_PER_GROUP, HEAD_DIM, tq), F32),
        ],
        compiler_params=pltpu.CompilerParams(dimension_semantics=("parallel", "parallel", "arbitrary"),
                                             vmem_limit_bytes=VMEM_LIMIT),
        name="nsa_attn",
    )(qT, qrT, gT, szT, kcmp, vcmpT, ks, kw, vsT, vwT)


def _nsa_out_kernel(y_ref, h_ref, woT_ref, fn_ref, o_ref):
    oT = _dot(woT_ref[...], y_ref[0])
    h2 = h_ref[0] + oT.T
    o_ref[0] = h2 * _inv_rms(h2) * fn_ref[...]


def _nsa_out(yT, h3, woT, final_norm, *, tm=KEY_CHUNK):
    bsz, seq, d = h3.shape
    attn_d = yT.shape[1]
    return pl.pallas_call(
        _nsa_out_kernel,
        out_shape=jax.ShapeDtypeStruct((bsz, seq, d), F32),
        grid=(bsz, seq // tm),
        in_specs=[
            pl.BlockSpec((1, attn_d, tm), lambda b, i: (b, 0, i)),
            pl.BlockSpec((1, tm, d), lambda b, i: (b, i, 0)),
            pl.BlockSpec((d, attn_d), lambda b, i: (0, 0)),
            pl.BlockSpec((1, d), lambda b, i: (0, 0)),
        ],
        out_specs=pl.BlockSpec((1, tm, d), lambda b, i: (b, i, 0)),
        compiler_params=pltpu.CompilerParams(dimension_semantics=("parallel", "parallel"),
                                             vmem_limit_bytes=VMEM_LIMIT),
        name="nsa_out",
    )(yT, h3, woT, final_norm.reshape(1, d))


def _rope_tables(seq):
    pos = jnp.arange(seq, dtype=F32)
    inv = ROPE_THETA ** (-jnp.arange(0, ROT_DIM, 2, dtype=F32) / ROT_DIM)
    ang = pos[:, None] * inv[None, :]
    cos, sin = jnp.cos(ang), jnp.sin(ang)
    z = lambda n: jnp.zeros((seq, n), F32)
    cosf = jnp.concatenate([cos, cos, jnp.ones((seq, LANES - ROT_DIM), F32)], axis=1)
    sina = jnp.concatenate([-sin, z(LANES - N_FREQ)], axis=1)
    sinb = jnp.concatenate([z(N_FREQ), sin, z(LANES - ROT_DIM)], axis=1)
    return cos.T, sin.T, cosf, sina, sinb


def _blockdiag2(w1):
    hidden = w1.shape[1]
    w = w1.reshape(CMP_BLOCK, HEAD_DIM, hidden)
    zero = jnp.zeros_like(w)
    top = jnp.concatenate([w, zero], axis=2)
    bot = jnp.concatenate([zero, w], axis=2)
    return jnp.concatenate([top, bot], axis=1)


def kernel(x, a_norm, a_w_in, a_conv_w, a_w_out, kv_norm, w_kv, cmp_pos_k, cmp_w1_k, cmp_w2_k,
           cmp_pos_v, cmp_w1_v, cmp_w2_v, b_norm, b_w_in, b_w_out, final_norm):
    bsz, seq, d = x.shape
    attn_d = N_HEADS * HEAD_DIM
    kv_d = N_KV_GROUPS * HEAD_DIM
    assert b_norm.shape[0] == 1, "one NSA layer reads the shared K/V side"
    assert seq % (2 * KEY_CHUNK) == 0

    h = x.reshape(bsz * seq, d)
    for layer in range(a_norm.shape[0]):
        h = _conv_layer(h, a_norm[layer], a_w_in[layer].astype(BF16), a_conv_w[layer],
                        a_w_out[layer].astype(BF16), seq=seq)
    h3 = h.reshape(bsz, seq, d)

    w_in = b_w_in[0]
    n_gate = N_HEADS * N_BRANCH
    wg = w_in[:, attn_d:attn_d + n_gate].reshape(d, N_KV_GROUPS, HEADS_PER_GROUP * N_BRANCH)
    wg = jnp.pad(wg, ((0, 0), (0, 0), (0, GATE_ROWS - HEADS_PER_GROUP * N_BRANCH)))
    waT = jnp.concatenate([w_in[:, :attn_d], w_in[:, attn_d + n_gate:], wg.reshape(d, -1)], axis=1).T.astype(BF16)
    wkv = w_kv.reshape(d, 2 * N_BRANCH, N_KV_GROUPS, HEAD_DIM)
    k_c, v_c, k_s, v_s, k_w, v_w = [wkv[:, i] for i in range(2 * N_BRANCH)]
    flat = lambda w: w.reshape(d, kv_d)
    pad_lanes = lambda w: jnp.pad(w, ((0, 0), (0, 0), (0, LANES - HEAD_DIM))).reshape(d, N_KV_GROUPS * LANES)
    wvT = jnp.concatenate([flat(v_s), flat(v_w)], axis=1).T.astype(BF16)
    wk = jnp.concatenate([flat(k_c), flat(v_c), pad_lanes(k_s), pad_lanes(k_w)], axis=1).astype(BF16)
    cosT, sinT, cosf, sina, sinb = _rope_tables(seq)

    qT, qrT, szT, gT, kvc, ks, kw, vsT, vwT = _nsa_proj(
        h3, b_norm[0], kv_norm, waT, wvT, wk, cosT, sinT, cosf, sina, sinb)

    w1bd = jnp.stack([_blockdiag2(cmp_w1_k), _blockdiag2(cmp_w1_v)]).astype(BF16)
    pos2 = jnp.stack([jnp.tile(cmp_pos_k, (1, 2)), jnp.tile(cmp_pos_v, (1, 2))])
    w2 = jnp.stack([cmp_w2_k, cmp_w2_v])
    w2p = jnp.pad(w2, ((0, 0), (0, 0), (0, LANES - HEAD_DIM))).astype(BF16)
    w2T = jnp.swapaxes(w2, 1, 2).astype(BF16)
    cmp_tm, cmp_fm = _compress(kvc, w1bd, pos2, w2p, w2T)

    yT = _nsa_attn(qT, qrT, gT, szT, cmp_tm[0], cmp_fm[1], ks, kw, vsT, vwT)
    return _nsa_out(yT, h3, b_w_out[0].T.astype(BF16), final_norm)
```

```python
import functools

import jax
import jax.numpy as jnp
from jax import lax
from jax.experimental import pallas as pl
from jax.experimental.pallas import tpu as pltpu

EPS = 1e-6
CONV_WIDTH = 3
N_HEADS = 16
HEAD_DIM = 64
N_KV_GROUPS = 4
HEADS_PER_GROUP = N_HEADS // N_KV_GROUPS
N_BRANCH = 3
ROT_DIM = HEAD_DIM // 4
N_FREQ = ROT_DIM // 2
ROPE_THETA = 500000.0
CMP_BLOCK = 32
CMP_STRIDE = 16
SLC_BLOCK = 64
N_SELECT = 16
N_LOCAL = 2
WINDOW = 512
FORCE_SCORE = 1e4

LANES = 128
SUBLANES = 8
KEY_CHUNK = 256
CHUNKS_PER_TRIP = 2
GATE_ROWS = 16
MASK_BIAS = -1e30
LOG2_E = 1.4426950408889634
VMEM_LIMIT = 56 * 1024 * 1024

BF16 = jnp.bfloat16
F32 = jnp.float32
NT_DIMS = (((1,), (1,)), ((), ()))


def _dot(a, b):
    return jnp.dot(a, b, preferred_element_type=F32)


def _dot_nt(a, b):
    return lax.dot_general(a, b, NT_DIMS, preferred_element_type=F32)


def _sigmoid(x):
    return 1.0 / (1.0 + jnp.exp(-x))


def _inv_rms(x):
    return lax.rsqrt(jnp.mean(x * x, axis=-1, keepdims=True) + EPS)


def _conv_layer_kernel(x_ref, g_ref, win_ref, cw_ref, wout_ref, o_ref, vbuf_ref, *,
                       tm, tiles_per_seq, conv_d, cchunk):
    @pl.when(pl.program_id(0) % tiles_per_seq == 0)
    def _():
        vbuf_ref[0:SUBLANES, :] = jnp.zeros((SUBLANES, conv_d), F32)

    x = x_ref[...]
    hn = (x * _inv_rms(x) * g_ref[...]).astype(BF16)
    acc = jnp.zeros(x.shape, F32)
    for cc in range(conv_d // cchunk):
        cs = cc * cchunk
        b = _dot(hn, win_ref[:, cs:cs + cchunk])
        c = _dot(hn, win_ref[:, conv_d + cs:conv_d + cs + cchunk])
        u = _dot(hn, win_ref[:, 2 * conv_d + cs:2 * conv_d + cs + cchunk])
        z = _dot(hn, win_ref[:, 3 * conv_d + cs:3 * conv_d + cs + cchunk])
        v = c * u
        vbuf_ref[SUBLANES:SUBLANES + tm, cs:cs + cchunk] = v
        v1 = vbuf_ref[SUBLANES - 1:SUBLANES - 1 + tm, cs:cs + cchunk]
        v2 = vbuf_ref[SUBLANES - 2:SUBLANES - 2 + tm, cs:cs + cchunk]
        conv = (cw_ref[0:1, cs:cs + cchunk] * v2 + cw_ref[1:2, cs:cs + cchunk] * v1
                + cw_ref[2:3, cs:cs + cchunk] * v)
        vbuf_ref[0:SUBLANES, cs:cs + cchunk] = v[tm - SUBLANES:tm, :]
        y = b * conv * (z * _sigmoid(z))
        acc = acc + _dot(y.astype(BF16), wout_ref[cs:cs + cchunk, :])
    o_ref[...] = x + acc


def _conv_layer(h, norm_g, w_in, conv_w, w_out, *, seq, tm=512, cchunk=512):
    t, d = h.shape
    conv_d = conv_w.shape[1]
    tm = min(tm, seq)
    const = lambda shape: pl.BlockSpec(shape, lambda i: (0,) * len(shape), pipeline_mode=pl.Buffered(1))
    return pl.pallas_call(
        functools.partial(_conv_layer_kernel, tm=tm, tiles_per_seq=seq // tm, conv_d=conv_d, cchunk=cchunk),
        out_shape=jax.ShapeDtypeStruct((t, d), F32),
        grid=(t // tm,),
        in_specs=[
            pl.BlockSpec((tm, d), lambda i: (i, 0)),
            const((1, d)),
            const((d, 4 * conv_d)),
            const((CONV_WIDTH, conv_d)),
            const((conv_d, d)),
        ],
        out_specs=pl.BlockSpec((tm, d), lambda i: (i, 0)),
        scratch_shapes=[pltpu.VMEM((SUBLANES + tm, conv_d), F32)],
        compiler_params=pltpu.CompilerParams(dimension_semantics=("arbitrary",), vmem_limit_bytes=VMEM_LIMIT),
        name="conv_layer",
    )(h, norm_g.reshape(1, d), w_in, conv_w, w_out)


def _nsa_proj_kernel(h_ref, bn_ref, kn_ref, waT_ref, wvT_ref, wk_ref, cosT_ref, sinT_ref,
                     cosf_ref, sina_ref, sinb_ref,
                     qT_ref, qrT_ref, szT_ref, gT_ref, kvc_ref, ks_ref, kw_ref, vsT_ref, vwT_ref, *, tm, d):
    attn_d = N_HEADS * HEAD_DIM
    kv_d = N_KV_GROUPS * HEAD_DIM
    h = h_ref[0]
    hr = h * _inv_rms(h)
    hq = (hr * bn_ref[...]).astype(BF16)
    hk = (hr * kn_ref[...]).astype(BF16)

    qT = _dot_nt(waT_ref[0:attn_d, :], hq) * (HEAD_DIM ** -0.5 * LOG2_E)
    cosT = cosT_ref[...]
    sinT = sinT_ref[...]
    for hd in range(N_HEADS):
        r0 = hd * HEAD_DIM
        blk = qT[r0:r0 + HEAD_DIM, :]
        x1 = blk[0:N_FREQ, :]
        x2 = blk[N_FREQ:ROT_DIM, :]
        rot = jnp.concatenate([x1 * cosT - x2 * sinT, x2 * cosT + x1 * sinT, blk[ROT_DIM:, :]], axis=0)
        qT_ref[0, r0:r0 + HEAD_DIM, :] = blk.astype(BF16)
        qrT_ref[0, r0:r0 + HEAD_DIM, :] = rot.astype(BF16)

    zT = _dot_nt(waT_ref[attn_d:2 * attn_d, :], hq)
    szT_ref[0] = zT * _sigmoid(zT)
    gT_ref[0] = _sigmoid(_dot_nt(waT_ref[2 * attn_d:2 * attn_d + N_KV_GROUPS * GATE_ROWS, :], hq))

    vT = _dot_nt(wvT_ref[...], hk)
    vsT_ref[0, 0] = vT[0:kv_d, :].astype(BF16)
    vwT_ref[0, 0] = vT[kv_d:2 * kv_d, :].astype(BF16)

    kk = _dot(hk, wk_ref[...])
    kvc_ref[0] = kk[:, 0:2 * kv_d]
    cosf = cosf_ref[...]
    sina = sina_ref[...]
    sinb = sinb_ref[...]
    row = lax.broadcasted_iota(jnp.int32, (tm, LANES), 0)
    lane = lax.broadcasted_iota(jnp.int32, (tm, LANES), 1)
    tok_blk = (pl.program_id(1) * tm + row) // SLC_BLOCK
    onehot = jnp.where((lane >= HEAD_DIM) & (lane - HEAD_DIM == tok_blk), 1.0, 0.0)
    for j in range(2 * N_KV_GROUPS):
        c0 = 2 * kv_d + j * LANES
        xb = kk[:, c0:c0 + LANES]
        rot = (xb * cosf + pltpu.roll(xb, LANES - N_FREQ, axis=1) * sina
               + pltpu.roll(xb, N_FREQ, axis=1) * sinb)
        if j < N_KV_GROUPS:
            ks_ref[0, :, j * LANES:(j + 1) * LANES] = (rot + onehot).astype(BF16)
        else:
            jj = j - N_KV_GROUPS
            kw_ref[0, :, jj * LANES:(jj + 1) * LANES] = rot.astype(BF16)


def _nsa_proj(h3, b_norm, kv_norm, waT, wvT, wk, cosT, sinT, cosf, sina, sinb, *, tm=KEY_CHUNK):
    bsz, seq, d = h3.shape
    attn_d = N_HEADS * HEAD_DIM
    kv_d = N_KV_GROUPS * HEAD_DIM
    kpad = N_KV_GROUPS * LANES
    nt = seq // tm
    const = lambda shape: pl.BlockSpec(shape, lambda b, i: (0,) * len(shape))
    fm = lambda rows: pl.BlockSpec((1, rows, tm), lambda b, i: (b, 0, i))
    tmj = lambda cols: pl.BlockSpec((1, tm, cols), lambda b, i: (b, i, 0))
    out_shape = [
        jax.ShapeDtypeStruct((bsz, attn_d, seq), BF16),
        jax.ShapeDtypeStruct((bsz, attn_d, seq), BF16),
        jax.ShapeDtypeStruct((bsz, attn_d, seq), F32),
        jax.ShapeDtypeStruct((bsz, N_KV_GROUPS * GATE_ROWS, seq), F32),
        jax.ShapeDtypeStruct((bsz, seq, 2 * kv_d), F32),
        jax.ShapeDtypeStruct((bsz, seq, kpad), BF16),
        jax.ShapeDtypeStruct((bsz, seq, kpad), BF16),
        jax.ShapeDtypeStruct((bsz, nt, kv_d, tm), BF16),
        jax.ShapeDtypeStruct((bsz, nt, kv_d, tm), BF16),
    ]
    out_specs = [
        fm(attn_d), fm(attn_d), fm(attn_d), fm(N_KV_GROUPS * GATE_ROWS),
        tmj(2 * kv_d), tmj(kpad), tmj(kpad),
        pl.BlockSpec((1, 1, kv_d, tm), lambda b, i: (b, i, 0, 0)),
        pl.BlockSpec((1, 1, kv_d, tm), lambda b, i: (b, i, 0, 0)),
    ]
    return pl.pallas_call(
        functools.partial(_nsa_proj_kernel, tm=tm, d=d),
        out_shape=out_shape,
        grid=(bsz, nt),
        in_specs=[
            tmj(d), const((1, d)), const((1, d)),
            const(waT.shape), const(wvT.shape), const(wk.shape),
            pl.BlockSpec((N_FREQ, tm), lambda b, i: (0, i)),
            pl.BlockSpec((N_FREQ, tm), lambda b, i: (0, i)),
            pl.BlockSpec((tm, LANES), lambda b, i: (i, 0)),
            pl.BlockSpec((tm, LANES), lambda b, i: (i, 0)),
            pl.BlockSpec((tm, LANES), lambda b, i: (i, 0)),
        ],
        out_specs=out_specs,
        compiler_params=pltpu.CompilerParams(dimension_semantics=("parallel", "parallel"),
                                             vmem_limit_bytes=VMEM_LIMIT),
        name="nsa_proj",
    )(h3, b_norm.reshape(1, d), kv_norm.reshape(1, d), waT, wvT, wk, cosT, sinT, cosf, sina, sinb)


def _gelu_tanh(x):
    return x * (0.5 * (1.0 + jnp.tanh(0.7978845608028654 * (x + 0.044715 * (x * x * x)))))


def _compress_kernel(x_ref, w1_ref, pos_ref, w2_ref, w2T_ref, kc_ref, cT_ref, *, nchunk, hidden):
    u = jnp.zeros((nchunk, 2 * hidden), F32)
    v = jnp.zeros((nchunk, 2 * hidden), F32)
    for l in range(CMP_STRIDE):
        xl = x_ref[0, pl.ds(l, nchunk, stride=CMP_STRIDE), :]
        u = u + _dot((xl + pos_ref[0, l:l + 1, :]).astype(BF16), w1_ref[0, l])
        l2 = CMP_STRIDE + l
        v = v + _dot((xl + pos_ref[0, l2:l2 + 1, :]).astype(BF16), w1_ref[0, l2])
    hid = u + pltpu.roll(v, nchunk - 1, axis=0)
    act = _gelu_tanh(hid).astype(BF16)
    rows_ok = lax.broadcasted_iota(jnp.int32, (nchunk, LANES), 0) < nchunk - 1
    cols_ok = lax.broadcasted_iota(jnp.int32, (HEAD_DIM, nchunk), 1) < nchunk - 1
    for p in range(2):
        a = act[:, p * hidden:(p + 1) * hidden]
        kc_ref[0, 0, p] = jnp.where(rows_ok, _dot(a, w2_ref[0]), 0.0).astype(BF16)
        cT_ref[0, 0, p] = jnp.where(cols_ok, _dot_nt(w2T_ref[0], a), 0.0).astype(BF16)


def _compress(kvc, w1bd, pos2, w2p, w2T):
    bsz, seq, _ = kvc.shape
    nchunk = seq // CMP_STRIDE
    hidden = w2T.shape[2]
    return pl.pallas_call(
        functools.partial(_compress_kernel, nchunk=nchunk, hidden=hidden),
        out_shape=[
            jax.ShapeDtypeStruct((2, bsz, N_KV_GROUPS, nchunk, LANES), BF16),
            jax.ShapeDtypeStruct((2, bsz, N_KV_GROUPS, HEAD_DIM, nchunk), BF16),
        ],
        grid=(2, bsz, N_KV_GROUPS // 2),
        in_specs=[
            pl.BlockSpec((1, seq, LANES), lambda s, b, p: (b, 0, 2 * s + p)),
            pl.BlockSpec((1, CMP_BLOCK, LANES, 2 * hidden), lambda s, b, p: (s, 0, 0, 0)),
            pl.BlockSpec((1, CMP_BLOCK, LANES), lambda s, b, p: (s, 0, 0)),
            pl.BlockSpec((1, hidden, LANES), lambda s, b, p: (s, 0, 0)),
            pl.BlockSpec((1, HEAD_DIM, hidden), lambda s, b, p: (s, 0, 0)),
        ],
        out_specs=[
            pl.BlockSpec((1, 1, 2, nchunk, LANES), lambda s, b, p: (s, b, p, 0, 0)),
            pl.BlockSpec((1, 1, 2, HEAD_DIM, nchunk), lambda s, b, p: (s, b, p, 0, 0)),
        ],
        compiler_params=pltpu.CompilerParams(dimension_semantics=("parallel", "parallel", "parallel"),
                                             vmem_limit_bytes=VMEM_LIMIT),
        name="compress",
    )(kvc, w1bd, pos2, w2p, w2T)


def _attn_kernel(q_ref, qr_ref, g_ref, sz_ref, kc_ref, vcT_ref, ks_ref, kw_ref, vsT_ref, vwT_ref, y_ref,
                 p_sc, sc_sc, oc_sc, qa_sc, s_sc, acc_sc, *, tq, n_cmp):
    qi = pl.program_id(2)
    t0 = qi * tq
    neg_inf = -jnp.inf
    tvec = t0 + lax.broadcasted_iota(jnp.int32, (1, tq), 1)

    kc = kc_ref[0, 0, :, 0:HEAD_DIM]
    cmp_end = lax.broadcasted_iota(jnp.int32, (n_cmp, tq), 0) * CMP_STRIDE + (CMP_BLOCK - 1)
    cmask = cmp_end <= tvec
    heads = range(HEADS_PER_GROUP)

    def fold8(x, op):
        slabs = [x[r:r + SUBLANES, :] for r in range(0, x.shape[0], SUBLANES)]
        while len(slabs) > 1:
            pairs = [op(slabs[i], slabs[i + 1]) for i in range(0, len(slabs) - 1, 2)]
            slabs = pairs + slabs[len(slabs) - len(slabs) % 2:]
        return slabs[0]

    for hh in heads:
        r0 = hh * HEAD_DIM
        s_sc[hh, 0, 0:n_cmp, :] = jnp.where(cmask, _dot(kc, q_ref[0, r0:r0 + HEAD_DIM, :]), neg_inf)
    probs = []
    for hh in heads:
        s = s_sc[hh, 0, 0:n_cmp, :]
        m = jnp.max(fold8(s, jnp.maximum), axis=0, keepdims=True)
        m = jnp.where(m == neg_inf, 0.0, m)
        e = jnp.exp2(s - m)
        den = jnp.sum(fold8(e, jnp.add), axis=0, keepdims=True)
        probs.append(e * (1.0 / jnp.maximum(den, 1e-30)))
    p_grp = functools.reduce(jnp.add, probs)
    for hh in heads:
        r0 = hh * HEAD_DIM
        oc_sc[r0:r0 + HEAD_DIM, :] = _dot(vcT_ref[0, 0], probs[hh].astype(BF16))

    n_sb = n_cmp // 4
    ratio = SLC_BLOCK // CMP_STRIDE
    imp_cols = []
    for c in range(tq // LANES):
        p_sc[c, 0:SUBLANES, :] = jnp.zeros((SUBLANES, LANES), F32)
        p_sc[c, SUBLANES:SUBLANES + n_cmp, :] = p_grp[:, c * LANES:(c + 1) * LANES]
        tap = lambda o: p_sc[c, pl.ds(SUBLANES + o, n_sb, stride=ratio), :]
        imp_cols.append(tap(-1) + 2.0 * (tap(0) + tap(1) + tap(2)) + tap(3))
    imp = jnp.concatenate(imp_cols, axis=1)

    jrow = lax.broadcasted_iota(jnp.int32, (n_sb, tq), 0)
    cur = tvec // SLC_BLOCK
    valid = jrow <= cur
    forced = (jrow == 0) | (valid & (jrow > cur - N_LOCAL))
    score = jnp.where(valid, jnp.where(forced, FORCE_SCORE, imp), neg_inf)
    sc_sc[...] = score

    def rank_body(jp, cnt):
        sb = sc_sc[pl.ds(jp, 1), :]
        before = (sb > score) | ((sb == score) & (jrow > jp))
        return cnt + jnp.where(before, 1, 0)

    n_valid = (t0 + tq) // SLC_BLOCK
    n_rank = jnp.where(n_valid > N_SELECT, n_valid, 0)
    rank = lax.fori_loop(0, n_rank, rank_body, jnp.zeros((n_sb, tq), jnp.int32))
    sel_bias = jnp.where((rank < N_SELECT) & valid, 0.0, MASK_BIAS).astype(BF16)
    if n_sb < HEAD_DIM:
        sel_bias = jnp.concatenate([sel_bias, jnp.zeros((HEAD_DIM - n_sb, tq), BF16)], axis=0)

    krow = lax.broadcasted_iota(jnp.int32, (KEY_CHUNK, tq), 0)

    def key_chunk(ref, c):
        return ref[0, pl.ds(pl.multiple_of(c * KEY_CHUNK, KEY_CHUNK), KEY_CHUNK), :]

    for hh in heads:
        r0 = hh * HEAD_DIM
        qa_sc[hh] = jnp.concatenate([qr_ref[0, r0:r0 + HEAD_DIM, :], sel_bias], axis=0)
    causal = (t0 + krow) <= tvec

    def score_body(i, mparts):
        out = list(mparts)
        for u in range(CHUNKS_PER_TRIP):
            c = CHUNKS_PER_TRIP * i + u
            kch = key_chunk(ks_ref, c)
            visible = krow <= tvec - c * KEY_CHUNK
            for hh in heads:
                s = jnp.where(visible, _dot(kch, qa_sc[hh]), neg_inf)
                s_sc[hh, c] = s
                out[hh] = jnp.maximum(out[hh], fold8(s, jnp.maximum))
        return tuple(out)

    n_trips = qi // CHUNKS_PER_TRIP + 1
    mparts = lax.fori_loop(0, n_trips, score_body,
                           tuple(jnp.full((SUBLANES, tq), neg_inf, F32) for _ in heads))
    m_sel = [jnp.max(mp, axis=0, keepdims=True) for mp in mparts]

    for hh in heads:
        acc_sc[hh] = jnp.zeros((HEAD_DIM, tq), F32)

    def pv_body(i, lparts):
        c = CHUNKS_PER_TRIP * i
        vT = jnp.concatenate([vsT_ref[0, c + u] for u in range(CHUNKS_PER_TRIP)], axis=1)
        out = []
        for hh in heads:
            s = s_sc[hh, pl.ds(c, CHUNKS_PER_TRIP)].reshape(CHUNKS_PER_TRIP * KEY_CHUNK, tq)
            p = jnp.exp2(s - m_sel[hh])
            acc_sc[hh] += _dot(vT, p.astype(BF16))
            out.append(lparts[hh] + fold8(p, jnp.add))
        return tuple(out)

    lparts = lax.fori_loop(0, n_trips, pv_body, tuple(jnp.zeros((SUBLANES, tq), F32) for _ in heads))

    c1 = jnp.maximum(qi - 1, 0)
    c2 = jnp.maximum(qi - 2, 0)
    kw0, kw1, kw2 = key_chunk(kw_ref, qi), key_chunk(kw_ref, c1), key_chunk(kw_ref, c2)
    kpos1 = t0 - KEY_CHUNK + krow
    kpos2 = kpos1 - KEY_CHUNK
    exists1 = kpos1 >= 0
    in_window = (kpos2 > tvec - WINDOW) & (kpos2 >= 0)
    for hh in heads:
        qa = qa_sc[hh]
        s_sc[hh, 0] = jnp.where(causal, _dot(kw0, qa), neg_inf)
        s_sc[hh, 1] = jnp.where(exists1, _dot(kw1, qa), neg_inf)
        s_sc[hh, 2] = jnp.where(in_window, _dot(kw2, qa), neg_inf)
    vw = jnp.concatenate([vwT_ref[0, qi], vwT_ref[0, c1], vwT_ref[0, c2]], axis=1)
    for hh in heads:
        r0 = hh * HEAD_DIM
        s = s_sc[hh, 0:3].reshape(3 * KEY_CHUNK, tq)
        m = jnp.max(fold8(s, jnp.maximum), axis=0, keepdims=True)
        p = jnp.exp2(s - m)
        l = jnp.sum(fold8(p, jnp.add), axis=0, keepdims=True)
        o_w = _dot(vw, p.astype(BF16)) * (1.0 / l)
        o_s = acc_sc[hh] * (1.0 / jnp.sum(lparts[hh], axis=0, keepdims=True))

        g0 = g_ref[0, N_BRANCH * hh:N_BRANCH * hh + 1, :]
        g1 = g_ref[0, N_BRANCH * hh + 1:N_BRANCH * hh + 2, :]
        g2 = g_ref[0, N_BRANCH * hh + 2:N_BRANCH * hh + 3, :]
        o = g0 * oc_sc[r0:r0 + HEAD_DIM, :] + g1 * o_s + g2 * o_w
        y_ref[0, r0:r0 + HEAD_DIM, :] = (o * sz_ref[0, r0:r0 + HEAD_DIM, :]).astype(BF16)


def _nsa_attn(qT, qrT, gT, szT, kcmp, vcmpT, ks, kw, vsT, vwT, *, tq=KEY_CHUNK):
    bsz, attn_d, seq = qT.shape
    n_cmp = kcmp.shape[2]
    nchunks = seq // KEY_CHUNK
    gd = HEADS_PER_GROUP * HEAD_DIM
    assert tq == KEY_CHUNK and nchunks % CHUNKS_PER_TRIP == 0 and nchunks >= 3 and n_cmp <= KEY_CHUNK
    qspec = pl.BlockSpec((1, gd, tq), lambda b, g, i: (b, g, i))
    kspec = pl.BlockSpec((1, seq, LANES), lambda b, g, i: (b, 0, g))
    vspec = pl.BlockSpec((1, nchunks, HEAD_DIM, KEY_CHUNK), lambda b, g, i: (b, 0, g, 0))
    return pl.pallas_call(
        functools.partial(_attn_kernel, tq=tq, n_cmp=n_cmp),
        out_shape=jax.ShapeDtypeStruct((bsz, attn_d, seq), BF16),
        grid=(bsz, N_KV_GROUPS, seq // tq),
        in_specs=[
            qspec, qspec,
            pl.BlockSpec((1, GATE_ROWS, tq), lambda b, g, i: (b, g, i)),
            qspec,
            pl.BlockSpec((1, 1, n_cmp, LANES), lambda b, g, i: (b, g, 0, 0)),
            pl.BlockSpec((1, 1, HEAD_DIM, n_cmp), lambda b, g, i: (b, g, 0, 0)),
            kspec, kspec, vspec, vspec,
        ],
        out_specs=qspec,
        scratch_shapes=[
            pltpu.VMEM((tq // LANES, SUBLANES + n_cmp, LANES), F32),
            pltpu.VMEM((n_cmp // 4, tq), F32),
            pltpu.VMEM((gd, tq), F32),
            pltpu.VMEM((HEADS_PER_GROUP, 2 * HEAD_DIM, tq), BF16),
            pltpu.VMEM((HEADS_PER_GROUP, nchunks, KEY_CHUNK, tq), F32),
            pltpu.VMEM((HEADS_PER_GROUP, HEAD_DIM, tq), F32),
        ],
        compiler_params=pltpu.CompilerParams(dimension_semantics=("parallel", "parallel", "arbitrary"),
                                             vmem_limit_bytes=VMEM_LIMIT),
        name="nsa_attn",
    )(qT, qrT, gT, szT, kcmp, vcmpT, ks, kw, vsT, vwT)


def _nsa_out_kernel(y_ref, h_ref, woT_ref, fn_ref, o_ref):
    oT = _dot(woT_ref[...], y_ref[0])
    h2 = h_ref[0] + oT.T
    o_ref[0] = h2 * _inv_rms(h2) * fn_ref[...]


def _nsa_out(yT, h3, woT, final_norm, *, tm=KEY_CHUNK):
    bsz, seq, d = h3.shape
    attn_d = yT.shape[1]
    return pl.pallas_call(
        _nsa_out_kernel,
        out_shape=jax.ShapeDtypeStruct((bsz, seq, d), F32),
        grid=(bsz, seq // tm),
        in_specs=[
            pl.BlockSpec((1, attn_d, tm), lambda b, i: (b, 0, i)),
            pl.BlockSpec((1, tm, d), lambda b, i: (b, i, 0)),
            pl.BlockSpec((d, attn_d), lambda b, i: (0, 0)),
            pl.BlockSpec((1, d), lambda b, i: (0, 0)),
        ],
        out_specs=pl.BlockSpec((1, tm, d), lambda b, i: (b, i, 0)),
        compiler_params=pltpu.CompilerParams(dimension_semantics=("parallel", "parallel"),
                                             vmem_limit_bytes=VMEM_LIMIT),
        name="nsa_out",
    )(yT, h3, woT, final_norm.reshape(1, d))


def _rope_tables(seq):
    pos = jnp.arange(seq, dtype=F32)
    inv = ROPE_THETA ** (-jnp.arange(0, ROT_DIM, 2, dtype=F32) / ROT_DIM)
    ang = pos[:, None] * inv[None, :]
    cos, sin = jnp.cos(ang), jnp.sin(ang)
    z = lambda n: jnp.zeros((seq, n), F32)
    cosf = jnp.concatenate([cos, cos, jnp.ones((seq, LANES - ROT_DIM), F32)], axis=1)
    sina = jnp.concatenate([-sin, z(LANES - N_FREQ)], axis=1)
    sinb = jnp.concatenate([z(N_FREQ), sin, z(LANES - ROT_DIM)], axis=1)
    return cos.T, sin.T, cosf, sina, sinb


def _blockdiag2(w1):
    hidden = w1.shape[1]
    w = w1.reshape(CMP_BLOCK, HEAD_DIM, hidden)
    zero = jnp.zeros_like(w)
    top = jnp.concatenate([w, zero], axis=2)
    bot = jnp.concatenate([zero, w], axis=2)
    return jnp.concatenate([top, bot], axis=1)


def kernel(x, a_norm, a_w_in, a_conv_w, a_w_out, kv_norm, w_kv, cmp_pos_k, cmp_w1_k, cmp_w2_k,
           cmp_pos_v, cmp_w1_v, cmp_w2_v, b_norm, b_w_in, b_w_out, final_norm):
    bsz, seq, d = x.shape
    attn_d = N_HEADS * HEAD_DIM
    kv_d = N_KV_GROUPS * HEAD_DIM
    assert b_norm.shape[0] == 1, "one NSA layer reads the shared K/V side"
    assert seq % (2 * KEY_CHUNK) == 0

    h = x.reshape(bsz * seq, d)
    for layer in range(a_norm.shape[0]):
        h = _conv_layer(h, a_norm[layer], a_w_in[layer].astype(BF16), a_conv_w[layer],
                        a_w_out[layer].astype(BF16), seq=seq)
    h3 = h.reshape(bsz, seq, d)

    w_in = b_w_in[0]
    n_gate = N_HEADS * N_BRANCH
    wg = w_in[:, attn_d:attn_d + n_gate].reshape(d, N_KV_GROUPS, HEADS_PER_GROUP * N_BRANCH)
    wg = jnp.pad(wg, ((0, 0), (0, 0), (0, GATE_ROWS - HEADS_PER_GROUP * N_BRANCH)))
    waT = jnp.concatenate([w_in[:, :attn_d], w_in[:, attn_d + n_gate:], wg.reshape(d, -1)], axis=1).T.astype(BF16)
    wkv = w_kv.reshape(d, 2 * N_BRANCH, N_KV_GROUPS, HEAD_DIM)
    k_c, v_c, k_s, v_s, k_w, v_w = [wkv[:, i] for i in range(2 * N_BRANCH)]
    flat = lambda w: w.reshape(d, kv_d)
    pad_lanes = lambda w: jnp.pad(w, ((0, 0), (0, 0), (0, LANES - HEAD_DIM))).reshape(d, N_KV_GROUPS * LANES)
    wvT = jnp.concatenate([flat(v_s), flat(v_w)], axis=1).T.astype(BF16)
    wk = jnp.concatenate([flat(k_c), flat(v_c), pad_lanes(k_s), pad_lanes(k_w)], axis=1).astype(BF16)
    cosT, sinT, cosf, sina, sinb = _rope_tables(seq)

    qT, qrT, szT, gT, kvc, ks, kw, vsT, vwT = _nsa_proj(
        h3, b_norm[0], kv_norm, waT, wvT, wk, cosT, sinT, cosf, sina, sinb)

    w1bd = jnp.stack([_blockdiag2(cmp_w1_k), _blockdiag2(cmp_w1_v)]).astype(BF16)
    pos2 = jnp.stack([jnp.tile(cmp_pos_k, (1, 2)), jnp.tile(cmp_pos_v, (1, 2))])
    w2 = jnp.stack([cmp_w2_k, cmp_w2_v])
    w2p = jnp.pad(w2, ((0, 0), (0, 0), (0, LANES - HEAD_DIM))).astype(BF16)
    w2T = jnp.swapaxes(w2, 1, 2).astype(BF16)
    cmp_tm, cmp_fm = _compress(kvc, w1bd, pos2, w2p, w2T)

    yT = _nsa_attn(qT, qrT, gT, szT, cmp_tm[0], cmp_fm[1], ks, kw, vsT, vwT)
    return _nsa_out(yT, h3, b_w_out[0].T.astype(BF16), final_norm)
```

```python
import functools

import jax
import jax.numpy as jnp
from jax import lax
from jax.experimental import pallas as pl
from jax.experimental.pallas import tpu as pltpu

EPS = 1e-6
CONV_WIDTH = 3
N_HEADS = 16
HEAD_DIM = 64
N_KV_GROUPS = 4
HEADS_PER_GROUP = N_HEADS // N_KV_GROUPS
N_BRANCH = 3
ROT_DIM = HEAD_DIM // 4
N_FREQ = ROT_DIM // 2
ROPE_THETA = 500000.0
CMP_BLOCK = 32
CMP_STRIDE = 16
SLC_BLOCK = 64
N_SELECT = 16
N_LOCAL = 2
WINDOW = 512
FORCE_SCORE = 1e4

LANES = 128
SUBLANES = 8
KEY_CHUNK = 256
CHUNKS_PER_TRIP = 2
SOFTMAX_ROWS = 64
GATE_ROWS = 16
MASK_BIAS = -1e30
LOG2_E = 1.4426950408889634
VMEM_LIMIT = 56 * 1024 * 1024

BF16 = jnp.bfloat16
F32 = jnp.float32
NT_DIMS = (((1,), (1,)), ((), ()))


def _dot(a, b):
    return jnp.dot(a, b, preferred_element_type=F32)


def _dot_nt(a, b):
    return lax.dot_general(a, b, NT_DIMS, preferred_element_type=F32)


def _sigmoid(x):
    return 1.0 / (1.0 + jnp.exp(-x))


def _inv_rms(x):
    return lax.rsqrt(jnp.mean(x * x, axis=-1, keepdims=True) + EPS)


def _conv_layer_kernel(x_ref, g_ref, win_ref, cw_ref, wout_ref, o_ref, vbuf_ref, *,
                       tm, tiles_per_seq, conv_d, cchunk):
    @pl.when(pl.program_id(0) % tiles_per_seq == 0)
    def _():
        vbuf_ref[0:SUBLANES, :] = jnp.zeros((SUBLANES, conv_d), F32)

    x = x_ref[...]
    hn = (x * _inv_rms(x) * g_ref[...]).astype(BF16)
    acc = jnp.zeros(x.shape, F32)
    for cc in range(conv_d // cchunk):
        cs = cc * cchunk
        b = _dot(hn, win_ref[:, cs:cs + cchunk])
        c = _dot(hn, win_ref[:, conv_d + cs:conv_d + cs + cchunk])
        u = _dot(hn, win_ref[:, 2 * conv_d + cs:2 * conv_d + cs + cchunk])
        z = _dot(hn, win_ref[:, 3 * conv_d + cs:3 * conv_d + cs + cchunk])
        v = c * u
        vbuf_ref[SUBLANES:SUBLANES + tm, cs:cs + cchunk] = v
        v1 = vbuf_ref[SUBLANES - 1:SUBLANES - 1 + tm, cs:cs + cchunk]
        v2 = vbuf_ref[SUBLANES - 2:SUBLANES - 2 + tm, cs:cs + cchunk]
        conv = (cw_ref[0:1, cs:cs + cchunk] * v2 + cw_ref[1:2, cs:cs + cchunk] * v1
                + cw_ref[2:3, cs:cs + cchunk] * v)
        vbuf_ref[0:SUBLANES, cs:cs + cchunk] = v[tm - SUBLANES:tm, :]
        y = b * conv * (z * _sigmoid(z))
        acc = acc + _dot(y.astype(BF16), wout_ref[cs:cs + cchunk, :])
    o_ref[...] = x + acc


def _conv_layer(h, norm_g, w_in, conv_w, w_out, *, seq, tm=512, cchunk=512):
    t, d = h.shape
    conv_d = conv_w.shape[1]
    tm = min(tm, seq)
    const = lambda shape: pl.BlockSpec(shape, lambda i: (0,) * len(shape), pipeline_mode=pl.Buffered(1))
    return pl.pallas_call(
        functools.partial(_conv_layer_kernel, tm=tm, tiles_per_seq=seq // tm, conv_d=conv_d, cchunk=cchunk),
        out_shape=jax.ShapeDtypeStruct((t, d), F32),
        grid=(t // tm,),
        in_specs=[
            pl.BlockSpec((tm, d), lambda i: (i, 0)),
            const((1, d)),
            const((d, 4 * conv_d)),
            const((CONV_WIDTH, conv_d)),
            const((conv_d, d)),
        ],
        out_specs=pl.BlockSpec((tm, d), lambda i: (i, 0)),
        scratch_shapes=[pltpu.VMEM((SUBLANES + tm, conv_d), F32)],
        compiler_params=pltpu.CompilerParams(dimension_semantics=("arbitrary",), vmem_limit_bytes=VMEM_LIMIT),
        name="conv_layer",
    )(h, norm_g.reshape(1, d), w_in, conv_w, w_out)


def _nsa_proj_kernel(h_ref, bn_ref, kn_ref, waT_ref, wvT_ref, wk_ref, cosT_ref, sinT_ref,
                     cosf_ref, sina_ref, sinb_ref,
                     qT_ref, qrT_ref, szT_ref, gT_ref, kvc_ref, ks_ref, kw_ref, vsT_ref, vwT_ref, *, tm, d):
    attn_d = N_HEADS * HEAD_DIM
    kv_d = N_KV_GROUPS * HEAD_DIM
    h = h_ref[0]
    hr = h * _inv_rms(h)
    hq = (hr * bn_ref[...]).astype(BF16)
    hk = (hr * kn_ref[...]).astype(BF16)

    qT = _dot_nt(waT_ref[0:attn_d, :], hq) * (HEAD_DIM ** -0.5 * LOG2_E)
    cosT = cosT_ref[...]
    sinT = sinT_ref[...]
    for hd in range(N_HEADS):
        r0 = hd * HEAD_DIM
        blk = qT[r0:r0 + HEAD_DIM, :]
        x1 = blk[0:N_FREQ, :]
        x2 = blk[N_FREQ:ROT_DIM, :]
        rot = jnp.concatenate([x1 * cosT - x2 * sinT, x2 * cosT + x1 * sinT, blk[ROT_DIM:, :]], axis=0)
        qT_ref[0, r0:r0 + HEAD_DIM, :] = blk.astype(BF16)
        qrT_ref[0, r0:r0 + HEAD_DIM, :] = rot.astype(BF16)

    zT = _dot_nt(waT_ref[attn_d:2 * attn_d, :], hq)
    szT_ref[0] = zT * _sigmoid(zT)
    gT_ref[0] = _sigmoid(_dot_nt(waT_ref[2 * attn_d:2 * attn_d + N_KV_GROUPS * GATE_ROWS, :], hq))

    vT = _dot_nt(wvT_ref[...], hk)
    vsT_ref[0, 0] = vT[0:kv_d, :].astype(BF16)
    vwT_ref[0, 0] = vT[kv_d:2 * kv_d, :].astype(BF16)

    kk = _dot(hk, wk_ref[...])
    kvc_ref[0] = kk[:, 0:2 * kv_d]
    cosf = cosf_ref[...]
    sina = sina_ref[...]
    sinb = sinb_ref[...]
    row = lax.broadcasted_iota(jnp.int32, (tm, LANES), 0)
    lane = lax.broadcasted_iota(jnp.int32, (tm, LANES), 1)
    tok_blk = (pl.program_id(1) * tm + row) // SLC_BLOCK
    onehot = jnp.where((lane >= HEAD_DIM) & (lane - HEAD_DIM == tok_blk), 1.0, 0.0)
    for j in range(2 * N_KV_GROUPS):
        c0 = 2 * kv_d + j * LANES
        xb = kk[:, c0:c0 + LANES]
        rot = (xb * cosf + pltpu.roll(xb, LANES - N_FREQ, axis=1) * sina
               + pltpu.roll(xb, N_FREQ, axis=1) * sinb)
        if j < N_KV_GROUPS:
            ks_ref[0, :, j * LANES:(j + 1) * LANES] = (rot + onehot).astype(BF16)
        else:
            jj = j - N_KV_GROUPS
            kw_ref[0, :, jj * LANES:(jj + 1) * LANES] = rot.astype(BF16)


def _nsa_proj(h3, b_norm, kv_norm, waT, wvT, wk, cosT, sinT, cosf, sina, sinb, *, tm=KEY_CHUNK):
    bsz, seq, d = h3.shape
    attn_d = N_HEADS * HEAD_DIM
    kv_d = N_KV_GROUPS * HEAD_DIM
    kpad = N_KV_GROUPS * LANES
    nt = seq // tm
    const = lambda shape: pl.BlockSpec(shape, lambda b, i: (0,) * len(shape))
    fm = lambda rows: pl.BlockSpec((1, rows, tm), lambda b, i: (b, 0, i))
    tmj = lambda cols: pl.BlockSpec((1, tm, cols), lambda b, i: (b, i, 0))
    out_shape = [
        jax.ShapeDtypeStruct((bsz, attn_d, seq), BF16),
        jax.ShapeDtypeStruct((bsz, attn_d, seq), BF16),
        jax.ShapeDtypeStruct((bsz, attn_d, seq), F32),
        jax.ShapeDtypeStruct((bsz, N_KV_GROUPS * GATE_ROWS, seq), F32),
        jax.ShapeDtypeStruct((bsz, seq, 2 * kv_d), F32),
        jax.ShapeDtypeStruct((bsz, seq, kpad), BF16),
        jax.ShapeDtypeStruct((bsz, seq, kpad), BF16),
        jax.ShapeDtypeStruct((bsz, nt, kv_d, tm), BF16),
        jax.ShapeDtypeStruct((bsz, nt, kv_d, tm), BF16),
    ]
    out_specs = [
        fm(attn_d), fm(attn_d), fm(attn_d), fm(N_KV_GROUPS * GATE_ROWS),
        tmj(2 * kv_d), tmj(kpad), tmj(kpad),
        pl.BlockSpec((1, 1, kv_d, tm), lambda b, i: (b, i, 0, 0)),
        pl.BlockSpec((1, 1, kv_d, tm), lambda b, i: (b, i, 0, 0)),
    ]
    return pl.pallas_call(
        functools.partial(_nsa_proj_kernel, tm=tm, d=d),
        out_shape=out_shape,
        grid=(bsz, nt),
        in_specs=[
            tmj(d), const((1, d)), const((1, d)),
            const(waT.shape), const(wvT.shape), const(wk.shape),
            pl.BlockSpec((N_FREQ, tm), lambda b, i: (0, i)),
            pl.BlockSpec((N_FREQ, tm), lambda b, i: (0, i)),
            pl.BlockSpec((tm, LANES), lambda b, i: (i, 0)),
            pl.BlockSpec((tm, LANES), lambda b, i: (i, 0)),
            pl.BlockSpec((tm, LANES), lambda b, i: (i, 0)),
        ],
        out_specs=out_specs,
        compiler_params=pltpu.CompilerParams(dimension_semantics=("parallel", "parallel"),
                                             vmem_limit_bytes=VMEM_LIMIT),
        name="nsa_proj",
    )(h3, b_norm.reshape(1, d), kv_norm.reshape(1, d), waT, wvT, wk, cosT, sinT, cosf, sina, sinb)


def _gelu_tanh(x):
    return x * (0.5 * (1.0 + jnp.tanh(0.7978845608028654 * (x + 0.044715 * (x * x * x)))))


def _compress_kernel(x_ref, w1_ref, pos_ref, w2_ref, w2T_ref, kc_ref, cT_ref, *, nchunk, hidden):
    u = jnp.zeros((nchunk, 2 * hidden), F32)
    v = jnp.zeros((nchunk, 2 * hidden), F32)
    for l in range(CMP_STRIDE):
        xl = x_ref[0, pl.ds(l, nchunk, stride=CMP_STRIDE), :]
        u = u + _dot((xl + pos_ref[0, l:l + 1, :]).astype(BF16), w1_ref[0, l])
        l2 = CMP_STRIDE + l
        v = v + _dot((xl + pos_ref[0, l2:l2 + 1, :]).astype(BF16), w1_ref[0, l2])
    hid = u + pltpu.roll(v, nchunk - 1, axis=0)
    act = _gelu_tanh(hid).astype(BF16)
    rows_ok = lax.broadcasted_iota(jnp.int32, (nchunk, LANES), 0) < nchunk - 1
    cols_ok = lax.broadcasted_iota(jnp.int32, (HEAD_DIM, nchunk), 1) < nchunk - 1
    for p in range(2):
        a = act[:, p * hidden:(p + 1) * hidden]
        kc_ref[0, 0, p] = jnp.where(rows_ok, _dot(a, w2_ref[0]), 0.0).astype(BF16)
        cT_ref[0, 0, p] = jnp.where(cols_ok, _dot_nt(w2T_ref[0], a), 0.0).astype(BF16)


def _compress(kvc, w1bd, pos2, w2p, w2T):
    bsz, seq, _ = kvc.shape
    nchunk = seq // CMP_STRIDE
    hidden = w2T.shape[2]
    return pl.pallas_call(
        functools.partial(_compress_kernel, nchunk=nchunk, hidden=hidden),
        out_shape=[
            jax.ShapeDtypeStruct((2, bsz, N_KV_GROUPS, nchunk, LANES), BF16),
            jax.ShapeDtypeStruct((2, bsz, N_KV_GROUPS, HEAD_DIM, nchunk), BF16),
        ],
        grid=(2, bsz, N_KV_GROUPS // 2),
        in_specs=[
            pl.BlockSpec((1, seq, LANES), lambda s, b, p: (b, 0, 2 * s + p)),
            pl.BlockSpec((1, CMP_BLOCK, LANES, 2 * hidden), lambda s, b, p: (s, 0, 0, 0)),
            pl.BlockSpec((1, CMP_BLOCK, LANES), lambda s, b, p: (s, 0, 0)),
            pl.BlockSpec((1, hidden, LANES), lambda s, b, p: (s, 0, 0)),
            pl.BlockSpec((1, HEAD_DIM, hidden), lambda s, b, p: (s, 0, 0)),
        ],
        out_specs=[
            pl.BlockSpec((1, 1, 2, nchunk, LANES), lambda s, b, p: (s, b, p, 0, 0)),
            pl.BlockSpec((1, 1, 2, HEAD_DIM, nchunk), lambda s, b, p: (s, b, p, 0, 0)),
        ],
        compiler_params=pltpu.CompilerParams(dimension_semantics=("parallel", "parallel", "parallel"),
                                             vmem_limit_bytes=VMEM_LIMIT),
        name="compress",
    )(kvc, w1bd, pos2, w2p, w2T)


def _attn_kernel(q_ref, qr_ref, g_ref, sz_ref, kc_ref, vcT_ref, ks_ref, kw_ref, vsT_ref, vwT_ref, y_ref,
                 p_sc, sc_sc, oc_sc, qa_sc, w_sc, s_sc, pb_sc, smax_sc, st_sc, acc_sc, *, tq, n_cmp):
    qi = pl.program_id(2)
    t0 = qi * tq
    neg_inf = -jnp.inf
    tvec = t0 + lax.broadcasted_iota(jnp.int32, (1, tq), 1)

    kc = kc_ref[0, 0, :, 0:HEAD_DIM]
    cmp_end = lax.broadcasted_iota(jnp.int32, (n_cmp, tq), 0) * CMP_STRIDE + (CMP_BLOCK - 1)
    cmask = cmp_end <= tvec
    heads = range(HEADS_PER_GROUP)

    def fold8(x, op, ways=4):
        parts = [None] * ways
        for idx, r in enumerate(range(0, x.shape[0], SUBLANES)):
            slab = x[r:r + SUBLANES, :]
            parts[idx % ways] = slab if parts[idx % ways] is None else op(parts[idx % ways], slab)
        return functools.reduce(op, [p for p in parts if p is not None])

    for hh in heads:
        r0 = hh * HEAD_DIM
        w_sc[hh, 0:n_cmp, :] = jnp.where(cmask, _dot(kc, q_ref[0, r0:r0 + HEAD_DIM, :]), neg_inf)
    probs = []
    for hh in heads:
        s = w_sc[hh, 0:n_cmp, :]
        m = jnp.max(fold8(s, jnp.maximum), axis=0, keepdims=True)
        m = jnp.where(m == neg_inf, 0.0, m)
        e = jnp.exp2(s - m)
        den = jnp.sum(fold8(e, jnp.add), axis=0, keepdims=True)
        probs.append(e * (1.0 / jnp.maximum(den, 1e-30)))
    p_grp = functools.reduce(jnp.add, probs)
    for hh in heads:
        r0 = hh * HEAD_DIM
        oc_sc[r0:r0 + HEAD_DIM, :] = _dot(vcT_ref[0, 0], probs[hh].astype(BF16))

    n_sb = n_cmp // 4
    ratio = SLC_BLOCK // CMP_STRIDE
    imp_cols = []
    for c in range(tq // LANES):
        p_sc[c, 0:SUBLANES, :] = jnp.zeros((SUBLANES, LANES), F32)
        p_sc[c, SUBLANES:SUBLANES + n_cmp, :] = p_grp[:, c * LANES:(c + 1) * LANES]
        tap = lambda o: p_sc[c, pl.ds(SUBLANES + o, n_sb, stride=ratio), :]
        imp_cols.append(tap(-1) + 2.0 * (tap(0) + tap(1) + tap(2)) + tap(3))
    imp = jnp.concatenate(imp_cols, axis=1)

    jrow = lax.broadcasted_iota(jnp.int32, (n_sb, tq), 0)
    cur = tvec // SLC_BLOCK
    valid = jrow <= cur
    forced = (jrow == 0) | (valid & (jrow > cur - N_LOCAL))
    score = jnp.where(valid, jnp.where(forced, FORCE_SCORE, imp), neg_inf)
    sc_sc[...] = score

    def rank_body(jp, cnt):
        sb = sc_sc[pl.ds(jp, 1), :]
        before = (sb > score) | ((sb == score) & (jrow > jp))
        return cnt + jnp.where(before, 1, 0)

    n_valid = (t0 + tq) // SLC_BLOCK
    n_rank = jnp.where(n_valid > N_SELECT, n_valid, 0)
    rank = lax.fori_loop(0, n_rank, rank_body, jnp.zeros((n_sb, tq), jnp.int32))
    sel_bias = jnp.where((rank < N_SELECT) & valid, 0.0, MASK_BIAS).astype(BF16)
    if n_sb < HEAD_DIM:
        sel_bias = jnp.concatenate([sel_bias, jnp.zeros((HEAD_DIM - n_sb, tq), BF16)], axis=0)

    krow = lax.broadcasted_iota(jnp.int32, (KEY_CHUNK, tq), 0)

    def key_chunk(ref, c):
        return ref[0, pl.ds(pl.multiple_of(c * KEY_CHUNK, KEY_CHUNK), KEY_CHUNK), :]

    dead_bias = jnp.full((HEAD_DIM, tq), MASK_BIAS, BF16)
    for hh in heads:
        r0 = hh * HEAD_DIM
        qa_sc[0, hh] = jnp.concatenate([qr_ref[0, r0:r0 + HEAD_DIM, :], sel_bias], axis=0)
        qa_sc[1, hh] = jnp.concatenate([qr_ref[0, r0:r0 + HEAD_DIM, :], dead_bias], axis=0)
    causal = (t0 + krow) <= tvec

    group = CHUNKS_PER_TRIP * KEY_CHUNK
    n_full = qi // CHUNKS_PER_TRIP
    n_double = (n_full + 1) // 2
    last_pos = 2 * n_double

    def group_at(pos):
        is_last = pos == last_pos
        dead = jnp.logical_and(pos >= n_full, jnp.logical_not(is_last))
        grp = jnp.where(is_last, n_full, jnp.minimum(pos, jnp.maximum(n_full - 1, 0)))
        return grp, dead.astype(jnp.int32)

    ST_MAX, ST_SUM, ST_RESCALE = 0, 1, 2
    rows8 = lambda x: jnp.broadcast_to(x, (SUBLANES, tq))

    def scores_to(slot, pos, hh):
        grp, dead = group_at(pos)
        smax = None
        for u in range(CHUNKS_PER_TRIP):
            s = _dot(key_chunk(ks_ref, CHUNKS_PER_TRIP * grp + u), qa_sc[dead, hh])
            s_sc[slot, hh, u * KEY_CHUNK:(u + 1) * KEY_CHUNK, :] = s
            f = fold8(s, jnp.maximum)
            smax = f if smax is None else jnp.maximum(smax, f)
        smax_sc[slot, hh] = smax

    def accumulate(pos, slot, hh):
        grp, _ = group_at(pos)
        vT = jnp.concatenate([vsT_ref[0, CHUNKS_PER_TRIP * grp + u] for u in range(CHUNKS_PER_TRIP)], axis=1)
        acc_sc[hh] = acc_sc[hh] * st_sc[ST_RESCALE + slot, hh, 0:1, :] + _dot(vT, pb_sc[slot, hh])

    def softmax_group(slot, hh, kpos_limit):
        m_old = st_sc[ST_MAX, hh, 0:1, :]
        if kpos_limit is None:
            cmax = smax_sc[slot, hh]
        else:
            cmax = None
            for r in range(0, group, SOFTMAX_ROWS):
                rows = r + lax.broadcasted_iota(jnp.int32, (SOFTMAX_ROWS, tq), 0)
                f = fold8(jnp.where(rows <= kpos_limit, s_sc[slot, hh, r:r + SOFTMAX_ROWS, :], neg_inf),
                          jnp.maximum)
                cmax = f if cmax is None else jnp.maximum(cmax, f)
        m_new = jnp.maximum(m_old, jnp.max(cmax, axis=0, keepdims=True))
        a = jnp.exp2(m_old - m_new)
        psum = None
        for r in range(0, group, SOFTMAX_ROWS):
            s = s_sc[slot, hh, r:r + SOFTMAX_ROWS, :]
            if kpos_limit is not None:
                rows = r + lax.broadcasted_iota(jnp.int32, (SOFTMAX_ROWS, tq), 0)
                s = jnp.where(rows <= kpos_limit, s, neg_inf)
            p = jnp.exp2(s - m_new)
            pb_sc[slot, hh, r:r + SOFTMAX_ROWS, :] = p.astype(BF16)
            f = fold8(p, jnp.add, ways=2)
            psum = f if psum is None else psum + f
        st_sc[ST_SUM, hh] = a * st_sc[ST_SUM, hh] + psum
        st_sc[ST_MAX, hh] = rows8(m_new)
        st_sc[ST_RESCALE + slot, hh] = rows8(a)

    def stage(slot, pos):
        for hh in heads:
            scores_to(1 - slot, pos + 1, hh)
            accumulate(jnp.maximum(pos - 1, 0), 1 - slot, hh)
            softmax_group(slot, hh, None)

    def double_trip(d, carry):
        stage(0, 2 * d)
        stage(1, 2 * d + 1)
        return carry

    for hh in heads:
        acc_sc[hh] = jnp.zeros((HEAD_DIM, tq), F32)
        pb_sc[1, hh] = jnp.zeros((group, tq), BF16)
        st_sc[ST_MAX, hh] = jnp.full((SUBLANES, tq), neg_inf, F32)
        st_sc[ST_SUM, hh] = jnp.zeros((SUBLANES, tq), F32)
        st_sc[ST_RESCALE + 1, hh] = jnp.ones((SUBLANES, tq), F32)
        scores_to(0, 0, hh)
    lax.fori_loop(0, n_double, double_trip, 0)
    for hh in heads:
        accumulate(jnp.maximum(last_pos - 1, 0), 1, hh)
        softmax_group(0, hh, tvec - n_full * group)
        accumulate(last_pos, 0, hh)

    c1 = jnp.maximum(qi - 1, 0)
    c2 = jnp.maximum(qi - 2, 0)
    kw0, kw1, kw2 = key_chunk(kw_ref, qi), key_chunk(kw_ref, c1), key_chunk(kw_ref, c2)
    kpos1 = t0 - KEY_CHUNK + krow
    kpos2 = kpos1 - KEY_CHUNK
    exists1 = kpos1 >= 0
    in_window = (kpos2 > tvec - WINDOW) & (kpos2 >= 0)
    for hh in heads:
        qa = qa_sc[0, hh]
        w_sc[hh, 0:KEY_CHUNK, :] = jnp.where(causal, _dot(kw0, qa), neg_inf)
        w_sc[hh, KEY_CHUNK:2 * KEY_CHUNK, :] = jnp.where(exists1, _dot(kw1, qa), neg_inf)
        w_sc[hh, 2 * KEY_CHUNK:3 * KEY_CHUNK, :] = jnp.where(in_window, _dot(kw2, qa), neg_inf)
    vw = jnp.concatenate([vwT_ref[0, qi], vwT_ref[0, c1], vwT_ref[0, c2]], axis=1)
    for hh in heads:
        r0 = hh * HEAD_DIM
        s = w_sc[hh]
        m = jnp.max(fold8(s, jnp.maximum), axis=0, keepdims=True)
        p = jnp.exp2(s - m)
        l = jnp.sum(fold8(p, jnp.add), axis=0, keepdims=True)
        o_w = _dot(vw, p.astype(BF16)) * (1.0 / l)
        o_s = acc_sc[hh] * (1.0 / jnp.sum(st_sc[ST_SUM, hh], axis=0, keepdims=True))

        g0 = g_ref[0, N_BRANCH * hh:N_BRANCH * hh + 1, :]
        g1 = g_ref[0, N_BRANCH * hh + 1:N_BRANCH * hh + 2, :]
        g2 = g_ref[0, N_BRANCH * hh + 2:N_BRANCH * hh + 3, :]
        o = g0 * oc_sc[r0:r0 + HEAD_DIM, :] + g1 * o_s + g2 * o_w
        y_ref[0, r0:r0 + HEAD_DIM, :] = (o * sz_ref[0, r0:r0 + HEAD_DIM, :]).astype(BF16)


def _nsa_attn(qT, qrT, gT, szT, kcmp, vcmpT, ks, kw, vsT, vwT, *, tq=KEY_CHUNK):
    bsz, attn_d, seq = qT.shape
    n_cmp = kcmp.shape[2]
    nchunks = seq // KEY_CHUNK
    gd = HEADS_PER_GROUP * HEAD_DIM
    assert tq == KEY_CHUNK and nchunks % CHUNKS_PER_TRIP == 0 and nchunks >= 3 and n_cmp <= KEY_CHUNK
    qspec = pl.BlockSpec((1, gd, tq), lambda b, g, i: (b, g, i))
    kspec = pl.BlockSpec((1, seq, LANES), lambda b, g, i: (b, 0, g))
    vspec = pl.BlockSpec((1, nchunks, HEAD_DIM, KEY_CHUNK), lambda b, g, i: (b, 0, g, 0))
    return pl.pallas_call(
        functools.partial(_attn_kernel, tq=tq, n_cmp=n_cmp),
        out_shape=jax.ShapeDtypeStruct((bsz, attn_d, seq), BF16),
        grid=(bsz, N_KV_GROUPS, seq // tq),
        in_specs=[
            qspec, qspec,
            pl.BlockSpec((1, GATE_ROWS, tq), lambda b, g, i: (b, g, i)),
            qspec,
            pl.BlockSpec((1, 1, n_cmp, LANES), lambda b, g, i: (b, g, 0, 0)),
            pl.BlockSpec((1, 1, HEAD_DIM, n_cmp), lambda b, g, i: (b, g, 0, 0)),
            kspec, kspec, vspec, vspec,
        ],
        out_specs=qspec,
        scratch_shapes=[
            pltpu.VMEM((tq // LANES, SUBLANES + n_cmp, LANES), F32),
            pltpu.VMEM((n_cmp // 4, tq), F32),
            pltpu.VMEM((gd, tq), F32),
            pltpu.VMEM((2, HEADS_PER_GROUP, 2 * HEAD_DIM, tq), BF16),
            pltpu.VMEM((HEADS_PER_GROUP, 3 * KEY_CHUNK, tq), F32),
            pltpu.VMEM((2, HEADS_PER_GROUP, CHUNKS_PER_TRIP * KEY_CHUNK, tq), F32),
            pltpu.VMEM((2, HEADS_PER_GROUP, CHUNKS_PER_TRIP * KEY_CHUNK, tq), BF16),
            pltpu.VMEM((2, HEADS_PER_GROUP, SUBLANES, tq), F32),
            pltpu.VMEM((4, HEADS_PER_GROUP, SUBLANES, tq), F32),
            pltpu.VMEM((HEADS_PER_GROUP, HEAD_DIM, tq), F32),
        ],
        compiler_params=pltpu.CompilerParams(dimension_semantics=("parallel", "parallel", "arbitrary"),
                                             vmem_limit_bytes=VMEM_LIMIT),
        name="nsa_attn",
    )(qT, qrT, gT, szT, kcmp, vcmpT, ks, kw, vsT, vwT)


def _nsa_out_kernel(y_ref, h_ref, woT_ref, fn_ref, o_ref):
    oT = _dot(woT_ref[...], y_ref[0])
    h2 = h_ref[0] + oT.T
    o_ref[0] = h2 * _inv_rms(h2) * fn_ref[...]


def _nsa_out(yT, h3, woT, final_norm, *, tm=KEY_CHUNK):
    bsz, seq, d = h3.shape
    attn_d = yT.shape[1]
    return pl.pallas_call(
        _nsa_out_kernel,
        out_shape=jax.ShapeDtypeStruct((bsz, seq, d), F32),
        grid=(bsz, seq // tm),
        in_specs=[
            pl.BlockSpec((1, attn_d, tm), lambda b, i: (b, 0, i)),
            pl.BlockSpec((1, tm, d), lambda b, i: (b, i, 0)),
            pl.BlockSpec((d, attn_d), lambda b, i: (0, 0)),
            pl.BlockSpec((1, d), lambda b, i: (0, 0)),
        ],
        out_specs=pl.BlockSpec((1, tm, d), lambda b, i: (b, i, 0)),
        compiler_params=pltpu.CompilerParams(dimension_semantics=("parallel", "parallel"),
                                             vmem_limit_bytes=VMEM_LIMIT),
        name="nsa_out",
    )(yT, h3, woT, final_norm.reshape(1, d))


def _rope_tables(seq):
    pos = jnp.arange(seq, dtype=F32)
    inv = ROPE_THETA ** (-jnp.arange(0, ROT_DIM, 2, dtype=F32) / ROT_DIM)
    ang = pos[:, None] * inv[None, :]
    cos, sin = jnp.cos(ang), jnp.sin(ang)
    z = lambda n: jnp.zeros((seq, n), F32)
    cosf = jnp.concatenate([cos, cos, jnp.ones((seq, LANES - ROT_DIM), F32)], axis=1)
    sina = jnp.concatenate([-sin, z(LANES - N_FREQ)], axis=1)
    sinb = jnp.concatenate([z(N_FREQ), sin, z(LANES - ROT_DIM)], axis=1)
    return cos.T, sin.T, cosf, sina, sinb


def _blockdiag2(w1):
    hidden = w1.shape[1]
    w = w1.reshape(CMP_BLOCK, HEAD_DIM, hidden)
    zero = jnp.zeros_like(w)
    top = jnp.concatenate([w, zero], axis=2)
    bot = jnp.concatenate([zero, w], axis=2)
    return jnp.concatenate([top, bot], axis=1)


def kernel(x, a_norm, a_w_in, a_conv_w, a_w_out, kv_norm, w_kv, cmp_pos_k, cmp_w1_k, cmp_w2_k,
           cmp_pos_v, cmp_w1_v, cmp_w2_v, b_norm, b_w_in, b_w_out, final_norm):
    bsz, seq, d = x.shape
    attn_d = N_HEADS * HEAD_DIM
    kv_d = N_KV_GROUPS * HEAD_DIM
    assert b_norm.shape[0] == 1, "one NSA layer reads the shared K/V side"
    assert seq % (2 * KEY_CHUNK) == 0

    h = x.reshape(bsz * seq, d)
    for layer in range(a_norm.shape[0]):
        h = _conv_layer(h, a_norm[layer], a_w_in[layer].astype(BF16), a_conv_w[layer],
                        a_w_out[layer].astype(BF16), seq=seq)
    h3 = h.reshape(bsz, seq, d)

    w_in = b_w_in[0]
    n_gate = N_HEADS * N_BRANCH
    wg = w_in[:, attn_d:attn_d + n_gate].reshape(d, N_KV_GROUPS, HEADS_PER_GROUP * N_BRANCH)
    wg = jnp.pad(wg, ((0, 0), (0, 0), (0, GATE_ROWS - HEADS_PER_GROUP * N_BRANCH)))
    waT = jnp.concatenate([w_in[:, :attn_d], w_in[:, attn_d + n_gate:], wg.reshape(d, -1)], axis=1).T.astype(BF16)
    wkv = w_kv.reshape(d, 2 * N_BRANCH, N_KV_GROUPS, HEAD_DIM)
    k_c, v_c, k_s, v_s, k_w, v_w = [wkv[:, i] for i in range(2 * N_BRANCH)]
    flat = lambda w: w.reshape(d, kv_d)
    pad_lanes = lambda w: jnp.pad(w, ((0, 0), (0, 0), (0, LANES - HEAD_DIM))).reshape(d, N_KV_GROUPS * LANES)
    wvT = jnp.concatenate([flat(v_s), flat(v_w)], axis=1).T.astype(BF16)
    wk = jnp.concatenate([flat(k_c), flat(v_c), pad_lanes(k_s), pad_lanes(k_w)], axis=1).astype(BF16)
    cosT, sinT, cosf, sina, sinb = _rope_tables(seq)

    qT, qrT, szT, gT, kvc, ks, kw, vsT, vwT = _nsa_proj(
        h3, b_norm[0], kv_norm, waT, wvT, wk, cosT, sinT, cosf, sina, sinb)

    w1bd = jnp.stack([_blockdiag2(cmp_w1_k), _blockdiag2(cmp_w1_v)]).astype(BF16)
    pos2 = jnp.stack([jnp.tile(cmp_pos_k, (1, 2)), jnp.tile(cmp_pos_v, (1, 2))])
    w2 = jnp.stack([cmp_w2_k, cmp_w2_v])
    w2p = jnp.pad(w2, ((0, 0), (0, 0), (0, LANES - HEAD_DIM))).astype(BF16)
    w2T = jnp.swapaxes(w2, 1, 2).astype(BF16)
    cmp_tm, cmp_fm = _compress(kvc, w1bd, pos2, w2p, w2T)

    yT = _nsa_attn(qT, qrT, gT, szT, cmp_tm[0], cmp_fm[1], ks, kw, vsT, vwT)
    return _nsa_out(yT, h3, b_w_out[0].T.astype(BF16), final_norm)
```

```python
import functools

import jax
import jax.numpy as jnp
from jax import lax
from jax.experimental import pallas as pl
from jax.experimental.pallas import tpu as pltpu

EPS = 1e-6
CONV_WIDTH = 3
N_HEADS = 16
HEAD_DIM = 64
N_KV_GROUPS = 4
HEADS_PER_GROUP = N_HEADS // N_KV_GROUPS
N_BRANCH = 3
ROT_DIM = HEAD_DIM // 4
N_FREQ = ROT_DIM // 2
ROPE_THETA = 500000.0
CMP_BLOCK = 32
CMP_STRIDE = 16
SLC_BLOCK = 64
N_SELECT = 16
N_LOCAL = 2
WINDOW = 512
FORCE_SCORE = 1e4

LANES = 128
SUBLANES = 8
KEY_CHUNK = 256
CHUNKS_PER_TRIP = 2
SOFTMAX_ROWS = 64
RANK_SECTION = 16
GATE_ROWS = 16
MASK_BIAS = -1e30
LOG2_E = 1.4426950408889634
VMEM_LIMIT = 56 * 1024 * 1024

BF16 = jnp.bfloat16
F32 = jnp.float32
NT_DIMS = (((1,), (1,)), ((), ()))


def _dot(a, b):
    return jnp.dot(a, b, preferred_element_type=F32)


def _dot_nt(a, b):
    return lax.dot_general(a, b, NT_DIMS, preferred_element_type=F32)


def _sigmoid(x):
    return 1.0 / (1.0 + jnp.exp(-x))


def _inv_rms(x):
    return lax.rsqrt(jnp.mean(x * x, axis=-1, keepdims=True) + EPS)


def _conv_layer_kernel(x_ref, g_ref, win_ref, cw_ref, wout_ref, o_ref, vbuf_ref, *,
                       tm, tiles_per_seq, conv_d, cchunk):
    @pl.when(pl.program_id(0) % tiles_per_seq == 0)
    def _():
        vbuf_ref[0:SUBLANES, :] = jnp.zeros((SUBLANES, conv_d), F32)

    x = x_ref[...]
    hn = (x * _inv_rms(x) * g_ref[...]).astype(BF16)
    acc = jnp.zeros(x.shape, F32)
    for cc in range(conv_d // cchunk):
        cs = cc * cchunk
        b = _dot(hn, win_ref[:, cs:cs + cchunk])
        c = _dot(hn, win_ref[:, conv_d + cs:conv_d + cs + cchunk])
        u = _dot(hn, win_ref[:, 2 * conv_d + cs:2 * conv_d + cs + cchunk])
        z = _dot(hn, win_ref[:, 3 * conv_d + cs:3 * conv_d + cs + cchunk])
        v = c * u
        vbuf_ref[SUBLANES:SUBLANES + tm, cs:cs + cchunk] = v
        v1 = vbuf_ref[SUBLANES - 1:SUBLANES - 1 + tm, cs:cs + cchunk]
        v2 = vbuf_ref[SUBLANES - 2:SUBLANES - 2 + tm, cs:cs + cchunk]
        conv = (cw_ref[0:1, cs:cs + cchunk] * v2 + cw_ref[1:2, cs:cs + cchunk] * v1
                + cw_ref[2:3, cs:cs + cchunk] * v)
        vbuf_ref[0:SUBLANES, cs:cs + cchunk] = v[tm - SUBLANES:tm, :]
        y = b * conv * (z * _sigmoid(z))
        acc = acc + _dot(y.astype(BF16), wout_ref[cs:cs + cchunk, :])
    o_ref[...] = x + acc


def _conv_layer(h, norm_g, w_in, conv_w, w_out, *, seq, tm=512, cchunk=512):
    t, d = h.shape
    conv_d = conv_w.shape[1]
    tm = min(tm, seq)
    const = lambda shape: pl.BlockSpec(shape, lambda i: (0,) * len(shape), pipeline_mode=pl.Buffered(1))
    return pl.pallas_call(
        functools.partial(_conv_layer_kernel, tm=tm, tiles_per_seq=seq // tm, conv_d=conv_d, cchunk=cchunk),
        out_shape=jax.ShapeDtypeStruct((t, d), F32),
        grid=(t // tm,),
        in_specs=[
            pl.BlockSpec((tm, d), lambda i: (i, 0)),
            const((1, d)),
            const((d, 4 * conv_d)),
            const((CONV_WIDTH, conv_d)),
            const((conv_d, d)),
        ],
        out_specs=pl.BlockSpec((tm, d), lambda i: (i, 0)),
        scratch_shapes=[pltpu.VMEM((SUBLANES + tm, conv_d), F32)],
        compiler_params=pltpu.CompilerParams(dimension_semantics=("arbitrary",), vmem_limit_bytes=VMEM_LIMIT),
        name="conv_layer",
    )(h, norm_g.reshape(1, d), w_in, conv_w, w_out)


def _nsa_proj_kernel(h_ref, bn_ref, kn_ref, waT_ref, wvT_ref, wk_ref, cosT_ref, sinT_ref,
                     cosf_ref, sina_ref, sinb_ref,
                     qT_ref, qrT_ref, szT_ref, gT_ref, kvc_ref, ks_ref, kw_ref, vsT_ref, vwT_ref, *, tm, d):
    attn_d = N_HEADS * HEAD_DIM
    kv_d = N_KV_GROUPS * HEAD_DIM
    h = h_ref[0]
    hr = h * _inv_rms(h)
    hq = (hr * bn_ref[...]).astype(BF16)
    hk = (hr * kn_ref[...]).astype(BF16)

    qT = _dot_nt(waT_ref[0:attn_d, :], hq) * (HEAD_DIM ** -0.5 * LOG2_E)
    cosT = cosT_ref[...]
    sinT = sinT_ref[...]
    for hd in range(N_HEADS):
        r0 = hd * HEAD_DIM
        blk = qT[r0:r0 + HEAD_DIM, :]
        x1 = blk[0:N_FREQ, :]
        x2 = blk[N_FREQ:ROT_DIM, :]
        rot = jnp.concatenate([x1 * cosT - x2 * sinT, x2 * cosT + x1 * sinT, blk[ROT_DIM:, :]], axis=0)
        qT_ref[0, r0:r0 + HEAD_DIM, :] = blk.astype(BF16)
        qrT_ref[0, r0:r0 + HEAD_DIM, :] = rot.astype(BF16)

    zT = _dot_nt(waT_ref[attn_d:2 * attn_d, :], hq)
    szT_ref[0] = zT * _sigmoid(zT)
    gT_ref[0] = _sigmoid(_dot_nt(waT_ref[2 * attn_d:2 * attn_d + N_KV_GROUPS * GATE_ROWS, :], hq))

    vT = _dot_nt(wvT_ref[...], hk)
    vsT_ref[0, 0] = vT[0:kv_d, :].astype(BF16)
    vwT_ref[0, 0] = vT[kv_d:2 * kv_d, :].astype(BF16)

    kk = _dot(hk, wk_ref[...])
    kvc_ref[0] = kk[:, 0:2 * kv_d]
    cosf = cosf_ref[...]
    sina = sina_ref[...]
    sinb = sinb_ref[...]
    row = lax.broadcasted_iota(jnp.int32, (tm, LANES), 0)
    lane = lax.broadcasted_iota(jnp.int32, (tm, LANES), 1)
    tok_blk = (pl.program_id(1) * tm + row) // SLC_BLOCK
    onehot = jnp.where((lane >= HEAD_DIM) & (lane - HEAD_DIM == tok_blk), 1.0, 0.0)
    for j in range(2 * N_KV_GROUPS):
        c0 = 2 * kv_d + j * LANES
        xb = kk[:, c0:c0 + LANES]
        rot = (xb * cosf + pltpu.roll(xb, LANES - N_FREQ, axis=1) * sina
               + pltpu.roll(xb, N_FREQ, axis=1) * sinb)
        if j < N_KV_GROUPS:
            ks_ref[0, :, j * LANES:(j + 1) * LANES] = (rot + onehot).astype(BF16)
        else:
            jj = j - N_KV_GROUPS
            kw_ref[0, :, jj * LANES:(jj + 1) * LANES] = (rot + onehot).astype(BF16)


def _nsa_proj(h3, b_norm, kv_norm, waT, wvT, wk, cosT, sinT, cosf, sina, sinb, *, tm=KEY_CHUNK):
    bsz, seq, d = h3.shape
    attn_d = N_HEADS * HEAD_DIM
    kv_d = N_KV_GROUPS * HEAD_DIM
    kpad = N_KV_GROUPS * LANES
    nt = seq // tm
    const = lambda shape: pl.BlockSpec(shape, lambda b, i: (0,) * len(shape))
    fm = lambda rows: pl.BlockSpec((1, rows, tm), lambda b, i: (b, 0, i))
    tmj = lambda cols: pl.BlockSpec((1, tm, cols), lambda b, i: (b, i, 0))
    out_shape = [
        jax.ShapeDtypeStruct((bsz, attn_d, seq), BF16),
        jax.ShapeDtypeStruct((bsz, attn_d, seq), BF16),
        jax.ShapeDtypeStruct((bsz, attn_d, seq), F32),
        jax.ShapeDtypeStruct((bsz, N_KV_GROUPS * GATE_ROWS, seq), F32),
        jax.ShapeDtypeStruct((bsz, seq, 2 * kv_d), F32),
        jax.ShapeDtypeStruct((bsz, seq, kpad), BF16),
        jax.ShapeDtypeStruct((bsz, seq, kpad), BF16),
        jax.ShapeDtypeStruct((bsz, nt, kv_d, tm), BF16),
        jax.ShapeDtypeStruct((bsz, nt, kv_d, tm), BF16),
    ]
    out_specs = [
        fm(attn_d), fm(attn_d), fm(attn_d), fm(N_KV_GROUPS * GATE_ROWS),
        tmj(2 * kv_d), tmj(kpad), tmj(kpad),
        pl.BlockSpec((1, 1, kv_d, tm), lambda b, i: (b, i, 0, 0)),
        pl.BlockSpec((1, 1, kv_d, tm), lambda b, i: (b, i, 0, 0)),
    ]
    return pl.pallas_call(
        functools.partial(_nsa_proj_kernel, tm=tm, d=d),
        out_shape=out_shape,
        grid=(bsz, nt),
        in_specs=[
            tmj(d), const((1, d)), const((1, d)),
            const(waT.shape), const(wvT.shape), const(wk.shape),
            pl.BlockSpec((N_FREQ, tm), lambda b, i: (0, i)),
            pl.BlockSpec((N_FREQ, tm), lambda b, i: (0, i)),
            pl.BlockSpec((tm, LANES), lambda b, i: (i, 0)),
            pl.BlockSpec((tm, LANES), lambda b, i: (i, 0)),
            pl.BlockSpec((tm, LANES), lambda b, i: (i, 0)),
        ],
        out_specs=out_specs,
        compiler_params=pltpu.CompilerParams(dimension_semantics=("parallel", "parallel"),
                                             vmem_limit_bytes=VMEM_LIMIT),
        name="nsa_proj",
    )(h3, b_norm.reshape(1, d), kv_norm.reshape(1, d), waT, wvT, wk, cosT, sinT, cosf, sina, sinb)


def _gelu_tanh(x):
    return x * (0.5 * (1.0 + jnp.tanh(0.7978845608028654 * (x + 0.044715 * (x * x * x)))))


def _compress_kernel(x_ref, w1_ref, pos_ref, w2_ref, w2T_ref, kc_ref, cT_ref, *, nchunk, hidden):
    u = jnp.zeros((nchunk, 2 * hidden), F32)
    v = jnp.zeros((nchunk, 2 * hidden), F32)
    for l in range(CMP_STRIDE):
        xl = x_ref[0, pl.ds(l, nchunk, stride=CMP_STRIDE), :]
        u = u + _dot((xl + pos_ref[0, l:l + 1, :]).astype(BF16), w1_ref[0, l])
        l2 = CMP_STRIDE + l
        v = v + _dot((xl + pos_ref[0, l2:l2 + 1, :]).astype(BF16), w1_ref[0, l2])
    hid = u + pltpu.roll(v, nchunk - 1, axis=0)
    act = _gelu_tanh(hid).astype(BF16)
    rows_ok = lax.broadcasted_iota(jnp.int32, (nchunk, LANES), 0) < nchunk - 1
    cols_ok = lax.broadcasted_iota(jnp.int32, (HEAD_DIM, nchunk), 1) < nchunk - 1
    for p in range(2):
        a = act[:, p * hidden:(p + 1) * hidden]
        kc_ref[0, 0, p] = jnp.where(rows_ok, _dot(a, w2_ref[0]), 0.0).astype(BF16)
        cT_ref[0, 0, p] = jnp.where(cols_ok, _dot_nt(w2T_ref[0], a), 0.0).astype(BF16)


def _compress(kvc, w1bd, pos2, w2p, w2T):
    bsz, seq, _ = kvc.shape
    nchunk = seq // CMP_STRIDE
    hidden = w2T.shape[2]
    return pl.pallas_call(
        functools.partial(_compress_kernel, nchunk=nchunk, hidden=hidden),
        out_shape=[
            jax.ShapeDtypeStruct((2, bsz, N_KV_GROUPS, nchunk, LANES), BF16),
            jax.ShapeDtypeStruct((2, bsz, N_KV_GROUPS, HEAD_DIM, nchunk), BF16),
        ],
        grid=(2, bsz, N_KV_GROUPS // 2),
        in_specs=[
            pl.BlockSpec((1, seq, LANES), lambda s, b, p: (b, 0, 2 * s + p)),
            pl.BlockSpec((1, CMP_BLOCK, LANES, 2 * hidden), lambda s, b, p: (s, 0, 0, 0)),
            pl.BlockSpec((1, CMP_BLOCK, LANES), lambda s, b, p: (s, 0, 0)),
            pl.BlockSpec((1, hidden, LANES), lambda s, b, p: (s, 0, 0)),
            pl.BlockSpec((1, HEAD_DIM, hidden), lambda s, b, p: (s, 0, 0)),
        ],
        out_specs=[
            pl.BlockSpec((1, 1, 2, nchunk, LANES), lambda s, b, p: (s, b, p, 0, 0)),
            pl.BlockSpec((1, 1, 2, HEAD_DIM, nchunk), lambda s, b, p: (s, b, p, 0, 0)),
        ],
        compiler_params=pltpu.CompilerParams(dimension_semantics=("parallel", "parallel", "parallel"),
                                             vmem_limit_bytes=VMEM_LIMIT),
        name="compress",
    )(kvc, w1bd, pos2, w2p, w2T)


def _attn_kernel(q_ref, qr_ref, g_ref, sz_ref, kc_ref, vcT_ref, ks_ref, kw_ref, vsT_ref, vwT_ref, y_ref,
                 p_sc, sc_sc, rank_sc, oc_sc, qa_sc, w_sc, s_sc, pb_sc, smax_sc, st_sc, acc_sc, wmax_sc, wp_sc,
                 ow_sc, *, tq, n_cmp):
    qi = pl.program_id(2)
    t0 = qi * tq
    neg_inf = -jnp.inf
    tvec = t0 + lax.broadcasted_iota(jnp.int32, (1, tq), 1)

    kc = kc_ref[0, 0, :, 0:HEAD_DIM]
    cmp_end = lax.broadcasted_iota(jnp.int32, (n_cmp, tq), 0) * CMP_STRIDE + (CMP_BLOCK - 1)
    cmask = cmp_end <= tvec
    heads = range(HEADS_PER_GROUP)

    def fold8(x, op, ways=4):
        parts = [None] * ways
        for idx, r in enumerate(range(0, x.shape[0], SUBLANES)):
            slab = x[r:r + SUBLANES, :]
            parts[idx % ways] = slab if parts[idx % ways] is None else op(parts[idx % ways], slab)
        return functools.reduce(op, [p for p in parts if p is not None])

    for hh in heads:
        r0 = hh * HEAD_DIM
        w_sc[hh, 0:n_cmp, :] = jnp.where(cmask, _dot(kc, q_ref[0, r0:r0 + HEAD_DIM, :]), neg_inf)
    probs = []
    for hh in heads:
        s = w_sc[hh, 0:n_cmp, :]
        m = jnp.max(fold8(s, jnp.maximum), axis=0, keepdims=True)
        m = jnp.where(m == neg_inf, 0.0, m)
        e = jnp.exp2(s - m)
        den = jnp.sum(fold8(e, jnp.add), axis=0, keepdims=True)
        probs.append(e * (1.0 / jnp.maximum(den, 1e-30)))
    p_grp = functools.reduce(jnp.add, probs)
    for hh in heads:
        r0 = hh * HEAD_DIM
        oc_sc[r0:r0 + HEAD_DIM, :] = _dot(vcT_ref[0, 0], probs[hh].astype(BF16))

    n_sb = n_cmp // 4
    ratio = SLC_BLOCK // CMP_STRIDE
    imp_cols = []
    for c in range(tq // LANES):
        p_sc[c, 0:SUBLANES, :] = jnp.zeros((SUBLANES, LANES), F32)
        p_sc[c, SUBLANES:SUBLANES + n_cmp, :] = p_grp[:, c * LANES:(c + 1) * LANES]
        tap = lambda o: p_sc[c, pl.ds(SUBLANES + o, n_sb, stride=ratio), :]
        imp_cols.append(tap(-1) + 2.0 * (tap(0) + tap(1) + tap(2)) + tap(3))
    imp = jnp.concatenate(imp_cols, axis=1)

    jrow = lax.broadcasted_iota(jnp.int32, (n_sb, tq), 0)
    cur = tvec // SLC_BLOCK
    valid = jrow <= cur
    forced = (jrow == 0) | (valid & (jrow > cur - N_LOCAL))
    score = jnp.where(valid, jnp.where(forced, FORCE_SCORE, imp), neg_inf)
    sc_sc[...] = score

    n_valid = (t0 + tq) // SLC_BLOCK
    n_slabs = n_sb // SUBLANES
    slab_row = lax.broadcasted_iota(jnp.int32, (SUBLANES, tq), 0)
    rank_sc[...] = jnp.zeros((n_sb, tq), jnp.int32)
    for first in range(0, n_sb, RANK_SECTION):
        @pl.when(n_valid > max(N_SELECT, first))
        def _():
            slabs = [sc_sc[k * SUBLANES:(k + 1) * SUBLANES, :] for k in range(n_slabs)]
            counts = [rank_sc[k * SUBLANES:(k + 1) * SUBLANES, :] for k in range(n_slabs)]
            for jp in range(first, first + RANK_SECTION):
                sb = sc_sc[jp:jp + 1, :]
                for k in range(n_slabs):
                    if k * SUBLANES > jp:
                        before = sb >= slabs[k]
                    elif k * SUBLANES + SUBLANES - 1 < jp:
                        before = sb > slabs[k]
                    else:
                        before = (sb > slabs[k]) | ((sb == slabs[k]) & (slab_row > jp % SUBLANES))
                    counts[k] = counts[k] + jnp.where(before, 1, 0)
            for k in range(n_slabs):
                rank_sc[k * SUBLANES:(k + 1) * SUBLANES, :] = counts[k]
    sel_bias = jnp.where((rank_sc[...] < N_SELECT) & valid, 0.0, MASK_BIAS).astype(BF16)

    def key_chunk(ref, c):
        return ref[0, pl.ds(pl.multiple_of(c * KEY_CHUNK, KEY_CHUNK), KEY_CHUNK), :]

    win_blocks = WINDOW // SLC_BLOCK
    win_bias = jnp.where((jrow >= cur - win_blocks) & valid, 0.0, MASK_BIAS).astype(BF16)
    dead_bias = jnp.full((n_sb, tq), MASK_BIAS, BF16)
    if n_sb < HEAD_DIM:
        pad = jnp.zeros((HEAD_DIM - n_sb, tq), BF16)
        sel_bias, win_bias, dead_bias = [jnp.concatenate([b, pad], axis=0) for b in (sel_bias, win_bias, dead_bias)]
    for hh in heads:
        r0 = hh * HEAD_DIM
        for idx, bias in enumerate((sel_bias, dead_bias, win_bias)):
            qa_sc[idx, hh] = jnp.concatenate([qr_ref[0, r0:r0 + HEAD_DIM, :], bias], axis=0)
    QA_SEL, QA_DEAD, QA_WIN = 0, 1, 2

    blk_row = lax.broadcasted_iota(jnp.int32, (SLC_BLOCK, LANES), 0)
    blk_lane = lax.broadcasted_iota(jnp.int32, (SLC_BLOCK, LANES), 1)
    edge = [blk_row - blk_lane, blk_row - (blk_lane - SLC_BLOCK)]
    off_diag = [blk_lane >= SLC_BLOCK, blk_lane < SLC_BLOCK]

    def mask_diagonal_blocks(s, keep):
        blocks = []
        for bb in range(KEY_CHUNK // SLC_BLOCK):
            rows = s[bb * SLC_BLOCK:(bb + 1) * SLC_BLOCK, :]
            halves = [rows[:, h * LANES:(h + 1) * LANES] for h in range(tq // LANES)]
            h = bb // 2
            halves[h] = jnp.where(off_diag[bb % 2] | keep(edge[bb % 2]), halves[h], neg_inf)
            blocks.append(jnp.concatenate(halves, axis=1))
        return jnp.concatenate(blocks, axis=0)

    group = CHUNKS_PER_TRIP * KEY_CHUNK
    n_full = qi // CHUNKS_PER_TRIP
    n_double = (n_full + 1) // 2
    last_pos = 2 * n_double

    def chunks_at(pos):
        is_last = pos == last_pos
        dead = jnp.logical_and(pos >= n_full, jnp.logical_not(is_last))
        grp = jnp.minimum(pos, jnp.maximum(n_full - 1, 0))
        c0 = jnp.where(is_last, qi, CHUNKS_PER_TRIP * grp)
        c1 = jnp.where(is_last, qi + 1 - 2 * (qi % 2), CHUNKS_PER_TRIP * grp + 1)
        return (c0, c1), is_last, jnp.where(dead, QA_DEAD, QA_SEL)

    ST_MAX, ST_SUM, ST_RESCALE = 0, 1, 2
    rows8 = lambda x: jnp.broadcast_to(x, (SUBLANES, tq))
    never = 1 << 20

    def scores_to(slot, pos, hh, may_be_last):
        chunks, is_last, operand = chunks_at(pos)
        causal_slack = jnp.where(is_last, 0, never)
        smax = None
        for u, c in enumerate(chunks):
            s = _dot(key_chunk(ks_ref, c), qa_sc[operand, hh])
            if u == 0 and may_be_last:
                s = mask_diagonal_blocks(s, lambda e: e <= causal_slack)
            s_sc[slot, hh, u * KEY_CHUNK:(u + 1) * KEY_CHUNK, :] = s
            f = fold8(s, jnp.maximum)
            smax = f if smax is None else jnp.maximum(smax, f)
        smax_sc[slot, hh] = smax

    def accumulate(pos, slot, hh):
        chunks, _, _ = chunks_at(pos)
        vT = jnp.concatenate([vsT_ref[0, c] for c in chunks], axis=1)
        acc_sc[hh] = acc_sc[hh] * st_sc[ST_RESCALE + slot, hh, 0:1, :] + _dot(vT, pb_sc[slot, hh])

    def softmax_group(slot, hh):
        m_old = st_sc[ST_MAX, hh, 0:1, :]
        m_new = jnp.maximum(m_old, jnp.max(smax_sc[slot, hh], axis=0, keepdims=True))
        a = jnp.exp2(m_old - m_new)
        psum = None
        for r in range(0, group, SOFTMAX_ROWS):
            s = s_sc[slot, hh, r:r + SOFTMAX_ROWS, :]
            p = jnp.exp2(s - m_new)
            pb_sc[slot, hh, r:r + SOFTMAX_ROWS, :] = p.astype(BF16)
            f = fold8(p, jnp.add, ways=2)
            psum = f if psum is None else psum + f
        st_sc[ST_SUM, hh] = a * st_sc[ST_SUM, hh] + psum
        st_sc[ST_MAX, hh] = rows8(m_new)
        st_sc[ST_RESCALE + slot, hh] = rows8(a)

    def stage(slot, pos):
        for hh in heads:
            scores_to(1 - slot, pos + 1, hh, may_be_last=(slot == 1))
            accumulate(jnp.maximum(pos - 1, 0), 1 - slot, hh)
            softmax_group(slot, hh)

    def double_trip(d, carry):
        stage(0, 2 * d)
        stage(1, 2 * d + 1)
        return carry

    win_chunks = (qi, jnp.maximum(qi - 1, 0), jnp.maximum(qi - 2, 0))
    win_operand = (QA_WIN, jnp.where(qi >= 1, QA_WIN, QA_DEAD), jnp.where(qi >= 2, QA_WIN, QA_DEAD))
    win_keep = (lambda e: e <= 0, None, lambda e: e > 0)
    vw = jnp.concatenate([vwT_ref[0, c] for c in win_chunks], axis=1)
    for hh in heads:
        acc_sc[hh] = jnp.zeros((HEAD_DIM, tq), F32)
        pb_sc[1, hh] = jnp.zeros((group, tq), BF16)
        st_sc[ST_MAX, hh] = jnp.full((SUBLANES, tq), neg_inf, F32)
        st_sc[ST_SUM, hh] = jnp.zeros((SUBLANES, tq), F32)
        st_sc[ST_RESCALE + 1, hh] = jnp.ones((SUBLANES, tq), F32)
        scores_to(0, 0, hh, may_be_last=True)
        wmax = None
        for u, c in enumerate(win_chunks):
            s = _dot(key_chunk(kw_ref, c), qa_sc[win_operand[u], hh])
            if win_keep[u] is not None:
                s = mask_diagonal_blocks(s, win_keep[u])
            w_sc[hh, u * KEY_CHUNK:(u + 1) * KEY_CHUNK, :] = s
            f = fold8(s, jnp.maximum)
            wmax = f if wmax is None else jnp.maximum(wmax, f)
        wmax_sc[hh] = wmax
    for hh in heads:
        m = jnp.max(wmax_sc[hh], axis=0, keepdims=True)
        psum = None
        for r in range(0, 3 * KEY_CHUNK, SOFTMAX_ROWS):
            p = jnp.exp2(w_sc[hh, r:r + SOFTMAX_ROWS, :] - m)
            wp_sc[hh, r:r + SOFTMAX_ROWS, :] = p.astype(BF16)
            f = fold8(p, jnp.add, ways=2)
            psum = f if psum is None else psum + f
        ow_sc[hh] = _dot(vw, wp_sc[hh]) * (1.0 / jnp.sum(psum, axis=0, keepdims=True))

    lax.fori_loop(0, n_double, double_trip, 0)

    for hh in heads:
        r0 = hh * HEAD_DIM
        accumulate(jnp.maximum(last_pos - 1, 0), 1, hh)
        softmax_group(0, hh)
        accumulate(last_pos, 0, hh)
        o_w = ow_sc[hh]
        o_s = acc_sc[hh] * (1.0 / jnp.sum(st_sc[ST_SUM, hh], axis=0, keepdims=True))

        g0 = g_ref[0, N_BRANCH * hh:N_BRANCH * hh + 1, :]
        g1 = g_ref[0, N_BRANCH * hh + 1:N_BRANCH * hh + 2, :]
        g2 = g_ref[0, N_BRANCH * hh + 2:N_BRANCH * hh + 3, :]
        o = g0 * oc_sc[r0:r0 + HEAD_DIM, :] + g1 * o_s + g2 * o_w
        y_ref[0, r0:r0 + HEAD_DIM, :] = (o * sz_ref[0, r0:r0 + HEAD_DIM, :]).astype(BF16)


def _nsa_attn(qT, qrT, gT, szT, kcmp, vcmpT, ks, kw, vsT, vwT, *, tq=KEY_CHUNK):
    bsz, attn_d, seq = qT.shape
    n_cmp = kcmp.shape[2]
    nchunks = seq // KEY_CHUNK
    gd = HEADS_PER_GROUP * HEAD_DIM
    assert tq == KEY_CHUNK == 4 * SLC_BLOCK and CHUNKS_PER_TRIP == 2 and nchunks % 2 == 0 and n_cmp <= KEY_CHUNK
    qspec = pl.BlockSpec((1, gd, tq), lambda b, g, i: (b, g, i))
    kspec = pl.BlockSpec((1, seq, LANES), lambda b, g, i: (b, 0, g))
    vspec = pl.BlockSpec((1, nchunks, HEAD_DIM, KEY_CHUNK), lambda b, g, i: (b, 0, g, 0))
    return pl.pallas_call(
        functools.partial(_attn_kernel, tq=tq, n_cmp=n_cmp),
        out_shape=jax.ShapeDtypeStruct((bsz, attn_d, seq), BF16),
        grid=(bsz, N_KV_GROUPS, seq // tq),
        in_specs=[
            qspec, qspec,
            pl.BlockSpec((1, GATE_ROWS, tq), lambda b, g, i: (b, g, i)),
            qspec,
            pl.BlockSpec((1, 1, n_cmp, LANES), lambda b, g, i: (b, g, 0, 0)),
            pl.BlockSpec((1, 1, HEAD_DIM, n_cmp), lambda b, g, i: (b, g, 0, 0)),
            kspec, kspec, vspec, vspec,
        ],
        out_specs=qspec,
        scratch_shapes=[
            pltpu.VMEM((tq // LANES, SUBLANES + n_cmp, LANES), F32),
            pltpu.VMEM((n_cmp // 4, tq), F32),
            pltpu.VMEM((n_cmp // 4, tq), jnp.int32),
            pltpu.VMEM((gd, tq), F32),
            pltpu.VMEM((3, HEADS_PER_GROUP, 2 * HEAD_DIM, tq), BF16),
            pltpu.VMEM((HEADS_PER_GROUP, 3 * KEY_CHUNK, tq), F32),
            pltpu.VMEM((2, HEADS_PER_GROUP, CHUNKS_PER_TRIP * KEY_CHUNK, tq), F32),
            pltpu.VMEM((2, HEADS_PER_GROUP, CHUNKS_PER_TRIP * KEY_CHUNK, tq), BF16),
            pltpu.VMEM((2, HEADS_PER_GROUP, SUBLANES, tq), F32),
            pltpu.VMEM((4, HEADS_PER_GROUP, SUBLANES, tq), F32),
            pltpu.VMEM((HEADS_PER_GROUP, HEAD_DIM, tq), F32),
            pltpu.VMEM((HEADS_PER_GROUP, SUBLANES, tq), F32),
            pltpu.VMEM((HEADS_PER_GROUP, 3 * KEY_CHUNK, tq), BF16),
            pltpu.VMEM((HEADS_PER_GROUP, HEAD_DIM, tq), F32),
        ],
        compiler_params=pltpu.CompilerParams(dimension_semantics=("parallel", "parallel", "arbitrary"),
                                             vmem_limit_bytes=VMEM_LIMIT),
        name="nsa_attn",
    )(qT, qrT, gT, szT, kcmp, vcmpT, ks, kw, vsT, vwT)


def _nsa_out_kernel(y_ref, h_ref, woT_ref, fn_ref, o_ref):
    oT = _dot(woT_ref[...], y_ref[0])
    h2 = h_ref[0] + oT.T
    o_ref[0] = h2 * _inv_rms(h2) * fn_ref[...]


def _nsa_out(yT, h3, woT, final_norm, *, tm=KEY_CHUNK):
    bsz, seq, d = h3.shape
    attn_d = yT.shape[1]
    return pl.pallas_call(
        _nsa_out_kernel,
        out_shape=jax.ShapeDtypeStruct((bsz, seq, d), F32),
        grid=(bsz, seq // tm),
        in_specs=[
            pl.BlockSpec((1, attn_d, tm), lambda b, i: (b, 0, i)),
            pl.BlockSpec((1, tm, d), lambda b, i: (b, i, 0)),
            pl.BlockSpec((d, attn_d), lambda b, i: (0, 0)),
            pl.BlockSpec((1, d), lambda b, i: (0, 0)),
        ],
        out_specs=pl.BlockSpec((1, tm, d), lambda b, i: (b, i, 0)),
        compiler_params=pltpu.CompilerParams(dimension_semantics=("parallel", "parallel"),
                                             vmem_limit_bytes=VMEM_LIMIT),
        name="nsa_out",
    )(yT, h3, woT, final_norm.reshape(1, d))


def _rope_tables(seq):
    pos = jnp.arange(seq, dtype=F32)
    inv = ROPE_THETA ** (-jnp.arange(0, ROT_DIM, 2, dtype=F32) / ROT_DIM)
    ang = pos[:, None] * inv[None, :]
    cos, sin = jnp.cos(ang), jnp.sin(ang)
    z = lambda n: jnp.zeros((seq, n), F32)
    cosf = jnp.concatenate([cos, cos, jnp.ones((seq, LANES - ROT_DIM), F32)], axis=1)
    sina = jnp.concatenate([-sin, z(LANES - N_FREQ)], axis=1)
    sinb = jnp.concatenate([z(N_FREQ), sin, z(LANES - ROT_DIM)], axis=1)
    return cos.T, sin.T, cosf, sina, sinb


def _blockdiag2(w1):
    hidden = w1.shape[1]
    w = w1.reshape(CMP_BLOCK, HEAD_DIM, hidden)
    zero = jnp.zeros_like(w)
    top = jnp.concatenate([w, zero], axis=2)
    bot = jnp.concatenate([zero, w], axis=2)
    return jnp.concatenate([top, bot], axis=1)


def kernel(x, a_norm, a_w_in, a_conv_w, a_w_out, kv_norm, w_kv, cmp_pos_k, cmp_w1_k, cmp_w2_k,
           cmp_pos_v, cmp_w1_v, cmp_w2_v, b_norm, b_w_in, b_w_out, final_norm):
    bsz, seq, d = x.shape
    attn_d = N_HEADS * HEAD_DIM
    kv_d = N_KV_GROUPS * HEAD_DIM
    assert b_norm.shape[0] == 1, "one NSA layer reads the shared K/V side"
    assert seq % (2 * KEY_CHUNK) == 0

    h = x.reshape(bsz * seq, d)
    for layer in range(a_norm.shape[0]):
        h = _conv_layer(h, a_norm[layer], a_w_in[layer].astype(BF16), a_conv_w[layer],
                        a_w_out[layer].astype(BF16), seq=seq)
    h3 = h.reshape(bsz, seq, d)

    w_in = b_w_in[0]
    n_gate = N_HEADS * N_BRANCH
    wg = w_in[:, attn_d:attn_d + n_gate].reshape(d, N_KV_GROUPS, HEADS_PER_GROUP * N_BRANCH)
    wg = jnp.pad(wg, ((0, 0), (0, 0), (0, GATE_ROWS - HEADS_PER_GROUP * N_BRANCH)))
    waT = jnp.concatenate([w_in[:, :attn_d], w_in[:, attn_d + n_gate:], wg.reshape(d, -1)], axis=1).T.astype(BF16)
    wkv = w_kv.reshape(d, 2 * N_BRANCH, N_KV_GROUPS, HEAD_DIM)
    k_c, v_c, k_s, v_s, k_w, v_w = [wkv[:, i] for i in range(2 * N_BRANCH)]
    flat = lambda w: w.reshape(d, kv_d)
    pad_lanes = lambda w: jnp.pad(w, ((0, 0), (0, 0), (0, LANES - HEAD_DIM))).reshape(d, N_KV_GROUPS * LANES)
    wvT = jnp.concatenate([flat(v_s), flat(v_w)], axis=1).T.astype(BF16)
    wk = jnp.concatenate([flat(k_c), flat(v_c), pad_lanes(k_s), pad_lanes(k_w)], axis=1).astype(BF16)
    cosT, sinT, cosf, sina, sinb = _rope_tables(seq)

    qT, qrT, szT, gT, kvc, ks, kw, vsT, vwT = _nsa_proj(
        h3, b_norm[0], kv_norm, waT, wvT, wk, cosT, sinT, cosf, sina, sinb)

    w1bd = jnp.stack([_blockdiag2(cmp_w1_k), _blockdiag2(cmp_w1_v)]).astype(BF16)
    pos2 = jnp.stack([jnp.tile(cmp_pos_k, (1, 2)), jnp.tile(cmp_pos_v, (1, 2))])
    w2 = jnp.stack([cmp_w2_k, cmp_w2_v])
    w2p = jnp.pad(w2, ((0, 0), (0, 0), (0, LANES - HEAD_DIM))).astype(BF16)
    w2T = jnp.swapaxes(w2, 1, 2).astype(BF16)
    cmp_tm, cmp_fm = _compress(kvc, w1bd, pos2, w2p, w2T)

    yT = _nsa_attn(qT, qrT, gT, szT, cmp_tm[0], cmp_fm[1], ks, kw, vsT, vwT)
    return _nsa_out(yT, h3, b_w_out[0].T.astype(BF16), final_norm)
```

```python
import functools

import jax
import jax.numpy as jnp
from jax import lax
from jax.experimental import pallas as pl
from jax.experimental.pallas import tpu as pltpu

EPS = 1e-6
CONV_WIDTH = 3
N_HEADS = 16
HEAD_DIM = 64
N_KV_GROUPS = 4
HEADS_PER_GROUP = N_HEADS // N_KV_GROUPS
N_BRANCH = 3
ROT_DIM = HEAD_DIM // 4
N_FREQ = ROT_DIM // 2
ROPE_THETA = 500000.0
CMP_BLOCK = 32
CMP_STRIDE = 16
SLC_BLOCK = 64
N_SELECT = 16
N_LOCAL = 2
WINDOW = 512
FORCE_SCORE = 1e4

LANES = 128
SUBLANES = 8
KEY_CHUNK = 512
PROJ_TILE = 256
SOFTMAX_ROWS = 32
RANK_SECTION = 16
GATE_ROWS = 16
MASK_BIAS = -1e30
LOG2_E = 1.4426950408889634
VMEM_LIMIT = 56 * 1024 * 1024

BF16 = jnp.bfloat16
F32 = jnp.float32
NT_DIMS = (((1,), (1,)), ((), ()))


def _dot(a, b):
    return jnp.dot(a, b, preferred_element_type=F32)


def _dot_nt(a, b):
    return lax.dot_general(a, b, NT_DIMS, preferred_element_type=F32)


def _sigmoid(x):
    return 1.0 / (1.0 + jnp.exp(-x))


def _inv_rms(x):
    return lax.rsqrt(jnp.mean(x * x, axis=-1, keepdims=True) + EPS)


def _conv_layer_kernel(x_ref, g_ref, win_ref, cw_ref, wout_ref, o_ref, vbuf_ref, *,
                       tm, tiles_per_seq, conv_d, cchunk):
    @pl.when(pl.program_id(0) % tiles_per_seq == 0)
    def _():
        vbuf_ref[0:SUBLANES, :] = jnp.zeros((SUBLANES, conv_d), F32)

    x = x_ref[...]
    hn = (x * _inv_rms(x) * g_ref[...]).astype(BF16)
    acc = jnp.zeros(x.shape, F32)
    for cc in range(conv_d // cchunk):
        cs = cc * cchunk
        b = _dot(hn, win_ref[:, cs:cs + cchunk])
        c = _dot(hn, win_ref[:, conv_d + cs:conv_d + cs + cchunk])
        u = _dot(hn, win_ref[:, 2 * conv_d + cs:2 * conv_d + cs + cchunk])
        z = _dot(hn, win_ref[:, 3 * conv_d + cs:3 * conv_d + cs + cchunk])
        v = c * u
        vbuf_ref[SUBLANES:SUBLANES + tm, cs:cs + cchunk] = v
        v1 = vbuf_ref[SUBLANES - 1:SUBLANES - 1 + tm, cs:cs + cchunk]
        v2 = vbuf_ref[SUBLANES - 2:SUBLANES - 2 + tm, cs:cs + cchunk]
        conv = (cw_ref[0:1, cs:cs + cchunk] * v2 + cw_ref[1:2, cs:cs + cchunk] * v1
                + cw_ref[2:3, cs:cs + cchunk] * v)
        vbuf_ref[0:SUBLANES, cs:cs + cchunk] = v[tm - SUBLANES:tm, :]
        y = b * conv * (z * _sigmoid(z))
        acc = acc + _dot(y.astype(BF16), wout_ref[cs:cs + cchunk, :])
    o_ref[...] = x + acc


def _conv_layer(h, norm_g, w_in, conv_w, w_out, *, seq, tm=512, cchunk=512):
    t, d = h.shape
    conv_d = conv_w.shape[1]
    tm = min(tm, seq)
    const = lambda shape: pl.BlockSpec(shape, lambda i: (0,) * len(shape), pipeline_mode=pl.Buffered(1))
    return pl.pallas_call(
        functools.partial(_conv_layer_kernel, tm=tm, tiles_per_seq=seq // tm, conv_d=conv_d, cchunk=cchunk),
        out_shape=jax.ShapeDtypeStruct((t, d), F32),
        grid=(t // tm,),
        in_specs=[
            pl.BlockSpec((tm, d), lambda i: (i, 0)),
            const((1, d)),
            const((d, 4 * conv_d)),
            const((CONV_WIDTH, conv_d)),
            const((conv_d, d)),
        ],
        out_specs=pl.BlockSpec((tm, d), lambda i: (i, 0)),
        scratch_shapes=[pltpu.VMEM((SUBLANES + tm, conv_d), F32)],
        compiler_params=pltpu.CompilerParams(dimension_semantics=("arbitrary",), vmem_limit_bytes=VMEM_LIMIT),
        name="conv_layer",
    )(h, norm_g.reshape(1, d), w_in, conv_w, w_out)


def _nsa_proj_kernel(h_ref, bn_ref, kn_ref, waT_ref, wvT_ref, wk_ref, cosT_ref, sinT_ref,
                     cosf_ref, sina_ref, sinb_ref,
                     qT_ref, qrT_ref, szT_ref, gT_ref, kvc_ref, ks_ref, kw_ref, vsT_ref, vwT_ref, *, tm, d):
    attn_d = N_HEADS * HEAD_DIM
    kv_d = N_KV_GROUPS * HEAD_DIM
    h = h_ref[0]
    hr = h * _inv_rms(h)
    hq = (hr * bn_ref[...]).astype(BF16)
    hk = (hr * kn_ref[...]).astype(BF16)

    qT = _dot_nt(waT_ref[0:attn_d, :], hq) * (HEAD_DIM ** -0.5 * LOG2_E)
    cosT = cosT_ref[...]
    sinT = sinT_ref[...]
    for hd in range(N_HEADS):
        r0 = hd * HEAD_DIM
        blk = qT[r0:r0 + HEAD_DIM, :]
        x1 = blk[0:N_FREQ, :]
        x2 = blk[N_FREQ:ROT_DIM, :]
        rot = jnp.concatenate([x1 * cosT - x2 * sinT, x2 * cosT + x1 * sinT, blk[ROT_DIM:, :]], axis=0)
        qT_ref[0, r0:r0 + HEAD_DIM, :] = blk.astype(BF16)
        qrT_ref[0, r0:r0 + HEAD_DIM, :] = rot.astype(BF16)

    zT = _dot_nt(waT_ref[attn_d:2 * attn_d, :], hq)
    szT_ref[0] = zT * _sigmoid(zT)
    gT_ref[0] = _sigmoid(_dot_nt(waT_ref[2 * attn_d:2 * attn_d + N_KV_GROUPS * GATE_ROWS, :], hq))

    vT = _dot_nt(wvT_ref[...], hk)
    vsT_ref[0, 0] = vT[0:kv_d, :].astype(BF16)
    vwT_ref[0, 0] = vT[kv_d:2 * kv_d, :].astype(BF16)

    kk = _dot(hk, wk_ref[...])
    kvc_ref[0] = kk[:, 0:2 * kv_d]
    cosf = cosf_ref[...]
    sina = sina_ref[...]
    sinb = sinb_ref[...]
    row = lax.broadcasted_iota(jnp.int32, (tm, LANES), 0)
    lane = lax.broadcasted_iota(jnp.int32, (tm, LANES), 1)
    tok_blk = (pl.program_id(1) * tm + row) // SLC_BLOCK
    onehot = jnp.where((lane >= HEAD_DIM) & (lane - HEAD_DIM == tok_blk), 1.0, 0.0)
    for j in range(2 * N_KV_GROUPS):
        c0 = 2 * kv_d + j * LANES
        xb = kk[:, c0:c0 + LANES]
        rot = (xb * cosf + pltpu.roll(xb, LANES - N_FREQ, axis=1) * sina
               + pltpu.roll(xb, N_FREQ, axis=1) * sinb)
        if j < N_KV_GROUPS:
            ks_ref[0, :, j * LANES:(j + 1) * LANES] = (rot + onehot).astype(BF16)
        else:
            jj = j - N_KV_GROUPS
            kw_ref[0, :, jj * LANES:(jj + 1) * LANES] = (rot + onehot).astype(BF16)


def _nsa_proj(h3, b_norm, kv_norm, waT, wvT, wk, cosT, sinT, cosf, sina, sinb, *, tm=PROJ_TILE):
    bsz, seq, d = h3.shape
    attn_d = N_HEADS * HEAD_DIM
    kv_d = N_KV_GROUPS * HEAD_DIM
    kpad = N_KV_GROUPS * LANES
    nt = seq // tm
    const = lambda shape: pl.BlockSpec(shape, lambda b, i: (0,) * len(shape))
    fm = lambda rows: pl.BlockSpec((1, rows, tm), lambda b, i: (b, 0, i))
    tmj = lambda cols: pl.BlockSpec((1, tm, cols), lambda b, i: (b, i, 0))
    out_shape = [
        jax.ShapeDtypeStruct((bsz, attn_d, seq), BF16),
        jax.ShapeDtypeStruct((bsz, attn_d, seq), BF16),
        jax.ShapeDtypeStruct((bsz, attn_d, seq), F32),
        jax.ShapeDtypeStruct((bsz, N_KV_GROUPS * GATE_ROWS, seq), F32),
        jax.ShapeDtypeStruct((bsz, seq, 2 * kv_d), F32),
        jax.ShapeDtypeStruct((bsz, seq, kpad), BF16),
        jax.ShapeDtypeStruct((bsz, seq, kpad), BF16),
        jax.ShapeDtypeStruct((bsz, nt, kv_d, tm), BF16),
        jax.ShapeDtypeStruct((bsz, nt, kv_d, tm), BF16),
    ]
    out_specs = [
        fm(attn_d), fm(attn_d), fm(attn_d), fm(N_KV_GROUPS * GATE_ROWS),
        tmj(2 * kv_d), tmj(kpad), tmj(kpad),
        pl.BlockSpec((1, 1, kv_d, tm), lambda b, i: (b, i, 0, 0)),
        pl.BlockSpec((1, 1, kv_d, tm), lambda b, i: (b, i, 0, 0)),
    ]
    return pl.pallas_call(
        functools.partial(_nsa_proj_kernel, tm=tm, d=d),
        out_shape=out_shape,
        grid=(bsz, nt),
        in_specs=[
            tmj(d), const((1, d)), const((1, d)),
            const(waT.shape), const(wvT.shape), const(wk.shape),
            pl.BlockSpec((N_FREQ, tm), lambda b, i: (0, i)),
            pl.BlockSpec((N_FREQ, tm), lambda b, i: (0, i)),
            pl.BlockSpec((tm, LANES), lambda b, i: (i, 0)),
            pl.BlockSpec((tm, LANES), lambda b, i: (i, 0)),
            pl.BlockSpec((tm, LANES), lambda b, i: (i, 0)),
        ],
        out_specs=out_specs,
        compiler_params=pltpu.CompilerParams(dimension_semantics=("parallel", "parallel"),
                                             vmem_limit_bytes=VMEM_LIMIT),
        name="nsa_proj",
    )(h3, b_norm.reshape(1, d), kv_norm.reshape(1, d), waT, wvT, wk, cosT, sinT, cosf, sina, sinb)


def _gelu_tanh(x):
    return x * (0.5 * (1.0 + jnp.tanh(0.7978845608028654 * (x + 0.044715 * (x * x * x)))))


def _compress_kernel(x_ref, w1_ref, pos_ref, w2_ref, w2T_ref, kc_ref, cT_ref, *, nchunk, hidden):
    u = jnp.zeros((nchunk, 2 * hidden), F32)
    v = jnp.zeros((nchunk, 2 * hidden), F32)
    for l in range(CMP_STRIDE):
        xl = x_ref[0, pl.ds(l, nchunk, stride=CMP_STRIDE), :]
        u = u + _dot((xl + pos_ref[0, l:l + 1, :]).astype(BF16), w1_ref[0, l])
        l2 = CMP_STRIDE + l
        v = v + _dot((xl + pos_ref[0, l2:l2 + 1, :]).astype(BF16), w1_ref[0, l2])
    hid = u + pltpu.roll(v, nchunk - 1, axis=0)
    act = _gelu_tanh(hid).astype(BF16)
    rows_ok = lax.broadcasted_iota(jnp.int32, (nchunk, LANES), 0) < nchunk - 1
    cols_ok = lax.broadcasted_iota(jnp.int32, (HEAD_DIM, nchunk), 1) < nchunk - 1
    for p in range(2):
        a = act[:, p * hidden:(p + 1) * hidden]
        kc_ref[0, 0, p] = jnp.where(rows_ok, _dot(a, w2_ref[0]), 0.0).astype(BF16)
        cT_ref[0, 0, p] = jnp.where(cols_ok, _dot_nt(w2T_ref[0], a), 0.0).astype(BF16)


def _compress(kvc, w1bd, pos2, w2p, w2T):
    bsz, seq, _ = kvc.shape
    nchunk = seq // CMP_STRIDE
    hidden = w2T.shape[2]
    return pl.pallas_call(
        functools.partial(_compress_kernel, nchunk=nchunk, hidden=hidden),
        out_shape=[
            jax.ShapeDtypeStruct((2, bsz, N_KV_GROUPS, nchunk, LANES), BF16),
            jax.ShapeDtypeStruct((2, bsz, N_KV_GROUPS, HEAD_DIM, nchunk), BF16),
        ],
        grid=(2, bsz, N_KV_GROUPS // 2),
        in_specs=[
            pl.BlockSpec((1, seq, LANES), lambda s, b, p: (b, 0, 2 * s + p)),
            pl.BlockSpec((1, CMP_BLOCK, LANES, 2 * hidden), lambda s, b, p: (s, 0, 0, 0)),
            pl.BlockSpec((1, CMP_BLOCK, LANES), lambda s, b, p: (s, 0, 0)),
            pl.BlockSpec((1, hidden, LANES), lambda s, b, p: (s, 0, 0)),
            pl.BlockSpec((1, HEAD_DIM, hidden), lambda s, b, p: (s, 0, 0)),
        ],
        out_specs=[
            pl.BlockSpec((1, 1, 2, nchunk, LANES), lambda s, b, p: (s, b, p, 0, 0)),
            pl.BlockSpec((1, 1, 2, HEAD_DIM, nchunk), lambda s, b, p: (s, b, p, 0, 0)),
        ],
        compiler_params=pltpu.CompilerParams(dimension_semantics=("parallel", "parallel", "parallel"),
                                             vmem_limit_bytes=VMEM_LIMIT),
        name="compress",
    )(kvc, w1bd, pos2, w2p, w2T)


def _attn_kernel(q_ref, qr_ref, g_ref, sz_ref, kc_ref, vcT_ref, ks_ref, kw_ref, vsT_ref, vwT_ref, y_ref,
                 p_sc, sc_sc, rank_sc, oc_sc, qa_sc, c_sc, w_sc, s_sc, pb_sc, smax_sc, st_sc, acc_sc, wmax_sc,
                 wp_sc, ow_sc, *, tq, n_cmp):
    qi = pl.program_id(2)
    t0 = qi * tq
    neg_inf = -jnp.inf
    tvec = t0 + lax.broadcasted_iota(jnp.int32, (1, tq), 1)

    kc = kc_ref[0, 0, :, 0:HEAD_DIM]
    cmp_end = lax.broadcasted_iota(jnp.int32, (n_cmp, tq), 0) * CMP_STRIDE + (CMP_BLOCK - 1)
    cmask = cmp_end <= tvec
    heads = range(HEADS_PER_GROUP)

    def fold8(x, op, ways=4):
        parts = [None] * ways
        for idx, r in enumerate(range(0, x.shape[0], SUBLANES)):
            slab = x[r:r + SUBLANES, :]
            parts[idx % ways] = slab if parts[idx % ways] is None else op(parts[idx % ways], slab)
        return functools.reduce(op, [p for p in parts if p is not None])

    def key_chunk(ref, c):
        return ref[0, pl.ds(pl.multiple_of(c * KEY_CHUNK, KEY_CHUNK), KEY_CHUNK), :]

    n_sb = n_cmp // 4
    jrow = lax.broadcasted_iota(jnp.int32, (n_sb, tq), 0)
    cur = tvec // SLC_BLOCK
    valid = jrow <= cur

    QA_SEL, QA_DEAD, QA_WIN = 0, 1, 2
    pad_blocks = lambda b: b if n_sb == HEAD_DIM else jnp.concatenate(
        [b, jnp.zeros((HEAD_DIM - n_sb, tq), BF16)], axis=0)

    def set_operand(idx, bias):
        for hh in heads:
            r0 = hh * HEAD_DIM
            qa_sc[idx, hh] = jnp.concatenate([qr_ref[0, r0:r0 + HEAD_DIM, :], pad_blocks(bias)], axis=0)

    win_blocks = WINDOW // SLC_BLOCK
    set_operand(QA_WIN, jnp.where((jrow >= cur - win_blocks) & valid, 0.0, MASK_BIAS).astype(BF16))
    set_operand(QA_DEAD, jnp.full((n_sb, tq), MASK_BIAS, BF16))

    blk_row = lax.broadcasted_iota(jnp.int32, (SLC_BLOCK, LANES), 0)
    blk_lane = lax.broadcasted_iota(jnp.int32, (SLC_BLOCK, LANES), 1)
    edge = [blk_row - blk_lane, blk_row - (blk_lane - SLC_BLOCK)]
    off_diag = [blk_lane >= SLC_BLOCK, blk_lane < SLC_BLOCK]

    def mask_diagonal_blocks(s, keep):
        blocks = []
        for bb in range(KEY_CHUNK // SLC_BLOCK):
            rows = s[bb * SLC_BLOCK:(bb + 1) * SLC_BLOCK, :]
            halves = [rows[:, h * LANES:(h + 1) * LANES] for h in range(tq // LANES)]
            h = bb // 2
            halves[h] = jnp.where(off_diag[bb % 2] | keep(edge[bb % 2]), halves[h], neg_inf)
            blocks.append(jnp.concatenate(halves, axis=1))
        return jnp.concatenate(blocks, axis=0)

    win_chunks = (qi, jnp.maximum(qi - 1, 0))
    win_operand = (QA_WIN, jnp.where(qi >= 1, QA_WIN, QA_DEAD))
    win_keep = (lambda e: e <= 0, lambda e: e > 0)
    n_win = len(win_chunks) * KEY_CHUNK
    for hh in heads:
        r0 = hh * HEAD_DIM
        c_sc[hh] = jnp.where(cmask, _dot(kc, q_ref[0, r0:r0 + HEAD_DIM, :]), neg_inf)
        wmax = None
        for u, c in enumerate(win_chunks):
            s = mask_diagonal_blocks(_dot(key_chunk(kw_ref, c), qa_sc[win_operand[u], hh]), win_keep[u])
            w_sc[hh, u * KEY_CHUNK:(u + 1) * KEY_CHUNK, :] = s
            f = fold8(s, jnp.maximum)
            wmax = f if wmax is None else jnp.maximum(wmax, f)
        wmax_sc[hh] = wmax
    probs = []
    for hh in heads:
        s = c_sc[hh]
        m = jnp.max(fold8(s, jnp.maximum), axis=0, keepdims=True)
        m = jnp.where(m == neg_inf, 0.0, m)
        e = jnp.exp2(s - m)
        den = jnp.sum(fold8(e, jnp.add), axis=0, keepdims=True)
        probs.append(e * (1.0 / jnp.maximum(den, 1e-30)))
    p_grp = functools.reduce(jnp.add, probs)
    for hh in heads:
        r0 = hh * HEAD_DIM
        oc_sc[r0:r0 + HEAD_DIM, :] = _dot(vcT_ref[0, 0], probs[hh].astype(BF16))

    ratio = SLC_BLOCK // CMP_STRIDE
    imp_cols = []
    for c in range(tq // LANES):
        p_sc[c, 0:SUBLANES, :] = jnp.zeros((SUBLANES, LANES), F32)
        p_sc[c, SUBLANES:SUBLANES + n_cmp, :] = p_grp[:, c * LANES:(c + 1) * LANES]
        tap = lambda o: p_sc[c, pl.ds(SUBLANES + o, n_sb, stride=ratio), :]
        imp_cols.append(tap(-1) + 2.0 * (tap(0) + tap(1) + tap(2)) + tap(3))
    imp = jnp.concatenate(imp_cols, axis=1)

    forced = (jrow == 0) | (valid & (jrow > cur - N_LOCAL))
    score = jnp.where(valid, jnp.where(forced, FORCE_SCORE, imp), neg_inf)
    sc_sc[...] = score

    n_valid = (t0 + tq) // SLC_BLOCK
    n_slabs = n_sb // SUBLANES
    slab_row = lax.broadcasted_iota(jnp.int32, (SUBLANES, tq), 0)
    rank_sc[...] = jnp.zeros((n_sb, tq), jnp.int32)
    for first in range(0, n_sb, RANK_SECTION):
        @pl.when(n_valid > max(N_SELECT, first))
        def _():
            slabs = [sc_sc[k * SUBLANES:(k + 1) * SUBLANES, :] for k in range(n_slabs)]
            counts = [rank_sc[k * SUBLANES:(k + 1) * SUBLANES, :] for k in range(n_slabs)]
            for jp in range(first, first + RANK_SECTION):
                sb = sc_sc[jp:jp + 1, :]
                for k in range(n_slabs):
                    if k * SUBLANES > jp:
                        before = sb >= slabs[k]
                    elif k * SUBLANES + SUBLANES - 1 < jp:
                        before = sb > slabs[k]
                    else:
                        before = (sb > slabs[k]) | ((sb == slabs[k]) & (slab_row > jp % SUBLANES))
                    counts[k] = counts[k] + jnp.where(before, 1, 0)
            for k in range(n_slabs):
                rank_sc[k * SUBLANES:(k + 1) * SUBLANES, :] = counts[k]
    set_operand(QA_SEL, jnp.where((rank_sc[...] < N_SELECT) & valid, 0.0, MASK_BIAS).astype(BF16))

    group = KEY_CHUNK
    n_full = qi
    n_double = (n_full + 1) // 2
    last_pos = 2 * n_double

    def chunk_at(pos):
        is_last = pos == last_pos
        dead = jnp.logical_and(pos >= n_full, jnp.logical_not(is_last))
        c = jnp.where(is_last, qi, jnp.minimum(pos, jnp.maximum(n_full - 1, 0)))
        return c, is_last, jnp.where(dead, QA_DEAD, QA_SEL)

    def values_of(ref, c):
        per = KEY_CHUNK // PROJ_TILE
        return jnp.concatenate([ref[0, per * c + v] for v in range(per)], axis=1)

    ST_MAX, ST_SUM, ST_RESCALE = 0, 1, 2
    rows8 = lambda x: jnp.broadcast_to(x, (SUBLANES, tq))
    never = 1 << 20

    def scores_to(slot, pos, hh, may_be_last):
        c, is_last, operand = chunk_at(pos)
        s = _dot(key_chunk(ks_ref, c), qa_sc[operand, hh])
        if may_be_last:
            causal_slack = jnp.where(is_last, 0, never)
            s = mask_diagonal_blocks(s, lambda e: e <= causal_slack)
        s_sc[slot, hh] = s
        smax_sc[slot, hh] = fold8(s, jnp.maximum)

    def accumulate(pos, slot, hh):
        c, _, _ = chunk_at(pos)
        acc_sc[hh] = (acc_sc[hh] * st_sc[ST_RESCALE + slot, hh, 0:1, :]
                      + _dot(values_of(vsT_ref, c), pb_sc[slot, hh]))

    def softmax_group(slot, hh):
        m_old = st_sc[ST_MAX, hh, 0:1, :]
        m_new = jnp.maximum(m_old, jnp.max(smax_sc[slot, hh], axis=0, keepdims=True))
        a = jnp.exp2(m_old - m_new)
        psum = None
        for r in range(0, group, SOFTMAX_ROWS):
            s = s_sc[slot, hh, r:r + SOFTMAX_ROWS, :]
            p = jnp.exp2(s - m_new)
            pb_sc[slot, hh, r:r + SOFTMAX_ROWS, :] = p.astype(BF16)
            f = fold8(p, jnp.add, ways=2)
            psum = f if psum is None else psum + f
        st_sc[ST_SUM, hh] = a * st_sc[ST_SUM, hh] + psum
        st_sc[ST_MAX, hh] = rows8(m_new)
        st_sc[ST_RESCALE + slot, hh] = rows8(a)

    def stage(slot, pos):
        for hh in heads:
            scores_to(1 - slot, pos + 1, hh, may_be_last=(slot == 1))
            accumulate(jnp.maximum(pos - 1, 0), 1 - slot, hh)
            softmax_group(slot, hh)

    def double_trip(d, carry):
        stage(0, 2 * d)
        stage(1, 2 * d + 1)
        return carry

    vw = jnp.concatenate([values_of(vwT_ref, c) for c in win_chunks], axis=1)
    for hh in heads:
        acc_sc[hh] = jnp.zeros((HEAD_DIM, tq), F32)
        pb_sc[1, hh] = jnp.zeros((group, tq), BF16)
        st_sc[ST_MAX, hh] = jnp.full((SUBLANES, tq), neg_inf, F32)
        st_sc[ST_SUM, hh] = jnp.zeros((SUBLANES, tq), F32)
        st_sc[ST_RESCALE + 1, hh] = jnp.ones((SUBLANES, tq), F32)
        scores_to(0, 0, hh, may_be_last=True)
    for hh in heads:
        m = jnp.max(wmax_sc[hh], axis=0, keepdims=True)
        psum = None
        for r in range(0, n_win, SOFTMAX_ROWS):
            p = jnp.exp2(w_sc[hh, r:r + SOFTMAX_ROWS, :] - m)
            wp_sc[hh, r:r + SOFTMAX_ROWS, :] = p.astype(BF16)
            f = fold8(p, jnp.add, ways=2)
            psum = f if psum is None else psum + f
        ow_sc[hh] = _dot(vw, wp_sc[hh]) * (1.0 / jnp.sum(psum, axis=0, keepdims=True))

    lax.fori_loop(0, n_double, double_trip, 0)

    for hh in heads:
        r0 = hh * HEAD_DIM
        accumulate(jnp.maximum(last_pos - 1, 0), 1, hh)
        softmax_group(0, hh)
        accumulate(last_pos, 0, hh)
        o_w = ow_sc[hh]
        o_s = acc_sc[hh] * (1.0 / jnp.sum(st_sc[ST_SUM, hh], axis=0, keepdims=True))

        g0 = g_ref[0, N_BRANCH * hh:N_BRANCH * hh + 1, :]
        g1 = g_ref[0, N_BRANCH * hh + 1:N_BRANCH * hh + 2, :]
        g2 = g_ref[0, N_BRANCH * hh + 2:N_BRANCH * hh + 3, :]
        o = g0 * oc_sc[r0:r0 + HEAD_DIM, :] + g1 * o_s + g2 * o_w
        y_ref[0, r0:r0 + HEAD_DIM, :] = (o * sz_ref[0, r0:r0 + HEAD_DIM, :]).astype(BF16)


def _nsa_attn(qT, qrT, gT, szT, kcmp, vcmpT, ks, kw, vsT, vwT, *, tq=KEY_CHUNK):
    bsz, attn_d, seq = qT.shape
    n_cmp = kcmp.shape[2]
    n_win = 2 * KEY_CHUNK
    gd = HEADS_PER_GROUP * HEAD_DIM
    assert tq == KEY_CHUNK == WINDOW and KEY_CHUNK % PROJ_TILE == 0 and vsT.shape[3] == PROJ_TILE
    qspec = pl.BlockSpec((1, gd, tq), lambda b, g, i: (b, g, i))
    kspec = pl.BlockSpec((1, seq, LANES), lambda b, g, i: (b, 0, g))
    vspec = pl.BlockSpec((1, seq // PROJ_TILE, HEAD_DIM, PROJ_TILE), lambda b, g, i: (b, 0, g, 0))
    return pl.pallas_call(
        functools.partial(_attn_kernel, tq=tq, n_cmp=n_cmp),
        out_shape=jax.ShapeDtypeStruct((bsz, attn_d, seq), BF16),
        grid=(bsz, N_KV_GROUPS, seq // tq),
        in_specs=[
            qspec, qspec,
            pl.BlockSpec((1, GATE_ROWS, tq), lambda b, g, i: (b, g, i)),
            qspec,
            pl.BlockSpec((1, 1, n_cmp, LANES), lambda b, g, i: (b, g, 0, 0)),
            pl.BlockSpec((1, 1, HEAD_DIM, n_cmp), lambda b, g, i: (b, g, 0, 0)),
            kspec, kspec, vspec, vspec,
        ],
        out_specs=qspec,
        scratch_shapes=[
            pltpu.VMEM((tq // LANES, SUBLANES + n_cmp, LANES), F32),
            pltpu.VMEM((n_cmp // 4, tq), F32),
            pltpu.VMEM((n_cmp // 4, tq), jnp.int32),
            pltpu.VMEM((gd, tq), F32),
            pltpu.VMEM((3, HEADS_PER_GROUP, 2 * HEAD_DIM, tq), BF16),
            pltpu.VMEM((HEADS_PER_GROUP, n_cmp, tq), F32),
            pltpu.VMEM((HEADS_PER_GROUP, n_win, tq), F32),
            pltpu.VMEM((2, HEADS_PER_GROUP, KEY_CHUNK, tq), F32),
            pltpu.VMEM((2, HEADS_PER_GROUP, KEY_CHUNK, tq), BF16),
            pltpu.VMEM((2, HEADS_PER_GROUP, SUBLANES, tq), F32),
            pltpu.VMEM((4, HEADS_PER_GROUP, SUBLANES, tq), F32),
            pltpu.VMEM((HEADS_PER_GROUP, HEAD_DIM, tq), F32),
            pltpu.VMEM((HEADS_PER_GROUP, SUBLANES, tq), F32),
            pltpu.VMEM((HEADS_PER_GROUP, n_win, tq), BF16),
            pltpu.VMEM((HEADS_PER_GROUP, HEAD_DIM, tq), F32),
        ],
        compiler_params=pltpu.CompilerParams(dimension_semantics=("parallel", "parallel", "arbitrary"),
                                             vmem_limit_bytes=VMEM_LIMIT),
        name="nsa_attn",
    )(qT, qrT, gT, szT, kcmp, vcmpT, ks, kw, vsT, vwT)


def _nsa_out_kernel(y_ref, h_ref, woT_ref, fn_ref, o_ref):
    oT = _dot(woT_ref[...], y_ref[0])
    h2 = h_ref[0] + oT.T
    o_ref[0] = h2 * _inv_rms(h2) * fn_ref[...]


def _nsa_out(yT, h3, woT, final_norm, *, tm=PROJ_TILE):
    bsz, seq, d = h3.shape
    attn_d = yT.shape[1]
    return pl.pallas_call(
        _nsa_out_kernel,
        out_shape=jax.ShapeDtypeStruct((bsz, seq, d), F32),
        grid=(bsz, seq // tm),
        in_specs=[
            pl.BlockSpec((1, attn_d, tm), lambda b, i: (b, 0, i)),
            pl.BlockSpec((1, tm, d), lambda b, i: (b, i, 0)),
            pl.BlockSpec((d, attn_d), lambda b, i: (0, 0)),
            pl.BlockSpec((1, d), lambda b, i: (0, 0)),
        ],
        out_specs=pl.BlockSpec((1, tm, d), lambda b, i: (b, i, 0)),
        compiler_params=pltpu.CompilerParams(dimension_semantics=("parallel", "parallel"),
                                             vmem_limit_bytes=VMEM_LIMIT),
        name="nsa_out",
    )(yT, h3, woT, final_norm.reshape(1, d))


def _rope_tables(seq):
    pos = jnp.arange(seq, dtype=F32)
    inv = ROPE_THETA ** (-jnp.arange(0, ROT_DIM, 2, dtype=F32) / ROT_DIM)
    ang = pos[:, None] * inv[None, :]
    cos, sin = jnp.cos(ang), jnp.sin(ang)
    z = lambda n: jnp.zeros((seq, n), F32)
    cosf = jnp.concatenate([cos, cos, jnp.ones((seq, LANES - ROT_DIM), F32)], axis=1)
    sina = jnp.concatenate([-sin, z(LANES - N_FREQ)], axis=1)
    sinb = jnp.concatenate([z(N_FREQ), sin, z(LANES - ROT_DIM)], axis=1)
    return cos.T, sin.T, cosf, sina, sinb


def _blockdiag2(w1):
    hidden = w1.shape[1]
    w = w1.reshape(CMP_BLOCK, HEAD_DIM, hidden)
    zero = jnp.zeros_like(w)
    top = jnp.concatenate([w, zero], axis=2)
    bot = jnp.concatenate([zero, w], axis=2)
    return jnp.concatenate([top, bot], axis=1)


def kernel(x, a_norm, a_w_in, a_conv_w, a_w_out, kv_norm, w_kv, cmp_pos_k, cmp_w1_k, cmp_w2_k,
           cmp_pos_v, cmp_w1_v, cmp_w2_v, b_norm, b_w_in, b_w_out, final_norm):
    bsz, seq, d = x.shape
    attn_d = N_HEADS * HEAD_DIM
    kv_d = N_KV_GROUPS * HEAD_DIM
    assert b_norm.shape[0] == 1, "one NSA layer reads the shared K/V side"
    assert seq % KEY_CHUNK == 0

    h = x.reshape(bsz * seq, d)
    for layer in range(a_norm.shape[0]):
        h = _conv_layer(h, a_norm[layer], a_w_in[layer].astype(BF16), a_conv_w[layer],
                        a_w_out[layer].astype(BF16), seq=seq)
    h3 = h.reshape(bsz, seq, d)

    w_in = b_w_in[0]
    n_gate = N_HEADS * N_BRANCH
    wg = w_in[:, attn_d:attn_d + n_gate].reshape(d, N_KV_GROUPS, HEADS_PER_GROUP * N_BRANCH)
    wg = jnp.pad(wg, ((0, 0), (0, 0), (0, GATE_ROWS - HEADS_PER_GROUP * N_BRANCH)))
    waT = jnp.concatenate([w_in[:, :attn_d], w_in[:, attn_d + n_gate:], wg.reshape(d, -1)], axis=1).T.astype(BF16)
    wkv = w_kv.reshape(d, 2 * N_BRANCH, N_KV_GROUPS, HEAD_DIM)
    k_c, v_c, k_s, v_s, k_w, v_w = [wkv[:, i] for i in range(2 * N_BRANCH)]
    flat = lambda w: w.reshape(d, kv_d)
    pad_lanes = lambda w: jnp.pad(w, ((0, 0), (0, 0), (0, LANES - HEAD_DIM))).reshape(d, N_KV_GROUPS * LANES)
    wvT = jnp.concatenate([flat(v_s), flat(v_w)], axis=1).T.astype(BF16)
    wk = jnp.concatenate([flat(k_c), flat(v_c), pad_lanes(k_s), pad_lanes(k_w)], axis=1).astype(BF16)
    cosT, sinT, cosf, sina, sinb = _rope_tables(seq)

    qT, qrT, szT, gT, kvc, ks, kw, vsT, vwT = _nsa_proj(
        h3, b_norm[0], kv_norm, waT, wvT, wk, cosT, sinT, cosf, sina, sinb)

    w1bd = jnp.stack([_blockdiag2(cmp_w1_k), _blockdiag2(cmp_w1_v)]).astype(BF16)
    pos2 = jnp.stack([jnp.tile(cmp_pos_k, (1, 2)), jnp.tile(cmp_pos_v, (1, 2))])
    w2 = jnp.stack([cmp_w2_k, cmp_w2_v])
    w2p = jnp.pad(w2, ((0, 0), (0, 0), (0, LANES - HEAD_DIM))).astype(BF16)
    w2T = jnp.swapaxes(w2, 1, 2).astype(BF16)
    cmp_tm, cmp_fm = _compress(kvc, w1bd, pos2, w2p, w2T)

    yT = _nsa_attn(qT, qrT, gT, szT, cmp_tm[0], cmp_fm[1], ks, kw, vsT, vwT)
    return _nsa_out(yT, h3, b_w_out[0].T.astype(BF16), final_norm)
```

```python
import functools

import jax
import jax.numpy as jnp
from jax import lax
from jax.experimental import pallas as pl
from jax.experimental.pallas import tpu as pltpu

EPS = 1e-6
CONV_WIDTH = 3
N_HEADS = 16
HEAD_DIM = 64
N_KV_GROUPS = 4
HEADS_PER_GROUP = N_HEADS // N_KV_GROUPS
N_BRANCH = 3
ROT_DIM = HEAD_DIM // 4
N_FREQ = ROT_DIM // 2
ROPE_THETA = 500000.0
CMP_BLOCK = 32
CMP_STRIDE = 16
SLC_BLOCK = 64
N_SELECT = 16
N_LOCAL = 2
WINDOW = 512
FORCE_SCORE = 1e4

LANES = 128
SUBLANES = 8
KEY_CHUNK = 512
PROJ_TILE = 256
SOFTMAX_ROWS = 32
RANK_SECTION = 16
GATE_ROWS = 16
V_ROWS = HEAD_DIM + 16
MASK_BIAS = -1e30
LOG2_E = 1.4426950408889634
VMEM_LIMIT = 56 * 1024 * 1024

BF16 = jnp.bfloat16
F32 = jnp.float32
NT_DIMS = (((1,), (1,)), ((), ()))


def _dot(a, b):
    return jnp.dot(a, b, preferred_element_type=F32)


def _dot_nt(a, b):
    return lax.dot_general(a, b, NT_DIMS, preferred_element_type=F32)


def _sigmoid(x):
    return 1.0 / (1.0 + jnp.exp(-x))


def _inv_rms(x):
    return lax.rsqrt(jnp.mean(x * x, axis=-1, keepdims=True) + EPS)


def _conv_layer_kernel(x_ref, g_ref, win_ref, cw_ref, wout_ref, o_ref, vbuf_ref, *,
                       tm, tiles_per_seq, conv_d, cchunk):
    @pl.when(pl.program_id(0) % tiles_per_seq == 0)
    def _():
        vbuf_ref[0:SUBLANES, :] = jnp.zeros((SUBLANES, conv_d), F32)

    x = x_ref[...]
    hn = (x * _inv_rms(x) * g_ref[...]).astype(BF16)
    acc = jnp.zeros(x.shape, F32)
    for cc in range(conv_d // cchunk):
        cs = cc * cchunk
        b = _dot(hn, win_ref[:, cs:cs + cchunk])
        c = _dot(hn, win_ref[:, conv_d + cs:conv_d + cs + cchunk])
        u = _dot(hn, win_ref[:, 2 * conv_d + cs:2 * conv_d + cs + cchunk])
        z = _dot(hn, win_ref[:, 3 * conv_d + cs:3 * conv_d + cs + cchunk])
        v = c * u
        vbuf_ref[SUBLANES:SUBLANES + tm, cs:cs + cchunk] = v
        v1 = vbuf_ref[SUBLANES - 1:SUBLANES - 1 + tm, cs:cs + cchunk]
        v2 = vbuf_ref[SUBLANES - 2:SUBLANES - 2 + tm, cs:cs + cchunk]
        conv = (cw_ref[0:1, cs:cs + cchunk] * v2 + cw_ref[1:2, cs:cs + cchunk] * v1
                + cw_ref[2:3, cs:cs + cchunk] * v)
        vbuf_ref[0:SUBLANES, cs:cs + cchunk] = v[tm - SUBLANES:tm, :]
        y = b * conv * (z * _sigmoid(z))
        acc = acc + _dot(y.astype(BF16), wout_ref[cs:cs + cchunk, :])
    o_ref[...] = x + acc


def _conv_layer(h, norm_g, w_in, conv_w, w_out, *, seq, tm=512, cchunk=512):
    t, d = h.shape
    conv_d = conv_w.shape[1]
    tm = min(tm, seq)
    const = lambda shape: pl.BlockSpec(shape, lambda i: (0,) * len(shape), pipeline_mode=pl.Buffered(1))
    return pl.pallas_call(
        functools.partial(_conv_layer_kernel, tm=tm, tiles_per_seq=seq // tm, conv_d=conv_d, cchunk=cchunk),
        out_shape=jax.ShapeDtypeStruct((t, d), F32),
        grid=(t // tm,),
        in_specs=[
            pl.BlockSpec((tm, d), lambda i: (i, 0)),
            const((1, d)),
            const((d, 4 * conv_d)),
            const((CONV_WIDTH, conv_d)),
            const((conv_d, d)),
        ],
        out_specs=pl.BlockSpec((tm, d), lambda i: (i, 0)),
        scratch_shapes=[pltpu.VMEM((SUBLANES + tm, conv_d), F32)],
        compiler_params=pltpu.CompilerParams(dimension_semantics=("arbitrary",), vmem_limit_bytes=VMEM_LIMIT),
        name="conv_layer",
    )(h, norm_g.reshape(1, d), w_in, conv_w, w_out)


def _nsa_proj_kernel(h_ref, bn_ref, kn_ref, waT_ref, wvT_ref, wk_ref, cosT_ref, sinT_ref,
                     cosf_ref, sina_ref, sinb_ref,
                     qT_ref, qrT_ref, szT_ref, gT_ref, kvc_ref, ks_ref, kw_ref, vsT_ref, vwT_ref, *, tm, d):
    attn_d = N_HEADS * HEAD_DIM
    kv_d = N_KV_GROUPS * HEAD_DIM
    h = h_ref[0]
    hr = h * _inv_rms(h)
    hq = (hr * bn_ref[...]).astype(BF16)
    hk = (hr * kn_ref[...]).astype(BF16)

    qT = _dot_nt(waT_ref[0:attn_d, :], hq) * (HEAD_DIM ** -0.5 * LOG2_E)
    cosT = cosT_ref[...]
    sinT = sinT_ref[...]
    for hd in range(N_HEADS):
        r0 = hd * HEAD_DIM
        blk = qT[r0:r0 + HEAD_DIM, :]
        x1 = blk[0:N_FREQ, :]
        x2 = blk[N_FREQ:ROT_DIM, :]
        rot = jnp.concatenate([x1 * cosT - x2 * sinT, x2 * cosT + x1 * sinT, blk[ROT_DIM:, :]], axis=0)
        qT_ref[0, r0:r0 + HEAD_DIM, :] = blk.astype(BF16)
        qrT_ref[0, r0:r0 + HEAD_DIM, :] = rot.astype(BF16)

    zT = _dot_nt(waT_ref[attn_d:2 * attn_d, :], hq)
    szT_ref[0] = zT * _sigmoid(zT)
    gT_ref[0] = _sigmoid(_dot_nt(waT_ref[2 * attn_d:2 * attn_d + N_KV_GROUPS * GATE_ROWS, :], hq))

    vT = _dot_nt(wvT_ref[...], hk)
    ones_rows = jnp.ones((V_ROWS - HEAD_DIM, tm), BF16)
    for i, ref in enumerate((vsT_ref, vwT_ref)):
        for g in range(N_KV_GROUPS):
            src = i * kv_d + g * HEAD_DIM
            ref[0, 0, g * V_ROWS:g * V_ROWS + HEAD_DIM, :] = vT[src:src + HEAD_DIM, :].astype(BF16)
            ref[0, 0, g * V_ROWS + HEAD_DIM:(g + 1) * V_ROWS, :] = ones_rows

    kk = _dot(hk, wk_ref[...])
    kvc_ref[0] = kk[:, 0:2 * kv_d]
    cosf = cosf_ref[...]
    sina = sina_ref[...]
    sinb = sinb_ref[...]
    row = lax.broadcasted_iota(jnp.int32, (tm, LANES), 0)
    lane = lax.broadcasted_iota(jnp.int32, (tm, LANES), 1)
    tok_blk = (pl.program_id(1) * tm + row) // SLC_BLOCK
    onehot = jnp.where((lane >= HEAD_DIM) & (lane - HEAD_DIM == tok_blk), 1.0, 0.0)
    for j in range(2 * N_KV_GROUPS):
        c0 = 2 * kv_d + j * LANES
        xb = kk[:, c0:c0 + LANES]
        rot = (xb * cosf + pltpu.roll(xb, LANES - N_FREQ, axis=1) * sina
               + pltpu.roll(xb, N_FREQ, axis=1) * sinb)
        if j < N_KV_GROUPS:
            ks_ref[0, :, j * LANES:(j + 1) * LANES] = (rot + onehot).astype(BF16)
        else:
            jj = j - N_KV_GROUPS
            kw_ref[0, :, jj * LANES:(jj + 1) * LANES] = (rot + onehot).astype(BF16)


def _nsa_proj(h3, b_norm, kv_norm, waT, wvT, wk, cosT, sinT, cosf, sina, sinb, *, tm=PROJ_TILE):
    bsz, seq, d = h3.shape
    attn_d = N_HEADS * HEAD_DIM
    kv_d = N_KV_GROUPS * HEAD_DIM
    kpad = N_KV_GROUPS * LANES
    nt = seq // tm
    const = lambda shape: pl.BlockSpec(shape, lambda b, i: (0,) * len(shape))
    fm = lambda rows: pl.BlockSpec((1, rows, tm), lambda b, i: (b, 0, i))
    tmj = lambda cols: pl.BlockSpec((1, tm, cols), lambda b, i: (b, i, 0))
    out_shape = [
        jax.ShapeDtypeStruct((bsz, attn_d, seq), BF16),
        jax.ShapeDtypeStruct((bsz, attn_d, seq), BF16),
        jax.ShapeDtypeStruct((bsz, attn_d, seq), F32),
        jax.ShapeDtypeStruct((bsz, N_KV_GROUPS * GATE_ROWS, seq), F32),
        jax.ShapeDtypeStruct((bsz, seq, 2 * kv_d), F32),
        jax.ShapeDtypeStruct((bsz, seq, kpad), BF16),
        jax.ShapeDtypeStruct((bsz, seq, kpad), BF16),
        jax.ShapeDtypeStruct((bsz, nt, N_KV_GROUPS * V_ROWS, tm), BF16),
        jax.ShapeDtypeStruct((bsz, nt, N_KV_GROUPS * V_ROWS, tm), BF16),
    ]
    out_specs = [
        fm(attn_d), fm(attn_d), fm(attn_d), fm(N_KV_GROUPS * GATE_ROWS),
        tmj(2 * kv_d), tmj(kpad), tmj(kpad),
        pl.BlockSpec((1, 1, N_KV_GROUPS * V_ROWS, tm), lambda b, i: (b, i, 0, 0)),
        pl.BlockSpec((1, 1, N_KV_GROUPS * V_ROWS, tm), lambda b, i: (b, i, 0, 0)),
    ]
    return pl.pallas_call(
        functools.partial(_nsa_proj_kernel, tm=tm, d=d),
        out_shape=out_shape,
        grid=(bsz, nt),
        in_specs=[
            tmj(d), const((1, d)), const((1, d)),
            const(waT.shape), const(wvT.shape), const(wk.shape),
            pl.BlockSpec((N_FREQ, tm), lambda b, i: (0, i)),
            pl.BlockSpec((N_FREQ, tm), lambda b, i: (0, i)),
            pl.BlockSpec((tm, LANES), lambda b, i: (i, 0)),
            pl.BlockSpec((tm, LANES), lambda b, i: (i, 0)),
            pl.BlockSpec((tm, LANES), lambda b, i: (i, 0)),
        ],
        out_specs=out_specs,
        compiler_params=pltpu.CompilerParams(dimension_semantics=("parallel", "parallel"),
                                             vmem_limit_bytes=VMEM_LIMIT),
        name="nsa_proj",
    )(h3, b_norm.reshape(1, d), kv_norm.reshape(1, d), waT, wvT, wk, cosT, sinT, cosf, sina, sinb)


def _gelu_tanh(x):
    return x * (0.5 * (1.0 + jnp.tanh(0.7978845608028654 * (x + 0.044715 * (x * x * x)))))


def _compress_kernel(x_ref, w1_ref, pos_ref, w2_ref, w2T_ref, kc_ref, cT_ref, *, nchunk, hidden):
    u = jnp.zeros((nchunk, 2 * hidden), F32)
    v = jnp.zeros((nchunk, 2 * hidden), F32)
    for l in range(CMP_STRIDE):
        xl = x_ref[0, pl.ds(l, nchunk, stride=CMP_STRIDE), :]
        u = u + _dot((xl + pos_ref[0, l:l + 1, :]).astype(BF16), w1_ref[0, l])
        l2 = CMP_STRIDE + l
        v = v + _dot((xl + pos_ref[0, l2:l2 + 1, :]).astype(BF16), w1_ref[0, l2])
    hid = u + pltpu.roll(v, nchunk - 1, axis=0)
    act = _gelu_tanh(hid).astype(BF16)
    rows_ok = lax.broadcasted_iota(jnp.int32, (nchunk, LANES), 0) < nchunk - 1
    cols_ok = lax.broadcasted_iota(jnp.int32, (HEAD_DIM, nchunk), 1) < nchunk - 1
    for p in range(2):
        a = act[:, p * hidden:(p + 1) * hidden]
        kc_ref[0, 0, p] = jnp.where(rows_ok, _dot(a, w2_ref[0]), 0.0).astype(BF16)
        cT_ref[0, 0, p] = jnp.where(cols_ok, _dot_nt(w2T_ref[0], a), 0.0).astype(BF16)


def _compress(kvc, w1bd, pos2, w2p, w2T):
    bsz, seq, _ = kvc.shape
    nchunk = seq // CMP_STRIDE
    hidden = w2T.shape[2]
    return pl.pallas_call(
        functools.partial(_compress_kernel, nchunk=nchunk, hidden=hidden),
        out_shape=[
            jax.ShapeDtypeStruct((2, bsz, N_KV_GROUPS, nchunk, LANES), BF16),
            jax.ShapeDtypeStruct((2, bsz, N_KV_GROUPS, HEAD_DIM, nchunk), BF16),
        ],
        grid=(2, bsz, N_KV_GROUPS // 2),
        in_specs=[
            pl.BlockSpec((1, seq, LANES), lambda s, b, p: (b, 0, 2 * s + p)),
            pl.BlockSpec((1, CMP_BLOCK, LANES, 2 * hidden), lambda s, b, p: (s, 0, 0, 0)),
            pl.BlockSpec((1, CMP_BLOCK, LANES), lambda s, b, p: (s, 0, 0)),
            pl.BlockSpec((1, hidden, LANES), lambda s, b, p: (s, 0, 0)),
            pl.BlockSpec((1, HEAD_DIM, hidden), lambda s, b, p: (s, 0, 0)),
        ],
        out_specs=[
            pl.BlockSpec((1, 1, 2, nchunk, LANES), lambda s, b, p: (s, b, p, 0, 0)),
            pl.BlockSpec((1, 1, 2, HEAD_DIM, nchunk), lambda s, b, p: (s, b, p, 0, 0)),
        ],
        compiler_params=pltpu.CompilerParams(dimension_semantics=("parallel", "parallel", "parallel"),
                                             vmem_limit_bytes=VMEM_LIMIT),
        name="compress",
    )(kvc, w1bd, pos2, w2p, w2T)


def _attn_kernel(q_ref, qr_ref, g_ref, sz_ref, kc_ref, vcT_ref, ks_ref, kw_ref, vsT_ref, vwT_ref, y_ref,
                 p_sc, sc_sc, rank_sc, oc_sc, qa_sc, c_sc, w_sc, s_sc, pb_sc, smax_sc, st_sc, acc_sc, wmax_sc,
                 wp_sc, ow_sc, *, tq, n_cmp):
    qi = pl.program_id(2)
    t0 = qi * tq
    neg_inf = -jnp.inf
    tvec = t0 + lax.broadcasted_iota(jnp.int32, (1, tq), 1)

    kc = kc_ref[0, 0, :, 0:HEAD_DIM]
    cmp_end = lax.broadcasted_iota(jnp.int32, (n_cmp, tq), 0) * CMP_STRIDE + (CMP_BLOCK - 1)
    cmask = cmp_end <= tvec
    heads = range(HEADS_PER_GROUP)

    def fold8(x, op, ways=4):
        parts = [None] * ways
        for idx, r in enumerate(range(0, x.shape[0], SUBLANES)):
            slab = x[r:r + SUBLANES, :]
            parts[idx % ways] = slab if parts[idx % ways] is None else op(parts[idx % ways], slab)
        return functools.reduce(op, [p for p in parts if p is not None])

    def key_chunk(ref, c):
        return ref[0, pl.ds(pl.multiple_of(c * KEY_CHUNK, KEY_CHUNK), KEY_CHUNK), :]

    n_sb = n_cmp // 4
    jrow = lax.broadcasted_iota(jnp.int32, (n_sb, tq), 0)
    cur = tvec // SLC_BLOCK
    valid = jrow <= cur

    QA_SEL, QA_DEAD, QA_WIN = 0, 1, 2
    pad_blocks = lambda b: b if n_sb == HEAD_DIM else jnp.concatenate(
        [b, jnp.zeros((HEAD_DIM - n_sb, tq), BF16)], axis=0)

    def set_operand(idx, bias):
        for hh in heads:
            r0 = hh * HEAD_DIM
            qa_sc[idx, hh] = jnp.concatenate([qr_ref[0, r0:r0 + HEAD_DIM, :], pad_blocks(bias)], axis=0)

    win_blocks = WINDOW // SLC_BLOCK
    set_operand(QA_WIN, jnp.where((jrow >= cur - win_blocks) & valid, 0.0, MASK_BIAS).astype(BF16))
    set_operand(QA_DEAD, jnp.full((n_sb, tq), MASK_BIAS, BF16))

    blk_row = lax.broadcasted_iota(jnp.int32, (SLC_BLOCK, LANES), 0)
    blk_lane = lax.broadcasted_iota(jnp.int32, (SLC_BLOCK, LANES), 1)
    edge = [blk_row - blk_lane, blk_row - (blk_lane - SLC_BLOCK)]
    off_diag = [blk_lane >= SLC_BLOCK, blk_lane < SLC_BLOCK]

    def mask_diagonal_blocks(s, keep):
        blocks = []
        for bb in range(KEY_CHUNK // SLC_BLOCK):
            rows = s[bb * SLC_BLOCK:(bb + 1) * SLC_BLOCK, :]
            halves = [rows[:, h * LANES:(h + 1) * LANES] for h in range(tq // LANES)]
            h = bb // 2
            halves[h] = jnp.where(off_diag[bb % 2] | keep(edge[bb % 2]), halves[h], neg_inf)
            blocks.append(jnp.concatenate(halves, axis=1))
        return jnp.concatenate(blocks, axis=0)

    win_chunks = (qi, jnp.maximum(qi - 1, 0))
    win_operand = (QA_WIN, jnp.where(qi >= 1, QA_WIN, QA_DEAD))
    win_keep = (lambda e: e <= 0, lambda e: e > 0)
    n_win = len(win_chunks) * KEY_CHUNK
    for hh in heads:
        r0 = hh * HEAD_DIM
        c_sc[hh] = jnp.where(cmask, _dot(kc, q_ref[0, r0:r0 + HEAD_DIM, :]), neg_inf)
        wmax = None
        for u, c in enumerate(win_chunks):
            s = mask_diagonal_blocks(_dot(key_chunk(kw_ref, c), qa_sc[win_operand[u], hh]), win_keep[u])
            w_sc[hh, u * KEY_CHUNK:(u + 1) * KEY_CHUNK, :] = s
            f = fold8(s, jnp.maximum)
            wmax = f if wmax is None else jnp.maximum(wmax, f)
        wmax_sc[hh] = wmax
    probs = []
    for hh in heads:
        s = c_sc[hh]
        m = jnp.max(fold8(s, jnp.maximum), axis=0, keepdims=True)
        m = jnp.where(m == neg_inf, 0.0, m)
        e = jnp.exp2(s - m)
        den = jnp.sum(fold8(e, jnp.add), axis=0, keepdims=True)
        probs.append(e * (1.0 / jnp.maximum(den, 1e-30)))
    p_grp = functools.reduce(jnp.add, probs)
    for hh in heads:
        r0 = hh * HEAD_DIM
        oc_sc[r0:r0 + HEAD_DIM, :] = _dot(vcT_ref[0, 0], probs[hh].astype(BF16))

    ratio = SLC_BLOCK // CMP_STRIDE
    imp_cols = []
    for c in range(tq // LANES):
        p_sc[c, 0:SUBLANES, :] = jnp.zeros((SUBLANES, LANES), F32)
        p_sc[c, SUBLANES:SUBLANES + n_cmp, :] = p_grp[:, c * LANES:(c + 1) * LANES]
        tap = lambda o: p_sc[c, pl.ds(SUBLANES + o, n_sb, stride=ratio), :]
        imp_cols.append(tap(-1) + 2.0 * (tap(0) + tap(1) + tap(2)) + tap(3))
    imp = jnp.concatenate(imp_cols, axis=1)

    forced = (jrow == 0) | (valid & (jrow > cur - N_LOCAL))
    score = jnp.where(valid, jnp.where(forced, FORCE_SCORE, imp), neg_inf)
    sc_sc[...] = score

    n_valid = (t0 + tq) // SLC_BLOCK
    n_slabs = n_sb // SUBLANES
    slab_row = lax.broadcasted_iota(jnp.int32, (SUBLANES, tq), 0)
    rank_sc[...] = jnp.zeros((n_sb, tq), jnp.int32)
    for first in range(0, n_sb, RANK_SECTION):
        @pl.when(n_valid > max(N_SELECT, first))
        def _():
            slabs = [sc_sc[k * SUBLANES:(k + 1) * SUBLANES, :] for k in range(n_slabs)]
            counts = [rank_sc[k * SUBLANES:(k + 1) * SUBLANES, :] for k in range(n_slabs)]
            for jp in range(first, first + RANK_SECTION):
                sb = sc_sc[jp:jp + 1, :]
                for k in range(n_slabs):
                    if k * SUBLANES > jp:
                        before = sb >= slabs[k]
                    elif k * SUBLANES + SUBLANES - 1 < jp:
                        before = sb > slabs[k]
                    else:
                        before = (sb > slabs[k]) | ((sb == slabs[k]) & (slab_row > jp % SUBLANES))
                    counts[k] = counts[k] + jnp.where(before, 1, 0)
            for k in range(n_slabs):
                rank_sc[k * SUBLANES:(k + 1) * SUBLANES, :] = counts[k]
    set_operand(QA_SEL, jnp.where((rank_sc[...] < N_SELECT) & valid, 0.0, MASK_BIAS).astype(BF16))

    group = KEY_CHUNK
    n_full = qi
    n_double = (n_full + 1) // 2
    last_pos = 2 * n_double

    def chunk_at(pos):
        is_last = pos == last_pos
        dead = jnp.logical_and(pos >= n_full, jnp.logical_not(is_last))
        c = jnp.where(is_last, qi, jnp.minimum(pos, jnp.maximum(n_full - 1, 0)))
        return c, is_last, jnp.where(dead, QA_DEAD, QA_SEL)

    def values_of(ref, c):
        per = KEY_CHUNK // PROJ_TILE
        return jnp.concatenate([ref[0, per * c + v] for v in range(per)], axis=1)

    ST_MAX, ST_RESCALE = 0, 1
    rows8 = lambda x: jnp.broadcast_to(x, (SUBLANES, tq))
    never = 1 << 20

    def scores_to(slot, pos, hh, may_be_last):
        c, is_last, operand = chunk_at(pos)
        s = _dot(key_chunk(ks_ref, c), qa_sc[operand, hh])
        if may_be_last:
            causal_slack = jnp.where(is_last, 0, never)
            s = mask_diagonal_blocks(s, lambda e: e <= causal_slack)
        s_sc[slot, hh] = s
        smax_sc[slot, hh] = fold8(s, jnp.maximum)

    def accumulate(pos, slot, hh):
        c, _, _ = chunk_at(pos)
        acc_sc[hh] = (acc_sc[hh] * st_sc[ST_RESCALE + slot, hh, 0:1, :]
                      + _dot(values_of(vsT_ref, c), pb_sc[slot, hh]))

    def softmax_group(slot, hh):
        m_old = st_sc[ST_MAX, hh, 0:1, :]
        m_new = jnp.maximum(m_old, jnp.max(smax_sc[slot, hh], axis=0, keepdims=True))
        for r in range(0, group, SOFTMAX_ROWS):
            pb_sc[slot, hh, r:r + SOFTMAX_ROWS, :] = jnp.exp2(
                s_sc[slot, hh, r:r + SOFTMAX_ROWS, :] - m_new).astype(BF16)
        st_sc[ST_MAX, hh] = rows8(m_new)
        st_sc[ST_RESCALE + slot, hh] = rows8(jnp.exp2(m_old - m_new))

    def stage(slot, pos):
        for hh in heads:
            scores_to(1 - slot, pos + 1, hh, may_be_last=(slot == 1))
            accumulate(jnp.maximum(pos - 1, 0), 1 - slot, hh)
            softmax_group(slot, hh)

    def double_trip(d, carry):
        stage(0, 2 * d)
        stage(1, 2 * d + 1)
        return carry

    vw = jnp.concatenate([values_of(vwT_ref, c) for c in win_chunks], axis=1)
    def normalized(acc):
        return acc[0:HEAD_DIM, :] * (1.0 / acc[HEAD_DIM:HEAD_DIM + 1, :])

    for hh in heads:
        acc_sc[hh] = jnp.zeros((V_ROWS, tq), F32)
        pb_sc[1, hh] = jnp.zeros((group, tq), BF16)
        st_sc[ST_MAX, hh] = jnp.full((SUBLANES, tq), neg_inf, F32)
        st_sc[ST_RESCALE + 1, hh] = jnp.ones((SUBLANES, tq), F32)
        scores_to(0, 0, hh, may_be_last=True)
    for hh in heads:
        m = jnp.max(wmax_sc[hh], axis=0, keepdims=True)
        for r in range(0, n_win, SOFTMAX_ROWS):
            wp_sc[hh, r:r + SOFTMAX_ROWS, :] = jnp.exp2(w_sc[hh, r:r + SOFTMAX_ROWS, :] - m).astype(BF16)
        ow_sc[hh] = normalized(_dot(vw, wp_sc[hh]))

    lax.fori_loop(0, n_double, double_trip, 0)

    for hh in heads:
        r0 = hh * HEAD_DIM
        accumulate(jnp.maximum(last_pos - 1, 0), 1, hh)
        softmax_group(0, hh)
        accumulate(last_pos, 0, hh)
        o_w = ow_sc[hh]
        o_s = normalized(acc_sc[hh])

        g0 = g_ref[0, N_BRANCH * hh:N_BRANCH * hh + 1, :]
        g1 = g_ref[0, N_BRANCH * hh + 1:N_BRANCH * hh + 2, :]
        g2 = g_ref[0, N_BRANCH * hh + 2:N_BRANCH * hh + 3, :]
        o = g0 * oc_sc[r0:r0 + HEAD_DIM, :] + g1 * o_s + g2 * o_w
        y_ref[0, r0:r0 + HEAD_DIM, :] = (o * sz_ref[0, r0:r0 + HEAD_DIM, :]).astype(BF16)


def _nsa_attn(qT, qrT, gT, szT, kcmp, vcmpT, ks, kw, vsT, vwT, *, tq=KEY_CHUNK):
    bsz, attn_d, seq = qT.shape
    n_cmp = kcmp.shape[2]
    n_win = 2 * KEY_CHUNK
    gd = HEADS_PER_GROUP * HEAD_DIM
    assert tq == KEY_CHUNK == WINDOW and KEY_CHUNK % PROJ_TILE == 0 and vsT.shape[3] == PROJ_TILE
    qspec = pl.BlockSpec((1, gd, tq), lambda b, g, i: (b, g, i))
    kspec = pl.BlockSpec((1, seq, LANES), lambda b, g, i: (b, 0, g))
    vspec = pl.BlockSpec((1, seq // PROJ_TILE, V_ROWS, PROJ_TILE), lambda b, g, i: (b, 0, g, 0))
    return pl.pallas_call(
        functools.partial(_attn_kernel, tq=tq, n_cmp=n_cmp),
        out_shape=jax.ShapeDtypeStruct((bsz, attn_d, seq), BF16),
        grid=(bsz, N_KV_GROUPS, seq // tq),
        in_specs=[
            qspec, qspec,
            pl.BlockSpec((1, GATE_ROWS, tq), lambda b, g, i: (b, g, i)),
            qspec,
            pl.BlockSpec((1, 1, n_cmp, LANES), lambda b, g, i: (b, g, 0, 0)),
            pl.BlockSpec((1, 1, HEAD_DIM, n_cmp), lambda b, g, i: (b, g, 0, 0)),
            kspec, kspec, vspec, vspec,
        ],
        out_specs=qspec,
        scratch_shapes=[
            pltpu.VMEM((tq // LANES, SUBLANES + n_cmp, LANES), F32),
            pltpu.VMEM((n_cmp // 4, tq), F32),
            pltpu.VMEM((n_cmp // 4, tq), jnp.int32),
            pltpu.VMEM((gd, tq), F32),
            pltpu.VMEM((3, HEADS_PER_GROUP, 2 * HEAD_DIM, tq), BF16),
            pltpu.VMEM((HEADS_PER_GROUP, n_cmp, tq), F32),
            pltpu.VMEM((HEADS_PER_GROUP, n_win, tq), F32),
            pltpu.VMEM((2, HEADS_PER_GROUP, KEY_CHUNK, tq), F32),
            pltpu.VMEM((2, HEADS_PER_GROUP, KEY_CHUNK, tq), BF16),
            pltpu.VMEM((2, HEADS_PER_GROUP, SUBLANES, tq), F32),
            pltpu.VMEM((3, HEADS_PER_GROUP, SUBLANES, tq), F32),
            pltpu.VMEM((HEADS_PER_GROUP, V_ROWS, tq), F32),
            pltpu.VMEM((HEADS_PER_GROUP, SUBLANES, tq), F32),
            pltpu.VMEM((HEADS_PER_GROUP, n_win, tq), BF16),
            pltpu.VMEM((HEADS_PER_GROUP, HEAD_DIM, tq), F32),
        ],
        compiler_params=pltpu.CompilerParams(dimension_semantics=("parallel", "parallel", "arbitrary"),
                                             vmem_limit_bytes=VMEM_LIMIT),
        name="nsa_attn",
    )(qT, qrT, gT, szT, kcmp, vcmpT, ks, kw, vsT, vwT)


def _nsa_out_kernel(y_ref, h_ref, woT_ref, fn_ref, o_ref):
    oT = _dot(woT_ref[...], y_ref[0])
    h2 = h_ref[0] + oT.T
    o_ref[0] = h2 * _inv_rms(h2) * fn_ref[...]


def _nsa_out(yT, h3, woT, final_norm, *, tm=PROJ_TILE):
    bsz, seq, d = h3.shape
    attn_d = yT.shape[1]
    return pl.pallas_call(
        _nsa_out_kernel,
        out_shape=jax.ShapeDtypeStruct((bsz, seq, d), F32),
        grid=(bsz, seq // tm),
        in_specs=[
            pl.BlockSpec((1, attn_d, tm), lambda b, i: (b, 0, i)),
            pl.BlockSpec((1, tm, d), lambda b, i: (b, i, 0)),
            pl.BlockSpec((d, attn_d), lambda b, i: (0, 0)),
            pl.BlockSpec((1, d), lambda b, i: (0, 0)),
        ],
        out_specs=pl.BlockSpec((1, tm, d), lambda b, i: (b, i, 0)),
        compiler_params=pltpu.CompilerParams(dimension_semantics=("parallel", "parallel"),
                                             vmem_limit_bytes=VMEM_LIMIT),
        name="nsa_out",
    )(yT, h3, woT, final_norm.reshape(1, d))


def _rope_tables(seq):
    pos = jnp.arange(seq, dtype=F32)
    inv = ROPE_THETA ** (-jnp.arange(0, ROT_DIM, 2, dtype=F32) / ROT_DIM)
    ang = pos[:, None] * inv[None, :]
    cos, sin = jnp.cos(ang), jnp.sin(ang)
    z = lambda n: jnp.zeros((seq, n), F32)
    cosf = jnp.concatenate([cos, cos, jnp.ones((seq, LANES - ROT_DIM), F32)], axis=1)
    sina = jnp.concatenate([-sin, z(LANES - N_FREQ)], axis=1)
    sinb = jnp.concatenate([z(N_FREQ), sin, z(LANES - ROT_DIM)], axis=1)
    return cos.T, sin.T, cosf, sina, sinb


def _blockdiag2(w1):
    hidden = w1.shape[1]
    w = w1.reshape(CMP_BLOCK, HEAD_DIM, hidden)
    zero = jnp.zeros_like(w)
    top = jnp.concatenate([w, zero], axis=2)
    bot = jnp.concatenate([zero, w], axis=2)
    return jnp.concatenate([top, bot], axis=1)


def kernel(x, a_norm, a_w_in, a_conv_w, a_w_out, kv_norm, w_kv, cmp_pos_k, cmp_w1_k, cmp_w2_k,
           cmp_pos_v, cmp_w1_v, cmp_w2_v, b_norm, b_w_in, b_w_out, final_norm):
    bsz, seq, d = x.shape
    attn_d = N_HEADS * HEAD_DIM
    kv_d = N_KV_GROUPS * HEAD_DIM
    assert b_norm.shape[0] == 1, "one NSA layer reads the shared K/V side"
    assert seq % KEY_CHUNK == 0

    h = x.reshape(bsz * seq, d)
    for layer in range(a_norm.shape[0]):
        h = _conv_layer(h, a_norm[layer], a_w_in[layer].astype(BF16), a_conv_w[layer],
                        a_w_out[layer].astype(BF16), seq=seq)
    h3 = h.reshape(bsz, seq, d)

    w_in = b_w_in[0]
    n_gate = N_HEADS * N_BRANCH
    wg = w_in[:, attn_d:attn_d + n_gate].reshape(d, N_KV_GROUPS, HEADS_PER_GROUP * N_BRANCH)
    wg = jnp.pad(wg, ((0, 0), (0, 0), (0, GATE_ROWS - HEADS_PER_GROUP * N_BRANCH)))
    waT = jnp.concatenate([w_in[:, :attn_d], w_in[:, attn_d + n_gate:], wg.reshape(d, -1)], axis=1).T.astype(BF16)
    wkv = w_kv.reshape(d, 2 * N_BRANCH, N_KV_GROUPS, HEAD_DIM)
    k_c, v_c, k_s, v_s, k_w, v_w = [wkv[:, i] for i in range(2 * N_BRANCH)]
    flat = lambda w: w.reshape(d, kv_d)
    pad_lanes = lambda w: jnp.pad(w, ((0, 0), (0, 0), (0, LANES - HEAD_DIM))).reshape(d, N_KV_GROUPS * LANES)
    wvT = jnp.concatenate([flat(v_s), flat(v_w)], axis=1).T.astype(BF16)
    wk = jnp.concatenate([flat(k_c), flat(v_c), pad_lanes(k_s), pad_lanes(k_w)], axis=1).astype(BF16)
    cosT, sinT, cosf, sina, sinb = _rope_tables(seq)

    qT, qrT, szT, gT, kvc, ks, kw, vsT, vwT = _nsa_proj(
        h3, b_norm[0], kv_norm, waT, wvT, wk, cosT, sinT, cosf, sina, sinb)

    w1bd = jnp.stack([_blockdiag2(cmp_w1_k), _blockdiag2(cmp_w1_v)]).astype(BF16)
    pos2 = jnp.stack([jnp.tile(cmp_pos_k, (1, 2)), jnp.tile(cmp_pos_v, (1, 2))])
    w2 = jnp.stack([cmp_w2_k, cmp_w2_v])
    w2p = jnp.pad(w2, ((0, 0), (0, 0), (0, LANES - HEAD_DIM))).astype(BF16)
    w2T = jnp.swapaxes(w2, 1, 2).astype(BF16)
    cmp_tm, cmp_fm = _compress(kvc, w1bd, pos2, w2p, w2T)

    yT = _nsa_attn(qT, qrT, gT, szT, cmp_tm[0], cmp_fm[1], ks, kw, vsT, vwT)
    return _nsa_out(yT, h3, b_w_out[0].T.astype(BF16), final_norm)
```

```python
import functools

import jax
import jax.numpy as jnp
from jax import lax
from jax.experimental import pallas as pl
from jax.experimental.pallas import tpu as pltpu

EPS = 1e-6
CONV_WIDTH = 3
N_HEADS = 16
HEAD_DIM = 64
N_KV_GROUPS = 4
HEADS_PER_GROUP = N_HEADS // N_KV_GROUPS
N_BRANCH = 3
ROT_DIM = HEAD_DIM // 4
N_FREQ = ROT_DIM // 2
ROPE_THETA = 500000.0
CMP_BLOCK = 32
CMP_STRIDE = 16
SLC_BLOCK = 64
N_SELECT = 16
N_LOCAL = 2
WINDOW = 512
FORCE_SCORE = 1e4

LANES = 128
SUBLANES = 8
KEY_CHUNK = 512
HALF = KEY_CHUNK // 2
PROJ_TILE = 256
SOFTMAX_ROWS = 32
RANK_SECTION = 16
GATE_ROWS = 16
MASK_BIAS = -1e30
LOG2_E = 1.4426950408889634
VMEM_LIMIT = 56 * 1024 * 1024

BF16 = jnp.bfloat16
F32 = jnp.float32
NT_DIMS = (((1,), (1,)), ((), ()))


def _dot(a, b):
    return jnp.dot(a, b, preferred_element_type=F32)


def _dot_nt(a, b):
    return lax.dot_general(a, b, NT_DIMS, preferred_element_type=F32)


def _sigmoid(x):
    return 1.0 / (1.0 + jnp.exp(-x))


def _inv_rms(x):
    return lax.rsqrt(jnp.mean(x * x, axis=-1, keepdims=True) + EPS)


def _conv_layer_kernel(x_ref, g_ref, win_ref, cw_ref, wout_ref, o_ref, vbuf_ref, *,
                       tm, tiles_per_seq, conv_d, cchunk):
    @pl.when(pl.program_id(0) % tiles_per_seq == 0)
    def _():
        vbuf_ref[0:SUBLANES, :] = jnp.zeros((SUBLANES, conv_d), F32)

    x = x_ref[...]
    hn = (x * _inv_rms(x) * g_ref[...]).astype(BF16)
    acc = jnp.zeros(x.shape, F32)
    for cc in range(conv_d // cchunk):
        cs = cc * cchunk
        b = _dot(hn, win_ref[:, cs:cs + cchunk])
        c = _dot(hn, win_ref[:, conv_d + cs:conv_d + cs + cchunk])
        u = _dot(hn, win_ref[:, 2 * conv_d + cs:2 * conv_d + cs + cchunk])
        z = _dot(hn, win_ref[:, 3 * conv_d + cs:3 * conv_d + cs + cchunk])
        v = c * u
        vbuf_ref[SUBLANES:SUBLANES + tm, cs:cs + cchunk] = v
        v1 = vbuf_ref[SUBLANES - 1:SUBLANES - 1 + tm, cs:cs + cchunk]
        v2 = vbuf_ref[SUBLANES - 2:SUBLANES - 2 + tm, cs:cs + cchunk]
        conv = (cw_ref[0:1, cs:cs + cchunk] * v2 + cw_ref[1:2, cs:cs + cchunk] * v1
                + cw_ref[2:3, cs:cs + cchunk] * v)
        vbuf_ref[0:SUBLANES, cs:cs + cchunk] = v[tm - SUBLANES:tm, :]
        y = b * conv * (z * _sigmoid(z))
        acc = acc + _dot(y.astype(BF16), wout_ref[cs:cs + cchunk, :])
    o_ref[...] = x + acc


def _conv_layer(h, norm_g, w_in, conv_w, w_out, *, seq, tm=512, cchunk=512):
    t, d = h.shape
    conv_d = conv_w.shape[1]
    tm = min(tm, seq)
    const = lambda shape: pl.BlockSpec(shape, lambda i: (0,) * len(shape), pipeline_mode=pl.Buffered(1))
    return pl.pallas_call(
        functools.partial(_conv_layer_kernel, tm=tm, tiles_per_seq=seq // tm, conv_d=conv_d, cchunk=cchunk),
        out_shape=jax.ShapeDtypeStruct((t, d), F32),
        grid=(t // tm,),
        in_specs=[
            pl.BlockSpec((tm, d), lambda i: (i, 0)),
            const((1, d)),
            const((d, 4 * conv_d)),
            const((CONV_WIDTH, conv_d)),
            const((conv_d, d)),
        ],
        out_specs=pl.BlockSpec((tm, d), lambda i: (i, 0)),
        scratch_shapes=[pltpu.VMEM((SUBLANES + tm, conv_d), F32)],
        compiler_params=pltpu.CompilerParams(dimension_semantics=("arbitrary",), vmem_limit_bytes=VMEM_LIMIT),
        name="conv_layer",
    )(h, norm_g.reshape(1, d), w_in, conv_w, w_out)


def _nsa_proj_kernel(h_ref, bn_ref, kn_ref, waT_ref, wvT_ref, wk_ref, cosT_ref, sinT_ref,
                     cosf_ref, sina_ref, sinb_ref,
                     qT_ref, qrT_ref, szT_ref, gT_ref, kvc_ref, ks_ref, kw_ref, vsT_ref, vwT_ref, *, tm, d):
    attn_d = N_HEADS * HEAD_DIM
    kv_d = N_KV_GROUPS * HEAD_DIM
    h = h_ref[0]
    hr = h * _inv_rms(h)
    hq = (hr * bn_ref[...]).astype(BF16)
    hk = (hr * kn_ref[...]).astype(BF16)

    qT = _dot_nt(waT_ref[0:attn_d, :], hq) * (HEAD_DIM ** -0.5 * LOG2_E)
    cosT = cosT_ref[...]
    sinT = sinT_ref[...]
    for hd in range(N_HEADS):
        r0 = hd * HEAD_DIM
        blk = qT[r0:r0 + HEAD_DIM, :]
        x1 = blk[0:N_FREQ, :]
        x2 = blk[N_FREQ:ROT_DIM, :]
        rot = jnp.concatenate([x1 * cosT - x2 * sinT, x2 * cosT + x1 * sinT, blk[ROT_DIM:, :]], axis=0)
        qT_ref[0, r0:r0 + HEAD_DIM, :] = blk.astype(BF16)
        qrT_ref[0, r0:r0 + HEAD_DIM, :] = rot.astype(BF16)

    zT = _dot_nt(waT_ref[attn_d:2 * attn_d, :], hq)
    szT_ref[0] = zT * _sigmoid(zT)
    gT_ref[0] = _sigmoid(_dot_nt(waT_ref[2 * attn_d:2 * attn_d + N_KV_GROUPS * GATE_ROWS, :], hq))

    vT = _dot_nt(wvT_ref[...], hk)
    vsT_ref[0, 0] = vT[0:kv_d, :].astype(BF16)
    vwT_ref[0, 0] = vT[kv_d:2 * kv_d, :].astype(BF16)

    kk = _dot(hk, wk_ref[...])
    kvc_ref[0] = kk[:, 0:2 * kv_d]
    cosf = cosf_ref[...]
    sina = sina_ref[...]
    sinb = sinb_ref[...]
    row = lax.broadcasted_iota(jnp.int32, (tm, LANES), 0)
    lane = lax.broadcasted_iota(jnp.int32, (tm, LANES), 1)
    tok_blk = (pl.program_id(1) * tm + row) // SLC_BLOCK
    onehot = jnp.where((lane >= HEAD_DIM) & (lane - HEAD_DIM == tok_blk), 1.0, 0.0)
    for j in range(2 * N_KV_GROUPS):
        c0 = 2 * kv_d + j * LANES
        xb = kk[:, c0:c0 + LANES]
        rot = (xb * cosf + pltpu.roll(xb, LANES - N_FREQ, axis=1) * sina
               + pltpu.roll(xb, N_FREQ, axis=1) * sinb)
        if j < N_KV_GROUPS:
            ks_ref[0, :, j * LANES:(j + 1) * LANES] = (rot + onehot).astype(BF16)
        else:
            jj = j - N_KV_GROUPS
            kw_ref[0, :, jj * LANES:(jj + 1) * LANES] = (rot + onehot).astype(BF16)


def _nsa_proj(h3, b_norm, kv_norm, waT, wvT, wk, cosT, sinT, cosf, sina, sinb, *, tm=PROJ_TILE):
    bsz, seq, d = h3.shape
    attn_d = N_HEADS * HEAD_DIM
    kv_d = N_KV_GROUPS * HEAD_DIM
    kpad = N_KV_GROUPS * LANES
    nt = seq // tm
    const = lambda shape: pl.BlockSpec(shape, lambda b, i: (0,) * len(shape))
    fm = lambda rows: pl.BlockSpec((1, rows, tm), lambda b, i: (b, 0, i))
    tmj = lambda cols: pl.BlockSpec((1, tm, cols), lambda b, i: (b, i, 0))
    out_shape = [
        jax.ShapeDtypeStruct((bsz, attn_d, seq), BF16),
        jax.ShapeDtypeStruct((bsz, attn_d, seq), BF16),
        jax.ShapeDtypeStruct((bsz, attn_d, seq), F32),
        jax.ShapeDtypeStruct((bsz, N_KV_GROUPS * GATE_ROWS, seq), F32),
        jax.ShapeDtypeStruct((bsz, seq, 2 * kv_d), F32),
        jax.ShapeDtypeStruct((bsz, seq, kpad), BF16),
        jax.ShapeDtypeStruct((bsz, seq, kpad), BF16),
        jax.ShapeDtypeStruct((bsz, nt, kv_d, tm), BF16),
        jax.ShapeDtypeStruct((bsz, nt, kv_d, tm), BF16),
    ]
    out_specs = [
        fm(attn_d), fm(attn_d), fm(attn_d), fm(N_KV_GROUPS * GATE_ROWS),
        tmj(2 * kv_d), tmj(kpad), tmj(kpad),
        pl.BlockSpec((1, 1, kv_d, tm), lambda b, i: (b, i, 0, 0)),
        pl.BlockSpec((1, 1, kv_d, tm), lambda b, i: (b, i, 0, 0)),
    ]
    return pl.pallas_call(
        functools.partial(_nsa_proj_kernel, tm=tm, d=d),
        out_shape=out_shape,
        grid=(bsz, nt),
        in_specs=[
            tmj(d), const((1, d)), const((1, d)),
            const(waT.shape), const(wvT.shape), const(wk.shape),
            pl.BlockSpec((N_FREQ, tm), lambda b, i: (0, i)),
            pl.BlockSpec((N_FREQ, tm), lambda b, i: (0, i)),
            pl.BlockSpec((tm, LANES), lambda b, i: (i, 0)),
            pl.BlockSpec((tm, LANES), lambda b, i: (i, 0)),
            pl.BlockSpec((tm, LANES), lambda b, i: (i, 0)),
        ],
        out_specs=out_specs,
        compiler_params=pltpu.CompilerParams(dimension_semantics=("parallel", "parallel"),
                                             vmem_limit_bytes=VMEM_LIMIT),
        name="nsa_proj",
    )(h3, b_norm.reshape(1, d), kv_norm.reshape(1, d), waT, wvT, wk, cosT, sinT, cosf, sina, sinb)


def _gelu_tanh(x):
    return x * (0.5 * (1.0 + jnp.tanh(0.7978845608028654 * (x + 0.044715 * (x * x * x)))))


def _compress_kernel(x_ref, w1_ref, pos_ref, w2_ref, w2T_ref, kc_ref, cT_ref, *, nchunk, hidden):
    u = jnp.zeros((nchunk, 2 * hidden), F32)
    v = jnp.zeros((nchunk, 2 * hidden), F32)
    for l in range(CMP_STRIDE):
        xl = x_ref[0, pl.ds(l, nchunk, stride=CMP_STRIDE), :]
        u = u + _dot((xl + pos_ref[0, l:l + 1, :]).astype(BF16), w1_ref[0, l])
        l2 = CMP_STRIDE + l
        v = v + _dot((xl + pos_ref[0, l2:l2 + 1, :]).astype(BF16), w1_ref[0, l2])
    hid = u + pltpu.roll(v, nchunk - 1, axis=0)
    act = _gelu_tanh(hid).astype(BF16)
    rows_ok = lax.broadcasted_iota(jnp.int32, (nchunk, LANES), 0) < nchunk - 1
    cols_ok = lax.broadcasted_iota(jnp.int32, (HEAD_DIM, nchunk), 1) < nchunk - 1
    for p in range(2):
        a = act[:, p * hidden:(p + 1) * hidden]
        kc_ref[0, 0, p] = jnp.where(rows_ok, _dot(a, w2_ref[0]), 0.0).astype(BF16)
        cT_ref[0, 0, p] = jnp.where(cols_ok, _dot_nt(w2T_ref[0], a), 0.0).astype(BF16)


def _compress(kvc, w1bd, pos2, w2p, w2T):
    bsz, seq, _ = kvc.shape
    nchunk = seq // CMP_STRIDE
    hidden = w2T.shape[2]
    return pl.pallas_call(
        functools.partial(_compress_kernel, nchunk=nchunk, hidden=hidden),
        out_shape=[
            jax.ShapeDtypeStruct((2, bsz, N_KV_GROUPS, nchunk, LANES), BF16),
            jax.ShapeDtypeStruct((2, bsz, N_KV_GROUPS, HEAD_DIM, nchunk), BF16),
        ],
        grid=(2, bsz, N_KV_GROUPS // 2),
        in_specs=[
            pl.BlockSpec((1, seq, LANES), lambda s, b, p: (b, 0, 2 * s + p)),
            pl.BlockSpec((1, CMP_BLOCK, LANES, 2 * hidden), lambda s, b, p: (s, 0, 0, 0)),
            pl.BlockSpec((1, CMP_BLOCK, LANES), lambda s, b, p: (s, 0, 0)),
            pl.BlockSpec((1, hidden, LANES), lambda s, b, p: (s, 0, 0)),
            pl.BlockSpec((1, HEAD_DIM, hidden), lambda s, b, p: (s, 0, 0)),
        ],
        out_specs=[
            pl.BlockSpec((1, 1, 2, nchunk, LANES), lambda s, b, p: (s, b, p, 0, 0)),
            pl.BlockSpec((1, 1, 2, HEAD_DIM, nchunk), lambda s, b, p: (s, b, p, 0, 0)),
        ],
        compiler_params=pltpu.CompilerParams(dimension_semantics=("parallel", "parallel", "parallel"),
                                             vmem_limit_bytes=VMEM_LIMIT),
        name="compress",
    )(kvc, w1bd, pos2, w2p, w2T)


def _attn_kernel(q_ref, qr_ref, g_ref, sz_ref, kc_ref, vcT_ref, ks_ref, kw_ref, vsT_ref, vwT_ref, y_ref,
                 p_sc, sc_sc, rank_sc, oc_sc, qa_sc, c_sc, w_sc, s_sc, pb_sc, smax_sc, st_sc, acc_sc, wmax_sc,
                 wp_sc, ow_sc, *, tq, n_cmp):
    qi = pl.program_id(2)
    t0 = qi * tq
    neg_inf = -jnp.inf
    tvec = t0 + lax.broadcasted_iota(jnp.int32, (1, tq), 1)

    kc = kc_ref[0, 0, :, 0:HEAD_DIM]
    cmp_end = lax.broadcasted_iota(jnp.int32, (n_cmp, tq), 0) * CMP_STRIDE + (CMP_BLOCK - 1)
    cmask = cmp_end <= tvec
    heads = range(HEADS_PER_GROUP)

    def fold8(x, op, ways=4):
        parts = [None] * ways
        for idx, r in enumerate(range(0, x.shape[0], SUBLANES)):
            slab = x[r:r + SUBLANES, :]
            parts[idx % ways] = slab if parts[idx % ways] is None else op(parts[idx % ways], slab)
        return functools.reduce(op, [p for p in parts if p is not None])

    def key_chunk(ref, c):
        return ref[0, pl.ds(pl.multiple_of(c * KEY_CHUNK, KEY_CHUNK), KEY_CHUNK), :]

    n_sb = n_cmp // 4
    jrow = lax.broadcasted_iota(jnp.int32, (n_sb, tq), 0)
    cur = tvec // SLC_BLOCK
    valid = jrow <= cur

    QA_SEL, QA_DEAD, QA_WIN = 0, 1, 2
    pad_blocks = lambda b: b if n_sb == HEAD_DIM else jnp.concatenate(
        [b, jnp.zeros((HEAD_DIM - n_sb, tq), BF16)], axis=0)

    def set_operand(idx, bias):
        for hh in heads:
            r0 = hh * HEAD_DIM
            qa_sc[idx, hh] = jnp.concatenate([qr_ref[0, r0:r0 + HEAD_DIM, :], pad_blocks(bias)], axis=0)

    win_blocks = WINDOW // SLC_BLOCK
    set_operand(QA_WIN, jnp.where((jrow >= cur - win_blocks) & valid, 0.0, MASK_BIAS).astype(BF16))
    set_operand(QA_DEAD, jnp.full((n_sb, tq), MASK_BIAS, BF16))

    blk_row = lax.broadcasted_iota(jnp.int32, (SLC_BLOCK, LANES), 0)
    blk_lane = lax.broadcasted_iota(jnp.int32, (SLC_BLOCK, LANES), 1)
    edge = [blk_row - blk_lane, blk_row - (blk_lane - SLC_BLOCK)]
    off_diag = [blk_lane >= SLC_BLOCK, blk_lane < SLC_BLOCK]

    def mask_diagonal_blocks(s, keep):
        blocks = []
        for bb in range(KEY_CHUNK // SLC_BLOCK):
            rows = s[bb * SLC_BLOCK:(bb + 1) * SLC_BLOCK, :]
            halves = [rows[:, h * LANES:(h + 1) * LANES] for h in range(tq // LANES)]
            h = bb // 2
            halves[h] = jnp.where(off_diag[bb % 2] | keep(edge[bb % 2]), halves[h], neg_inf)
            blocks.append(jnp.concatenate(halves, axis=1))
        return jnp.concatenate(blocks, axis=0)

    n_win = KEY_CHUNK + HALF
    prev_operand = jnp.where(qi >= 1, QA_WIN, QA_DEAD)
    prev_base = jnp.maximum(qi - 1, 0) * KEY_CHUNK

    def mask_half_blocks(s, first_block, keep):
        blocks = [s[r:r + SLC_BLOCK, :] for r in range(0, s.shape[0], SLC_BLOCK)]
        for j in range(HALF // SLC_BLOCK):
            halves = [blocks[first_block + j][:, h * LANES:(h + 1) * LANES] for h in range(HALF // LANES)]
            halves[j // 2] = jnp.where(off_diag[j % 2] | keep(edge[j % 2]), halves[j // 2], neg_inf)
            blocks[first_block + j] = jnp.concatenate(halves, axis=1)
        return jnp.concatenate(blocks, axis=0)

    def window_pieces(half):
        first = half * HALF
        n_old = KEY_CHUNK - first
        old = (pl.multiple_of(prev_base + first, HALF), n_old)
        new = (pl.multiple_of(qi * KEY_CHUNK, HALF), HALF + first)
        return old, new

    for hh in heads:
        r0 = hh * HEAD_DIM
        c_sc[hh] = jnp.where(cmask, _dot(kc, q_ref[0, r0:r0 + HEAD_DIM, :]), neg_inf)
        for half in range(tq // HALF):
            (old_start, n_old), (new_start, n_new) = window_pieces(half)
            lanes = slice(half * HALF, (half + 1) * HALF)
            s_old = _dot(kw_ref[0, pl.ds(old_start, n_old), :], qa_sc[prev_operand, hh, :, lanes])
            s_new = _dot(kw_ref[0, pl.ds(new_start, n_new), :], qa_sc[QA_WIN, hh, :, lanes])
            s_old = mask_half_blocks(s_old, 0, lambda e: e > 0)
            s_new = mask_half_blocks(s_new, half * (HALF // SLC_BLOCK), lambda e: e <= 0)
            w_sc[hh, half, 0:n_old, :] = s_old
            w_sc[hh, half, n_old:n_win, :] = s_new
            wmax_sc[hh, half] = jnp.maximum(fold8(s_old, jnp.maximum), fold8(s_new, jnp.maximum))
    probs = []
    for hh in heads:
        s = c_sc[hh]
        m = jnp.max(fold8(s, jnp.maximum), axis=0, keepdims=True)
        m = jnp.where(m == neg_inf, 0.0, m)
        e = jnp.exp2(s - m)
        den = jnp.sum(fold8(e, jnp.add), axis=0, keepdims=True)
        probs.append(e * (1.0 / jnp.maximum(den, 1e-30)))
    p_grp = functools.reduce(jnp.add, probs)
    for hh in heads:
        r0 = hh * HEAD_DIM
        oc_sc[r0:r0 + HEAD_DIM, :] = _dot(vcT_ref[0, 0], probs[hh].astype(BF16))

    ratio = SLC_BLOCK // CMP_STRIDE
    imp_cols = []
    for c in range(tq // LANES):
        p_sc[c, 0:SUBLANES, :] = jnp.zeros((SUBLANES, LANES), F32)
        p_sc[c, SUBLANES:SUBLANES + n_cmp, :] = p_grp[:, c * LANES:(c + 1) * LANES]
        tap = lambda o: p_sc[c, pl.ds(SUBLANES + o, n_sb, stride=ratio), :]
        imp_cols.append(tap(-1) + 2.0 * (tap(0) + tap(1) + tap(2)) + tap(3))
    imp = jnp.concatenate(imp_cols, axis=1)

    forced = (jrow == 0) | (valid & (jrow > cur - N_LOCAL))
    score = jnp.where(valid, jnp.where(forced, FORCE_SCORE, imp), neg_inf)
    sc_sc[...] = score

    n_valid = (t0 + tq) // SLC_BLOCK
    n_slabs = n_sb // SUBLANES
    slab_row = lax.broadcasted_iota(jnp.int32, (SUBLANES, tq), 0)
    rank_sc[...] = jnp.zeros((n_sb, tq), jnp.int32)
    for first in range(0, n_sb, RANK_SECTION):
        @pl.when(n_valid > max(N_SELECT, first))
        def _():
            slabs = [sc_sc[k * SUBLANES:(k + 1) * SUBLANES, :] for k in range(n_slabs)]
            counts = [rank_sc[k * SUBLANES:(k + 1) * SUBLANES, :] for k in range(n_slabs)]
            for jp in range(first, first + RANK_SECTION):
                sb = sc_sc[jp:jp + 1, :]
                for k in range(n_slabs):
                    if k * SUBLANES > jp:
                        before = sb >= slabs[k]
                    elif k * SUBLANES + SUBLANES - 1 < jp:
                        before = sb > slabs[k]
                    else:
                        before = (sb > slabs[k]) | ((sb == slabs[k]) & (slab_row > jp % SUBLANES))
                    counts[k] = counts[k] + jnp.where(before, 1, 0)
            for k in range(n_slabs):
                rank_sc[k * SUBLANES:(k + 1) * SUBLANES, :] = counts[k]
    set_operand(QA_SEL, jnp.where((rank_sc[...] < N_SELECT) & valid, 0.0, MASK_BIAS).astype(BF16))

    group = KEY_CHUNK
    n_double = qi // 2

    def values_of(ref, c):
        per = KEY_CHUNK // PROJ_TILE
        return jnp.concatenate([ref[0, per * c + v] for v in range(per)], axis=1)

    ST_MAX, ST_SUM, ST_RESCALE = 0, 1, 2
    rows8 = lambda x: jnp.broadcast_to(x, (SUBLANES, tq))
    never = 1 << 20

    def scores_to(slot, pos, hh, may_be_last):
        s = _dot(key_chunk(ks_ref, pos), qa_sc[QA_SEL, hh])
        if may_be_last:
            causal_slack = jnp.where(pos == qi, 0, never)
            s = mask_diagonal_blocks(s, lambda e: e <= causal_slack)
        s_sc[slot, hh] = s
        smax_sc[slot, hh] = fold8(s, jnp.maximum)

    def accumulate(pos, slot, hh):
        acc_sc[hh] = (acc_sc[hh] * st_sc[ST_RESCALE + slot, hh, 0:1, :]
                      + _dot(values_of(vsT_ref, pos), pb_sc[slot, hh]))

    def softmax_group(slot, hh):
        m_old = st_sc[ST_MAX, hh, 0:1, :]
        m_new = jnp.maximum(m_old, jnp.max(smax_sc[slot, hh], axis=0, keepdims=True))
        a = jnp.exp2(m_old - m_new)
        psum = None
        for r in range(0, group, SOFTMAX_ROWS):
            s = s_sc[slot, hh, r:r + SOFTMAX_ROWS, :]
            p = jnp.exp2(s - m_new)
            pb_sc[slot, hh, r:r + SOFTMAX_ROWS, :] = p.astype(BF16)
            f = fold8(p, jnp.add, ways=2)
            psum = f if psum is None else psum + f
        st_sc[ST_SUM, hh] = a * st_sc[ST_SUM, hh] + psum
        st_sc[ST_MAX, hh] = rows8(m_new)
        st_sc[ST_RESCALE + slot, hh] = rows8(a)

    def stage(slot, pos, next_may_be_last):
        for hh in heads:
            scores_to(1 - slot, pos + 1, hh, next_may_be_last)
            accumulate(jnp.maximum(pos - 1, 0), 1 - slot, hh)
            softmax_group(slot, hh)

    def drain(slot):
        for hh in heads:
            accumulate(jnp.maximum(qi - 1, 0), 1 - slot, hh)
            softmax_group(slot, hh)
            accumulate(qi, slot, hh)

    def double_trip(d, carry):
        stage(0, 2 * d, next_may_be_last=False)
        stage(1, 2 * d + 1, next_may_be_last=True)
        return carry

    def window_values(half):
        pieces = []
        for start, rows in window_pieces(half):
            pieces += [vwT_ref[0, start // PROJ_TILE + v] for v in range(rows // PROJ_TILE)]
        return jnp.concatenate(pieces, axis=1)

    for hh in heads:
        acc_sc[hh] = jnp.zeros((HEAD_DIM, tq), F32)
        pb_sc[1, hh] = jnp.zeros((group, tq), BF16)
        st_sc[ST_MAX, hh] = jnp.full((SUBLANES, tq), neg_inf, F32)
        st_sc[ST_SUM, hh] = jnp.zeros((SUBLANES, tq), F32)
        st_sc[ST_RESCALE + 1, hh] = jnp.ones((SUBLANES, tq), F32)
        scores_to(0, 0, hh, may_be_last=True)
    for hh in heads:
        for half in range(tq // HALF):
            m = jnp.max(wmax_sc[hh, half], axis=0, keepdims=True)
            psum = None
            for r in range(0, n_win, 2 * SOFTMAX_ROWS):
                p = jnp.exp2(w_sc[hh, half, r:r + 2 * SOFTMAX_ROWS, :] - m)
                wp_sc[hh, half, r:r + 2 * SOFTMAX_ROWS, :] = p.astype(BF16)
                f = fold8(p, jnp.add, ways=2)
                psum = f if psum is None else psum + f
            ow_sc[hh, :, half * HALF:(half + 1) * HALF] = (
                _dot(window_values(half), wp_sc[hh, half]) * (1.0 / jnp.sum(psum, axis=0, keepdims=True)))

    lax.fori_loop(0, n_double, double_trip, 0)

    @pl.when(qi % 2 == 0)
    def _():
        drain(0)

    @pl.when(qi % 2 == 1)
    def _():
        stage(0, qi - 1, next_may_be_last=True)
        drain(1)

    for hh in heads:
        r0 = hh * HEAD_DIM
        o_w = ow_sc[hh]
        o_s = acc_sc[hh] * (1.0 / jnp.sum(st_sc[ST_SUM, hh], axis=0, keepdims=True))

        g0 = g_ref[0, N_BRANCH * hh:N_BRANCH * hh + 1, :]
        g1 = g_ref[0, N_BRANCH * hh + 1:N_BRANCH * hh + 2, :]
        g2 = g_ref[0, N_BRANCH * hh + 2:N_BRANCH * hh + 3, :]
        o = g0 * oc_sc[r0:r0 + HEAD_DIM, :] + g1 * o_s + g2 * o_w
        y_ref[0, r0:r0 + HEAD_DIM, :] = (o * sz_ref[0, r0:r0 + HEAD_DIM, :]).astype(BF16)


def _nsa_attn(qT, qrT, gT, szT, kcmp, vcmpT, ks, kw, vsT, vwT, *, tq=KEY_CHUNK):
    bsz, attn_d, seq = qT.shape
    n_cmp = kcmp.shape[2]
    n_win = KEY_CHUNK + HALF
    gd = HEADS_PER_GROUP * HEAD_DIM
    assert tq == KEY_CHUNK == WINDOW and HALF % PROJ_TILE == 0 and vsT.shape[3] == PROJ_TILE
    qspec = pl.BlockSpec((1, gd, tq), lambda b, g, i: (b, g, i))
    kspec = pl.BlockSpec((1, seq, LANES), lambda b, g, i: (b, 0, g))
    vspec = pl.BlockSpec((1, seq // PROJ_TILE, HEAD_DIM, PROJ_TILE), lambda b, g, i: (b, 0, g, 0))
    return pl.pallas_call(
        functools.partial(_attn_kernel, tq=tq, n_cmp=n_cmp),
        out_shape=jax.ShapeDtypeStruct((bsz, attn_d, seq), BF16),
        grid=(bsz, N_KV_GROUPS, seq // tq),
        in_specs=[
            qspec, qspec,
            pl.BlockSpec((1, GATE_ROWS, tq), lambda b, g, i: (b, g, i)),
            qspec,
            pl.BlockSpec((1, 1, n_cmp, LANES), lambda b, g, i: (b, g, 0, 0)),
            pl.BlockSpec((1, 1, HEAD_DIM, n_cmp), lambda b, g, i: (b, g, 0, 0)),
            kspec, kspec, vspec, vspec,
        ],
        out_specs=qspec,
        scratch_shapes=[
            pltpu.VMEM((tq // LANES, SUBLANES + n_cmp, LANES), F32),
            pltpu.VMEM((n_cmp // 4, tq), F32),
            pltpu.VMEM((n_cmp // 4, tq), jnp.int32),
            pltpu.VMEM((gd, tq), F32),
            pltpu.VMEM((3, HEADS_PER_GROUP, 2 * HEAD_DIM, tq), BF16),
            pltpu.VMEM((HEADS_PER_GROUP, n_cmp, tq), F32),
            pltpu.VMEM((HEADS_PER_GROUP, tq // HALF, n_win, HALF), F32),
            pltpu.VMEM((2, HEADS_PER_GROUP, KEY_CHUNK, tq), F32),
            pltpu.VMEM((2, HEADS_PER_GROUP, KEY_CHUNK, tq), BF16),
            pltpu.VMEM((2, HEADS_PER_GROUP, SUBLANES, tq), F32),
            pltpu.VMEM((4, HEADS_PER_GROUP, SUBLANES, tq), F32),
            pltpu.VMEM((HEADS_PER_GROUP, HEAD_DIM, tq), F32),
            pltpu.VMEM((HEADS_PER_GROUP, tq // HALF, SUBLANES, HALF), F32),
            pltpu.VMEM((HEADS_PER_GROUP, tq // HALF, n_win, HALF), BF16),
            pltpu.VMEM((HEADS_PER_GROUP, HEAD_DIM, tq), F32),
        ],
        compiler_params=pltpu.CompilerParams(dimension_semantics=("parallel", "parallel", "arbitrary"),
                                             vmem_limit_bytes=VMEM_LIMIT),
        name="nsa_attn",
    )(qT, qrT, gT, szT, kcmp, vcmpT, ks, kw, vsT, vwT)


def _nsa_out_kernel(y_ref, h_ref, woT_ref, fn_ref, o_ref):
    oT = _dot(woT_ref[...], y_ref[0])
    h2 = h_ref[0] + oT.T
    o_ref[0] = h2 * _inv_rms(h2) * fn_ref[...]


def _nsa_out(yT, h3, woT, final_norm, *, tm=PROJ_TILE):
    bsz, seq, d = h3.shape
    attn_d = yT.shape[1]
    return pl.pallas_call(
        _nsa_out_kernel,
        out_shape=jax.ShapeDtypeStruct((bsz, seq, d), F32),
        grid=(bsz, seq // tm),
        in_specs=[
            pl.BlockSpec((1, attn_d, tm), lambda b, i: (b, 0, i)),
            pl.BlockSpec((1, tm, d), lambda b, i: (b, i, 0)),
            pl.BlockSpec((d, attn_d), lambda b, i: (0, 0)),
            pl.BlockSpec((1, d), lambda b, i: (0, 0)),
        ],
        out_specs=pl.BlockSpec((1, tm, d), lambda b, i: (b, i, 0)),
        compiler_params=pltpu.CompilerParams(dimension_semantics=("parallel", "parallel"),
                                             vmem_limit_bytes=VMEM_LIMIT),
        name="nsa_out",
    )(yT, h3, woT, final_norm.reshape(1, d))


def _rope_tables(seq):
    pos = jnp.arange(seq, dtype=F32)
    inv = ROPE_THETA ** (-jnp.arange(0, ROT_DIM, 2, dtype=F32) / ROT_DIM)
    ang = pos[:, None] * inv[None, :]
    cos, sin = jnp.cos(ang), jnp.sin(ang)
    z = lambda n: jnp.zeros((seq, n), F32)
    cosf = jnp.concatenate([cos, cos, jnp.ones((seq, LANES - ROT_DIM), F32)], axis=1)
    sina = jnp.concatenate([-sin, z(LANES - N_FREQ)], axis=1)
    sinb = jnp.concatenate([z(N_FREQ), sin, z(LANES - ROT_DIM)], axis=1)
    return cos.T, sin.T, cosf, sina, sinb


def _blockdiag2(w1):
    hidden = w1.shape[1]
    w = w1.reshape(CMP_BLOCK, HEAD_DIM, hidden)
    zero = jnp.zeros_like(w)
    top = jnp.concatenate([w, zero], axis=2)
    bot = jnp.concatenate([zero, w], axis=2)
    return jnp.concatenate([top, bot], axis=1)


def kernel(x, a_norm, a_w_in, a_conv_w, a_w_out, kv_norm, w_kv, cmp_pos_k, cmp_w1_k, cmp_w2_k,
           cmp_pos_v, cmp_w1_v, cmp_w2_v, b_norm, b_w_in, b_w_out, final_norm):
    bsz, seq, d = x.shape
    attn_d = N_HEADS * HEAD_DIM
    kv_d = N_KV_GROUPS * HEAD_DIM
    assert b_norm.shape[0] == 1, "one NSA layer reads the shared K/V side"
    assert seq % KEY_CHUNK == 0

    h = x.reshape(bsz * seq, d)
    for layer in range(a_norm.shape[0]):
        h = _conv_layer(h, a_norm[layer], a_w_in[layer].astype(BF16), a_conv_w[layer],
                        a_w_out[layer].astype(BF16), seq=seq)
    h3 = h.reshape(bsz, seq, d)

    w_in = b_w_in[0]
    n_gate = N_HEADS * N_BRANCH
    wg = w_in[:, attn_d:attn_d + n_gate].reshape(d, N_KV_GROUPS, HEADS_PER_GROUP * N_BRANCH)
    wg = jnp.pad(wg, ((0, 0), (0, 0), (0, GATE_ROWS - HEADS_PER_GROUP * N_BRANCH)))
    waT = jnp.concatenate([w_in[:, :attn_d], w_in[:, attn_d + n_gate:], wg.reshape(d, -1)], axis=1).T.astype(BF16)
    wkv = w_kv.reshape(d, 2 * N_BRANCH, N_KV_GROUPS, HEAD_DIM)
    k_c, v_c, k_s, v_s, k_w, v_w = [wkv[:, i] for i in range(2 * N_BRANCH)]
    flat = lambda w: w.reshape(d, kv_d)
    pad_lanes = lambda w: jnp.pad(w, ((0, 0), (0, 0), (0, LANES - HEAD_DIM))).reshape(d, N_KV_GROUPS * LANES)
    wvT = jnp.concatenate([flat(v_s), flat(v_w)], axis=1).T.astype(BF16)
    wk = jnp.concatenate([flat(k_c), flat(v_c), pad_lanes(k_s), pad_lanes(k_w)], axis=1).astype(BF16)
    cosT, sinT, cosf, sina, sinb = _rope_tables(seq)

    qT, qrT, szT, gT, kvc, ks, kw, vsT, vwT = _nsa_proj(
        h3, b_norm[0], kv_norm, waT, wvT, wk, cosT, sinT, cosf, sina, sinb)

    w1bd = jnp.stack([_blockdiag2(cmp_w1_k), _blockdiag2(cmp_w1_v)]).astype(BF16)
    pos2 = jnp.stack([jnp.tile(cmp_pos_k, (1, 2)), jnp.tile(cmp_pos_v, (1, 2))])
    w2 = jnp.stack([cmp_w2_k, cmp_w2_v])
    w2p = jnp.pad(w2, ((0, 0), (0, 0), (0, LANES - HEAD_DIM))).astype(BF16)
    w2T = jnp.swapaxes(w2, 1, 2).astype(BF16)
    cmp_tm, cmp_fm = _compress(kvc, w1bd, pos2, w2p, w2T)

    yT = _nsa_attn(qT, qrT, gT, szT, cmp_tm[0], cmp_fm[1], ks, kw, vsT, vwT)
    return _nsa_out(yT, h3, b_w_out[0].T.astype(BF16), final_norm)
```

```python
import functools

import jax
import jax.numpy as jnp
from jax import lax
from jax.experimental import pallas as pl
from jax.experimental.pallas import tpu as pltpu

EPS = 1e-6
CONV_WIDTH = 3
N_HEADS = 16
HEAD_DIM = 64
N_KV_GROUPS = 4
HEADS_PER_GROUP = N_HEADS // N_KV_GROUPS
N_BRANCH = 3
ROT_DIM = HEAD_DIM // 4
N_FREQ = ROT_DIM // 2
ROPE_THETA = 500000.0
CMP_BLOCK = 32
CMP_STRIDE = 16
SLC_BLOCK = 64
N_SELECT = 16
N_LOCAL = 2
WINDOW = 512
FORCE_SCORE = 1e4

LANES = 128
SUBLANES = 8
KEY_CHUNK = 512
HALF = KEY_CHUNK // 2
PROJ_TILE = 256
OUT_TILE = 512
SOFTMAX_ROWS = 32
RANK_SECTION = 16
GATE_ROWS = 16
MASK_BIAS = -1e30
LOG2_E = 1.4426950408889634
VMEM_LIMIT = 56 * 1024 * 1024

BF16 = jnp.bfloat16
F32 = jnp.float32
NT_DIMS = (((1,), (1,)), ((), ()))


def _dot(a, b):
    return jnp.dot(a, b, preferred_element_type=F32)


def _dot_nt(a, b):
    return lax.dot_general(a, b, NT_DIMS, preferred_element_type=F32)


def _sigmoid(x):
    return 1.0 / (1.0 + jnp.exp(-x))


def _inv_rms(x):
    return lax.rsqrt(jnp.mean(x * x, axis=-1, keepdims=True) + EPS)


def _conv_layer_kernel(x_ref, g_ref, win_ref, cw_ref, wout_ref, o_ref, vbuf_ref, *,
                       tm, tiles_per_seq, conv_d, cchunk):
    @pl.when(pl.program_id(0) % tiles_per_seq == 0)
    def _():
        vbuf_ref[0:SUBLANES, :] = jnp.zeros((SUBLANES, conv_d), F32)

    x = x_ref[...]
    hn = (x * _inv_rms(x) * g_ref[...]).astype(BF16)
    acc = jnp.zeros(x.shape, F32)
    for cc in range(conv_d // cchunk):
        cs = cc * cchunk
        b = _dot(hn, win_ref[:, cs:cs + cchunk])
        c = _dot(hn, win_ref[:, conv_d + cs:conv_d + cs + cchunk])
        u = _dot(hn, win_ref[:, 2 * conv_d + cs:2 * conv_d + cs + cchunk])
        z = _dot(hn, win_ref[:, 3 * conv_d + cs:3 * conv_d + cs + cchunk])
        v = c * u
        vbuf_ref[SUBLANES:SUBLANES + tm, cs:cs + cchunk] = v
        v1 = vbuf_ref[SUBLANES - 1:SUBLANES - 1 + tm, cs:cs + cchunk]
        v2 = vbuf_ref[SUBLANES - 2:SUBLANES - 2 + tm, cs:cs + cchunk]
        conv = (cw_ref[0:1, cs:cs + cchunk] * v2 + cw_ref[1:2, cs:cs + cchunk] * v1
                + cw_ref[2:3, cs:cs + cchunk] * v)
        vbuf_ref[0:SUBLANES, cs:cs + cchunk] = v[tm - SUBLANES:tm, :]
        y = b * conv * (z * _sigmoid(z))
        acc = acc + _dot(y.astype(BF16), wout_ref[cs:cs + cchunk, :])
    o_ref[...] = x + acc


def _conv_layer(h, norm_g, w_in, conv_w, w_out, *, seq, tm=512, cchunk=512):
    t, d = h.shape
    conv_d = conv_w.shape[1]
    tm = min(tm, seq)
    const = lambda shape: pl.BlockSpec(shape, lambda i: (0,) * len(shape), pipeline_mode=pl.Buffered(1))
    return pl.pallas_call(
        functools.partial(_conv_layer_kernel, tm=tm, tiles_per_seq=seq // tm, conv_d=conv_d, cchunk=cchunk),
        out_shape=jax.ShapeDtypeStruct((t, d), F32),
        grid=(t // tm,),
        in_specs=[
            pl.BlockSpec((tm, d), lambda i: (i, 0)),
            const((1, d)),
            const((d, 4 * conv_d)),
            const((CONV_WIDTH, conv_d)),
            const((conv_d, d)),
        ],
        out_specs=pl.BlockSpec((tm, d), lambda i: (i, 0)),
        scratch_shapes=[pltpu.VMEM((SUBLANES + tm, conv_d), F32)],
        compiler_params=pltpu.CompilerParams(dimension_semantics=("arbitrary",), vmem_limit_bytes=VMEM_LIMIT),
        name="conv_layer",
    )(h, norm_g.reshape(1, d), w_in, conv_w, w_out)


def _nsa_proj_kernel(h_ref, bn_ref, kn_ref, waT_ref, wvT_ref, wk_ref, cosT_ref, sinT_ref,
                     cosf_ref, sina_ref, sinb_ref,
                     qT_ref, qrT_ref, szT_ref, gT_ref, kvc_ref, ks_ref, kw_ref, vsT_ref, vwT_ref, *, tm, d):
    attn_d = N_HEADS * HEAD_DIM
    kv_d = N_KV_GROUPS * HEAD_DIM
    h = h_ref[0]
    hr = h * _inv_rms(h)
    hq = (hr * bn_ref[...]).astype(BF16)
    hk = (hr * kn_ref[...]).astype(BF16)

    qT = _dot_nt(waT_ref[0:attn_d, :], hq) * (HEAD_DIM ** -0.5 * LOG2_E)
    cosT = cosT_ref[...]
    sinT = sinT_ref[...]
    for hd in range(N_HEADS):
        r0 = hd * HEAD_DIM
        blk = qT[r0:r0 + HEAD_DIM, :]
        x1 = blk[0:N_FREQ, :]
        x2 = blk[N_FREQ:ROT_DIM, :]
        rot = jnp.concatenate([x1 * cosT - x2 * sinT, x2 * cosT + x1 * sinT, blk[ROT_DIM:, :]], axis=0)
        qT_ref[0, r0:r0 + HEAD_DIM, :] = blk.astype(BF16)
        qrT_ref[0, r0:r0 + HEAD_DIM, :] = rot.astype(BF16)

    zT = _dot_nt(waT_ref[attn_d:2 * attn_d, :], hq)
    szT_ref[0] = zT * _sigmoid(zT)
    gT_ref[0] = _sigmoid(_dot_nt(waT_ref[2 * attn_d:2 * attn_d + N_KV_GROUPS * GATE_ROWS, :], hq))

    vT = _dot_nt(wvT_ref[...], hk)
    vsT_ref[0, 0] = vT[0:kv_d, :].astype(BF16)
    vwT_ref[0, 0] = vT[kv_d:2 * kv_d, :].astype(BF16)

    kk = _dot(hk, wk_ref[...])
    kvc_ref[0] = kk[:, 0:2 * kv_d]
    cosf = cosf_ref[...]
    sina = sina_ref[...]
    sinb = sinb_ref[...]
    row = lax.broadcasted_iota(jnp.int32, (tm, LANES), 0)
    lane = lax.broadcasted_iota(jnp.int32, (tm, LANES), 1)
    tok_blk = (pl.program_id(1) * tm + row) // SLC_BLOCK
    onehot = jnp.where((lane >= HEAD_DIM) & (lane - HEAD_DIM == tok_blk), 1.0, 0.0)
    for j in range(2 * N_KV_GROUPS):
        c0 = 2 * kv_d + j * LANES
        xb = kk[:, c0:c0 + LANES]
        rot = (xb * cosf + pltpu.roll(xb, LANES - N_FREQ, axis=1) * sina
               + pltpu.roll(xb, N_FREQ, axis=1) * sinb)
        if j < N_KV_GROUPS:
            ks_ref[0, :, j * LANES:(j + 1) * LANES] = (rot + onehot).astype(BF16)
        else:
            jj = j - N_KV_GROUPS
            kw_ref[0, :, jj * LANES:(jj + 1) * LANES] = (rot + onehot).astype(BF16)


def _nsa_proj(h3, b_norm, kv_norm, waT, wvT, wk, cosT, sinT, cosf, sina, sinb, *, tm=PROJ_TILE):
    bsz, seq, d = h3.shape
    attn_d = N_HEADS * HEAD_DIM
    kv_d = N_KV_GROUPS * HEAD_DIM
    kpad = N_KV_GROUPS * LANES
    nt = seq // tm
    const = lambda shape: pl.BlockSpec(shape, lambda b, i: (0,) * len(shape))
    fm = lambda rows: pl.BlockSpec((1, rows, tm), lambda b, i: (b, 0, i))
    tmj = lambda cols: pl.BlockSpec((1, tm, cols), lambda b, i: (b, i, 0))
    out_shape = [
        jax.ShapeDtypeStruct((bsz, attn_d, seq), BF16),
        jax.ShapeDtypeStruct((bsz, attn_d, seq), BF16),
        jax.ShapeDtypeStruct((bsz, attn_d, seq), F32),
        jax.ShapeDtypeStruct((bsz, N_KV_GROUPS * GATE_ROWS, seq), F32),
        jax.ShapeDtypeStruct((bsz, seq, 2 * kv_d), F32),
        jax.ShapeDtypeStruct((bsz, seq, kpad), BF16),
        jax.ShapeDtypeStruct((bsz, seq, kpad), BF16),
        jax.ShapeDtypeStruct((bsz, nt, kv_d, tm), BF16),
        jax.ShapeDtypeStruct((bsz, nt, kv_d, tm), BF16),
    ]
    out_specs = [
        fm(attn_d), fm(attn_d), fm(attn_d), fm(N_KV_GROUPS * GATE_ROWS),
        tmj(2 * kv_d), tmj(kpad), tmj(kpad),
        pl.BlockSpec((1, 1, kv_d, tm), lambda b, i: (b, i, 0, 0)),
        pl.BlockSpec((1, 1, kv_d, tm), lambda b, i: (b, i, 0, 0)),
    ]
    return pl.pallas_call(
        functools.partial(_nsa_proj_kernel, tm=tm, d=d),
        out_shape=out_shape,
        grid=(bsz, nt),
        in_specs=[
            tmj(d), const((1, d)), const((1, d)),
            const(waT.shape), const(wvT.shape), const(wk.shape),
            pl.BlockSpec((N_FREQ, tm), lambda b, i: (0, i)),
            pl.BlockSpec((N_FREQ, tm), lambda b, i: (0, i)),
            pl.BlockSpec((tm, LANES), lambda b, i: (i, 0)),
            pl.BlockSpec((tm, LANES), lambda b, i: (i, 0)),
            pl.BlockSpec((tm, LANES), lambda b, i: (i, 0)),
        ],
        out_specs=out_specs,
        compiler_params=pltpu.CompilerParams(dimension_semantics=("parallel", "parallel"),
                                             vmem_limit_bytes=VMEM_LIMIT),
        name="nsa_proj",
    )(h3, b_norm.reshape(1, d), kv_norm.reshape(1, d), waT, wvT, wk, cosT, sinT, cosf, sina, sinb)


def _gelu_tanh(x):
    return x * (0.5 * (1.0 + jnp.tanh(0.7978845608028654 * (x + 0.044715 * (x * x * x)))))


def _compress_kernel(x_ref, w1_ref, pos_ref, w2_ref, w2T_ref, kc_ref, cT_ref, *, nchunk, hidden):
    u = jnp.zeros((nchunk, 2 * hidden), F32)
    v = jnp.zeros((nchunk, 2 * hidden), F32)
    for l in range(CMP_STRIDE):
        xl = x_ref[0, pl.ds(l, nchunk, stride=CMP_STRIDE), :]
        u = u + _dot((xl + pos_ref[0, l:l + 1, :]).astype(BF16), w1_ref[0, l])
        l2 = CMP_STRIDE + l
        v = v + _dot((xl + pos_ref[0, l2:l2 + 1, :]).astype(BF16), w1_ref[0, l2])
    hid = u + pltpu.roll(v, nchunk - 1, axis=0)
    act = _gelu_tanh(hid).astype(BF16)
    rows_ok = lax.broadcasted_iota(jnp.int32, (nchunk, LANES), 0) < nchunk - 1
    cols_ok = lax.broadcasted_iota(jnp.int32, (HEAD_DIM, nchunk), 1) < nchunk - 1
    for p in range(2):
        a = act[:, p * hidden:(p + 1) * hidden]
        kc_ref[0, 0, p] = jnp.where(rows_ok, _dot(a, w2_ref[0]), 0.0).astype(BF16)
        cT_ref[0, 0, p] = jnp.where(cols_ok, _dot_nt(w2T_ref[0], a), 0.0).astype(BF16)


def _compress(kvc, w1bd, pos2, w2p, w2T):
    bsz, seq, _ = kvc.shape
    nchunk = seq // CMP_STRIDE
    hidden = w2T.shape[2]
    return pl.pallas_call(
        functools.partial(_compress_kernel, nchunk=nchunk, hidden=hidden),
        out_shape=[
            jax.ShapeDtypeStruct((2, bsz, N_KV_GROUPS, nchunk, LANES), BF16),
            jax.ShapeDtypeStruct((2, bsz, N_KV_GROUPS, HEAD_DIM, nchunk), BF16),
        ],
        grid=(2, bsz, N_KV_GROUPS // 2),
        in_specs=[
            pl.BlockSpec((1, seq, LANES), lambda s, b, p: (b, 0, 2 * s + p)),
            pl.BlockSpec((1, CMP_BLOCK, LANES, 2 * hidden), lambda s, b, p: (s, 0, 0, 0)),
            pl.BlockSpec((1, CMP_BLOCK, LANES), lambda s, b, p: (s, 0, 0)),
            pl.BlockSpec((1, hidden, LANES), lambda s, b, p: (s, 0, 0)),
            pl.BlockSpec((1, HEAD_DIM, hidden), lambda s, b, p: (s, 0, 0)),
        ],
        out_specs=[
            pl.BlockSpec((1, 1, 2, nchunk, LANES), lambda s, b, p: (s, b, p, 0, 0)),
            pl.BlockSpec((1, 1, 2, HEAD_DIM, nchunk), lambda s, b, p: (s, b, p, 0, 0)),
        ],
        compiler_params=pltpu.CompilerParams(dimension_semantics=("parallel", "parallel", "parallel"),
                                             vmem_limit_bytes=VMEM_LIMIT),
        name="compress",
    )(kvc, w1bd, pos2, w2p, w2T)


def _attn_kernel(q_ref, qr_ref, g_ref, sz_ref, kc_ref, vcT_ref, ks_ref, kw_ref, vsT_ref, vwT_ref, y_ref,
                 p_sc, sc_sc, rank_sc, oc_sc, qa_sc, c_sc, w_sc, s_sc, pb_sc, smax_sc, st_sc, acc_sc, wmax_sc,
                 wp_sc, ow_sc, *, tq, n_cmp):
    qi = pl.program_id(2)
    t0 = qi * tq
    neg_inf = -jnp.inf
    tvec = t0 + lax.broadcasted_iota(jnp.int32, (1, tq), 1)

    kc = kc_ref[0, 0, :, 0:HEAD_DIM]
    cmp_end = lax.broadcasted_iota(jnp.int32, (n_cmp, tq), 0) * CMP_STRIDE + (CMP_BLOCK - 1)
    cmask = cmp_end <= tvec
    heads = range(HEADS_PER_GROUP)

    def fold8(x, op, ways=4):
        parts = [None] * ways
        for idx, r in enumerate(range(0, x.shape[0], SUBLANES)):
            slab = x[r:r + SUBLANES, :]
            parts[idx % ways] = slab if parts[idx % ways] is None else op(parts[idx % ways], slab)
        return functools.reduce(op, [p for p in parts if p is not None])

    def key_chunk(ref, c):
        return ref[0, pl.ds(pl.multiple_of(c * KEY_CHUNK, KEY_CHUNK), KEY_CHUNK), :]

    n_sb = n_cmp // 4
    jrow = lax.broadcasted_iota(jnp.int32, (n_sb, tq), 0)
    cur = tvec // SLC_BLOCK
    valid = jrow <= cur

    QA_SEL, QA_DEAD, QA_WIN = 0, 1, 2
    pad_blocks = lambda b: b if n_sb == HEAD_DIM else jnp.concatenate(
        [b, jnp.zeros((HEAD_DIM - n_sb, tq), BF16)], axis=0)

    def set_operand(idx, bias):
        for hh in heads:
            r0 = hh * HEAD_DIM
            qa_sc[idx, hh] = jnp.concatenate([qr_ref[0, r0:r0 + HEAD_DIM, :], pad_blocks(bias)], axis=0)

    win_blocks = WINDOW // SLC_BLOCK
    set_operand(QA_WIN, jnp.where((jrow >= cur - win_blocks) & valid, 0.0, MASK_BIAS).astype(BF16))
    set_operand(QA_DEAD, jnp.full((n_sb, tq), MASK_BIAS, BF16))

    blk_row = lax.broadcasted_iota(jnp.int32, (SLC_BLOCK, LANES), 0)
    blk_lane = lax.broadcasted_iota(jnp.int32, (SLC_BLOCK, LANES), 1)
    edge = [blk_row - blk_lane, blk_row - (blk_lane - SLC_BLOCK)]
    off_diag = [blk_lane >= SLC_BLOCK, blk_lane < SLC_BLOCK]

    def mask_diagonal_blocks(s, keep):
        blocks = []
        for bb in range(KEY_CHUNK // SLC_BLOCK):
            rows = s[bb * SLC_BLOCK:(bb + 1) * SLC_BLOCK, :]
            halves = [rows[:, h * LANES:(h + 1) * LANES] for h in range(tq // LANES)]
            h = bb // 2
            halves[h] = jnp.where(off_diag[bb % 2] | keep(edge[bb % 2]), halves[h], neg_inf)
            blocks.append(jnp.concatenate(halves, axis=1))
        return jnp.concatenate(blocks, axis=0)

    n_win = KEY_CHUNK + HALF
    prev_operand = jnp.where(qi >= 1, QA_WIN, QA_DEAD)
    prev_base = jnp.maximum(qi - 1, 0) * KEY_CHUNK

    def mask_half_blocks(s, first_block, keep):
        blocks = [s[r:r + SLC_BLOCK, :] for r in range(0, s.shape[0], SLC_BLOCK)]
        for j in range(HALF // SLC_BLOCK):
            halves = [blocks[first_block + j][:, h * LANES:(h + 1) * LANES] for h in range(HALF // LANES)]
            halves[j // 2] = jnp.where(off_diag[j % 2] | keep(edge[j % 2]), halves[j // 2], neg_inf)
            blocks[first_block + j] = jnp.concatenate(halves, axis=1)
        return jnp.concatenate(blocks, axis=0)

    def window_pieces(half):
        first = half * HALF
        n_old = KEY_CHUNK - first
        old = (pl.multiple_of(prev_base + first, HALF), n_old)
        new = (pl.multiple_of(qi * KEY_CHUNK, HALF), HALF + first)
        return old, new

    for hh in heads:
        r0 = hh * HEAD_DIM
        c_sc[hh] = jnp.where(cmask, _dot(kc, q_ref[0, r0:r0 + HEAD_DIM, :]), neg_inf)
        for half in range(tq // HALF):
            (old_start, n_old), (new_start, n_new) = window_pieces(half)
            lanes = slice(half * HALF, (half + 1) * HALF)
            s_old = _dot(kw_ref[0, pl.ds(old_start, n_old), :], qa_sc[prev_operand, hh, :, lanes])
            s_new = _dot(kw_ref[0, pl.ds(new_start, n_new), :], qa_sc[QA_WIN, hh, :, lanes])
            s_old = mask_half_blocks(s_old, 0, lambda e: e > 0)
            s_new = mask_half_blocks(s_new, half * (HALF // SLC_BLOCK), lambda e: e <= 0)
            w_sc[hh, half, 0:n_old, :] = s_old
            w_sc[hh, half, n_old:n_win, :] = s_new
            wmax_sc[hh, half] = jnp.maximum(fold8(s_old, jnp.maximum), fold8(s_new, jnp.maximum))
    probs = []
    for hh in heads:
        s = c_sc[hh]
        m = jnp.max(fold8(s, jnp.maximum), axis=0, keepdims=True)
        m = jnp.where(m == neg_inf, 0.0, m)
        e = jnp.exp2(s - m)
        den = jnp.sum(fold8(e, jnp.add), axis=0, keepdims=True)
        probs.append(e * (1.0 / jnp.maximum(den, 1e-30)))
    p_grp = functools.reduce(jnp.add, probs)
    for hh in heads:
        r0 = hh * HEAD_DIM
        oc_sc[r0:r0 + HEAD_DIM, :] = _dot(vcT_ref[0, 0], probs[hh].astype(BF16))

    ratio = SLC_BLOCK // CMP_STRIDE
    imp_cols = []
    for c in range(tq // LANES):
        p_sc[c, 0:SUBLANES, :] = jnp.zeros((SUBLANES, LANES), F32)
        p_sc[c, SUBLANES:SUBLANES + n_cmp, :] = p_grp[:, c * LANES:(c + 1) * LANES]
        tap = lambda o: p_sc[c, pl.ds(SUBLANES + o, n_sb, stride=ratio), :]
        imp_cols.append(tap(-1) + 2.0 * (tap(0) + tap(1) + tap(2)) + tap(3))
    imp = jnp.concatenate(imp_cols, axis=1)

    forced = (jrow == 0) | (valid & (jrow > cur - N_LOCAL))
    score = jnp.where(valid, jnp.where(forced, FORCE_SCORE, imp), neg_inf)
    sc_sc[...] = score

    n_valid = (t0 + tq) // SLC_BLOCK
    n_slabs = n_sb // SUBLANES
    slab_row = lax.broadcasted_iota(jnp.int32, (SUBLANES, tq), 0)
    rank_sc[...] = jnp.zeros((n_sb, tq), jnp.int32)
    for first in range(0, n_sb, RANK_SECTION):
        @pl.when(n_valid > max(N_SELECT, first))
        def _():
            slabs = [sc_sc[k * SUBLANES:(k + 1) * SUBLANES, :] for k in range(n_slabs)]
            counts = [rank_sc[k * SUBLANES:(k + 1) * SUBLANES, :] for k in range(n_slabs)]
            for jp in range(first, first + RANK_SECTION):
                sb = sc_sc[jp:jp + 1, :]
                for k in range(n_slabs):
                    if k * SUBLANES > jp:
                        before = sb >= slabs[k]
                    elif k * SUBLANES + SUBLANES - 1 < jp:
                        before = sb > slabs[k]
                    else:
                        before = (sb > slabs[k]) | ((sb == slabs[k]) & (slab_row > jp % SUBLANES))
                    counts[k] = counts[k] + jnp.where(before, 1, 0)
            for k in range(n_slabs):
                rank_sc[k * SUBLANES:(k + 1) * SUBLANES, :] = counts[k]
    set_operand(QA_SEL, jnp.where((rank_sc[...] < N_SELECT) & valid, 0.0, MASK_BIAS).astype(BF16))

    group = KEY_CHUNK
    n_double = qi // 2

    def values_of(ref, c):
        per = KEY_CHUNK // PROJ_TILE
        return jnp.concatenate([ref[0, per * c + v] for v in range(per)], axis=1)

    ST_MAX, ST_SUM, ST_RESCALE = 0, 1, 2
    rows8 = lambda x: jnp.broadcast_to(x, (SUBLANES, tq))
    never = 1 << 20

    def scores_to(slot, pos, hh, may_be_last):
        s = _dot(key_chunk(ks_ref, pos), qa_sc[QA_SEL, hh])
        if may_be_last:
            causal_slack = jnp.where(pos == qi, 0, never)
            s = mask_diagonal_blocks(s, lambda e: e <= causal_slack)
        s_sc[slot, hh] = s
        smax_sc[slot, hh] = fold8(s, jnp.maximum)

    def accumulate(pos, slot, hh):
        acc_sc[hh] = (acc_sc[hh] * st_sc[ST_RESCALE + slot, hh, 0:1, :]
                      + _dot(values_of(vsT_ref, pos), pb_sc[slot, hh]))

    def softmax_group(slot, hh):
        m_old = st_sc[ST_MAX, hh, 0:1, :]
        m_new = jnp.maximum(m_old, jnp.max(smax_sc[slot, hh], axis=0, keepdims=True))
        a = jnp.exp2(m_old - m_new)
        psum = None
        for r in range(0, group, SOFTMAX_ROWS):
            s = s_sc[slot, hh, r:r + SOFTMAX_ROWS, :]
            p = jnp.exp2(s - m_new)
            pb_sc[slot, hh, r:r + SOFTMAX_ROWS, :] = p.astype(BF16)
            f = fold8(p, jnp.add, ways=2)
            psum = f if psum is None else psum + f
        st_sc[ST_SUM, hh] = a * st_sc[ST_SUM, hh] + psum
        st_sc[ST_MAX, hh] = rows8(m_new)
        st_sc[ST_RESCALE + slot, hh] = rows8(a)

    def stage(slot, pos, next_may_be_last):
        for hh in heads:
            scores_to(1 - slot, pos + 1, hh, next_may_be_last)
            accumulate(jnp.maximum(pos - 1, 0), 1 - slot, hh)
            softmax_group(slot, hh)

    def drain(slot):
        for hh in heads:
            accumulate(jnp.maximum(qi - 1, 0), 1 - slot, hh)
            softmax_group(slot, hh)
            accumulate(qi, slot, hh)

    def double_trip(d, carry):
        stage(0, 2 * d, next_may_be_last=False)
        stage(1, 2 * d + 1, next_may_be_last=True)
        return carry

    def window_values(half):
        pieces = []
        for start, rows in window_pieces(half):
            pieces += [vwT_ref[0, start // PROJ_TILE + v] for v in range(rows // PROJ_TILE)]
        return jnp.concatenate(pieces, axis=1)

    for hh in heads:
        acc_sc[hh] = jnp.zeros((HEAD_DIM, tq), F32)
        pb_sc[1, hh] = jnp.zeros((group, tq), BF16)
        st_sc[ST_MAX, hh] = jnp.full((SUBLANES, tq), neg_inf, F32)
        st_sc[ST_SUM, hh] = jnp.zeros((SUBLANES, tq), F32)
        st_sc[ST_RESCALE + 1, hh] = jnp.ones((SUBLANES, tq), F32)
        scores_to(0, 0, hh, may_be_last=True)
    for hh in heads:
        for half in range(tq // HALF):
            m = jnp.max(wmax_sc[hh, half], axis=0, keepdims=True)
            psum = None
            for r in range(0, n_win, 2 * SOFTMAX_ROWS):
                p = jnp.exp2(w_sc[hh, half, r:r + 2 * SOFTMAX_ROWS, :] - m)
                wp_sc[hh, half, r:r + 2 * SOFTMAX_ROWS, :] = p.astype(BF16)
                f = fold8(p, jnp.add, ways=2)
                psum = f if psum is None else psum + f
            ow_sc[hh, :, half * HALF:(half + 1) * HALF] = (
                _dot(window_values(half), wp_sc[hh, half]) * (1.0 / jnp.sum(psum, axis=0, keepdims=True)))

    lax.fori_loop(0, n_double, double_trip, 0)

    @pl.when(qi % 2 == 0)
    def _():
        drain(0)

    @pl.when(qi % 2 == 1)
    def _():
        stage(0, qi - 1, next_may_be_last=True)
        drain(1)

    for hh in heads:
        r0 = hh * HEAD_DIM
        o_w = ow_sc[hh]
        o_s = acc_sc[hh] * (1.0 / jnp.sum(st_sc[ST_SUM, hh], axis=0, keepdims=True))

        g0 = g_ref[0, N_BRANCH * hh:N_BRANCH * hh + 1, :]
        g1 = g_ref[0, N_BRANCH * hh + 1:N_BRANCH * hh + 2, :]
        g2 = g_ref[0, N_BRANCH * hh + 2:N_BRANCH * hh + 3, :]
        o = g0 * oc_sc[r0:r0 + HEAD_DIM, :] + g1 * o_s + g2 * o_w
        y_ref[0, r0:r0 + HEAD_DIM, :] = (o * sz_ref[0, r0:r0 + HEAD_DIM, :]).astype(BF16)


def _nsa_attn(qT, qrT, gT, szT, kcmp, vcmpT, ks, kw, vsT, vwT, *, tq=KEY_CHUNK):
    bsz, attn_d, seq = qT.shape
    n_cmp = kcmp.shape[2]
    n_win = KEY_CHUNK + HALF
    gd = HEADS_PER_GROUP * HEAD_DIM
    assert tq == KEY_CHUNK == WINDOW and HALF % PROJ_TILE == 0 and vsT.shape[3] == PROJ_TILE
    qspec = pl.BlockSpec((1, gd, tq), lambda b, g, i: (b, g, i))
    kspec = pl.BlockSpec((1, seq, LANES), lambda b, g, i: (b, 0, g))
    vspec = pl.BlockSpec((1, seq // PROJ_TILE, HEAD_DIM, PROJ_TILE), lambda b, g, i: (b, 0, g, 0))
    return pl.pallas_call(
        functools.partial(_attn_kernel, tq=tq, n_cmp=n_cmp),
        out_shape=jax.ShapeDtypeStruct((bsz, attn_d, seq), BF16),
        grid=(bsz, N_KV_GROUPS, seq // tq),
        in_specs=[
            qspec, qspec,
            pl.BlockSpec((1, GATE_ROWS, tq), lambda b, g, i: (b, g, i)),
            qspec,
            pl.BlockSpec((1, 1, n_cmp, LANES), lambda b, g, i: (b, g, 0, 0)),
            pl.BlockSpec((1, 1, HEAD_DIM, n_cmp), lambda b, g, i: (b, g, 0, 0)),
            kspec, kspec, vspec, vspec,
        ],
        out_specs=qspec,
        scratch_shapes=[
            pltpu.VMEM((tq // LANES, SUBLANES + n_cmp, LANES), F32),
            pltpu.VMEM((n_cmp // 4, tq), F32),
            pltpu.VMEM((n_cmp // 4, tq), jnp.int32),
            pltpu.VMEM((gd, tq), F32),
            pltpu.VMEM((3, HEADS_PER_GROUP, 2 * HEAD_DIM, tq), BF16),
            pltpu.VMEM((HEADS_PER_GROUP, n_cmp, tq), F32),
            pltpu.VMEM((HEADS_PER_GROUP, tq // HALF, n_win, HALF), F32),
            pltpu.VMEM((2, HEADS_PER_GROUP, KEY_CHUNK, tq), F32),
            pltpu.VMEM((2, HEADS_PER_GROUP, KEY_CHUNK, tq), BF16),
            pltpu.VMEM((2, HEADS_PER_GROUP, SUBLANES, tq), F32),
            pltpu.VMEM((4, HEADS_PER_GROUP, SUBLANES, tq), F32),
            pltpu.VMEM((HEADS_PER_GROUP, HEAD_DIM, tq), F32),
            pltpu.VMEM((HEADS_PER_GROUP, tq // HALF, SUBLANES, HALF), F32),
            pltpu.VMEM((HEADS_PER_GROUP, tq // HALF, n_win, HALF), BF16),
            pltpu.VMEM((HEADS_PER_GROUP, HEAD_DIM, tq), F32),
        ],
        compiler_params=pltpu.CompilerParams(dimension_semantics=("parallel", "parallel", "arbitrary"),
                                             vmem_limit_bytes=VMEM_LIMIT),
        name="nsa_attn",
    )(qT, qrT, gT, szT, kcmp, vcmpT, ks, kw, vsT, vwT)


def _nsa_out_kernel(y_ref, h_ref, woT_ref, fn_ref, o_ref):
    oT = _dot(woT_ref[...], y_ref[0])
    h2 = h_ref[0] + oT.T
    o_ref[0] = h2 * _inv_rms(h2) * fn_ref[...]


def _nsa_out(yT, h3, woT, final_norm, *, tm=OUT_TILE):
    bsz, seq, d = h3.shape
    attn_d = yT.shape[1]
    return pl.pallas_call(
        _nsa_out_kernel,
        out_shape=jax.ShapeDtypeStruct((bsz, seq, d), F32),
        grid=(bsz, seq // tm),
        in_specs=[
            pl.BlockSpec((1, attn_d, tm), lambda b, i: (b, 0, i)),
            pl.BlockSpec((1, tm, d), lambda b, i: (b, i, 0)),
            pl.BlockSpec((d, attn_d), lambda b, i: (0, 0)),
            pl.BlockSpec((1, d), lambda b, i: (0, 0)),
        ],
        out_specs=pl.BlockSpec((1, tm, d), lambda b, i: (b, i, 0)),
        compiler_params=pltpu.CompilerParams(dimension_semantics=("parallel", "parallel"),
                                             vmem_limit_bytes=VMEM_LIMIT),
        name="nsa_out",
    )(yT, h3, woT, final_norm.reshape(1, d))


def _rope_tables(seq):
    pos = jnp.arange(seq, dtype=F32)
    inv = ROPE_THETA ** (-jnp.arange(0, ROT_DIM, 2, dtype=F32) / ROT_DIM)
    ang = pos[:, None] * inv[None, :]
    cos, sin = jnp.cos(ang), jnp.sin(ang)
    z = lambda n: jnp.zeros((seq, n), F32)
    cosf = jnp.concatenate([cos, cos, jnp.ones((seq, LANES - ROT_DIM), F32)], axis=1)
    sina = jnp.concatenate([-sin, z(LANES - N_FREQ)], axis=1)
    sinb = jnp.concatenate([z(N_FREQ), sin, z(LANES - ROT_DIM)], axis=1)
    return cos.T, sin.T, cosf, sina, sinb


def _blockdiag2(w1):
    hidden = w1.shape[1]
    w = w1.reshape(CMP_BLOCK, HEAD_DIM, hidden)
    zero = jnp.zeros_like(w)
    top = jnp.concatenate([w, zero], axis=2)
    bot = jnp.concatenate([zero, w], axis=2)
    return jnp.concatenate([top, bot], axis=1)


def kernel(x, a_norm, a_w_in, a_conv_w, a_w_out, kv_norm, w_kv, cmp_pos_k, cmp_w1_k, cmp_w2_k,
           cmp_pos_v, cmp_w1_v, cmp_w2_v, b_norm, b_w_in, b_w_out, final_norm):
    bsz, seq, d = x.shape
    attn_d = N_HEADS * HEAD_DIM
    kv_d = N_KV_GROUPS * HEAD_DIM
    assert b_norm.shape[0] == 1, "one NSA layer reads the shared K/V side"
    assert seq % KEY_CHUNK == 0

    h = x.reshape(bsz * seq, d)
    for layer in range(a_norm.shape[0]):
        h = _conv_layer(h, a_norm[layer], a_w_in[layer].astype(BF16), a_conv_w[layer],
                        a_w_out[layer].astype(BF16), seq=seq)
    h3 = h.reshape(bsz, seq, d)

    w_in = b_w_in[0]
    n_gate = N_HEADS * N_BRANCH
    wg = w_in[:, attn_d:attn_d + n_gate].reshape(d, N_KV_GROUPS, HEADS_PER_GROUP * N_BRANCH)
    wg = jnp.pad(wg, ((0, 0), (0, 0), (0, GATE_ROWS - HEADS_PER_GROUP * N_BRANCH)))
    waT = jnp.concatenate([w_in[:, :attn_d], w_in[:, attn_d + n_gate:], wg.reshape(d, -1)], axis=1).T.astype(BF16)
    wkv = w_kv.reshape(d, 2 * N_BRANCH, N_KV_GROUPS, HEAD_DIM)
    k_c, v_c, k_s, v_s, k_w, v_w = [wkv[:, i] for i in range(2 * N_BRANCH)]
    flat = lambda w: w.reshape(d, kv_d)
    pad_lanes = lambda w: jnp.pad(w, ((0, 0), (0, 0), (0, LANES - HEAD_DIM))).reshape(d, N_KV_GROUPS * LANES)
    wvT = jnp.concatenate([flat(v_s), flat(v_w)], axis=1).T.astype(BF16)
    wk = jnp.concatenate([flat(k_c), flat(v_c), pad_lanes(k_s), pad_lanes(k_w)], axis=1).astype(BF16)
    cosT, sinT, cosf, sina, sinb = _rope_tables(seq)

    qT, qrT, szT, gT, kvc, ks, kw, vsT, vwT = _nsa_proj(
        h3, b_norm[0], kv_norm, waT, wvT, wk, cosT, sinT, cosf, sina, sinb)

    w1bd = jnp.stack([_blockdiag2(cmp_w1_k), _blockdiag2(cmp_w1_v)]).astype(BF16)
    pos2 = jnp.stack([jnp.tile(cmp_pos_k, (1, 2)), jnp.tile(cmp_pos_v, (1, 2))])
    w2 = jnp.stack([cmp_w2_k, cmp_w2_v])
    w2p = jnp.pad(w2, ((0, 0), (0, 0), (0, LANES - HEAD_DIM))).astype(BF16)
    w2T = jnp.swapaxes(w2, 1, 2).astype(BF16)
    cmp_tm, cmp_fm = _compress(kvc, w1bd, pos2, w2p, w2T)

    yT = _nsa_attn(qT, qrT, gT, szT, cmp_tm[0], cmp_fm[1], ks, kw, vsT, vwT)
    return _nsa_out(yT, h3, b_w_out[0].T.astype(BF16), final_norm)
```

```python
import functools

import jax
import jax.numpy as jnp
from jax import lax
from jax.experimental import pallas as pl
from jax.experimental.pallas import tpu as pltpu

EPS = 1e-6
CONV_WIDTH = 3
N_HEADS = 16
HEAD_DIM = 64
N_KV_GROUPS = 4
HEADS_PER_GROUP = N_HEADS // N_KV_GROUPS
N_BRANCH = 3
ROT_DIM = HEAD_DIM // 4
N_FREQ = ROT_DIM // 2
ROPE_THETA = 500000.0
CMP_BLOCK = 32
CMP_STRIDE = 16
SLC_BLOCK = 64
N_SELECT = 16
N_LOCAL = 2
WINDOW = 512
FORCE_SCORE = 1e4

LANES = 128
SUBLANES = 8
KEY_CHUNK = 512
HALF = KEY_CHUNK // 2
PROJ_TILE = 256
OUT_TILE = 512
SOFTMAX_ROWS = 32
RANK_SECTION = 16
GATE_ROWS = 16
MASK_BIAS = -1e30
LOG2_E = 1.4426950408889634
VMEM_LIMIT = 56 * 1024 * 1024

BF16 = jnp.bfloat16
F32 = jnp.float32
NT_DIMS = (((1,), (1,)), ((), ()))


def _dot(a, b):
    return jnp.dot(a, b, preferred_element_type=F32)


def _dot_nt(a, b):
    return lax.dot_general(a, b, NT_DIMS, preferred_element_type=F32)


def _sigmoid(x):
    return 1.0 / (1.0 + jnp.exp(-x))


def _inv_rms(x):
    return lax.rsqrt(jnp.mean(x * x, axis=-1, keepdims=True) + EPS)


def _conv_layer_kernel(x_ref, g_ref, win_ref, cw_ref, wout_ref, o_ref, vbuf_ref, *,
                       tm, tiles_per_seq, conv_d, cchunk):
    @pl.when(pl.program_id(0) % tiles_per_seq == 0)
    def _():
        vbuf_ref[0:SUBLANES, :] = jnp.zeros((SUBLANES, conv_d), F32)

    x = x_ref[...]
    hn = (x * _inv_rms(x) * g_ref[...]).astype(BF16)
    acc = jnp.zeros(x.shape, F32)
    for cc in range(conv_d // cchunk):
        cs = cc * cchunk
        b = _dot(hn, win_ref[:, cs:cs + cchunk])
        c = _dot(hn, win_ref[:, conv_d + cs:conv_d + cs + cchunk])
        u = _dot(hn, win_ref[:, 2 * conv_d + cs:2 * conv_d + cs + cchunk])
        z = _dot(hn, win_ref[:, 3 * conv_d + cs:3 * conv_d + cs + cchunk])
        v = c * u
        vbuf_ref[SUBLANES:SUBLANES + tm, cs:cs + cchunk] = v
        v1 = vbuf_ref[SUBLANES - 1:SUBLANES - 1 + tm, cs:cs + cchunk]
        v2 = vbuf_ref[SUBLANES - 2:SUBLANES - 2 + tm, cs:cs + cchunk]
        conv = (cw_ref[0:1, cs:cs + cchunk] * v2 + cw_ref[1:2, cs:cs + cchunk] * v1
                + cw_ref[2:3, cs:cs + cchunk] * v)
        vbuf_ref[0:SUBLANES, cs:cs + cchunk] = v[tm - SUBLANES:tm, :]
        y = b * conv * (z * _sigmoid(z))
        acc = acc + _dot(y.astype(BF16), wout_ref[cs:cs + cchunk, :])
    o_ref[...] = x + acc


def _conv_layer(h, norm_g, w_in, conv_w, w_out, *, seq, tm=512, cchunk=1024):
    t, d = h.shape
    conv_d = conv_w.shape[1]
    tm = min(tm, seq)
    const = lambda shape: pl.BlockSpec(shape, lambda i: (0,) * len(shape), pipeline_mode=pl.Buffered(1))
    return pl.pallas_call(
        functools.partial(_conv_layer_kernel, tm=tm, tiles_per_seq=seq // tm, conv_d=conv_d, cchunk=cchunk),
        out_shape=jax.ShapeDtypeStruct((t, d), F32),
        grid=(t // tm,),
        in_specs=[
            pl.BlockSpec((tm, d), lambda i: (i, 0)),
            const((1, d)),
            const((d, 4 * conv_d)),
            const((CONV_WIDTH, conv_d)),
            const((conv_d, d)),
        ],
        out_specs=pl.BlockSpec((tm, d), lambda i: (i, 0)),
        scratch_shapes=[pltpu.VMEM((SUBLANES + tm, conv_d), F32)],
        compiler_params=pltpu.CompilerParams(dimension_semantics=("arbitrary",), vmem_limit_bytes=VMEM_LIMIT),
        name="conv_layer",
    )(h, norm_g.reshape(1, d), w_in, conv_w, w_out)


def _nsa_proj_kernel(h_ref, bn_ref, kn_ref, waT_ref, wvT_ref, wk_ref, cosT_ref, sinT_ref,
                     cosf_ref, sina_ref, sinb_ref,
                     qT_ref, qrT_ref, szT_ref, gT_ref, kvc_ref, ks_ref, kw_ref, vsT_ref, vwT_ref, *, tm, d):
    attn_d = N_HEADS * HEAD_DIM
    kv_d = N_KV_GROUPS * HEAD_DIM
    h = h_ref[0]
    hr = h * _inv_rms(h)
    hq = (hr * bn_ref[...]).astype(BF16)
    hk = (hr * kn_ref[...]).astype(BF16)

    qT = _dot_nt(waT_ref[0:attn_d, :], hq) * (HEAD_DIM ** -0.5 * LOG2_E)
    cosT = cosT_ref[...]
    sinT = sinT_ref[...]
    for hd in range(N_HEADS):
        r0 = hd * HEAD_DIM
        blk = qT[r0:r0 + HEAD_DIM, :]
        x1 = blk[0:N_FREQ, :]
        x2 = blk[N_FREQ:ROT_DIM, :]
        rot = jnp.concatenate([x1 * cosT - x2 * sinT, x2 * cosT + x1 * sinT, blk[ROT_DIM:, :]], axis=0)
        qT_ref[0, r0:r0 + HEAD_DIM, :] = blk.astype(BF16)
        qrT_ref[0, r0:r0 + HEAD_DIM, :] = rot.astype(BF16)

    zT = _dot_nt(waT_ref[attn_d:2 * attn_d, :], hq)
    szT_ref[0] = zT * _sigmoid(zT)
    gT_ref[0] = _sigmoid(_dot_nt(waT_ref[2 * attn_d:2 * attn_d + N_KV_GROUPS * GATE_ROWS, :], hq))

    vT = _dot_nt(wvT_ref[...], hk)
    vsT_ref[0, 0] = vT[0:kv_d, :].astype(BF16)
    vwT_ref[0, 0] = vT[kv_d:2 * kv_d, :].astype(BF16)

    kk = _dot(hk, wk_ref[...])
    kvc_ref[0] = kk[:, 0:2 * kv_d]
    cosf = cosf_ref[...]
    sina = sina_ref[...]
    sinb = sinb_ref[...]
    row = lax.broadcasted_iota(jnp.int32, (tm, LANES), 0)
    lane = lax.broadcasted_iota(jnp.int32, (tm, LANES), 1)
    tok_blk = (pl.program_id(1) * tm + row) // SLC_BLOCK
    onehot = jnp.where((lane >= HEAD_DIM) & (lane - HEAD_DIM == tok_blk), 1.0, 0.0)
    for j in range(2 * N_KV_GROUPS):
        c0 = 2 * kv_d + j * LANES
        xb = kk[:, c0:c0 + LANES]
        rot = (xb * cosf + pltpu.roll(xb, LANES - N_FREQ, axis=1) * sina
               + pltpu.roll(xb, N_FREQ, axis=1) * sinb)
        if j < N_KV_GROUPS:
            ks_ref[0, :, j * LANES:(j + 1) * LANES] = (rot + onehot).astype(BF16)
        else:
            jj = j - N_KV_GROUPS
            kw_ref[0, :, jj * LANES:(jj + 1) * LANES] = (rot + onehot).astype(BF16)


def _nsa_proj(h3, b_norm, kv_norm, waT, wvT, wk, cosT, sinT, cosf, sina, sinb, *, tm=PROJ_TILE):
    bsz, seq, d = h3.shape
    attn_d = N_HEADS * HEAD_DIM
    kv_d = N_KV_GROUPS * HEAD_DIM
    kpad = N_KV_GROUPS * LANES
    nt = seq // tm
    const = lambda shape: pl.BlockSpec(shape, lambda b, i: (0,) * len(shape))
    fm = lambda rows: pl.BlockSpec((1, rows, tm), lambda b, i: (b, 0, i))
    tmj = lambda cols: pl.BlockSpec((1, tm, cols), lambda b, i: (b, i, 0))
    out_shape = [
        jax.ShapeDtypeStruct((bsz, attn_d, seq), BF16),
        jax.ShapeDtypeStruct((bsz, attn_d, seq), BF16),
        jax.ShapeDtypeStruct((bsz, attn_d, seq), F32),
        jax.ShapeDtypeStruct((bsz, N_KV_GROUPS * GATE_ROWS, seq), F32),
        jax.ShapeDtypeStruct((bsz, seq, 2 * kv_d), F32),
        jax.ShapeDtypeStruct((bsz, seq, kpad), BF16),
        jax.ShapeDtypeStruct((bsz, seq, kpad), BF16),
        jax.ShapeDtypeStruct((bsz, nt, kv_d, tm), BF16),
        jax.ShapeDtypeStruct((bsz, nt, kv_d, tm), BF16),
    ]
    out_specs = [
        fm(attn_d), fm(attn_d), fm(attn_d), fm(N_KV_GROUPS * GATE_ROWS),
        tmj(2 * kv_d), tmj(kpad), tmj(kpad),
        pl.BlockSpec((1, 1, kv_d, tm), lambda b, i: (b, i, 0, 0)),
        pl.BlockSpec((1, 1, kv_d, tm), lambda b, i: (b, i, 0, 0)),
    ]
    return pl.pallas_call(
        functools.partial(_nsa_proj_kernel, tm=tm, d=d),
        out_shape=out_shape,
        grid=(bsz, nt),
        in_specs=[
            tmj(d), const((1, d)), const((1, d)),
            const(waT.shape), const(wvT.shape), const(wk.shape),
            pl.BlockSpec((N_FREQ, tm), lambda b, i: (0, i)),
            pl.BlockSpec((N_FREQ, tm), lambda b, i: (0, i)),
            pl.BlockSpec((tm, LANES), lambda b, i: (i, 0)),
            pl.BlockSpec((tm, LANES), lambda b, i: (i, 0)),
            pl.BlockSpec((tm, LANES), lambda b, i: (i, 0)),
        ],
        out_specs=out_specs,
        compiler_params=pltpu.CompilerParams(dimension_semantics=("parallel", "parallel"),
                                             vmem_limit_bytes=VMEM_LIMIT),
        name="nsa_proj",
    )(h3, b_norm.reshape(1, d), kv_norm.reshape(1, d), waT, wvT, wk, cosT, sinT, cosf, sina, sinb)


def _gelu_tanh(x):
    return x * (0.5 * (1.0 + jnp.tanh(0.7978845608028654 * (x + 0.044715 * (x * x * x)))))


def _compress_kernel(x_ref, w1_ref, pos_ref, w2_ref, w2T_ref, kc_ref, cT_ref, *, nchunk, hidden):
    u = jnp.zeros((nchunk, 2 * hidden), F32)
    v = jnp.zeros((nchunk, 2 * hidden), F32)
    for l in range(CMP_STRIDE):
        xl = x_ref[0, pl.ds(l, nchunk, stride=CMP_STRIDE), :]
        u = u + _dot((xl + pos_ref[0, l:l + 1, :]).astype(BF16), w1_ref[0, l])
        l2 = CMP_STRIDE + l
        v = v + _dot((xl + pos_ref[0, l2:l2 + 1, :]).astype(BF16), w1_ref[0, l2])
    hid = u + pltpu.roll(v, nchunk - 1, axis=0)
    act = _gelu_tanh(hid).astype(BF16)
    rows_ok = lax.broadcasted_iota(jnp.int32, (nchunk, LANES), 0) < nchunk - 1
    cols_ok = lax.broadcasted_iota(jnp.int32, (HEAD_DIM, nchunk), 1) < nchunk - 1
    for p in range(2):
        a = act[:, p * hidden:(p + 1) * hidden]
        kc_ref[0, 0, p] = jnp.where(rows_ok, _dot(a, w2_ref[0]), 0.0).astype(BF16)
        cT_ref[0, 0, p] = jnp.where(cols_ok, _dot_nt(w2T_ref[0], a), 0.0).astype(BF16)


def _compress(kvc, w1bd, pos2, w2p, w2T):
    bsz, seq, _ = kvc.shape
    nchunk = seq // CMP_STRIDE
    hidden = w2T.shape[2]
    return pl.pallas_call(
        functools.partial(_compress_kernel, nchunk=nchunk, hidden=hidden),
        out_shape=[
            jax.ShapeDtypeStruct((2, bsz, N_KV_GROUPS, nchunk, LANES), BF16),
            jax.ShapeDtypeStruct((2, bsz, N_KV_GROUPS, HEAD_DIM, nchunk), BF16),
        ],
        grid=(2, bsz, N_KV_GROUPS // 2),
        in_specs=[
            pl.BlockSpec((1, seq, LANES), lambda s, b, p: (b, 0, 2 * s + p)),
            pl.BlockSpec((1, CMP_BLOCK, LANES, 2 * hidden), lambda s, b, p: (s, 0, 0, 0)),
            pl.BlockSpec((1, CMP_BLOCK, LANES), lambda s, b, p: (s, 0, 0)),
            pl.BlockSpec((1, hidden, LANES), lambda s, b, p: (s, 0, 0)),
            pl.BlockSpec((1, HEAD_DIM, hidden), lambda s, b, p: (s, 0, 0)),
        ],
        out_specs=[
            pl.BlockSpec((1, 1, 2, nchunk, LANES), lambda s, b, p: (s, b, p, 0, 0)),
            pl.BlockSpec((1, 1, 2, HEAD_DIM, nchunk), lambda s, b, p: (s, b, p, 0, 0)),
        ],
        compiler_params=pltpu.CompilerParams(dimension_semantics=("parallel", "parallel", "parallel"),
                                             vmem_limit_bytes=VMEM_LIMIT),
        name="compress",
    )(kvc, w1bd, pos2, w2p, w2T)


def _attn_kernel(q_ref, qr_ref, g_ref, sz_ref, kc_ref, vcT_ref, ks_ref, kw_ref, vsT_ref, vwT_ref, y_ref,
                 p_sc, sc_sc, rank_sc, oc_sc, qa_sc, c_sc, w_sc, s_sc, pb_sc, smax_sc, st_sc, acc_sc, wmax_sc,
                 wp_sc, ow_sc, *, tq, n_cmp):
    qi = pl.program_id(2)
    t0 = qi * tq
    neg_inf = -jnp.inf
    tvec = t0 + lax.broadcasted_iota(jnp.int32, (1, tq), 1)

    kc = kc_ref[0, 0, :, 0:HEAD_DIM]
    cmp_end = lax.broadcasted_iota(jnp.int32, (n_cmp, tq), 0) * CMP_STRIDE + (CMP_BLOCK - 1)
    cmask = cmp_end <= tvec
    heads = range(HEADS_PER_GROUP)

    def fold8(x, op, ways=4):
        parts = [None] * ways
        for idx, r in enumerate(range(0, x.shape[0], SUBLANES)):
            slab = x[r:r + SUBLANES, :]
            parts[idx % ways] = slab if parts[idx % ways] is None else op(parts[idx % ways], slab)
        return functools.reduce(op, [p for p in parts if p is not None])

    def key_chunk(ref, c):
        return ref[0, pl.ds(pl.multiple_of(c * KEY_CHUNK, KEY_CHUNK), KEY_CHUNK), :]

    n_sb = n_cmp // 4
    jrow = lax.broadcasted_iota(jnp.int32, (n_sb, tq), 0)
    cur = tvec // SLC_BLOCK
    valid = jrow <= cur

    QA_SEL, QA_DEAD, QA_WIN = 0, 1, 2
    pad_blocks = lambda b: b if n_sb == HEAD_DIM else jnp.concatenate(
        [b, jnp.zeros((HEAD_DIM - n_sb, tq), BF16)], axis=0)

    def set_operand(idx, bias):
        for hh in heads:
            r0 = hh * HEAD_DIM
            qa_sc[idx, hh] = jnp.concatenate([qr_ref[0, r0:r0 + HEAD_DIM, :], pad_blocks(bias)], axis=0)

    win_blocks = WINDOW // SLC_BLOCK
    set_operand(QA_WIN, jnp.where((jrow >= cur - win_blocks) & valid, 0.0, MASK_BIAS).astype(BF16))
    set_operand(QA_DEAD, jnp.full((n_sb, tq), MASK_BIAS, BF16))

    blk_row = lax.broadcasted_iota(jnp.int32, (SLC_BLOCK, LANES), 0)
    blk_lane = lax.broadcasted_iota(jnp.int32, (SLC_BLOCK, LANES), 1)
    edge = [blk_row - blk_lane, blk_row - (blk_lane - SLC_BLOCK)]
    off_diag = [blk_lane >= SLC_BLOCK, blk_lane < SLC_BLOCK]

    def mask_diagonal_blocks(s, keep):
        blocks = []
        for bb in range(KEY_CHUNK // SLC_BLOCK):
            rows = s[bb * SLC_BLOCK:(bb + 1) * SLC_BLOCK, :]
            halves = [rows[:, h * LANES:(h + 1) * LANES] for h in range(tq // LANES)]
            h = bb // 2
            halves[h] = jnp.where(off_diag[bb % 2] | keep(edge[bb % 2]), halves[h], neg_inf)
            blocks.append(jnp.concatenate(halves, axis=1))
        return jnp.concatenate(blocks, axis=0)

    n_win = KEY_CHUNK + HALF
    prev_operand = jnp.where(qi >= 1, QA_WIN, QA_DEAD)
    prev_base = jnp.maximum(qi - 1, 0) * KEY_CHUNK

    def mask_half_blocks(s, first_block, keep):
        blocks = [s[r:r + SLC_BLOCK, :] for r in range(0, s.shape[0], SLC_BLOCK)]
        for j in range(HALF // SLC_BLOCK):
            halves = [blocks[first_block + j][:, h * LANES:(h + 1) * LANES] for h in range(HALF // LANES)]
            halves[j // 2] = jnp.where(off_diag[j % 2] | keep(edge[j % 2]), halves[j // 2], neg_inf)
            blocks[first_block + j] = jnp.concatenate(halves, axis=1)
        return jnp.concatenate(blocks, axis=0)

    def window_pieces(half):
        first = half * HALF
        n_old = KEY_CHUNK - first
        old = (pl.multiple_of(prev_base + first, HALF), n_old)
        new = (pl.multiple_of(qi * KEY_CHUNK, HALF), HALF + first)
        return old, new

    for hh in heads:
        r0 = hh * HEAD_DIM
        c_sc[hh] = jnp.where(cmask, _dot(kc, q_ref[0, r0:r0 + HEAD_DIM, :]), neg_inf)
        for half in range(tq // HALF):
            (old_start, n_old), (new_start, n_new) = window_pieces(half)
            lanes = slice(half * HALF, (half + 1) * HALF)
            s_old = _dot(kw_ref[0, pl.ds(old_start, n_old), :], qa_sc[prev_operand, hh, :, lanes])
            s_new = _dot(kw_ref[0, pl.ds(new_start, n_new), :], qa_sc[QA_WIN, hh, :, lanes])
            s_old = mask_half_blocks(s_old, 0, lambda e: e > 0)
            s_new = mask_half_blocks(s_new, half * (HALF // SLC_BLOCK), lambda e: e <= 0)
            w_sc[hh, half, 0:n_old, :] = s_old
            w_sc[hh, half, n_old:n_win, :] = s_new
            wmax_sc[hh, half] = jnp.maximum(fold8(s_old, jnp.maximum), fold8(s_new, jnp.maximum))
    probs = []
    for hh in heads:
        s = c_sc[hh]
        m = jnp.max(fold8(s, jnp.maximum), axis=0, keepdims=True)
        m = jnp.where(m == neg_inf, 0.0, m)
        e = jnp.exp2(s - m)
        den = jnp.sum(fold8(e, jnp.add), axis=0, keepdims=True)
        probs.append(e * (1.0 / jnp.maximum(den, 1e-30)))
    p_grp = functools.reduce(jnp.add, probs)
    for hh in heads:
        r0 = hh * HEAD_DIM
        oc_sc[r0:r0 + HEAD_DIM, :] = _dot(vcT_ref[0, 0], probs[hh].astype(BF16))

    ratio = SLC_BLOCK // CMP_STRIDE
    imp_cols = []
    for c in range(tq // LANES):
        p_sc[c, 0:SUBLANES, :] = jnp.zeros((SUBLANES, LANES), F32)
        p_sc[c, SUBLANES:SUBLANES + n_cmp, :] = p_grp[:, c * LANES:(c + 1) * LANES]
        tap = lambda o: p_sc[c, pl.ds(SUBLANES + o, n_sb, stride=ratio), :]
        imp_cols.append(tap(-1) + 2.0 * (tap(0) + tap(1) + tap(2)) + tap(3))
    imp = jnp.concatenate(imp_cols, axis=1)

    forced = (jrow == 0) | (valid & (jrow > cur - N_LOCAL))
    score = jnp.where(valid, jnp.where(forced, FORCE_SCORE, imp), neg_inf)
    sc_sc[...] = score

    n_valid = (t0 + tq) // SLC_BLOCK
    n_slabs = n_sb // SUBLANES
    slab_row = lax.broadcasted_iota(jnp.int32, (SUBLANES, tq), 0)
    rank_sc[...] = jnp.zeros((n_sb, tq), jnp.int32)
    for first in range(0, n_sb, RANK_SECTION):
        @pl.when(n_valid > max(N_SELECT, first))
        def _():
            slabs = [sc_sc[k * SUBLANES:(k + 1) * SUBLANES, :] for k in range(n_slabs)]
            counts = [rank_sc[k * SUBLANES:(k + 1) * SUBLANES, :] for k in range(n_slabs)]
            for jp in range(first, first + RANK_SECTION):
                sb = sc_sc[jp:jp + 1, :]
                for k in range(n_slabs):
                    if k * SUBLANES > jp:
                        before = sb >= slabs[k]
                    elif k * SUBLANES + SUBLANES - 1 < jp:
                        before = sb > slabs[k]
                    else:
                        before = (sb > slabs[k]) | ((sb == slabs[k]) & (slab_row > jp % SUBLANES))
                    counts[k] = counts[k] + jnp.where(before, 1, 0)
            for k in range(n_slabs):
                rank_sc[k * SUBLANES:(k + 1) * SUBLANES, :] = counts[k]
    set_operand(QA_SEL, jnp.where((rank_sc[...] < N_SELECT) & valid, 0.0, MASK_BIAS).astype(BF16))

    group = KEY_CHUNK
    n_double = qi // 2

    def values_of(ref, c):
        per = KEY_CHUNK // PROJ_TILE
        return jnp.concatenate([ref[0, per * c + v] for v in range(per)], axis=1)

    ST_MAX, ST_SUM, ST_RESCALE = 0, 1, 2
    rows8 = lambda x: jnp.broadcast_to(x, (SUBLANES, tq))
    never = 1 << 20

    def scores_to(slot, pos, hh, may_be_last):
        s = _dot(key_chunk(ks_ref, pos), qa_sc[QA_SEL, hh])
        if may_be_last:
            causal_slack = jnp.where(pos == qi, 0, never)
            s = mask_diagonal_blocks(s, lambda e: e <= causal_slack)
        s_sc[slot, hh] = s
        smax_sc[slot, hh] = fold8(s, jnp.maximum)

    def accumulate(pos, slot, hh):
        acc_sc[hh] = (acc_sc[hh] * st_sc[ST_RESCALE + slot, hh, 0:1, :]
                      + _dot(values_of(vsT_ref, pos), pb_sc[slot, hh]))

    def softmax_group(slot, hh):
        m_old = st_sc[ST_MAX, hh, 0:1, :]
        m_new = jnp.maximum(m_old, jnp.max(smax_sc[slot, hh], axis=0, keepdims=True))
        a = jnp.exp2(m_old - m_new)
        psum = None
        for r in range(0, group, SOFTMAX_ROWS):
            s = s_sc[slot, hh, r:r + SOFTMAX_ROWS, :]
            p = jnp.exp2(s - m_new)
            pb_sc[slot, hh, r:r + SOFTMAX_ROWS, :] = p.astype(BF16)
            f = fold8(p, jnp.add, ways=2)
            psum = f if psum is None else psum + f
        st_sc[ST_SUM, hh] = a * st_sc[ST_SUM, hh] + psum
        st_sc[ST_MAX, hh] = rows8(m_new)
        st_sc[ST_RESCALE + slot, hh] = rows8(a)

    def stage(slot, pos, next_may_be_last):
        for hh in heads:
            scores_to(1 - slot, pos + 1, hh, next_may_be_last)
            accumulate(jnp.maximum(pos - 1, 0), 1 - slot, hh)
            softmax_group(slot, hh)

    def drain(slot):
        for hh in heads:
            accumulate(jnp.maximum(qi - 1, 0), 1 - slot, hh)
            softmax_group(slot, hh)
            accumulate(qi, slot, hh)

    def double_trip(d, carry):
        stage(0, 2 * d, next_may_be_last=False)
        stage(1, 2 * d + 1, next_may_be_last=True)
        return carry

    def window_values(half):
        pieces = []
        for start, rows in window_pieces(half):
            pieces += [vwT_ref[0, start // PROJ_TILE + v] for v in range(rows // PROJ_TILE)]
        return jnp.concatenate(pieces, axis=1)

    for hh in heads:
        acc_sc[hh] = jnp.zeros((HEAD_DIM, tq), F32)
        pb_sc[1, hh] = jnp.zeros((group, tq), BF16)
        st_sc[ST_MAX, hh] = jnp.full((SUBLANES, tq), neg_inf, F32)
        st_sc[ST_SUM, hh] = jnp.zeros((SUBLANES, tq), F32)
        st_sc[ST_RESCALE + 1, hh] = jnp.ones((SUBLANES, tq), F32)
        scores_to(0, 0, hh, may_be_last=True)
        for half in range(tq // HALF):
            m = jnp.max(wmax_sc[hh, half], axis=0, keepdims=True)
            psum = None
            for r in range(0, n_win, 2 * SOFTMAX_ROWS):
                p = jnp.exp2(w_sc[hh, half, r:r + 2 * SOFTMAX_ROWS, :] - m)
                wp_sc[hh, half, r:r + 2 * SOFTMAX_ROWS, :] = p.astype(BF16)
                f = fold8(p, jnp.add, ways=2)
                psum = f if psum is None else psum + f
            ow_sc[hh, :, half * HALF:(half + 1) * HALF] = (
                _dot(window_values(half), wp_sc[hh, half]) * (1.0 / jnp.sum(psum, axis=0, keepdims=True)))

    lax.fori_loop(0, n_double, double_trip, 0)

    @pl.when(qi % 2 == 0)
    def _():
        drain(0)

    @pl.when(qi % 2 == 1)
    def _():
        stage(0, qi - 1, next_may_be_last=True)
        drain(1)

    for hh in heads:
        r0 = hh * HEAD_DIM
        o_w = ow_sc[hh]
        o_s = acc_sc[hh] * (1.0 / jnp.sum(st_sc[ST_SUM, hh], axis=0, keepdims=True))

        g0 = g_ref[0, N_BRANCH * hh:N_BRANCH * hh + 1, :]
        g1 = g_ref[0, N_BRANCH * hh + 1:N_BRANCH * hh + 2, :]
        g2 = g_ref[0, N_BRANCH * hh + 2:N_BRANCH * hh + 3, :]
        o = g0 * oc_sc[r0:r0 + HEAD_DIM, :] + g1 * o_s + g2 * o_w
        y_ref[0, r0:r0 + HEAD_DIM, :] = (o * sz_ref[0, r0:r0 + HEAD_DIM, :]).astype(BF16)


def _nsa_attn(qT, qrT, gT, szT, kcmp, vcmpT, ks, kw, vsT, vwT, *, tq=KEY_CHUNK):
    bsz, attn_d, seq = qT.shape
    n_cmp = kcmp.shape[2]
    n_win = KEY_CHUNK + HALF
    gd = HEADS_PER_GROUP * HEAD_DIM
    assert tq == KEY_CHUNK == WINDOW and HALF % PROJ_TILE == 0 and vsT.shape[3] == PROJ_TILE
    qspec = pl.BlockSpec((1, gd, tq), lambda b, g, i: (b, g, i))
    kspec = pl.BlockSpec((1, seq, LANES), lambda b, g, i: (b, 0, g))
    vspec = pl.BlockSpec((1, seq // PROJ_TILE, HEAD_DIM, PROJ_TILE), lambda b, g, i: (b, 0, g, 0))
    return pl.pallas_call(
        functools.partial(_attn_kernel, tq=tq, n_cmp=n_cmp),
        out_shape=jax.ShapeDtypeStruct((bsz, attn_d, seq), BF16),
        grid=(bsz, N_KV_GROUPS, seq // tq),
        in_specs=[
            qspec, qspec,
            pl.BlockSpec((1, GATE_ROWS, tq), lambda b, g, i: (b, g, i)),
            qspec,
            pl.BlockSpec((1, 1, n_cmp, LANES), lambda b, g, i: (b, g, 0, 0)),
            pl.BlockSpec((1, 1, HEAD_DIM, n_cmp), lambda b, g, i: (b, g, 0, 0)),
            kspec, kspec, vspec, vspec,
        ],
        out_specs=qspec,
        scratch_shapes=[
            pltpu.VMEM((tq // LANES, SUBLANES + n_cmp, LANES), F32),
            pltpu.VMEM((n_cmp // 4, tq), F32),
            pltpu.VMEM((n_cmp // 4, tq), jnp.int32),
            pltpu.VMEM((gd, tq), F32),
            pltpu.VMEM((3, HEADS_PER_GROUP, 2 * HEAD_DIM, tq), BF16),
            pltpu.VMEM((HEADS_PER_GROUP, n_cmp, tq), F32),
            pltpu.VMEM((HEADS_PER_GROUP, tq // HALF, n_win, HALF), F32),
            pltpu.VMEM((2, HEADS_PER_GROUP, KEY_CHUNK, tq), F32),
            pltpu.VMEM((2, HEADS_PER_GROUP, KEY_CHUNK, tq), BF16),
            pltpu.VMEM((2, HEADS_PER_GROUP, SUBLANES, tq), F32),
            pltpu.VMEM((4, HEADS_PER_GROUP, SUBLANES, tq), F32),
            pltpu.VMEM((HEADS_PER_GROUP, HEAD_DIM, tq), F32),
            pltpu.VMEM((HEADS_PER_GROUP, tq // HALF, SUBLANES, HALF), F32),
            pltpu.VMEM((HEADS_PER_GROUP, tq // HALF, n_win, HALF), BF16),
            pltpu.VMEM((HEADS_PER_GROUP, HEAD_DIM, tq), F32),
        ],
        compiler_params=pltpu.CompilerParams(dimension_semantics=("parallel", "parallel", "arbitrary"),
                                             vmem_limit_bytes=VMEM_LIMIT),
        name="nsa_attn",
    )(qT, qrT, gT, szT, kcmp, vcmpT, ks, kw, vsT, vwT)


def _nsa_out_kernel(y_ref, h_ref, woT_ref, fn_ref, o_ref):
    oT = _dot(woT_ref[...], y_ref[0])
    h2 = h_ref[0] + oT.T
    o_ref[0] = h2 * _inv_rms(h2) * fn_ref[...]


def _nsa_out(yT, h3, woT, final_norm, *, tm=OUT_TILE):
    bsz, seq, d = h3.shape
    attn_d = yT.shape[1]
    return pl.pallas_call(
        _nsa_out_kernel,
        out_shape=jax.ShapeDtypeStruct((bsz, seq, d), F32),
        grid=(bsz, seq // tm),
        in_specs=[
            pl.BlockSpec((1, attn_d, tm), lambda b, i: (b, 0, i)),
            pl.BlockSpec((1, tm, d), lambda b, i: (b, i, 0)),
            pl.BlockSpec((d, attn_d), lambda b, i: (0, 0)),
            pl.BlockSpec((1, d), lambda b, i: (0, 0)),
        ],
        out_specs=pl.BlockSpec((1, tm, d), lambda b, i: (b, i, 0)),
        compiler_params=pltpu.CompilerParams(dimension_semantics=("parallel", "parallel"),
                                             vmem_limit_bytes=VMEM_LIMIT),
        name="nsa_out",
    )(yT, h3, woT, final_norm.reshape(1, d))


def _rope_tables(seq):
    pos = jnp.arange(seq, dtype=F32)
    inv = ROPE_THETA ** (-jnp.arange(0, ROT_DIM, 2, dtype=F32) / ROT_DIM)
    ang = pos[:, None] * inv[None, :]
    cos, sin = jnp.cos(ang), jnp.sin(ang)
    z = lambda n: jnp.zeros((seq, n), F32)
    cosf = jnp.concatenate([cos, cos, jnp.ones((seq, LANES - ROT_DIM), F32)], axis=1)
    sina = jnp.concatenate([-sin, z(LANES - N_FREQ)], axis=1)
    sinb = jnp.concatenate([z(N_FREQ), sin, z(LANES - ROT_DIM)], axis=1)
    return cos.T, sin.T, cosf, sina, sinb


def _blockdiag2(w1):
    hidden = w1.shape[1]
    w = w1.reshape(CMP_BLOCK, HEAD_DIM, hidden)
    zero = jnp.zeros_like(w)
    top = jnp.concatenate([w, zero], axis=2)
    bot = jnp.concatenate([zero, w], axis=2)
    return jnp.concatenate([top, bot], axis=1)


def kernel(x, a_norm, a_w_in, a_conv_w, a_w_out, kv_norm, w_kv, cmp_pos_k, cmp_w1_k, cmp_w2_k,
           cmp_pos_v, cmp_w1_v, cmp_w2_v, b_norm, b_w_in, b_w_out, final_norm):
    bsz, seq, d = x.shape
    attn_d = N_HEADS * HEAD_DIM
    kv_d = N_KV_GROUPS * HEAD_DIM
    assert b_norm.shape[0] == 1, "one NSA layer reads the shared K/V side"
    assert seq % KEY_CHUNK == 0

    h = x.reshape(bsz * seq, d)
    for layer in range(a_norm.shape[0]):
        h = _conv_layer(h, a_norm[layer], a_w_in[layer].astype(BF16), a_conv_w[layer],
                        a_w_out[layer].astype(BF16), seq=seq)
    h3 = h.reshape(bsz, seq, d)

    w_in = b_w_in[0]
    n_gate = N_HEADS * N_BRANCH
    wg = w_in[:, attn_d:attn_d + n_gate].reshape(d, N_KV_GROUPS, HEADS_PER_GROUP * N_BRANCH)
    wg = jnp.pad(wg, ((0, 0), (0, 0), (0, GATE_ROWS - HEADS_PER_GROUP * N_BRANCH)))
    waT = jnp.concatenate([w_in[:, :attn_d], w_in[:, attn_d + n_gate:], wg.reshape(d, -1)], axis=1).T.astype(BF16)
    wkv = w_kv.reshape(d, 2 * N_BRANCH, N_KV_GROUPS, HEAD_DIM)
    k_c, v_c, k_s, v_s, k_w, v_w = [wkv[:, i] for i in range(2 * N_BRANCH)]
    flat = lambda w: w.reshape(d, kv_d)
    pad_lanes = lambda w: jnp.pad(w, ((0, 0), (0, 0), (0, LANES - HEAD_DIM))).reshape(d, N_KV_GROUPS * LANES)
    wvT = jnp.concatenate([flat(v_s), flat(v_w)], axis=1).T.astype(BF16)
    wk = jnp.concatenate([flat(k_c), flat(v_c), pad_lanes(k_s), pad_lanes(k_w)], axis=1).astype(BF16)
    cosT, sinT, cosf, sina, sinb = _rope_tables(seq)

    qT, qrT, szT, gT, kvc, ks, kw, vsT, vwT = _nsa_proj(
        h3, b_norm[0], kv_norm, waT, wvT, wk, cosT, sinT, cosf, sina, sinb)

    w1bd = jnp.stack([_blockdiag2(cmp_w1_k), _blockdiag2(cmp_w1_v)]).astype(BF16)
    pos2 = jnp.stack([jnp.tile(cmp_pos_k, (1, 2)), jnp.tile(cmp_pos_v, (1, 2))])
    w2 = jnp.stack([cmp_w2_k, cmp_w2_v])
    w2p = jnp.pad(w2, ((0, 0), (0, 0), (0, LANES - HEAD_DIM))).astype(BF16)
    w2T = jnp.swapaxes(w2, 1, 2).astype(BF16)
    cmp_tm, cmp_fm = _compress(kvc, w1bd, pos2, w2p, w2T)

    yT = _nsa_attn(qT, qrT, gT, szT, cmp_tm[0], cmp_fm[1], ks, kw, vsT, vwT)
    return _nsa_out(yT, h3, b_w_out[0].T.astype(BF16), final_norm)
```

```python
import functools

import jax
import jax.numpy as jnp
from jax import lax
from jax.experimental import pallas as pl
from jax.experimental.pallas import tpu as pltpu

EPS = 1e-6
CONV_WIDTH = 3
N_HEADS = 16
HEAD_DIM = 64
N_KV_GROUPS = 4
HEADS_PER_GROUP = N_HEADS // N_KV_GROUPS
N_BRANCH = 3
ROT_DIM = HEAD_DIM // 4
N_FREQ = ROT_DIM // 2
ROPE_THETA = 500000.0
CMP_BLOCK = 32
CMP_STRIDE = 16
SLC_BLOCK = 64
N_SELECT = 16
N_LOCAL = 2
WINDOW = 512
FORCE_SCORE = 1e4

LANES = 128
SUBLANES = 8
KEY_CHUNK = 512
HALF = KEY_CHUNK // 2
PROJ_TILE = 256
PROJ_STEP = 512
OUT_TILE = 512
SOFTMAX_ROWS = 32
RANK_SECTION = 16
GATE_ROWS = 16
MASK_BIAS = -1e30
LOG2_E = 1.4426950408889634
VMEM_LIMIT = 56 * 1024 * 1024

BF16 = jnp.bfloat16
F32 = jnp.float32
NT_DIMS = (((1,), (1,)), ((), ()))


def _dot(a, b):
    return jnp.dot(a, b, preferred_element_type=F32)


def _dot_nt(a, b):
    return lax.dot_general(a, b, NT_DIMS, preferred_element_type=F32)


def _sigmoid(x):
    return 1.0 / (1.0 + jnp.exp(-x))


def _inv_rms(x):
    return lax.rsqrt(jnp.mean(x * x, axis=-1, keepdims=True) + EPS)


def _conv_layer_kernel(x_ref, g_ref, win_ref, cw_ref, wout_ref, o_ref, vbuf_ref, *,
                       tm, tiles_per_seq, conv_d, cchunk):
    @pl.when(pl.program_id(0) % tiles_per_seq == 0)
    def _():
        vbuf_ref[0:SUBLANES, :] = jnp.zeros((SUBLANES, conv_d), F32)

    x = x_ref[...]
    hn = (x * _inv_rms(x) * g_ref[...]).astype(BF16)
    acc = jnp.zeros(x.shape, F32)
    for cc in range(conv_d // cchunk):
        cs = cc * cchunk
        b = _dot(hn, win_ref[:, cs:cs + cchunk])
        c = _dot(hn, win_ref[:, conv_d + cs:conv_d + cs + cchunk])
        u = _dot(hn, win_ref[:, 2 * conv_d + cs:2 * conv_d + cs + cchunk])
        z = _dot(hn, win_ref[:, 3 * conv_d + cs:3 * conv_d + cs + cchunk])
        v = c * u
        vbuf_ref[SUBLANES:SUBLANES + tm, cs:cs + cchunk] = v
        v1 = vbuf_ref[SUBLANES - 1:SUBLANES - 1 + tm, cs:cs + cchunk]
        v2 = vbuf_ref[SUBLANES - 2:SUBLANES - 2 + tm, cs:cs + cchunk]
        conv = (cw_ref[0:1, cs:cs + cchunk] * v2 + cw_ref[1:2, cs:cs + cchunk] * v1
                + cw_ref[2:3, cs:cs + cchunk] * v)
        vbuf_ref[0:SUBLANES, cs:cs + cchunk] = v[tm - SUBLANES:tm, :]
        y = b * conv * (z * _sigmoid(z))
        acc = acc + _dot(y.astype(BF16), wout_ref[cs:cs + cchunk, :])
    o_ref[...] = x + acc


def _conv_layer(h, norm_g, w_in, conv_w, w_out, *, seq, tm=512, cchunk=1024):
    t, d = h.shape
    conv_d = conv_w.shape[1]
    tm = min(tm, seq)
    const = lambda shape: pl.BlockSpec(shape, lambda i: (0,) * len(shape), pipeline_mode=pl.Buffered(1))
    return pl.pallas_call(
        functools.partial(_conv_layer_kernel, tm=tm, tiles_per_seq=seq // tm, conv_d=conv_d, cchunk=cchunk),
        out_shape=jax.ShapeDtypeStruct((t, d), F32),
        grid=(t // tm,),
        in_specs=[
            pl.BlockSpec((tm, d), lambda i: (i, 0)),
            const((1, d)),
            const((d, 4 * conv_d)),
            const((CONV_WIDTH, conv_d)),
            const((conv_d, d)),
        ],
        out_specs=pl.BlockSpec((tm, d), lambda i: (i, 0)),
        scratch_shapes=[pltpu.VMEM((SUBLANES + tm, conv_d), F32)],
        compiler_params=pltpu.CompilerParams(dimension_semantics=("arbitrary",), vmem_limit_bytes=VMEM_LIMIT),
        name="conv_layer",
    )(h, norm_g.reshape(1, d), w_in, conv_w, w_out)


def _nsa_proj_kernel(h_ref, bn_ref, kn_ref, waT_ref, wvT_ref, wk_ref, cosT_ref, sinT_ref,
                     cosf_ref, sina_ref, sinb_ref,
                     qT_ref, qrT_ref, szT_ref, gT_ref, kvc_ref, ks_ref, kw_ref, vsT_ref, vwT_ref, *, tm, d):
    attn_d = N_HEADS * HEAD_DIM
    kv_d = N_KV_GROUPS * HEAD_DIM
    h = h_ref[0]
    hr = h * _inv_rms(h)
    hq = (hr * bn_ref[...]).astype(BF16)
    hk = (hr * kn_ref[...]).astype(BF16)

    qT = _dot_nt(waT_ref[0:attn_d, :], hq) * (HEAD_DIM ** -0.5 * LOG2_E)
    cosT = cosT_ref[...]
    sinT = sinT_ref[...]
    for hd in range(N_HEADS):
        r0 = hd * HEAD_DIM
        blk = qT[r0:r0 + HEAD_DIM, :]
        x1 = blk[0:N_FREQ, :]
        x2 = blk[N_FREQ:ROT_DIM, :]
        rot = jnp.concatenate([x1 * cosT - x2 * sinT, x2 * cosT + x1 * sinT, blk[ROT_DIM:, :]], axis=0)
        qT_ref[0, r0:r0 + HEAD_DIM, :] = blk.astype(BF16)
        qrT_ref[0, r0:r0 + HEAD_DIM, :] = rot.astype(BF16)

    zT = _dot_nt(waT_ref[attn_d:2 * attn_d, :], hq)
    szT_ref[0] = zT * _sigmoid(zT)
    gT_ref[0] = _sigmoid(_dot_nt(waT_ref[2 * attn_d:2 * attn_d + N_KV_GROUPS * GATE_ROWS, :], hq))

    vT = _dot_nt(wvT_ref[...], hk)
    for v in range(tm // PROJ_TILE):
        cols = slice(v * PROJ_TILE, (v + 1) * PROJ_TILE)
        vsT_ref[0, v] = vT[0:kv_d, cols].astype(BF16)
        vwT_ref[0, v] = vT[kv_d:2 * kv_d, cols].astype(BF16)

    kk = _dot(hk, wk_ref[...])
    kvc_ref[0] = kk[:, 0:2 * kv_d]
    cosf = cosf_ref[...]
    sina = sina_ref[...]
    sinb = sinb_ref[...]
    row = lax.broadcasted_iota(jnp.int32, (tm, LANES), 0)
    lane = lax.broadcasted_iota(jnp.int32, (tm, LANES), 1)
    tok_blk = (pl.program_id(1) * tm + row) // SLC_BLOCK
    onehot = jnp.where((lane >= HEAD_DIM) & (lane - HEAD_DIM == tok_blk), 1.0, 0.0)
    for j in range(2 * N_KV_GROUPS):
        c0 = 2 * kv_d + j * LANES
        xb = kk[:, c0:c0 + LANES]
        rot = (xb * cosf + pltpu.roll(xb, LANES - N_FREQ, axis=1) * sina
               + pltpu.roll(xb, N_FREQ, axis=1) * sinb)
        if j < N_KV_GROUPS:
            ks_ref[0, :, j * LANES:(j + 1) * LANES] = (rot + onehot).astype(BF16)
        else:
            jj = j - N_KV_GROUPS
            kw_ref[0, :, jj * LANES:(jj + 1) * LANES] = (rot + onehot).astype(BF16)


def _nsa_proj(h3, b_norm, kv_norm, waT, wvT, wk, cosT, sinT, cosf, sina, sinb, *, tm=PROJ_STEP):
    bsz, seq, d = h3.shape
    attn_d = N_HEADS * HEAD_DIM
    kv_d = N_KV_GROUPS * HEAD_DIM
    kpad = N_KV_GROUPS * LANES
    nt = seq // tm
    const = lambda shape: pl.BlockSpec(shape, lambda b, i: (0,) * len(shape))
    fm = lambda rows: pl.BlockSpec((1, rows, tm), lambda b, i: (b, 0, i))
    tmj = lambda cols: pl.BlockSpec((1, tm, cols), lambda b, i: (b, i, 0))
    out_shape = [
        jax.ShapeDtypeStruct((bsz, attn_d, seq), BF16),
        jax.ShapeDtypeStruct((bsz, attn_d, seq), BF16),
        jax.ShapeDtypeStruct((bsz, attn_d, seq), F32),
        jax.ShapeDtypeStruct((bsz, N_KV_GROUPS * GATE_ROWS, seq), F32),
        jax.ShapeDtypeStruct((bsz, seq, 2 * kv_d), F32),
        jax.ShapeDtypeStruct((bsz, seq, kpad), BF16),
        jax.ShapeDtypeStruct((bsz, seq, kpad), BF16),
        jax.ShapeDtypeStruct((bsz, seq // PROJ_TILE, kv_d, PROJ_TILE), BF16),
        jax.ShapeDtypeStruct((bsz, seq // PROJ_TILE, kv_d, PROJ_TILE), BF16),
    ]
    out_specs = [
        fm(attn_d), fm(attn_d), fm(attn_d), fm(N_KV_GROUPS * GATE_ROWS),
        tmj(2 * kv_d), tmj(kpad), tmj(kpad),
        pl.BlockSpec((1, tm // PROJ_TILE, kv_d, PROJ_TILE), lambda b, i: (b, i, 0, 0)),
        pl.BlockSpec((1, tm // PROJ_TILE, kv_d, PROJ_TILE), lambda b, i: (b, i, 0, 0)),
    ]
    return pl.pallas_call(
        functools.partial(_nsa_proj_kernel, tm=tm, d=d),
        out_shape=out_shape,
        grid=(bsz, nt),
        in_specs=[
            tmj(d), const((1, d)), const((1, d)),
            const(waT.shape), const(wvT.shape), const(wk.shape),
            pl.BlockSpec((N_FREQ, tm), lambda b, i: (0, i)),
            pl.BlockSpec((N_FREQ, tm), lambda b, i: (0, i)),
            pl.BlockSpec((tm, LANES), lambda b, i: (i, 0)),
            pl.BlockSpec((tm, LANES), lambda b, i: (i, 0)),
            pl.BlockSpec((tm, LANES), lambda b, i: (i, 0)),
        ],
        out_specs=out_specs,
        compiler_params=pltpu.CompilerParams(dimension_semantics=("parallel", "parallel"),
                                             vmem_limit_bytes=VMEM_LIMIT),
        name="nsa_proj",
    )(h3, b_norm.reshape(1, d), kv_norm.reshape(1, d), waT, wvT, wk, cosT, sinT, cosf, sina, sinb)


def _gelu_tanh(x):
    return x * (0.5 * (1.0 + jnp.tanh(0.7978845608028654 * (x + 0.044715 * (x * x * x)))))


def _compress_kernel(x_ref, w1_ref, pos_ref, w2_ref, w2T_ref, kc_ref, cT_ref, *, nchunk, hidden):
    u = jnp.zeros((nchunk, 2 * hidden), F32)
    v = jnp.zeros((nchunk, 2 * hidden), F32)
    for l in range(CMP_STRIDE):
        xl = x_ref[0, pl.ds(l, nchunk, stride=CMP_STRIDE), :]
        u = u + _dot((xl + pos_ref[0, l:l + 1, :]).astype(BF16), w1_ref[0, l])
        l2 = CMP_STRIDE + l
        v = v + _dot((xl + pos_ref[0, l2:l2 + 1, :]).astype(BF16), w1_ref[0, l2])
    hid = u + pltpu.roll(v, nchunk - 1, axis=0)
    act = _gelu_tanh(hid).astype(BF16)
    rows_ok = lax.broadcasted_iota(jnp.int32, (nchunk, LANES), 0) < nchunk - 1
    cols_ok = lax.broadcasted_iota(jnp.int32, (HEAD_DIM, nchunk), 1) < nchunk - 1
    for p in range(2):
        a = act[:, p * hidden:(p + 1) * hidden]
        kc_ref[0, 0, p] = jnp.where(rows_ok, _dot(a, w2_ref[0]), 0.0).astype(BF16)
        cT_ref[0, 0, p] = jnp.where(cols_ok, _dot_nt(w2T_ref[0], a), 0.0).astype(BF16)


def _compress(kvc, w1bd, pos2, w2p, w2T):
    bsz, seq, _ = kvc.shape
    nchunk = seq // CMP_STRIDE
    hidden = w2T.shape[2]
    return pl.pallas_call(
        functools.partial(_compress_kernel, nchunk=nchunk, hidden=hidden),
        out_shape=[
            jax.ShapeDtypeStruct((2, bsz, N_KV_GROUPS, nchunk, LANES), BF16),
            jax.ShapeDtypeStruct((2, bsz, N_KV_GROUPS, HEAD_DIM, nchunk), BF16),
        ],
        grid=(2, bsz, N_KV_GROUPS // 2),
        in_specs=[
            pl.BlockSpec((1, seq, LANES), lambda s, b, p: (b, 0, 2 * s + p)),
            pl.BlockSpec((1, CMP_BLOCK, LANES, 2 * hidden), lambda s, b, p: (s, 0, 0, 0)),
            pl.BlockSpec((1, CMP_BLOCK, LANES), lambda s, b, p: (s, 0, 0)),
            pl.BlockSpec((1, hidden, LANES), lambda s, b, p: (s, 0, 0)),
            pl.BlockSpec((1, HEAD_DIM, hidden), lambda s, b, p: (s, 0, 0)),
        ],
        out_specs=[
            pl.BlockSpec((1, 1, 2, nchunk, LANES), lambda s, b, p: (s, b, p, 0, 0)),
            pl.BlockSpec((1, 1, 2, HEAD_DIM, nchunk), lambda s, b, p: (s, b, p, 0, 0)),
        ],
        compiler_params=pltpu.CompilerParams(dimension_semantics=("parallel", "parallel", "parallel"),
                                             vmem_limit_bytes=VMEM_LIMIT),
        name="compress",
    )(kvc, w1bd, pos2, w2p, w2T)


def _attn_kernel(q_ref, qr_ref, g_ref, sz_ref, kc_ref, vcT_ref, ks_ref, kw_ref, vsT_ref, vwT_ref, y_ref,
                 p_sc, sc_sc, rank_sc, oc_sc, qa_sc, c_sc, w_sc, s_sc, pb_sc, smax_sc, st_sc, acc_sc, wmax_sc,
                 wp_sc, ow_sc, *, tq, n_cmp):
    qi = pl.program_id(2)
    t0 = qi * tq
    neg_inf = -jnp.inf
    tvec = t0 + lax.broadcasted_iota(jnp.int32, (1, tq), 1)

    kc = kc_ref[0, 0, :, 0:HEAD_DIM]
    cmp_end = lax.broadcasted_iota(jnp.int32, (n_cmp, tq), 0) * CMP_STRIDE + (CMP_BLOCK - 1)
    cmask = cmp_end <= tvec
    heads = range(HEADS_PER_GROUP)

    def fold8(x, op, ways=4):
        parts = [None] * ways
        for idx, r in enumerate(range(0, x.shape[0], SUBLANES)):
            slab = x[r:r + SUBLANES, :]
            parts[idx % ways] = slab if parts[idx % ways] is None else op(parts[idx % ways], slab)
        return functools.reduce(op, [p for p in parts if p is not None])

    def key_chunk(ref, c):
        return ref[0, pl.ds(pl.multiple_of(c * KEY_CHUNK, KEY_CHUNK), KEY_CHUNK), :]

    n_sb = n_cmp // 4
    jrow = lax.broadcasted_iota(jnp.int32, (n_sb, tq), 0)
    cur = tvec // SLC_BLOCK
    valid = jrow <= cur

    QA_SEL, QA_DEAD, QA_WIN = 0, 1, 2
    pad_blocks = lambda b: b if n_sb == HEAD_DIM else jnp.concatenate(
        [b, jnp.zeros((HEAD_DIM - n_sb, tq), BF16)], axis=0)

    def set_operand(idx, bias):
        for hh in heads:
            r0 = hh * HEAD_DIM
            qa_sc[idx, hh] = jnp.concatenate([qr_ref[0, r0:r0 + HEAD_DIM, :], pad_blocks(bias)], axis=0)

    win_blocks = WINDOW // SLC_BLOCK
    set_operand(QA_WIN, jnp.where((jrow >= cur - win_blocks) & valid, 0.0, MASK_BIAS).astype(BF16))
    @pl.when(qi == 0)
    def _():
        set_operand(QA_DEAD, jnp.full((n_sb, tq), MASK_BIAS, BF16))

    blk_row = lax.broadcasted_iota(jnp.int32, (SLC_BLOCK, LANES), 0)
    blk_lane = lax.broadcasted_iota(jnp.int32, (SLC_BLOCK, LANES), 1)
    edge = [blk_row - blk_lane, blk_row - (blk_lane - SLC_BLOCK)]
    off_diag = [blk_lane >= SLC_BLOCK, blk_lane < SLC_BLOCK]

    def mask_diagonal_blocks(s, keep):
        blocks = []
        for bb in range(KEY_CHUNK // SLC_BLOCK):
            rows = s[bb * SLC_BLOCK:(bb + 1) * SLC_BLOCK, :]
            halves = [rows[:, h * LANES:(h + 1) * LANES] for h in range(tq // LANES)]
            h = bb // 2
            halves[h] = jnp.where(off_diag[bb % 2] | keep(edge[bb % 2]), halves[h], neg_inf)
            blocks.append(jnp.concatenate(halves, axis=1))
        return jnp.concatenate(blocks, axis=0)

    n_win = KEY_CHUNK + HALF
    prev_operand = jnp.where(qi >= 1, QA_WIN, QA_DEAD)
    prev_base = jnp.maximum(qi - 1, 0) * KEY_CHUNK

    def mask_half_blocks(s, first_block, keep):
        blocks = [s[r:r + SLC_BLOCK, :] for r in range(0, s.shape[0], SLC_BLOCK)]
        for j in range(HALF // SLC_BLOCK):
            halves = [blocks[first_block + j][:, h * LANES:(h + 1) * LANES] for h in range(HALF // LANES)]
            halves[j // 2] = jnp.where(off_diag[j % 2] | keep(edge[j % 2]), halves[j // 2], neg_inf)
            blocks[first_block + j] = jnp.concatenate(halves, axis=1)
        return jnp.concatenate(blocks, axis=0)

    def window_pieces(half):
        first = half * HALF
        n_old = KEY_CHUNK - first
        old = (pl.multiple_of(prev_base + first, HALF), n_old)
        new = (pl.multiple_of(qi * KEY_CHUNK, HALF), HALF + first)
        return old, new

    for hh in heads:
        r0 = hh * HEAD_DIM
        c_sc[hh] = jnp.where(cmask, _dot(kc, q_ref[0, r0:r0 + HEAD_DIM, :]), neg_inf)
        for half in range(tq // HALF):
            (old_start, n_old), (new_start, n_new) = window_pieces(half)
            lanes = slice(half * HALF, (half + 1) * HALF)
            s_old = _dot(kw_ref[0, pl.ds(old_start, n_old), :], qa_sc[prev_operand, hh, :, lanes])
            s_new = _dot(kw_ref[0, pl.ds(new_start, n_new), :], qa_sc[QA_WIN, hh, :, lanes])
            s_old = mask_half_blocks(s_old, 0, lambda e: e > 0)
            s_new = mask_half_blocks(s_new, half * (HALF // SLC_BLOCK), lambda e: e <= 0)
            w_sc[hh, half, 0:n_old, :] = s_old
            w_sc[hh, half, n_old:n_win, :] = s_new
            wmax_sc[hh, half] = jnp.maximum(fold8(s_old, jnp.maximum), fold8(s_new, jnp.maximum))
    probs = []
    for hh in heads:
        s = c_sc[hh]
        m = jnp.max(fold8(s, jnp.maximum), axis=0, keepdims=True)
        m = jnp.where(m == neg_inf, 0.0, m)
        e = jnp.exp2(s - m)
        den = jnp.sum(fold8(e, jnp.add), axis=0, keepdims=True)
        probs.append(e * (1.0 / jnp.maximum(den, 1e-30)))
    p_grp = functools.reduce(jnp.add, probs)
    for hh in heads:
        r0 = hh * HEAD_DIM
        oc_sc[r0:r0 + HEAD_DIM, :] = _dot(vcT_ref[0, 0], probs[hh].astype(BF16))

    ratio = SLC_BLOCK // CMP_STRIDE
    imp_cols = []
    for c in range(tq // LANES):
        p_sc[c, 0:SUBLANES, :] = jnp.zeros((SUBLANES, LANES), F32)
        p_sc[c, SUBLANES:SUBLANES + n_cmp, :] = p_grp[:, c * LANES:(c + 1) * LANES]
        tap = lambda o: p_sc[c, pl.ds(SUBLANES + o, n_sb, stride=ratio), :]
        imp_cols.append(tap(-1) + 2.0 * (tap(0) + tap(1) + tap(2)) + tap(3))
    imp = jnp.concatenate(imp_cols, axis=1)

    forced = (jrow == 0) | (valid & (jrow > cur - N_LOCAL))
    score = jnp.where(valid, jnp.where(forced, FORCE_SCORE, imp), neg_inf)
    sc_sc[...] = score

    n_valid = (t0 + tq) // SLC_BLOCK
    n_slabs = n_sb // SUBLANES
    slab_row = lax.broadcasted_iota(jnp.int32, (SUBLANES, tq), 0)
    rank_sc[...] = jnp.zeros((n_sb, tq), jnp.int32)
    for first in range(0, n_sb, RANK_SECTION):
        @pl.when(n_valid > max(N_SELECT, first))
        def _():
            slabs = [sc_sc[k * SUBLANES:(k + 1) * SUBLANES, :] for k in range(n_slabs)]
            counts = [rank_sc[k * SUBLANES:(k + 1) * SUBLANES, :] for k in range(n_slabs)]
            for jp in range(first, first + RANK_SECTION):
                sb = sc_sc[jp:jp + 1, :]
                for k in range(n_slabs):
                    if k * SUBLANES > jp:
                        before = sb >= slabs[k]
                    elif k * SUBLANES + SUBLANES - 1 < jp:
                        before = sb > slabs[k]
                    else:
                        before = (sb > slabs[k]) | ((sb == slabs[k]) & (slab_row > jp % SUBLANES))
                    counts[k] = counts[k] + jnp.where(before, 1, 0)
            for k in range(n_slabs):
                rank_sc[k * SUBLANES:(k + 1) * SUBLANES, :] = counts[k]
    set_operand(QA_SEL, jnp.where((rank_sc[...] < N_SELECT) & valid, 0.0, MASK_BIAS).astype(BF16))

    group = KEY_CHUNK
    n_double = qi // 2

    def values_of(ref, c):
        per = KEY_CHUNK // PROJ_TILE
        return jnp.concatenate([ref[0, per * c + v] for v in range(per)], axis=1)

    ST_MAX, ST_SUM, ST_RESCALE = 0, 1, 2
    rows8 = lambda x: jnp.broadcast_to(x, (SUBLANES, tq))
    never = 1 << 20

    def scores_to(slot, pos, hh, may_be_last):
        s = _dot(key_chunk(ks_ref, pos), qa_sc[QA_SEL, hh])
        if may_be_last:
            causal_slack = jnp.where(pos == qi, 0, never)
            s = mask_diagonal_blocks(s, lambda e: e <= causal_slack)
        s_sc[slot, hh] = s
        smax_sc[slot, hh] = fold8(s, jnp.maximum)

    def accumulate(pos, slot, hh):
        acc_sc[hh] = (acc_sc[hh] * st_sc[ST_RESCALE + slot, hh, 0:1, :]
                      + _dot(values_of(vsT_ref, pos), pb_sc[slot, hh]))

    def softmax_group(slot, hh):
        m_old = st_sc[ST_MAX, hh, 0:1, :]
        m_new = jnp.maximum(m_old, jnp.max(smax_sc[slot, hh], axis=0, keepdims=True))
        a = jnp.exp2(m_old - m_new)
        psum = None
        for r in range(0, group, SOFTMAX_ROWS):
            s = s_sc[slot, hh, r:r + SOFTMAX_ROWS, :]
            p = jnp.exp2(s - m_new)
            pb_sc[slot, hh, r:r + SOFTMAX_ROWS, :] = p.astype(BF16)
            f = fold8(p, jnp.add, ways=2)
            psum = f if psum is None else psum + f
        st_sc[ST_SUM, hh] = a * st_sc[ST_SUM, hh] + psum
        st_sc[ST_MAX, hh] = rows8(m_new)
        st_sc[ST_RESCALE + slot, hh] = rows8(a)

    def stage(slot, pos, next_may_be_last):
        for hh in heads:
            scores_to(1 - slot, pos + 1, hh, next_may_be_last)
            accumulate(jnp.maximum(pos - 1, 0), 1 - slot, hh)
            softmax_group(slot, hh)

    def drain(slot):
        for hh in heads:
            accumulate(jnp.maximum(qi - 1, 0), 1 - slot, hh)
            softmax_group(slot, hh)
            accumulate(qi, slot, hh)

    def double_trip(d, carry):
        stage(0, 2 * d, next_may_be_last=False)
        stage(1, 2 * d + 1, next_may_be_last=True)
        return carry

    def window_values(half):
        pieces = []
        for start, rows in window_pieces(half):
            pieces += [vwT_ref[0, start // PROJ_TILE + v] for v in range(rows // PROJ_TILE)]
        return jnp.concatenate(pieces, axis=1)

    for hh in heads:
        acc_sc[hh] = jnp.zeros((HEAD_DIM, tq), F32)
        pb_sc[1, hh] = jnp.zeros((group, tq), BF16)
        st_sc[ST_MAX, hh] = jnp.full((SUBLANES, tq), neg_inf, F32)
        st_sc[ST_SUM, hh] = jnp.zeros((SUBLANES, tq), F32)
        st_sc[ST_RESCALE + 1, hh] = jnp.ones((SUBLANES, tq), F32)
        scores_to(0, 0, hh, may_be_last=True)
        for half in range(tq // HALF):
            m = jnp.max(wmax_sc[hh, half], axis=0, keepdims=True)
            psum = None
            for r in range(0, n_win, 2 * SOFTMAX_ROWS):
                p = jnp.exp2(w_sc[hh, half, r:r + 2 * SOFTMAX_ROWS, :] - m)
                wp_sc[hh, half, r:r + 2 * SOFTMAX_ROWS, :] = p.astype(BF16)
                f = fold8(p, jnp.add, ways=2)
                psum = f if psum is None else psum + f
            ow_sc[hh, :, half * HALF:(half + 1) * HALF] = (
                _dot(window_values(half), wp_sc[hh, half]) * (1.0 / jnp.sum(psum, axis=0, keepdims=True)))

    lax.fori_loop(0, n_double, double_trip, 0)

    @pl.when(qi % 2 == 0)
    def _():
        drain(0)

    @pl.when(qi % 2 == 1)
    def _():
        stage(0, qi - 1, next_may_be_last=True)
        drain(1)

    for hh in heads:
        r0 = hh * HEAD_DIM
        o_w = ow_sc[hh]
        o_s = acc_sc[hh] * (1.0 / jnp.sum(st_sc[ST_SUM, hh], axis=0, keepdims=True))

        g0 = g_ref[0, N_BRANCH * hh:N_BRANCH * hh + 1, :]
        g1 = g_ref[0, N_BRANCH * hh + 1:N_BRANCH * hh + 2, :]
        g2 = g_ref[0, N_BRANCH * hh + 2:N_BRANCH * hh + 3, :]
        o = g0 * oc_sc[r0:r0 + HEAD_DIM, :] + g1 * o_s + g2 * o_w
        y_ref[0, r0:r0 + HEAD_DIM, :] = (o * sz_ref[0, r0:r0 + HEAD_DIM, :]).astype(BF16)


def _nsa_attn(qT, qrT, gT, szT, kcmp, vcmpT, ks, kw, vsT, vwT, *, tq=KEY_CHUNK):
    bsz, attn_d, seq = qT.shape
    n_cmp = kcmp.shape[2]
    n_win = KEY_CHUNK + HALF
    gd = HEADS_PER_GROUP * HEAD_DIM
    assert tq == KEY_CHUNK == WINDOW and HALF % PROJ_TILE == 0 and vsT.shape[3] == PROJ_TILE
    qspec = pl.BlockSpec((1, gd, tq), lambda b, g, i: (b, g, i))
    kspec = pl.BlockSpec((1, seq, LANES), lambda b, g, i: (b, 0, g))
    vspec = pl.BlockSpec((1, seq // PROJ_TILE, HEAD_DIM, PROJ_TILE), lambda b, g, i: (b, 0, g, 0))
    return pl.pallas_call(
        functools.partial(_attn_kernel, tq=tq, n_cmp=n_cmp),
        out_shape=jax.ShapeDtypeStruct((bsz, attn_d, seq), BF16),
        grid=(bsz, N_KV_GROUPS, seq // tq),
        in_specs=[
            qspec, qspec,
            pl.BlockSpec((1, GATE_ROWS, tq), lambda b, g, i: (b, g, i)),
            qspec,
            pl.BlockSpec((1, 1, n_cmp, LANES), lambda b, g, i: (b, g, 0, 0)),
            pl.BlockSpec((1, 1, HEAD_DIM, n_cmp), lambda b, g, i: (b, g, 0, 0)),
            kspec, kspec, vspec, vspec,
        ],
        out_specs=qspec,
        scratch_shapes=[
            pltpu.VMEM((tq // LANES, SUBLANES + n_cmp, LANES), F32),
            pltpu.VMEM((n_cmp // 4, tq), F32),
            pltpu.VMEM((n_cmp // 4, tq), jnp.int32),
            pltpu.VMEM((gd, tq), F32),
            pltpu.VMEM((3, HEADS_PER_GROUP, 2 * HEAD_DIM, tq), BF16),
            pltpu.VMEM((HEADS_PER_GROUP, n_cmp, tq), F32),
            pltpu.VMEM((HEADS_PER_GROUP, tq // HALF, n_win, HALF), F32),
            pltpu.VMEM((2, HEADS_PER_GROUP, KEY_CHUNK, tq), F32),
            pltpu.VMEM((2, HEADS_PER_GROUP, KEY_CHUNK, tq), BF16),
            pltpu.VMEM((2, HEADS_PER_GROUP, SUBLANES, tq), F32),
            pltpu.VMEM((4, HEADS_PER_GROUP, SUBLANES, tq), F32),
            pltpu.VMEM((HEADS_PER_GROUP, HEAD_DIM, tq), F32),
            pltpu.VMEM((HEADS_PER_GROUP, tq // HALF, SUBLANES, HALF), F32),
            pltpu.VMEM((HEADS_PER_GROUP, tq // HALF, n_win, HALF), BF16),
            pltpu.VMEM((HEADS_PER_GROUP, HEAD_DIM, tq), F32),
        ],
        compiler_params=pltpu.CompilerParams(dimension_semantics=("parallel", "parallel", "arbitrary"),
                                             vmem_limit_bytes=VMEM_LIMIT),
        name="nsa_attn",
    )(qT, qrT, gT, szT, kcmp, vcmpT, ks, kw, vsT, vwT)


def _nsa_out_kernel(y_ref, h_ref, woT_ref, fn_ref, o_ref):
    oT = _dot(woT_ref[...], y_ref[0])
    h2 = h_ref[0] + oT.T
    o_ref[0] = h2 * _inv_rms(h2) * fn_ref[...]


def _nsa_out(yT, h3, woT, final_norm, *, tm=OUT_TILE):
    bsz, seq, d = h3.shape
    attn_d = yT.shape[1]
    return pl.pallas_call(
        _nsa_out_kernel,
        out_shape=jax.ShapeDtypeStruct((bsz, seq, d), F32),
        grid=(bsz, seq // tm),
        in_specs=[
            pl.BlockSpec((1, attn_d, tm), lambda b, i: (b, 0, i)),
            pl.BlockSpec((1, tm, d), lambda b, i: (b, i, 0)),
            pl.BlockSpec((d, attn_d), lambda b, i: (0, 0)),
            pl.BlockSpec((1, d), lambda b, i: (0, 0)),
        ],
        out_specs=pl.BlockSpec((1, tm, d), lambda b, i: (b, i, 0)),
        compiler_params=pltpu.CompilerParams(dimension_semantics=("parallel", "parallel"),
                                             vmem_limit_bytes=VMEM_LIMIT),
        name="nsa_out",
    )(yT, h3, woT, final_norm.reshape(1, d))


def _rope_tables(seq):
    pos = jnp.arange(seq, dtype=F32)
    inv = ROPE_THETA ** (-jnp.arange(0, ROT_DIM, 2, dtype=F32) / ROT_DIM)
    ang = pos[:, None] * inv[None, :]
    cos, sin = jnp.cos(ang), jnp.sin(ang)
    z = lambda n: jnp.zeros((seq, n), F32)
    cosf = jnp.concatenate([cos, cos, jnp.ones((seq, LANES - ROT_DIM), F32)], axis=1)
    sina = jnp.concatenate([-sin, z(LANES - N_FREQ)], axis=1)
    sinb = jnp.concatenate([z(N_FREQ), sin, z(LANES - ROT_DIM)], axis=1)
    return cos.T, sin.T, cosf, sina, sinb


def _blockdiag2(w1):
    hidden = w1.shape[1]
    w = w1.reshape(CMP_BLOCK, HEAD_DIM, hidden)
    zero = jnp.zeros_like(w)
    top = jnp.concatenate([w, zero], axis=2)
    bot = jnp.concatenate([zero, w], axis=2)
    return jnp.concatenate([top, bot], axis=1)


def kernel(x, a_norm, a_w_in, a_conv_w, a_w_out, kv_norm, w_kv, cmp_pos_k, cmp_w1_k, cmp_w2_k,
           cmp_pos_v, cmp_w1_v, cmp_w2_v, b_norm, b_w_in, b_w_out, final_norm):
    bsz, seq, d = x.shape
    attn_d = N_HEADS * HEAD_DIM
    kv_d = N_KV_GROUPS * HEAD_DIM
    assert b_norm.shape[0] == 1, "one NSA layer reads the shared K/V side"
    assert seq % KEY_CHUNK == 0

    h = x.reshape(bsz * seq, d)
    for layer in range(a_norm.shape[0]):
        h = _conv_layer(h, a_norm[layer], a_w_in[layer].astype(BF16), a_conv_w[layer],
                        a_w_out[layer].astype(BF16), seq=seq)
    h3 = h.reshape(bsz, seq, d)

    w_in = b_w_in[0]
    n_gate = N_HEADS * N_BRANCH
    wg = w_in[:, attn_d:attn_d + n_gate].reshape(d, N_KV_GROUPS, HEADS_PER_GROUP * N_BRANCH)
    wg = jnp.pad(wg, ((0, 0), (0, 0), (0, GATE_ROWS - HEADS_PER_GROUP * N_BRANCH)))
    waT = jnp.concatenate([w_in[:, :attn_d], w_in[:, attn_d + n_gate:], wg.reshape(d, -1)], axis=1).T.astype(BF16)
    wkv = w_kv.reshape(d, 2 * N_BRANCH, N_KV_GROUPS, HEAD_DIM)
    k_c, v_c, k_s, v_s, k_w, v_w = [wkv[:, i] for i in range(2 * N_BRANCH)]
    flat = lambda w: w.reshape(d, kv_d)
    pad_lanes = lambda w: jnp.pad(w, ((0, 0), (0, 0), (0, LANES - HEAD_DIM))).reshape(d, N_KV_GROUPS * LANES)
    wvT = jnp.concatenate([flat(v_s), flat(v_w)], axis=1).T.astype(BF16)
    wk = jnp.concatenate([flat(k_c), flat(v_c), pad_lanes(k_s), pad_lanes(k_w)], axis=1).astype(BF16)
    cosT, sinT, cosf, sina, sinb = _rope_tables(seq)

    qT, qrT, szT, gT, kvc, ks, kw, vsT, vwT = _nsa_proj(
        h3, b_norm[0], kv_norm, waT, wvT, wk, cosT, sinT, cosf, sina, sinb)

    w1bd = jnp.stack([_blockdiag2(cmp_w1_k), _blockdiag2(cmp_w1_v)]).astype(BF16)
    pos2 = jnp.stack([jnp.tile(cmp_pos_k, (1, 2)), jnp.tile(cmp_pos_v, (1, 2))])
    w2 = jnp.stack([cmp_w2_k, cmp_w2_v])
    w2p = jnp.pad(w2, ((0, 0), (0, 0), (0, LANES - HEAD_DIM))).astype(BF16)
    w2T = jnp.swapaxes(w2, 1, 2).astype(BF16)
    cmp_tm, cmp_fm = _compress(kvc, w1bd, pos2, w2p, w2T)

    yT = _nsa_attn(qT, qrT, gT, szT, cmp_tm[0], cmp_fm[1], ks, kw, vsT, vwT)
    return _nsa_out(yT, h3, b_w_out[0].T.astype(BF16), final_norm)
```

```python
import functools

import jax
import jax.numpy as jnp
from jax import lax
from jax.experimental import pallas as pl
from jax.experimental.pallas import tpu as pltpu

EPS = 1e-6
CONV_WIDTH = 3
N_HEADS = 16
HEAD_DIM = 64
N_KV_GROUPS = 4
HEADS_PER_GROUP = N_HEADS // N_KV_GROUPS
N_BRANCH = 3
ROT_DIM = HEAD_DIM // 4
N_FREQ = ROT_DIM // 2
ROPE_THETA = 500000.0
CMP_BLOCK = 32
CMP_STRIDE = 16
SLC_BLOCK = 64
N_SELECT = 16
N_LOCAL = 2
WINDOW = 512
FORCE_SCORE = 1e4

LANES = 128
SUBLANES = 8
KEY_CHUNK = 512
HALF = KEY_CHUNK // 2
PROJ_TILE = 256
OUT_TILE = 512
SOFTMAX_ROWS = 32
RANK_SECTION = 16
GATE_ROWS = 16
MASK_BIAS = -1e30
LOG2_E = 1.4426950408889634
VMEM_LIMIT = 56 * 1024 * 1024

BF16 = jnp.bfloat16
F32 = jnp.float32
NT_DIMS = (((1,), (1,)), ((), ()))


def _dot(a, b):
    return jnp.dot(a, b, preferred_element_type=F32)


def _dot_nt(a, b):
    return lax.dot_general(a, b, NT_DIMS, preferred_element_type=F32)


def _sigmoid(x):
    return 1.0 / (1.0 + jnp.exp(-x))


def _inv_rms(x):
    return lax.rsqrt(jnp.mean(x * x, axis=-1, keepdims=True) + EPS)


def _conv_layer_kernel(x_ref, g_ref, win_ref, cw_ref, wout_ref, o_ref, vbuf_ref, *,
                       tm, tiles_per_seq, conv_d, cchunk):
    @pl.when(pl.program_id(0) % tiles_per_seq == 0)
    def _():
        vbuf_ref[0:SUBLANES, :] = jnp.zeros((SUBLANES, conv_d), F32)

    x = x_ref[...]
    hn = (x * _inv_rms(x) * g_ref[...]).astype(BF16)
    acc = jnp.zeros(x.shape, F32)
    for cc in range(conv_d // cchunk):
        cs = cc * cchunk
        b = _dot(hn, win_ref[:, cs:cs + cchunk])
        c = _dot(hn, win_ref[:, conv_d + cs:conv_d + cs + cchunk])
        u = _dot(hn, win_ref[:, 2 * conv_d + cs:2 * conv_d + cs + cchunk])
        z = _dot(hn, win_ref[:, 3 * conv_d + cs:3 * conv_d + cs + cchunk])
        v = c * u
        vbuf_ref[SUBLANES:SUBLANES + tm, cs:cs + cchunk] = v
        v1 = vbuf_ref[SUBLANES - 1:SUBLANES - 1 + tm, cs:cs + cchunk]
        v2 = vbuf_ref[SUBLANES - 2:SUBLANES - 2 + tm, cs:cs + cchunk]
        conv = (cw_ref[0:1, cs:cs + cchunk] * v2 + cw_ref[1:2, cs:cs + cchunk] * v1
                + cw_ref[2:3, cs:cs + cchunk] * v)
        vbuf_ref[0:SUBLANES, cs:cs + cchunk] = v[tm - SUBLANES:tm, :]
        y = b * conv * (z * _sigmoid(z))
        acc = acc + _dot(y.astype(BF16), wout_ref[cs:cs + cchunk, :])
    o_ref[...] = x + acc


def _conv_layer(h, norm_g, w_in, conv_w, w_out, *, seq, tm=512, cchunk=1024):
    t, d = h.shape
    conv_d = conv_w.shape[1]
    tm = min(tm, seq)
    const = lambda shape: pl.BlockSpec(shape, lambda i: (0,) * len(shape), pipeline_mode=pl.Buffered(1))
    return pl.pallas_call(
        functools.partial(_conv_layer_kernel, tm=tm, tiles_per_seq=seq // tm, conv_d=conv_d, cchunk=cchunk),
        out_shape=jax.ShapeDtypeStruct((t, d), F32),
        grid=(t // tm,),
        in_specs=[
            pl.BlockSpec((tm, d), lambda i: (i, 0)),
            const((1, d)),
            const((d, 4 * conv_d)),
            const((CONV_WIDTH, conv_d)),
            const((conv_d, d)),
        ],
        out_specs=pl.BlockSpec((tm, d), lambda i: (i, 0)),
        scratch_shapes=[pltpu.VMEM((SUBLANES + tm, conv_d), F32)],
        compiler_params=pltpu.CompilerParams(dimension_semantics=("arbitrary",), vmem_limit_bytes=VMEM_LIMIT),
        name="conv_layer",
    )(h, norm_g.reshape(1, d), w_in, conv_w, w_out)


def _nsa_proj_kernel(h_ref, bn_ref, kn_ref, waT_ref, wvT_ref, wk_ref, cosT_ref, sinT_ref,
                     cosf_ref, sina_ref, sinb_ref,
                     qT_ref, qrT_ref, szT_ref, gT_ref, kvc_ref, ks_ref, kw_ref, vsT_ref, vwT_ref, *, tm, d):
    attn_d = N_HEADS * HEAD_DIM
    kv_d = N_KV_GROUPS * HEAD_DIM
    h = h_ref[0]
    hr = h * _inv_rms(h)
    hq = (hr * bn_ref[...]).astype(BF16)
    hk = (hr * kn_ref[...]).astype(BF16)

    qT = _dot_nt(waT_ref[0:attn_d, :], hq) * (HEAD_DIM ** -0.5 * LOG2_E)
    cosT = cosT_ref[...]
    sinT = sinT_ref[...]
    for hd in range(N_HEADS):
        r0 = hd * HEAD_DIM
        blk = qT[r0:r0 + HEAD_DIM, :]
        x1 = blk[0:N_FREQ, :]
        x2 = blk[N_FREQ:ROT_DIM, :]
        rot = jnp.concatenate([x1 * cosT - x2 * sinT, x2 * cosT + x1 * sinT, blk[ROT_DIM:, :]], axis=0)
        qT_ref[0, r0:r0 + HEAD_DIM, :] = blk.astype(BF16)
        qrT_ref[0, r0:r0 + HEAD_DIM, :] = rot.astype(BF16)

    zT = _dot_nt(waT_ref[attn_d:2 * attn_d, :], hq)
    szT_ref[0] = zT * _sigmoid(zT)
    gT_ref[0] = _sigmoid(_dot_nt(waT_ref[2 * attn_d:2 * attn_d + N_KV_GROUPS * GATE_ROWS, :], hq))

    vT = _dot_nt(wvT_ref[...], hk)
    vsT_ref[0, 0] = vT[0:kv_d, :].astype(BF16)
    vwT_ref[0, 0] = vT[kv_d:2 * kv_d, :].astype(BF16)

    kk = _dot(hk, wk_ref[...])
    kvc_ref[0] = kk[:, 0:2 * kv_d]
    cosf = cosf_ref[...]
    sina = sina_ref[...]
    sinb = sinb_ref[...]
    row = lax.broadcasted_iota(jnp.int32, (tm, LANES), 0)
    lane = lax.broadcasted_iota(jnp.int32, (tm, LANES), 1)
    tok_blk = (pl.program_id(1) * tm + row) // SLC_BLOCK
    onehot = jnp.where((lane >= HEAD_DIM) & (lane - HEAD_DIM == tok_blk), 1.0, 0.0)
    lower = lane < HEAD_DIM
    for i, ref in enumerate((ks_ref, kw_ref)):
        for pair in range(N_KV_GROUPS // 2):
            c0 = (2 + i) * kv_d + pair * LANES
            xb = kk[:, c0:c0 + LANES]
            rot = (xb * cosf + pltpu.roll(xb, LANES - N_FREQ, axis=1) * sina
                   + pltpu.roll(xb, N_FREQ, axis=1) * sinb)
            for half, keys in enumerate((rot, pltpu.roll(rot, HEAD_DIM, axis=1))):
                g = 2 * pair + half
                ref[0, :, g * LANES:(g + 1) * LANES] = (jnp.where(lower, keys, 0.0) + onehot).astype(BF16)


def _nsa_proj(h3, b_norm, kv_norm, waT, wvT, wk, cosT, sinT, cosf, sina, sinb, *, tm=PROJ_TILE):
    bsz, seq, d = h3.shape
    attn_d = N_HEADS * HEAD_DIM
    kv_d = N_KV_GROUPS * HEAD_DIM
    kpad = N_KV_GROUPS * LANES
    nt = seq // tm
    const = lambda shape: pl.BlockSpec(shape, lambda b, i: (0,) * len(shape))
    fm = lambda rows: pl.BlockSpec((1, rows, tm), lambda b, i: (b, 0, i))
    tmj = lambda cols: pl.BlockSpec((1, tm, cols), lambda b, i: (b, i, 0))
    out_shape = [
        jax.ShapeDtypeStruct((bsz, attn_d, seq), BF16),
        jax.ShapeDtypeStruct((bsz, attn_d, seq), BF16),
        jax.ShapeDtypeStruct((bsz, attn_d, seq), F32),
        jax.ShapeDtypeStruct((bsz, N_KV_GROUPS * GATE_ROWS, seq), F32),
        jax.ShapeDtypeStruct((bsz, seq, 2 * kv_d), F32),
        jax.ShapeDtypeStruct((bsz, seq, kpad), BF16),
        jax.ShapeDtypeStruct((bsz, seq, kpad), BF16),
        jax.ShapeDtypeStruct((bsz, nt, kv_d, tm), BF16),
        jax.ShapeDtypeStruct((bsz, nt, kv_d, tm), BF16),
    ]
    out_specs = [
        fm(attn_d), fm(attn_d), fm(attn_d), fm(N_KV_GROUPS * GATE_ROWS),
        tmj(2 * kv_d), tmj(kpad), tmj(kpad),
        pl.BlockSpec((1, 1, kv_d, tm), lambda b, i: (b, i, 0, 0)),
        pl.BlockSpec((1, 1, kv_d, tm), lambda b, i: (b, i, 0, 0)),
    ]
    return pl.pallas_call(
        functools.partial(_nsa_proj_kernel, tm=tm, d=d),
        out_shape=out_shape,
        grid=(bsz, nt),
        in_specs=[
            tmj(d), const((1, d)), const((1, d)),
            const(waT.shape), const(wvT.shape), const(wk.shape),
            pl.BlockSpec((N_FREQ, tm), lambda b, i: (0, i)),
            pl.BlockSpec((N_FREQ, tm), lambda b, i: (0, i)),
            pl.BlockSpec((tm, LANES), lambda b, i: (i, 0)),
            pl.BlockSpec((tm, LANES), lambda b, i: (i, 0)),
            pl.BlockSpec((tm, LANES), lambda b, i: (i, 0)),
        ],
        out_specs=out_specs,
        compiler_params=pltpu.CompilerParams(dimension_semantics=("parallel", "parallel"),
                                             vmem_limit_bytes=VMEM_LIMIT),
        name="nsa_proj",
    )(h3, b_norm.reshape(1, d), kv_norm.reshape(1, d), waT, wvT, wk, cosT, sinT, cosf, sina, sinb)


def _gelu_tanh(x):
    return x * (0.5 * (1.0 + jnp.tanh(0.7978845608028654 * (x + 0.044715 * (x * x * x)))))


def _compress_kernel(x_ref, w1_ref, pos_ref, w2_ref, w2T_ref, kc_ref, cT_ref, rows_sc, *, nchunk):
    half = CMP_STRIDE * HEAD_DIM
    for l in range(CMP_STRIDE):
        xl = x_ref[0, pl.ds(l, nchunk, stride=CMP_STRIDE), :]
        for p in range(LANES // HEAD_DIM):
            rows_sc[p, :, l * HEAD_DIM:(l + 1) * HEAD_DIM] = xl[:, p * HEAD_DIM:(p + 1) * HEAD_DIM]
    rows_ok = lax.broadcasted_iota(jnp.int32, (nchunk, LANES), 0) < nchunk - 1
    cols_ok = lax.broadcasted_iota(jnp.int32, (HEAD_DIM, nchunk), 1) < nchunk - 1
    for p in range(LANES // HEAD_DIM):
        x = rows_sc[p]
        u = _dot((x + pos_ref[0, :, 0:half]).astype(BF16), w1_ref[0, 0:half, :])
        v = _dot((x + pos_ref[0, :, half:2 * half]).astype(BF16), w1_ref[0, half:2 * half, :])
        act = _gelu_tanh(u + pltpu.roll(v, nchunk - 1, axis=0)).astype(BF16)
        kc_ref[0, 0, p] = jnp.where(rows_ok, _dot(act, w2_ref[0]), 0.0).astype(BF16)
        cT_ref[0, 0, p] = jnp.where(cols_ok, _dot_nt(w2T_ref[0], act), 0.0).astype(BF16)


def _compress(kvc, w1, pos, w2p, w2T):
    bsz, seq, _ = kvc.shape
    nchunk = seq // CMP_STRIDE
    hidden = w2T.shape[2]
    width = CMP_BLOCK * HEAD_DIM
    pair = LANES // HEAD_DIM
    return pl.pallas_call(
        functools.partial(_compress_kernel, nchunk=nchunk),
        out_shape=[
            jax.ShapeDtypeStruct((2, bsz, N_KV_GROUPS, nchunk, LANES), BF16),
            jax.ShapeDtypeStruct((2, bsz, N_KV_GROUPS, HEAD_DIM, nchunk), BF16),
        ],
        grid=(2, bsz, N_KV_GROUPS // pair),
        in_specs=[
            pl.BlockSpec((1, seq, LANES), lambda s, b, p: (b, 0, (N_KV_GROUPS // pair) * s + p)),
            pl.BlockSpec((1, width, hidden), lambda s, b, p: (s, 0, 0)),
            pl.BlockSpec((1, 1, width), lambda s, b, p: (s, 0, 0)),
            pl.BlockSpec((1, hidden, LANES), lambda s, b, p: (s, 0, 0)),
            pl.BlockSpec((1, HEAD_DIM, hidden), lambda s, b, p: (s, 0, 0)),
        ],
        out_specs=[
            pl.BlockSpec((1, 1, pair, nchunk, LANES), lambda s, b, p: (s, b, p, 0, 0)),
            pl.BlockSpec((1, 1, pair, HEAD_DIM, nchunk), lambda s, b, p: (s, b, p, 0, 0)),
        ],
        scratch_shapes=[pltpu.VMEM((pair, nchunk, CMP_STRIDE * HEAD_DIM), F32)],
        compiler_params=pltpu.CompilerParams(dimension_semantics=("parallel", "parallel", "parallel"),
                                             vmem_limit_bytes=VMEM_LIMIT),
        name="compress",
    )(kvc, w1, pos, w2p, w2T)


def _attn_kernel(q_ref, qr_ref, g_ref, sz_ref, kc_ref, vcT_ref, ks_ref, kw_ref, vsT_ref, vwT_ref, y_ref,
                 p_sc, sc_sc, rank_sc, oc_sc, qa_sc, c_sc, w_sc, s_sc, pb_sc, smax_sc, st_sc, acc_sc, wmax_sc,
                 wp_sc, ow_sc, *, tq, n_cmp):
    qi = pl.program_id(2)
    t0 = qi * tq
    neg_inf = -jnp.inf
    tvec = t0 + lax.broadcasted_iota(jnp.int32, (1, tq), 1)

    kc = kc_ref[0, 0, :, 0:HEAD_DIM]
    cmp_end = lax.broadcasted_iota(jnp.int32, (n_cmp, tq), 0) * CMP_STRIDE + (CMP_BLOCK - 1)
    cmask = cmp_end <= tvec
    heads = range(HEADS_PER_GROUP)

    def fold8(x, op, ways=4):
        parts = [None] * ways
        for idx, r in enumerate(range(0, x.shape[0], SUBLANES)):
            slab = x[r:r + SUBLANES, :]
            parts[idx % ways] = slab if parts[idx % ways] is None else op(parts[idx % ways], slab)
        return functools.reduce(op, [p for p in parts if p is not None])

    def key_chunk(ref, c):
        return ref[0, pl.ds(pl.multiple_of(c * KEY_CHUNK, KEY_CHUNK), KEY_CHUNK), :]

    n_sb = n_cmp // 4
    jrow = lax.broadcasted_iota(jnp.int32, (n_sb, tq), 0)
    cur = tvec // SLC_BLOCK
    valid = jrow <= cur

    QA_SEL, QA_DEAD, QA_WIN = 0, 1, 2
    pad_blocks = lambda b: b if n_sb == HEAD_DIM else jnp.concatenate(
        [b, jnp.zeros((HEAD_DIM - n_sb, tq), BF16)], axis=0)

    def set_operand(idx, bias):
        for hh in heads:
            r0 = hh * HEAD_DIM
            qa_sc[idx, hh] = jnp.concatenate([qr_ref[0, r0:r0 + HEAD_DIM, :], pad_blocks(bias)], axis=0)

    win_blocks = WINDOW // SLC_BLOCK
    set_operand(QA_WIN, jnp.where((jrow >= cur - win_blocks) & valid, 0.0, MASK_BIAS).astype(BF16))
    set_operand(QA_DEAD, jnp.full((n_sb, tq), MASK_BIAS, BF16))

    blk_row = lax.broadcasted_iota(jnp.int32, (SLC_BLOCK, LANES), 0)
    blk_lane = lax.broadcasted_iota(jnp.int32, (SLC_BLOCK, LANES), 1)
    edge = [blk_row - blk_lane, blk_row - (blk_lane - SLC_BLOCK)]
    off_diag = [blk_lane >= SLC_BLOCK, blk_lane < SLC_BLOCK]

    def mask_diagonal_blocks(s, keep):
        blocks = []
        for bb in range(KEY_CHUNK // SLC_BLOCK):
            rows = s[bb * SLC_BLOCK:(bb + 1) * SLC_BLOCK, :]
            halves = [rows[:, h * LANES:(h + 1) * LANES] for h in range(tq // LANES)]
            h = bb // 2
            halves[h] = jnp.where(off_diag[bb % 2] | keep(edge[bb % 2]), halves[h], neg_inf)
            blocks.append(jnp.concatenate(halves, axis=1))
        return jnp.concatenate(blocks, axis=0)

    n_win = KEY_CHUNK + HALF
    prev_operand = jnp.where(qi >= 1, QA_WIN, QA_DEAD)
    prev_base = jnp.maximum(qi - 1, 0) * KEY_CHUNK

    def mask_half_blocks(s, first_block, keep):
        blocks = [s[r:r + SLC_BLOCK, :] for r in range(0, s.shape[0], SLC_BLOCK)]
        for j in range(HALF // SLC_BLOCK):
            halves = [blocks[first_block + j][:, h * LANES:(h + 1) * LANES] for h in range(HALF // LANES)]
            halves[j // 2] = jnp.where(off_diag[j % 2] | keep(edge[j % 2]), halves[j // 2], neg_inf)
            blocks[first_block + j] = jnp.concatenate(halves, axis=1)
        return jnp.concatenate(blocks, axis=0)

    def window_pieces(half):
        first = half * HALF
        n_old = KEY_CHUNK - first
        old = (pl.multiple_of(prev_base + first, HALF), n_old)
        new = (pl.multiple_of(qi * KEY_CHUNK, HALF), HALF + first)
        return old, new

    for hh in heads:
        r0 = hh * HEAD_DIM
        c_sc[hh] = jnp.where(cmask, _dot(kc, q_ref[0, r0:r0 + HEAD_DIM, :]), neg_inf)
        for half in range(tq // HALF):
            (old_start, n_old), (new_start, n_new) = window_pieces(half)
            lanes = slice(half * HALF, (half + 1) * HALF)
            s_old = _dot(kw_ref[0, pl.ds(old_start, n_old), :], qa_sc[prev_operand, hh, :, lanes])
            s_new = _dot(kw_ref[0, pl.ds(new_start, n_new), :], qa_sc[QA_WIN, hh, :, lanes])
            s_old = mask_half_blocks(s_old, 0, lambda e: e > 0)
            s_new = mask_half_blocks(s_new, half * (HALF // SLC_BLOCK), lambda e: e <= 0)
            w_sc[hh, half, 0:n_old, :] = s_old
            w_sc[hh, half, n_old:n_win, :] = s_new
            wmax_sc[hh, half] = jnp.maximum(fold8(s_old, jnp.maximum), fold8(s_new, jnp.maximum))
    probs = []
    for hh in heads:
        s = c_sc[hh]
        m = jnp.max(fold8(s, jnp.maximum), axis=0, keepdims=True)
        m = jnp.where(m == neg_inf, 0.0, m)
        e = jnp.exp2(s - m)
        den = jnp.sum(fold8(e, jnp.add), axis=0, keepdims=True)
        probs.append(e * (1.0 / jnp.maximum(den, 1e-30)))
    p_grp = functools.reduce(jnp.add, probs)
    for hh in heads:
        r0 = hh * HEAD_DIM
        oc_sc[r0:r0 + HEAD_DIM, :] = _dot(vcT_ref[0, 0], probs[hh].astype(BF16))

    ratio = SLC_BLOCK // CMP_STRIDE
    imp_cols = []
    for c in range(tq // LANES):
        p_sc[c, 0:SUBLANES, :] = jnp.zeros((SUBLANES, LANES), F32)
        p_sc[c, SUBLANES:SUBLANES + n_cmp, :] = p_grp[:, c * LANES:(c + 1) * LANES]
        tap = lambda o: p_sc[c, pl.ds(SUBLANES + o, n_sb, stride=ratio), :]
        imp_cols.append(tap(-1) + 2.0 * (tap(0) + tap(1) + tap(2)) + tap(3))
    imp = jnp.concatenate(imp_cols, axis=1)

    forced = (jrow == 0) | (valid & (jrow > cur - N_LOCAL))
    score = jnp.where(valid, jnp.where(forced, FORCE_SCORE, imp), neg_inf)
    sc_sc[...] = score

    n_valid = (t0 + tq) // SLC_BLOCK
    n_slabs = n_sb // SUBLANES
    slab_row = lax.broadcasted_iota(jnp.int32, (SUBLANES, tq), 0)
    rank_sc[...] = jnp.zeros((n_sb, tq), jnp.int32)
    for first in range(0, n_sb, RANK_SECTION):
        @pl.when(n_valid > max(N_SELECT, first))
        def _():
            slabs = [sc_sc[k * SUBLANES:(k + 1) * SUBLANES, :] for k in range(n_slabs)]
            counts = [rank_sc[k * SUBLANES:(k + 1) * SUBLANES, :] for k in range(n_slabs)]
            for jp in range(first, first + RANK_SECTION):
                sb = sc_sc[jp:jp + 1, :]
                for k in range(n_slabs):
                    if k * SUBLANES > jp:
                        before = sb >= slabs[k]
                    elif k * SUBLANES + SUBLANES - 1 < jp:
                        before = sb > slabs[k]
                    else:
                        before = (sb > slabs[k]) | ((sb == slabs[k]) & (slab_row > jp % SUBLANES))
                    counts[k] = counts[k] + jnp.where(before, 1, 0)
            for k in range(n_slabs):
                rank_sc[k * SUBLANES:(k + 1) * SUBLANES, :] = counts[k]
    set_operand(QA_SEL, jnp.where((rank_sc[...] < N_SELECT) & valid, 0.0, MASK_BIAS).astype(BF16))

    group = KEY_CHUNK
    n_double = qi // 2

    def values_of(ref, c):
        per = KEY_CHUNK // PROJ_TILE
        return jnp.concatenate([ref[0, per * c + v] for v in range(per)], axis=1)

    ST_MAX, ST_SUM, ST_RESCALE = 0, 1, 2
    rows8 = lambda x: jnp.broadcast_to(x, (SUBLANES, tq))
    never = 1 << 20

    def scores_to(slot, pos, hh, may_be_last):
        s = _dot(key_chunk(ks_ref, pos), qa_sc[QA_SEL, hh])
        if may_be_last:
            causal_slack = jnp.where(pos == qi, 0, never)
            s = mask_diagonal_blocks(s, lambda e: e <= causal_slack)
        s_sc[slot, hh] = s
        smax_sc[slot, hh] = fold8(s, jnp.maximum)

    def accumulate(pos, slot, hh):
        acc_sc[hh] = (acc_sc[hh] * st_sc[ST_RESCALE + slot, hh, 0:1, :]
                      + _dot(values_of(vsT_ref, pos), pb_sc[slot, hh]))

    def softmax_group(slot, hh):
        m_old = st_sc[ST_MAX, hh, 0:1, :]
        m_new = jnp.maximum(m_old, jnp.max(smax_sc[slot, hh], axis=0, keepdims=True))
        a = jnp.exp2(m_old - m_new)
        psum = None
        for r in range(0, group, SOFTMAX_ROWS):
            s = s_sc[slot, hh, r:r + SOFTMAX_ROWS, :]
            p = jnp.exp2(s - m_new)
            pb_sc[slot, hh, r:r + SOFTMAX_ROWS, :] = p.astype(BF16)
            f = fold8(p, jnp.add, ways=2)
            psum = f if psum is None else psum + f
        st_sc[ST_SUM, hh] = a * st_sc[ST_SUM, hh] + psum
        st_sc[ST_MAX, hh] = rows8(m_new)
        st_sc[ST_RESCALE + slot, hh] = rows8(a)

    def stage(slot, pos, next_may_be_last):
        for hh in heads:
            scores_to(1 - slot, pos + 1, hh, next_may_be_last)
            accumulate(jnp.maximum(pos - 1, 0), 1 - slot, hh)
            softmax_group(slot, hh)

    def drain(slot):
        for hh in heads:
            accumulate(jnp.maximum(qi - 1, 0), 1 - slot, hh)
            softmax_group(slot, hh)
            accumulate(qi, slot, hh)

    def double_trip(d, carry):
        stage(0, 2 * d, next_may_be_last=False)
        stage(1, 2 * d + 1, next_may_be_last=True)
        return carry

    def window_values(half):
        pieces = []
        for start, rows in window_pieces(half):
            pieces += [vwT_ref[0, start // PROJ_TILE + v] for v in range(rows // PROJ_TILE)]
        return jnp.concatenate(pieces, axis=1)

    for hh in heads:
        acc_sc[hh] = jnp.zeros((HEAD_DIM, tq), F32)
        pb_sc[1, hh] = jnp.zeros((group, tq), BF16)
        st_sc[ST_MAX, hh] = jnp.full((SUBLANES, tq), neg_inf, F32)
        st_sc[ST_SUM, hh] = jnp.zeros((SUBLANES, tq), F32)
        st_sc[ST_RESCALE + 1, hh] = jnp.ones((SUBLANES, tq), F32)
        scores_to(0, 0, hh, may_be_last=True)
        for half in range(tq // HALF):
            m = jnp.max(wmax_sc[hh, half], axis=0, keepdims=True)
            psum = None
            for r in range(0, n_win, 2 * SOFTMAX_ROWS):
                p = jnp.exp2(w_sc[hh, half, r:r + 2 * SOFTMAX_ROWS, :] - m)
                wp_sc[hh, half, r:r + 2 * SOFTMAX_ROWS, :] = p.astype(BF16)
                f = fold8(p, jnp.add, ways=2)
                psum = f if psum is None else psum + f
            ow_sc[hh, :, half * HALF:(half + 1) * HALF] = (
                _dot(window_values(half), wp_sc[hh, half]) * (1.0 / jnp.sum(psum, axis=0, keepdims=True)))

    lax.fori_loop(0, n_double, double_trip, 0)

    @pl.when(qi % 2 == 0)
    def _():
        drain(0)

    @pl.when(qi % 2 == 1)
    def _():
        stage(0, qi - 1, next_may_be_last=True)
        drain(1)

    for hh in heads:
        r0 = hh * HEAD_DIM
        o_w = ow_sc[hh]
        o_s = acc_sc[hh] * (1.0 / jnp.sum(st_sc[ST_SUM, hh], axis=0, keepdims=True))

        g0 = g_ref[0, N_BRANCH * hh:N_BRANCH * hh + 1, :]
        g1 = g_ref[0, N_BRANCH * hh + 1:N_BRANCH * hh + 2, :]
        g2 = g_ref[0, N_BRANCH * hh + 2:N_BRANCH * hh + 3, :]
        o = g0 * oc_sc[r0:r0 + HEAD_DIM, :] + g1 * o_s + g2 * o_w
        y_ref[0, r0:r0 + HEAD_DIM, :] = (o * sz_ref[0, r0:r0 + HEAD_DIM, :]).astype(BF16)


def _nsa_attn(qT, qrT, gT, szT, kcmp, vcmpT, ks, kw, vsT, vwT, *, tq=KEY_CHUNK):
    bsz, attn_d, seq = qT.shape
    n_cmp = kcmp.shape[2]
    n_win = KEY_CHUNK + HALF
    gd = HEADS_PER_GROUP * HEAD_DIM
    assert tq == KEY_CHUNK == WINDOW and HALF % PROJ_TILE == 0 and vsT.shape[3] == PROJ_TILE
    qspec = pl.BlockSpec((1, gd, tq), lambda b, g, i: (b, g, i))
    kspec = pl.BlockSpec((1, seq, LANES), lambda b, g, i: (b, 0, g))
    vspec = pl.BlockSpec((1, seq // PROJ_TILE, HEAD_DIM, PROJ_TILE), lambda b, g, i: (b, 0, g, 0))
    return pl.pallas_call(
        functools.partial(_attn_kernel, tq=tq, n_cmp=n_cmp),
        out_shape=jax.ShapeDtypeStruct((bsz, attn_d, seq), BF16),
        grid=(bsz, N_KV_GROUPS, seq // tq),
        in_specs=[
            qspec, qspec,
            pl.BlockSpec((1, GATE_ROWS, tq), lambda b, g, i: (b, g, i)),
            qspec,
            pl.BlockSpec((1, 1, n_cmp, LANES), lambda b, g, i: (b, g, 0, 0)),
            pl.BlockSpec((1, 1, HEAD_DIM, n_cmp), lambda b, g, i: (b, g, 0, 0)),
            kspec, kspec, vspec, vspec,
        ],
        out_specs=qspec,
        scratch_shapes=[
            pltpu.VMEM((tq // LANES, SUBLANES + n_cmp, LANES), F32),
            pltpu.VMEM((n_cmp // 4, tq), F32),
            pltpu.VMEM((n_cmp // 4, tq), jnp.int32),
            pltpu.VMEM((gd, tq), F32),
            pltpu.VMEM((3, HEADS_PER_GROUP, 2 * HEAD_DIM, tq), BF16),
            pltpu.VMEM((HEADS_PER_GROUP, n_cmp, tq), F32),
            pltpu.VMEM((HEADS_PER_GROUP, tq // HALF, n_win, HALF), F32),
            pltpu.VMEM((2, HEADS_PER_GROUP, KEY_CHUNK, tq), F32),
            pltpu.VMEM((2, HEADS_PER_GROUP, KEY_CHUNK, tq), BF16),
            pltpu.VMEM((2, HEADS_PER_GROUP, SUBLANES, tq), F32),
            pltpu.VMEM((4, HEADS_PER_GROUP, SUBLANES, tq), F32),
            pltpu.VMEM((HEADS_PER_GROUP, HEAD_DIM, tq), F32),
            pltpu.VMEM((HEADS_PER_GROUP, tq // HALF, SUBLANES, HALF), F32),
            pltpu.VMEM((HEADS_PER_GROUP, tq // HALF, n_win, HALF), BF16),
            pltpu.VMEM((HEADS_PER_GROUP, HEAD_DIM, tq), F32),
        ],
        compiler_params=pltpu.CompilerParams(dimension_semantics=("parallel", "parallel", "arbitrary"),
                                             vmem_limit_bytes=VMEM_LIMIT),
        name="nsa_attn",
    )(qT, qrT, gT, szT, kcmp, vcmpT, ks, kw, vsT, vwT)


def _nsa_out_kernel(y_ref, h_ref, woT_ref, fn_ref, o_ref):
    oT = _dot(woT_ref[...], y_ref[0])
    h2 = h_ref[0] + oT.T
    o_ref[0] = h2 * _inv_rms(h2) * fn_ref[...]


def _nsa_out(yT, h3, woT, final_norm, *, tm=OUT_TILE):
    bsz, seq, d = h3.shape
    attn_d = yT.shape[1]
    return pl.pallas_call(
        _nsa_out_kernel,
        out_shape=jax.ShapeDtypeStruct((bsz, seq, d), F32),
        grid=(bsz, seq // tm),
        in_specs=[
            pl.BlockSpec((1, attn_d, tm), lambda b, i: (b, 0, i)),
            pl.BlockSpec((1, tm, d), lambda b, i: (b, i, 0)),
            pl.BlockSpec((d, attn_d), lambda b, i: (0, 0)),
            pl.BlockSpec((1, d), lambda b, i: (0, 0)),
        ],
        out_specs=pl.BlockSpec((1, tm, d), lambda b, i: (b, i, 0)),
        compiler_params=pltpu.CompilerParams(dimension_semantics=("parallel", "parallel"),
                                             vmem_limit_bytes=VMEM_LIMIT),
        name="nsa_out",
    )(yT, h3, woT, final_norm.reshape(1, d))


def _rope_tables(seq):
    pos = jnp.arange(seq, dtype=F32)
    inv = ROPE_THETA ** (-jnp.arange(0, ROT_DIM, 2, dtype=F32) / ROT_DIM)
    ang = pos[:, None] * inv[None, :]
    cos, sin = jnp.cos(ang), jnp.sin(ang)
    z = lambda n: jnp.zeros((seq, n), F32)
    per_head = lambda parts: jnp.tile(jnp.concatenate(parts, axis=1), (1, LANES // HEAD_DIM))
    cosf = per_head([cos, cos, jnp.ones((seq, HEAD_DIM - ROT_DIM), F32)])
    sina = per_head([-sin, z(HEAD_DIM - N_FREQ)])
    sinb = per_head([z(N_FREQ), sin, z(HEAD_DIM - ROT_DIM)])
    return cos.T, sin.T, cosf, sina, sinb


def kernel(x, a_norm, a_w_in, a_conv_w, a_w_out, kv_norm, w_kv, cmp_pos_k, cmp_w1_k, cmp_w2_k,
           cmp_pos_v, cmp_w1_v, cmp_w2_v, b_norm, b_w_in, b_w_out, final_norm):
    bsz, seq, d = x.shape
    attn_d = N_HEADS * HEAD_DIM
    kv_d = N_KV_GROUPS * HEAD_DIM
    assert b_norm.shape[0] == 1, "one NSA layer reads the shared K/V side"
    assert seq % KEY_CHUNK == 0

    h = x.reshape(bsz * seq, d)
    for layer in range(a_norm.shape[0]):
        h = _conv_layer(h, a_norm[layer], a_w_in[layer].astype(BF16), a_conv_w[layer],
                        a_w_out[layer].astype(BF16), seq=seq)
    h3 = h.reshape(bsz, seq, d)

    w_in = b_w_in[0]
    n_gate = N_HEADS * N_BRANCH
    wg = w_in[:, attn_d:attn_d + n_gate].reshape(d, N_KV_GROUPS, HEADS_PER_GROUP * N_BRANCH)
    wg = jnp.pad(wg, ((0, 0), (0, 0), (0, GATE_ROWS - HEADS_PER_GROUP * N_BRANCH)))
    waT = jnp.concatenate([w_in[:, :attn_d], w_in[:, attn_d + n_gate:], wg.reshape(d, -1)], axis=1).T.astype(BF16)
    wkv = w_kv.reshape(d, 2 * N_BRANCH, N_KV_GROUPS, HEAD_DIM)
    k_c, v_c, k_s, v_s, k_w, v_w = [wkv[:, i] for i in range(2 * N_BRANCH)]
    flat = lambda w: w.reshape(d, kv_d)
    wvT = jnp.concatenate([flat(v_s), flat(v_w)], axis=1).T.astype(BF16)
    wk = jnp.concatenate([flat(k_c), flat(v_c), flat(k_s), flat(k_w)], axis=1).astype(BF16)
    cosT, sinT, cosf, sina, sinb = _rope_tables(seq)

    qT, qrT, szT, gT, kvc, ks, kw, vsT, vwT = _nsa_proj(
        h3, b_norm[0], kv_norm, waT, wvT, wk, cosT, sinT, cosf, sina, sinb)

    w1 = jnp.stack([cmp_w1_k, cmp_w1_v]).astype(BF16)
    pos = jnp.stack([cmp_pos_k.reshape(1, -1), cmp_pos_v.reshape(1, -1)])
    w2 = jnp.stack([cmp_w2_k, cmp_w2_v])
    w2p = jnp.pad(w2, ((0, 0), (0, 0), (0, LANES - HEAD_DIM))).astype(BF16)
    w2T = jnp.swapaxes(w2, 1, 2).astype(BF16)
    cmp_tm, cmp_fm = _compress(kvc, w1, pos, w2p, w2T)

    yT = _nsa_attn(qT, qrT, gT, szT, cmp_tm[0], cmp_fm[1], ks, kw, vsT, vwT)
    return _nsa_out(yT, h3, b_w_out[0].T.astype(BF16), final_norm)
```

```python
import functools

import jax
import jax.numpy as jnp
from jax import lax
from jax.experimental import pallas as pl
from jax.experimental.pallas import tpu as pltpu

EPS = 1e-6
CONV_WIDTH = 3
N_HEADS = 16
HEAD_DIM = 64
N_KV_GROUPS = 4
HEADS_PER_GROUP = N_HEADS // N_KV_GROUPS
N_BRANCH = 3
ROT_DIM = HEAD_DIM // 4
N_FREQ = ROT_DIM // 2
ROPE_THETA = 500000.0
CMP_BLOCK = 32
CMP_STRIDE = 16
SLC_BLOCK = 64
N_SELECT = 16
N_LOCAL = 2
WINDOW = 512
FORCE_SCORE = 1e4

LANES = 128
SUBLANES = 8
KEY_CHUNK = 512
HALF = KEY_CHUNK // 2
PROJ_TILE = 256
OUT_TILE = 1024
SOFTMAX_ROWS = 32
RANK_SECTION = 16
GATE_ROWS = 16
MASK_BIAS = -1e30
LOG2_E = 1.4426950408889634
VMEM_LIMIT = 56 * 1024 * 1024

BF16 = jnp.bfloat16
F32 = jnp.float32
NT_DIMS = (((1,), (1,)), ((), ()))


def _dot(a, b):
    return jnp.dot(a, b, preferred_element_type=F32)


def _dot_nt(a, b):
    return lax.dot_general(a, b, NT_DIMS, preferred_element_type=F32)


def _sigmoid(x):
    return 1.0 / (1.0 + jnp.exp(-x))


def _inv_rms(x):
    return lax.rsqrt(jnp.mean(x * x, axis=-1, keepdims=True) + EPS)


def _conv_layer_kernel(x_ref, g_ref, win_ref, cw_ref, wout_ref, o_ref, vbuf_ref, *,
                       tm, tiles_per_seq, conv_d, cchunk):
    @pl.when(pl.program_id(0) % tiles_per_seq == 0)
    def _():
        vbuf_ref[0:SUBLANES, :] = jnp.zeros((SUBLANES, conv_d), F32)

    x = x_ref[...]
    hn = (x * _inv_rms(x) * g_ref[...]).astype(BF16)
    acc = jnp.zeros(x.shape, F32)
    for cc in range(conv_d // cchunk):
        cs = cc * cchunk
        b = _dot(hn, win_ref[:, cs:cs + cchunk])
        c = _dot(hn, win_ref[:, conv_d + cs:conv_d + cs + cchunk])
        u = _dot(hn, win_ref[:, 2 * conv_d + cs:2 * conv_d + cs + cchunk])
        z = _dot(hn, win_ref[:, 3 * conv_d + cs:3 * conv_d + cs + cchunk])
        v = c * u
        vbuf_ref[SUBLANES:SUBLANES + tm, cs:cs + cchunk] = v
        v1 = vbuf_ref[SUBLANES - 1:SUBLANES - 1 + tm, cs:cs + cchunk]
        v2 = vbuf_ref[SUBLANES - 2:SUBLANES - 2 + tm, cs:cs + cchunk]
        conv = (cw_ref[0:1, cs:cs + cchunk] * v2 + cw_ref[1:2, cs:cs + cchunk] * v1
                + cw_ref[2:3, cs:cs + cchunk] * v)
        vbuf_ref[0:SUBLANES, cs:cs + cchunk] = v[tm - SUBLANES:tm, :]
        y = b * conv * (z * _sigmoid(z))
        acc = acc + _dot(y.astype(BF16), wout_ref[cs:cs + cchunk, :])
    o_ref[...] = x + acc


def _conv_layer(h, norm_g, w_in, conv_w, w_out, *, seq, tm=512, cchunk=1024):
    t, d = h.shape
    conv_d = conv_w.shape[1]
    tm = min(tm, seq)
    const = lambda shape: pl.BlockSpec(shape, lambda i: (0,) * len(shape), pipeline_mode=pl.Buffered(1))
    return pl.pallas_call(
        functools.partial(_conv_layer_kernel, tm=tm, tiles_per_seq=seq // tm, conv_d=conv_d, cchunk=cchunk),
        out_shape=jax.ShapeDtypeStruct((t, d), F32),
        grid=(t // tm,),
        in_specs=[
            pl.BlockSpec((tm, d), lambda i: (i, 0)),
            const((1, d)),
            const((d, 4 * conv_d)),
            const((CONV_WIDTH, conv_d)),
            const((conv_d, d)),
        ],
        out_specs=pl.BlockSpec((tm, d), lambda i: (i, 0)),
        scratch_shapes=[pltpu.VMEM((SUBLANES + tm, conv_d), F32)],
        compiler_params=pltpu.CompilerParams(dimension_semantics=("arbitrary",), vmem_limit_bytes=VMEM_LIMIT),
        name="conv_layer",
    )(h, norm_g.reshape(1, d), w_in, conv_w, w_out)


def _nsa_proj_kernel(h_ref, bn_ref, kn_ref, waT_ref, wvT_ref, wk_ref, cosT_ref, sinT_ref,
                     cosf_ref, sina_ref, sinb_ref,
                     qT_ref, qrT_ref, szT_ref, gT_ref, kvc_ref, ks_ref, kw_ref, vsT_ref, vwT_ref, *, tm, d):
    attn_d = N_HEADS * HEAD_DIM
    kv_d = N_KV_GROUPS * HEAD_DIM
    h = h_ref[0]
    hr = h * _inv_rms(h)
    hq = (hr * bn_ref[...]).astype(BF16)
    hk = (hr * kn_ref[...]).astype(BF16)

    qT = _dot_nt(waT_ref[0:attn_d, :], hq) * (HEAD_DIM ** -0.5 * LOG2_E)
    cosT = cosT_ref[...]
    sinT = sinT_ref[...]
    for hd in range(N_HEADS):
        r0 = hd * HEAD_DIM
        blk = qT[r0:r0 + HEAD_DIM, :]
        x1 = blk[0:N_FREQ, :]
        x2 = blk[N_FREQ:ROT_DIM, :]
        rot = jnp.concatenate([x1 * cosT - x2 * sinT, x2 * cosT + x1 * sinT, blk[ROT_DIM:, :]], axis=0)
        qT_ref[0, r0:r0 + HEAD_DIM, :] = blk.astype(BF16)
        qrT_ref[0, r0:r0 + HEAD_DIM, :] = rot.astype(BF16)

    zT = _dot_nt(waT_ref[attn_d:2 * attn_d, :], hq)
    szT_ref[0] = zT * _sigmoid(zT)
    gT_ref[0] = _sigmoid(_dot_nt(waT_ref[2 * attn_d:2 * attn_d + N_KV_GROUPS * GATE_ROWS, :], hq))

    vT = _dot_nt(wvT_ref[...], hk)
    vsT_ref[0, 0] = vT[0:kv_d, :].astype(BF16)
    vwT_ref[0, 0] = vT[kv_d:2 * kv_d, :].astype(BF16)

    kk = _dot(hk, wk_ref[...])
    kvc_ref[0] = kk[:, 0:2 * kv_d]
    cosf = cosf_ref[...]
    sina = sina_ref[...]
    sinb = sinb_ref[...]
    row = lax.broadcasted_iota(jnp.int32, (tm, LANES), 0)
    lane = lax.broadcasted_iota(jnp.int32, (tm, LANES), 1)
    tok_blk = (pl.program_id(1) * tm + row) // SLC_BLOCK
    onehot = jnp.where((lane >= HEAD_DIM) & (lane - HEAD_DIM == tok_blk), 1.0, 0.0)
    lower = lane < HEAD_DIM
    for i, ref in enumerate((ks_ref, kw_ref)):
        for pair in range(N_KV_GROUPS // 2):
            c0 = (2 + i) * kv_d + pair * LANES
            xb = kk[:, c0:c0 + LANES]
            rot = (xb * cosf + pltpu.roll(xb, LANES - N_FREQ, axis=1) * sina
                   + pltpu.roll(xb, N_FREQ, axis=1) * sinb)
            for half, keys in enumerate((rot, pltpu.roll(rot, HEAD_DIM, axis=1))):
                g = 2 * pair + half
                ref[0, :, g * LANES:(g + 1) * LANES] = (jnp.where(lower, keys, 0.0) + onehot).astype(BF16)


def _nsa_proj(h3, b_norm, kv_norm, waT, wvT, wk, cosT, sinT, cosf, sina, sinb, *, tm=PROJ_TILE):
    bsz, seq, d = h3.shape
    attn_d = N_HEADS * HEAD_DIM
    kv_d = N_KV_GROUPS * HEAD_DIM
    kpad = N_KV_GROUPS * LANES
    nt = seq // tm
    const = lambda shape: pl.BlockSpec(shape, lambda b, i: (0,) * len(shape))
    fm = lambda rows: pl.BlockSpec((1, rows, tm), lambda b, i: (b, 0, i))
    tmj = lambda cols: pl.BlockSpec((1, tm, cols), lambda b, i: (b, i, 0))
    out_shape = [
        jax.ShapeDtypeStruct((bsz, attn_d, seq), BF16),
        jax.ShapeDtypeStruct((bsz, attn_d, seq), BF16),
        jax.ShapeDtypeStruct((bsz, attn_d, seq), F32),
        jax.ShapeDtypeStruct((bsz, N_KV_GROUPS * GATE_ROWS, seq), F32),
        jax.ShapeDtypeStruct((bsz, seq, 2 * kv_d), F32),
        jax.ShapeDtypeStruct((bsz, seq, kpad), BF16),
        jax.ShapeDtypeStruct((bsz, seq, kpad), BF16),
        jax.ShapeDtypeStruct((bsz, nt, kv_d, tm), BF16),
        jax.ShapeDtypeStruct((bsz, nt, kv_d, tm), BF16),
    ]
    out_specs = [
        fm(attn_d), fm(attn_d), fm(attn_d), fm(N_KV_GROUPS * GATE_ROWS),
        tmj(2 * kv_d), tmj(kpad), tmj(kpad),
        pl.BlockSpec((1, 1, kv_d, tm), lambda b, i: (b, i, 0, 0)),
        pl.BlockSpec((1, 1, kv_d, tm), lambda b, i: (b, i, 0, 0)),
    ]
    return pl.pallas_call(
        functools.partial(_nsa_proj_kernel, tm=tm, d=d),
        out_shape=out_shape,
        grid=(bsz, nt),
        in_specs=[
            tmj(d), const((1, d)), const((1, d)),
            const(waT.shape), const(wvT.shape), const(wk.shape),
            pl.BlockSpec((N_FREQ, tm), lambda b, i: (0, i)),
            pl.BlockSpec((N_FREQ, tm), lambda b, i: (0, i)),
            pl.BlockSpec((tm, LANES), lambda b, i: (i, 0)),
            pl.BlockSpec((tm, LANES), lambda b, i: (i, 0)),
            pl.BlockSpec((tm, LANES), lambda b, i: (i, 0)),
        ],
        out_specs=out_specs,
        compiler_params=pltpu.CompilerParams(dimension_semantics=("parallel", "parallel"),
                                             vmem_limit_bytes=VMEM_LIMIT),
        name="nsa_proj",
    )(h3, b_norm.reshape(1, d), kv_norm.reshape(1, d), waT, wvT, wk, cosT, sinT, cosf, sina, sinb)


def _gelu_tanh(x):
    return x * (0.5 * (1.0 + jnp.tanh(0.7978845608028654 * (x + 0.044715 * (x * x * x)))))


def _compress_kernel(x_ref, w1_ref, pos_ref, w2_ref, w2T_ref, kc_ref, cT_ref, rows_sc, *, nchunk):
    half = CMP_STRIDE * HEAD_DIM
    for l in range(CMP_STRIDE):
        xl = x_ref[0, pl.ds(l, nchunk, stride=CMP_STRIDE), :]
        for p in range(LANES // HEAD_DIM):
            rows_sc[p, :, l * HEAD_DIM:(l + 1) * HEAD_DIM] = xl[:, p * HEAD_DIM:(p + 1) * HEAD_DIM]
    rows_ok = lax.broadcasted_iota(jnp.int32, (nchunk, LANES), 0) < nchunk - 1
    cols_ok = lax.broadcasted_iota(jnp.int32, (HEAD_DIM, nchunk), 1) < nchunk - 1
    for p in range(LANES // HEAD_DIM):
        x = rows_sc[p]
        u = _dot((x + pos_ref[0, :, 0:half]).astype(BF16), w1_ref[0, 0:half, :])
        v = _dot((x + pos_ref[0, :, half:2 * half]).astype(BF16), w1_ref[0, half:2 * half, :])
        act = _gelu_tanh(u + pltpu.roll(v, nchunk - 1, axis=0)).astype(BF16)
        kc_ref[0, 0, p] = jnp.where(rows_ok, _dot(act, w2_ref[0]), 0.0).astype(BF16)
        cT_ref[0, 0, p] = jnp.where(cols_ok, _dot_nt(w2T_ref[0], act), 0.0).astype(BF16)


def _compress(kvc, w1, pos, w2p, w2T):
    bsz, seq, _ = kvc.shape
    nchunk = seq // CMP_STRIDE
    hidden = w2T.shape[2]
    width = CMP_BLOCK * HEAD_DIM
    pair = LANES // HEAD_DIM
    return pl.pallas_call(
        functools.partial(_compress_kernel, nchunk=nchunk),
        out_shape=[
            jax.ShapeDtypeStruct((2, bsz, N_KV_GROUPS, nchunk, LANES), BF16),
            jax.ShapeDtypeStruct((2, bsz, N_KV_GROUPS, HEAD_DIM, nchunk), BF16),
        ],
        grid=(2, bsz, N_KV_GROUPS // pair),
        in_specs=[
            pl.BlockSpec((1, seq, LANES), lambda s, b, p: (b, 0, (N_KV_GROUPS // pair) * s + p)),
            pl.BlockSpec((1, width, hidden), lambda s, b, p: (s, 0, 0)),
            pl.BlockSpec((1, 1, width), lambda s, b, p: (s, 0, 0)),
            pl.BlockSpec((1, hidden, LANES), lambda s, b, p: (s, 0, 0)),
            pl.BlockSpec((1, HEAD_DIM, hidden), lambda s, b, p: (s, 0, 0)),
        ],
        out_specs=[
            pl.BlockSpec((1, 1, pair, nchunk, LANES), lambda s, b, p: (s, b, p, 0, 0)),
            pl.BlockSpec((1, 1, pair, HEAD_DIM, nchunk), lambda s, b, p: (s, b, p, 0, 0)),
        ],
        scratch_shapes=[pltpu.VMEM((pair, nchunk, CMP_STRIDE * HEAD_DIM), F32)],
        compiler_params=pltpu.CompilerParams(dimension_semantics=("parallel", "parallel", "parallel"),
                                             vmem_limit_bytes=VMEM_LIMIT),
        name="compress",
    )(kvc, w1, pos, w2p, w2T)


def _attn_kernel(q_ref, qr_ref, g_ref, sz_ref, kc_ref, vcT_ref, ks_ref, kw_ref, vsT_ref, vwT_ref, y_ref,
                 p_sc, sc_sc, rank_sc, oc_sc, qa_sc, c_sc, w_sc, s_sc, pb_sc, smax_sc, st_sc, acc_sc, wmax_sc,
                 wp_sc, ow_sc, *, tq, n_cmp):
    qi = pl.program_id(2)
    t0 = qi * tq
    neg_inf = -jnp.inf
    tvec = t0 + lax.broadcasted_iota(jnp.int32, (1, tq), 1)

    kc = kc_ref[0, 0, :, 0:HEAD_DIM]
    cmp_end = lax.broadcasted_iota(jnp.int32, (n_cmp, tq), 0) * CMP_STRIDE + (CMP_BLOCK - 1)
    cmask = cmp_end <= tvec
    heads = range(HEADS_PER_GROUP)

    def fold8(x, op, ways=4):
        parts = [None] * ways
        for idx, r in enumerate(range(0, x.shape[0], SUBLANES)):
            slab = x[r:r + SUBLANES, :]
            parts[idx % ways] = slab if parts[idx % ways] is None else op(parts[idx % ways], slab)
        return functools.reduce(op, [p for p in parts if p is not None])

    def key_chunk(ref, c):
        return ref[0, pl.ds(pl.multiple_of(c * KEY_CHUNK, KEY_CHUNK), KEY_CHUNK), :]

    n_sb = n_cmp // 4
    jrow = lax.broadcasted_iota(jnp.int32, (n_sb, tq), 0)
    cur = tvec // SLC_BLOCK
    valid = jrow <= cur

    QA_SEL, QA_DEAD, QA_WIN = 0, 1, 2
    pad_blocks = lambda b: b if n_sb == HEAD_DIM else jnp.concatenate(
        [b, jnp.zeros((HEAD_DIM - n_sb, tq), BF16)], axis=0)

    def set_operand(idx, bias):
        for hh in heads:
            r0 = hh * HEAD_DIM
            qa_sc[idx, hh] = jnp.concatenate([qr_ref[0, r0:r0 + HEAD_DIM, :], pad_blocks(bias)], axis=0)

    win_blocks = WINDOW // SLC_BLOCK
    set_operand(QA_WIN, jnp.where((jrow >= cur - win_blocks) & valid, 0.0, MASK_BIAS).astype(BF16))
    set_operand(QA_DEAD, jnp.full((n_sb, tq), MASK_BIAS, BF16))

    blk_row = lax.broadcasted_iota(jnp.int32, (SLC_BLOCK, LANES), 0)
    blk_lane = lax.broadcasted_iota(jnp.int32, (SLC_BLOCK, LANES), 1)
    edge = [blk_row - blk_lane, blk_row - (blk_lane - SLC_BLOCK)]
    off_diag = [blk_lane >= SLC_BLOCK, blk_lane < SLC_BLOCK]

    def mask_diagonal_blocks(s, keep):
        blocks = []
        for bb in range(KEY_CHUNK // SLC_BLOCK):
            rows = s[bb * SLC_BLOCK:(bb + 1) * SLC_BLOCK, :]
            halves = [rows[:, h * LANES:(h + 1) * LANES] for h in range(tq // LANES)]
            h = bb // 2
            halves[h] = jnp.where(off_diag[bb % 2] | keep(edge[bb % 2]), halves[h], neg_inf)
            blocks.append(jnp.concatenate(halves, axis=1))
        return jnp.concatenate(blocks, axis=0)

    n_win = KEY_CHUNK + HALF
    prev_operand = jnp.where(qi >= 1, QA_WIN, QA_DEAD)
    prev_base = jnp.maximum(qi - 1, 0) * KEY_CHUNK

    def mask_half_blocks(s, first_block, keep):
        blocks = [s[r:r + SLC_BLOCK, :] for r in range(0, s.shape[0], SLC_BLOCK)]
        for j in range(HALF // SLC_BLOCK):
            halves = [blocks[first_block + j][:, h * LANES:(h + 1) * LANES] for h in range(HALF // LANES)]
            halves[j // 2] = jnp.where(off_diag[j % 2] | keep(edge[j % 2]), halves[j // 2], neg_inf)
            blocks[first_block + j] = jnp.concatenate(halves, axis=1)
        return jnp.concatenate(blocks, axis=0)

    def window_pieces(half):
        first = half * HALF
        n_old = KEY_CHUNK - first
        old = (pl.multiple_of(prev_base + first, HALF), n_old)
        new = (pl.multiple_of(qi * KEY_CHUNK, HALF), HALF + first)
        return old, new

    for hh in heads:
        r0 = hh * HEAD_DIM
        c_sc[hh] = jnp.where(cmask, _dot(kc, q_ref[0, r0:r0 + HEAD_DIM, :]), neg_inf)
        for half in range(tq // HALF):
            (old_start, n_old), (new_start, n_new) = window_pieces(half)
            lanes = slice(half * HALF, (half + 1) * HALF)
            s_old = _dot(kw_ref[0, pl.ds(old_start, n_old), :], qa_sc[prev_operand, hh, :, lanes])
            s_new = _dot(kw_ref[0, pl.ds(new_start, n_new), :], qa_sc[QA_WIN, hh, :, lanes])
            s_old = mask_half_blocks(s_old, 0, lambda e: e > 0)
            s_new = mask_half_blocks(s_new, half * (HALF // SLC_BLOCK), lambda e: e <= 0)
            w_sc[hh, half, 0:n_old, :] = s_old
            w_sc[hh, half, n_old:n_win, :] = s_new
            wmax_sc[hh, half] = jnp.maximum(fold8(s_old, jnp.maximum), fold8(s_new, jnp.maximum))
    probs = []
    for hh in heads:
        s = c_sc[hh]
        m = jnp.max(fold8(s, jnp.maximum), axis=0, keepdims=True)
        m = jnp.where(m == neg_inf, 0.0, m)
        e = jnp.exp2(s - m)
        den = jnp.sum(fold8(e, jnp.add), axis=0, keepdims=True)
        probs.append(e * (1.0 / jnp.maximum(den, 1e-30)))
    p_grp = functools.reduce(jnp.add, probs)
    for hh in heads:
        r0 = hh * HEAD_DIM
        oc_sc[r0:r0 + HEAD_DIM, :] = _dot(vcT_ref[0, 0], probs[hh].astype(BF16))

    ratio = SLC_BLOCK // CMP_STRIDE
    imp_cols = []
    for c in range(tq // LANES):
        p_sc[c, 0:SUBLANES, :] = jnp.zeros((SUBLANES, LANES), F32)
        p_sc[c, SUBLANES:SUBLANES + n_cmp, :] = p_grp[:, c * LANES:(c + 1) * LANES]
        tap = lambda o: p_sc[c, pl.ds(SUBLANES + o, n_sb, stride=ratio), :]
        imp_cols.append(tap(-1) + 2.0 * (tap(0) + tap(1) + tap(2)) + tap(3))
    imp = jnp.concatenate(imp_cols, axis=1)

    forced = (jrow == 0) | (valid & (jrow > cur - N_LOCAL))
    score = jnp.where(valid, jnp.where(forced, FORCE_SCORE, imp), neg_inf)
    sc_sc[...] = score

    n_valid = (t0 + tq) // SLC_BLOCK
    n_slabs = n_sb // SUBLANES
    slab_row = lax.broadcasted_iota(jnp.int32, (SUBLANES, tq), 0)
    rank_sc[...] = jnp.zeros((n_sb, tq), jnp.int32)
    for first in range(0, n_sb, RANK_SECTION):
        @pl.when(n_valid > max(N_SELECT, first))
        def _():
            slabs = [sc_sc[k * SUBLANES:(k + 1) * SUBLANES, :] for k in range(n_slabs)]
            counts = [rank_sc[k * SUBLANES:(k + 1) * SUBLANES, :] for k in range(n_slabs)]
            for jp in range(first, first + RANK_SECTION):
                sb = sc_sc[jp:jp + 1, :]
                for k in range(n_slabs):
                    if k * SUBLANES > jp:
                        before = sb >= slabs[k]
                    elif k * SUBLANES + SUBLANES - 1 < jp:
                        before = sb > slabs[k]
                    else:
                        before = (sb > slabs[k]) | ((sb == slabs[k]) & (slab_row > jp % SUBLANES))
                    counts[k] = counts[k] + jnp.where(before, 1, 0)
            for k in range(n_slabs):
                rank_sc[k * SUBLANES:(k + 1) * SUBLANES, :] = counts[k]
    set_operand(QA_SEL, jnp.where((rank_sc[...] < N_SELECT) & valid, 0.0, MASK_BIAS).astype(BF16))

    group = KEY_CHUNK
    n_double = qi // 2

    def values_of(ref, c):
        per = KEY_CHUNK // PROJ_TILE
        return jnp.concatenate([ref[0, per * c + v] for v in range(per)], axis=1)

    ST_MAX, ST_SUM, ST_RESCALE = 0, 1, 2
    rows8 = lambda x: jnp.broadcast_to(x, (SUBLANES, tq))
    never = 1 << 20

    def scores_to(slot, pos, hh, may_be_last):
        s = _dot(key_chunk(ks_ref, pos), qa_sc[QA_SEL, hh])
        if may_be_last:
            causal_slack = jnp.where(pos == qi, 0, never)
            s = mask_diagonal_blocks(s, lambda e: e <= causal_slack)
        s_sc[slot, hh] = s
        smax_sc[slot, hh] = fold8(s, jnp.maximum)

    def accumulate(pos, slot, hh):
        acc_sc[hh] = (acc_sc[hh] * st_sc[ST_RESCALE + slot, hh, 0:1, :]
                      + _dot(values_of(vsT_ref, pos), pb_sc[slot, hh]))

    def softmax_group(slot, hh):
        m_old = st_sc[ST_MAX, hh, 0:1, :]
        m_new = jnp.maximum(m_old, jnp.max(smax_sc[slot, hh], axis=0, keepdims=True))
        a = jnp.exp2(m_old - m_new)
        psum = None
        for r in range(0, group, SOFTMAX_ROWS):
            s = s_sc[slot, hh, r:r + SOFTMAX_ROWS, :]
            p = jnp.exp2(s - m_new)
            pb_sc[slot, hh, r:r + SOFTMAX_ROWS, :] = p.astype(BF16)
            f = fold8(p, jnp.add, ways=2)
            psum = f if psum is None else psum + f
        st_sc[ST_SUM, hh] = a * st_sc[ST_SUM, hh] + psum
        st_sc[ST_MAX, hh] = rows8(m_new)
        st_sc[ST_RESCALE + slot, hh] = rows8(a)

    def stage(slot, pos, next_may_be_last):
        for hh in heads:
            scores_to(1 - slot, pos + 1, hh, next_may_be_last)
            accumulate(jnp.maximum(pos - 1, 0), 1 - slot, hh)
            softmax_group(slot, hh)

    def drain(slot):
        for hh in heads:
            accumulate(jnp.maximum(qi - 1, 0), 1 - slot, hh)
            softmax_group(slot, hh)
            accumulate(qi, slot, hh)

    def double_trip(d, carry):
        stage(0, 2 * d, next_may_be_last=False)
        stage(1, 2 * d + 1, next_may_be_last=True)
        return carry

    def window_values(half):
        pieces = []
        for start, rows in window_pieces(half):
            pieces += [vwT_ref[0, start // PROJ_TILE + v] for v in range(rows // PROJ_TILE)]
        return jnp.concatenate(pieces, axis=1)

    for hh in heads:
        acc_sc[hh] = jnp.zeros((HEAD_DIM, tq), F32)
        pb_sc[1, hh] = jnp.zeros((group, tq), BF16)
        st_sc[ST_MAX, hh] = jnp.full((SUBLANES, tq), neg_inf, F32)
        st_sc[ST_SUM, hh] = jnp.zeros((SUBLANES, tq), F32)
        st_sc[ST_RESCALE + 1, hh] = jnp.ones((SUBLANES, tq), F32)
        scores_to(0, 0, hh, may_be_last=True)
        for half in range(tq // HALF):
            m = jnp.max(wmax_sc[hh, half], axis=0, keepdims=True)
            psum = None
            for r in range(0, n_win, 2 * SOFTMAX_ROWS):
                p = jnp.exp2(w_sc[hh, half, r:r + 2 * SOFTMAX_ROWS, :] - m)
                wp_sc[hh, half, r:r + 2 * SOFTMAX_ROWS, :] = p.astype(BF16)
                f = fold8(p, jnp.add, ways=2)
                psum = f if psum is None else psum + f
            ow_sc[hh, :, half * HALF:(half + 1) * HALF] = (
                _dot(window_values(half), wp_sc[hh, half]) * (1.0 / jnp.sum(psum, axis=0, keepdims=True)))

    lax.fori_loop(0, n_double, double_trip, 0)

    @pl.when(qi % 2 == 0)
    def _():
        drain(0)

    @pl.when(qi % 2 == 1)
    def _():
        stage(0, qi - 1, next_may_be_last=True)
        drain(1)

    for hh in heads:
        r0 = hh * HEAD_DIM
        o_w = ow_sc[hh]
        o_s = acc_sc[hh] * (1.0 / jnp.sum(st_sc[ST_SUM, hh], axis=0, keepdims=True))

        g0 = g_ref[0, N_BRANCH * hh:N_BRANCH * hh + 1, :]
        g1 = g_ref[0, N_BRANCH * hh + 1:N_BRANCH * hh + 2, :]
        g2 = g_ref[0, N_BRANCH * hh + 2:N_BRANCH * hh + 3, :]
        o = g0 * oc_sc[r0:r0 + HEAD_DIM, :] + g1 * o_s + g2 * o_w
        y_ref[0, r0:r0 + HEAD_DIM, :] = (o * sz_ref[0, r0:r0 + HEAD_DIM, :]).astype(BF16)


def _nsa_attn(qT, qrT, gT, szT, kcmp, vcmpT, ks, kw, vsT, vwT, *, tq=KEY_CHUNK):
    bsz, attn_d, seq = qT.shape
    n_cmp = kcmp.shape[2]
    n_win = KEY_CHUNK + HALF
    gd = HEADS_PER_GROUP * HEAD_DIM
    assert tq == KEY_CHUNK == WINDOW and HALF % PROJ_TILE == 0 and vsT.shape[3] == PROJ_TILE
    qspec = pl.BlockSpec((1, gd, tq), lambda b, g, i: (b, g, i))
    kspec = pl.BlockSpec((1, seq, LANES), lambda b, g, i: (b, 0, g))
    vspec = pl.BlockSpec((1, seq // PROJ_TILE, HEAD_DIM, PROJ_TILE), lambda b, g, i: (b, 0, g, 0))
    return pl.pallas_call(
        functools.partial(_attn_kernel, tq=tq, n_cmp=n_cmp),
        out_shape=jax.ShapeDtypeStruct((bsz, attn_d, seq), BF16),
        grid=(bsz, N_KV_GROUPS, seq // tq),
        in_specs=[
            qspec, qspec,
            pl.BlockSpec((1, GATE_ROWS, tq), lambda b, g, i: (b, g, i)),
            qspec,
            pl.BlockSpec((1, 1, n_cmp, LANES), lambda b, g, i: (b, g, 0, 0)),
            pl.BlockSpec((1, 1, HEAD_DIM, n_cmp), lambda b, g, i: (b, g, 0, 0)),
            kspec, kspec, vspec, vspec,
        ],
        out_specs=qspec,
        scratch_shapes=[
            pltpu.VMEM((tq // LANES, SUBLANES + n_cmp, LANES), F32),
            pltpu.VMEM((n_cmp // 4, tq), F32),
            pltpu.VMEM((n_cmp // 4, tq), jnp.int32),
            pltpu.VMEM((gd, tq), F32),
            pltpu.VMEM((3, HEADS_PER_GROUP, 2 * HEAD_DIM, tq), BF16),
            pltpu.VMEM((HEADS_PER_GROUP, n_cmp, tq), F32),
            pltpu.VMEM((HEADS_PER_GROUP, tq // HALF, n_win, HALF), F32),
            pltpu.VMEM((2, HEADS_PER_GROUP, KEY_CHUNK, tq), F32),
            pltpu.VMEM((2, HEADS_PER_GROUP, KEY_CHUNK, tq), BF16),
            pltpu.VMEM((2, HEADS_PER_GROUP, SUBLANES, tq), F32),
            pltpu.VMEM((4, HEADS_PER_GROUP, SUBLANES, tq), F32),
            pltpu.VMEM((HEADS_PER_GROUP, HEAD_DIM, tq), F32),
            pltpu.VMEM((HEADS_PER_GROUP, tq // HALF, SUBLANES, HALF), F32),
            pltpu.VMEM((HEADS_PER_GROUP, tq // HALF, n_win, HALF), BF16),
            pltpu.VMEM((HEADS_PER_GROUP, HEAD_DIM, tq), F32),
        ],
        compiler_params=pltpu.CompilerParams(dimension_semantics=("parallel", "parallel", "arbitrary"),
                                             vmem_limit_bytes=VMEM_LIMIT),
        name="nsa_attn",
    )(qT, qrT, gT, szT, kcmp, vcmpT, ks, kw, vsT, vwT)


def _nsa_out_kernel(y_ref, h_ref, woT_ref, fn_ref, o_ref):
    oT = _dot(woT_ref[...], y_ref[0])
    h2 = h_ref[0] + oT.T
    o_ref[0] = h2 * _inv_rms(h2) * fn_ref[...]


def _nsa_out(yT, h3, woT, final_norm, *, tm=OUT_TILE):
    bsz, seq, d = h3.shape
    attn_d = yT.shape[1]
    return pl.pallas_call(
        _nsa_out_kernel,
        out_shape=jax.ShapeDtypeStruct((bsz, seq, d), F32),
        grid=(bsz, seq // tm),
        in_specs=[
            pl.BlockSpec((1, attn_d, tm), lambda b, i: (b, 0, i)),
            pl.BlockSpec((1, tm, d), lambda b, i: (b, i, 0)),
            pl.BlockSpec((d, attn_d), lambda b, i: (0, 0)),
            pl.BlockSpec((1, d), lambda b, i: (0, 0)),
        ],
        out_specs=pl.BlockSpec((1, tm, d), lambda b, i: (b, i, 0)),
        compiler_params=pltpu.CompilerParams(dimension_semantics=("parallel", "parallel"),
                                             vmem_limit_bytes=VMEM_LIMIT),
        name="nsa_out",
    )(yT, h3, woT, final_norm.reshape(1, d))


def _rope_tables(seq):
    pos = jnp.arange(seq, dtype=F32)
    inv = ROPE_THETA ** (-jnp.arange(0, ROT_DIM, 2, dtype=F32) / ROT_DIM)
    ang = pos[:, None] * inv[None, :]
    cos, sin = jnp.cos(ang), jnp.sin(ang)
    z = lambda n: jnp.zeros((seq, n), F32)
    per_head = lambda parts: jnp.tile(jnp.concatenate(parts, axis=1), (1, LANES // HEAD_DIM))
    cosf = per_head([cos, cos, jnp.ones((seq, HEAD_DIM - ROT_DIM), F32)])
    sina = per_head([-sin, z(HEAD_DIM - N_FREQ)])
    sinb = per_head([z(N_FREQ), sin, z(HEAD_DIM - ROT_DIM)])
    return cos.T, sin.T, cosf, sina, sinb


def kernel(x, a_norm, a_w_in, a_conv_w, a_w_out, kv_norm, w_kv, cmp_pos_k, cmp_w1_k, cmp_w2_k,
           cmp_pos_v, cmp_w1_v, cmp_w2_v, b_norm, b_w_in, b_w_out, final_norm):
    bsz, seq, d = x.shape
    attn_d = N_HEADS * HEAD_DIM
    kv_d = N_KV_GROUPS * HEAD_DIM
    assert b_norm.shape[0] == 1, "one NSA layer reads the shared K/V side"
    assert seq % KEY_CHUNK == 0

    h = x.reshape(bsz * seq, d)
    for layer in range(a_norm.shape[0]):
        h = _conv_layer(h, a_norm[layer], a_w_in[layer].astype(BF16), a_conv_w[layer],
                        a_w_out[layer].astype(BF16), seq=seq)
    h3 = h.reshape(bsz, seq, d)

    w_in = b_w_in[0]
    n_gate = N_HEADS * N_BRANCH
    wg = w_in[:, attn_d:attn_d + n_gate].reshape(d, N_KV_GROUPS, HEADS_PER_GROUP * N_BRANCH)
    wg = jnp.pad(wg, ((0, 0), (0, 0), (0, GATE_ROWS - HEADS_PER_GROUP * N_BRANCH)))
    waT = jnp.concatenate([w_in[:, :attn_d], w_in[:, attn_d + n_gate:], wg.reshape(d, -1)], axis=1).T.astype(BF16)
    wkv = w_kv.reshape(d, 2 * N_BRANCH, N_KV_GROUPS, HEAD_DIM)
    k_c, v_c, k_s, v_s, k_w, v_w = [wkv[:, i] for i in range(2 * N_BRANCH)]
    flat = lambda w: w.reshape(d, kv_d)
    wvT = jnp.concatenate([flat(v_s), flat(v_w)], axis=1).T.astype(BF16)
    wk = jnp.concatenate([flat(k_c), flat(v_c), flat(k_s), flat(k_w)], axis=1).astype(BF16)
    cosT, sinT, cosf, sina, sinb = _rope_tables(seq)

    qT, qrT, szT, gT, kvc, ks, kw, vsT, vwT = _nsa_proj(
        h3, b_norm[0], kv_norm, waT, wvT, wk, cosT, sinT, cosf, sina, sinb)

    w1 = jnp.stack([cmp_w1_k, cmp_w1_v]).astype(BF16)
    pos = jnp.stack([cmp_pos_k.reshape(1, -1), cmp_pos_v.reshape(1, -1)])
    w2 = jnp.stack([cmp_w2_k, cmp_w2_v])
    w2p = jnp.pad(w2, ((0, 0), (0, 0), (0, LANES - HEAD_DIM))).astype(BF16)
    w2T = jnp.swapaxes(w2, 1, 2).astype(BF16)
    cmp_tm, cmp_fm = _compress(kvc, w1, pos, w2p, w2T)

    yT = _nsa_attn(qT, qrT, gT, szT, cmp_tm[0], cmp_fm[1], ks, kw, vsT, vwT)
    return _nsa_out(yT, h3, b_w_out[0].T.astype(BF16), final_norm)
```

```python
import functools

import jax
import jax.numpy as jnp
from jax import lax
from jax.experimental import pallas as pl
from jax.experimental.pallas import tpu as pltpu

EPS = 1e-6
CONV_WIDTH = 3
N_HEADS = 16
HEAD_DIM = 64
N_KV_GROUPS = 4
HEADS_PER_GROUP = N_HEADS // N_KV_GROUPS
N_BRANCH = 3
ROT_DIM = HEAD_DIM // 4
N_FREQ = ROT_DIM // 2
ROPE_THETA = 500000.0
CMP_BLOCK = 32
CMP_STRIDE = 16
SLC_BLOCK = 64
N_SELECT = 16
N_LOCAL = 2
WINDOW = 512
FORCE_SCORE = 1e4

LANES = 128
SUBLANES = 8
KEY_CHUNK = 512
HALF = KEY_CHUNK // 2
PROJ_TILE = 256
OUT_TILE = 1024
SOFTMAX_ROWS = 32
RANK_SECTION = 16
GATE_ROWS = 16
MASK_BIAS = -1e30
LOG2_E = 1.4426950408889634
VMEM_LIMIT = 56 * 1024 * 1024

BF16 = jnp.bfloat16
F32 = jnp.float32
NT_DIMS = (((1,), (1,)), ((), ()))


def _dot(a, b):
    return jnp.dot(a, b, preferred_element_type=F32)


def _dot_nt(a, b):
    return lax.dot_general(a, b, NT_DIMS, preferred_element_type=F32)


def _dot_tn(a, b):
    return lax.dot_general(a, b, (((0,), (0,)), ((), ())), preferred_element_type=F32)


def _sigmoid(x):
    return 1.0 / (1.0 + jnp.exp(-x))


def _inv_rms(x):
    return lax.rsqrt(jnp.mean(x * x, axis=-1, keepdims=True) + EPS)


def _conv_layer_kernel(x_ref, g_ref, win_ref, cw_ref, wout_ref, o_ref, vbuf_ref, *,
                       tm, tiles_per_seq, conv_d, cchunk):
    @pl.when(pl.program_id(0) % tiles_per_seq == 0)
    def _():
        vbuf_ref[0:SUBLANES, :] = jnp.zeros((SUBLANES, conv_d), F32)

    x = x_ref[...]
    hn = (x * _inv_rms(x) * g_ref[...]).astype(BF16)
    acc = jnp.zeros(x.shape, F32)
    for cc in range(conv_d // cchunk):
        cs = cc * cchunk
        b = _dot(hn, win_ref[:, cs:cs + cchunk])
        c = _dot(hn, win_ref[:, conv_d + cs:conv_d + cs + cchunk])
        u = _dot(hn, win_ref[:, 2 * conv_d + cs:2 * conv_d + cs + cchunk])
        z = _dot(hn, win_ref[:, 3 * conv_d + cs:3 * conv_d + cs + cchunk])
        v = c * u
        vbuf_ref[SUBLANES:SUBLANES + tm, cs:cs + cchunk] = v
        v1 = vbuf_ref[SUBLANES - 1:SUBLANES - 1 + tm, cs:cs + cchunk]
        v2 = vbuf_ref[SUBLANES - 2:SUBLANES - 2 + tm, cs:cs + cchunk]
        conv = (cw_ref[0:1, cs:cs + cchunk] * v2 + cw_ref[1:2, cs:cs + cchunk] * v1
                + cw_ref[2:3, cs:cs + cchunk] * v)
        vbuf_ref[0:SUBLANES, cs:cs + cchunk] = v[tm - SUBLANES:tm, :]
        y = b * conv * (z * _sigmoid(z))
        acc = acc + _dot(y.astype(BF16), wout_ref[cs:cs + cchunk, :])
    o_ref[...] = x + acc


def _conv_layer(h, norm_g, w_in, conv_w, w_out, *, seq, tm=512, cchunk=1024):
    t, d = h.shape
    conv_d = conv_w.shape[1]
    tm = min(tm, seq)
    const = lambda shape: pl.BlockSpec(shape, lambda i: (0,) * len(shape), pipeline_mode=pl.Buffered(1))
    return pl.pallas_call(
        functools.partial(_conv_layer_kernel, tm=tm, tiles_per_seq=seq // tm, conv_d=conv_d, cchunk=cchunk),
        out_shape=jax.ShapeDtypeStruct((t, d), F32),
        grid=(t // tm,),
        in_specs=[
            pl.BlockSpec((tm, d), lambda i: (i, 0)),
            const((1, d)),
            const((d, 4 * conv_d)),
            const((CONV_WIDTH, conv_d)),
            const((conv_d, d)),
        ],
        out_specs=pl.BlockSpec((tm, d), lambda i: (i, 0)),
        scratch_shapes=[pltpu.VMEM((SUBLANES + tm, conv_d), F32)],
        compiler_params=pltpu.CompilerParams(dimension_semantics=("arbitrary",), vmem_limit_bytes=VMEM_LIMIT),
        name="conv_layer",
    )(h, norm_g.reshape(1, d), w_in, conv_w, w_out)


def _nsa_proj_kernel(h_ref, bn_ref, kn_ref, waT_ref, wvT_ref, wk_ref, cosT_ref, sinT_ref,
                     cosf_ref, sina_ref, sinb_ref,
                     qT_ref, qrT_ref, szT_ref, gT_ref, kvc_ref, ks_ref, kw_ref, vsT_ref, vwT_ref, *, tm, d):
    attn_d = N_HEADS * HEAD_DIM
    kv_d = N_KV_GROUPS * HEAD_DIM
    h = h_ref[0]
    hr = h * _inv_rms(h)
    hq = (hr * bn_ref[...]).astype(BF16)
    hk = (hr * kn_ref[...]).astype(BF16)

    qT = _dot_nt(waT_ref[0:attn_d, :], hq) * (HEAD_DIM ** -0.5 * LOG2_E)
    cosT = cosT_ref[...]
    sinT = sinT_ref[...]
    for hd in range(N_HEADS):
        r0 = hd * HEAD_DIM
        blk = qT[r0:r0 + HEAD_DIM, :]
        x1 = blk[0:N_FREQ, :]
        x2 = blk[N_FREQ:ROT_DIM, :]
        rot = jnp.concatenate([x1 * cosT - x2 * sinT, x2 * cosT + x1 * sinT, blk[ROT_DIM:, :]], axis=0)
        qT_ref[0, r0:r0 + HEAD_DIM, :] = blk.astype(BF16)
        qrT_ref[0, r0:r0 + HEAD_DIM, :] = rot.astype(BF16)

    zT = _dot_nt(waT_ref[attn_d:2 * attn_d, :], hq)
    szT_ref[0] = zT * _sigmoid(zT)
    gT_ref[0] = _sigmoid(_dot_nt(waT_ref[2 * attn_d:2 * attn_d + N_KV_GROUPS * GATE_ROWS, :], hq))

    vT = _dot_nt(wvT_ref[...], hk)
    vsT_ref[0, 0] = vT[0:kv_d, :].astype(BF16)
    vwT_ref[0, 0] = vT[kv_d:2 * kv_d, :].astype(BF16)

    kk = _dot(hk, wk_ref[...])
    kvc_ref[0] = kk[:, 0:2 * kv_d]
    cosf = cosf_ref[...]
    sina = sina_ref[...]
    sinb = sinb_ref[...]
    row = lax.broadcasted_iota(jnp.int32, (tm, LANES), 0)
    lane = lax.broadcasted_iota(jnp.int32, (tm, LANES), 1)
    tok_blk = (pl.program_id(1) * tm + row) // SLC_BLOCK
    onehot = jnp.where((lane >= HEAD_DIM) & (lane - HEAD_DIM == tok_blk), 1.0, 0.0)
    lower = lane < HEAD_DIM
    for i, ref in enumerate((ks_ref, kw_ref)):
        for pair in range(N_KV_GROUPS // 2):
            c0 = (2 + i) * kv_d + pair * LANES
            xb = kk[:, c0:c0 + LANES]
            rot = (xb * cosf + pltpu.roll(xb, LANES - N_FREQ, axis=1) * sina
                   + pltpu.roll(xb, N_FREQ, axis=1) * sinb)
            for half, keys in enumerate((rot, pltpu.roll(rot, HEAD_DIM, axis=1))):
                g = 2 * pair + half
                ref[0, :, g * LANES:(g + 1) * LANES] = (jnp.where(lower, keys, 0.0) + onehot).astype(BF16)


def _nsa_proj(h3, b_norm, kv_norm, waT, wvT, wk, cosT, sinT, cosf, sina, sinb, *, tm=PROJ_TILE):
    bsz, seq, d = h3.shape
    attn_d = N_HEADS * HEAD_DIM
    kv_d = N_KV_GROUPS * HEAD_DIM
    kpad = N_KV_GROUPS * LANES
    nt = seq // tm
    const = lambda shape: pl.BlockSpec(shape, lambda b, i: (0,) * len(shape))
    fm = lambda rows: pl.BlockSpec((1, rows, tm), lambda b, i: (b, 0, i))
    tmj = lambda cols: pl.BlockSpec((1, tm, cols), lambda b, i: (b, i, 0))
    out_shape = [
        jax.ShapeDtypeStruct((bsz, attn_d, seq), BF16),
        jax.ShapeDtypeStruct((bsz, attn_d, seq), BF16),
        jax.ShapeDtypeStruct((bsz, attn_d, seq), F32),
        jax.ShapeDtypeStruct((bsz, N_KV_GROUPS * GATE_ROWS, seq), F32),
        jax.ShapeDtypeStruct((bsz, seq, 2 * kv_d), F32),
        jax.ShapeDtypeStruct((bsz, seq, kpad), BF16),
        jax.ShapeDtypeStruct((bsz, seq, kpad), BF16),
        jax.ShapeDtypeStruct((bsz, nt, kv_d, tm), BF16),
        jax.ShapeDtypeStruct((bsz, nt, kv_d, tm), BF16),
    ]
    out_specs = [
        fm(attn_d), fm(attn_d), fm(attn_d), fm(N_KV_GROUPS * GATE_ROWS),
        tmj(2 * kv_d), tmj(kpad), tmj(kpad),
        pl.BlockSpec((1, 1, kv_d, tm), lambda b, i: (b, i, 0, 0)),
        pl.BlockSpec((1, 1, kv_d, tm), lambda b, i: (b, i, 0, 0)),
    ]
    return pl.pallas_call(
        functools.partial(_nsa_proj_kernel, tm=tm, d=d),
        out_shape=out_shape,
        grid=(bsz, nt),
        in_specs=[
            tmj(d), const((1, d)), const((1, d)),
            const(waT.shape), const(wvT.shape), const(wk.shape),
            pl.BlockSpec((N_FREQ, tm), lambda b, i: (0, i)),
            pl.BlockSpec((N_FREQ, tm), lambda b, i: (0, i)),
            pl.BlockSpec((tm, LANES), lambda b, i: (i, 0)),
            pl.BlockSpec((tm, LANES), lambda b, i: (i, 0)),
            pl.BlockSpec((tm, LANES), lambda b, i: (i, 0)),
        ],
        out_specs=out_specs,
        compiler_params=pltpu.CompilerParams(dimension_semantics=("parallel", "parallel"),
                                             vmem_limit_bytes=VMEM_LIMIT),
        name="nsa_proj",
    )(h3, b_norm.reshape(1, d), kv_norm.reshape(1, d), waT, wvT, wk, cosT, sinT, cosf, sina, sinb)


def _gelu_tanh(x):
    return x * (0.5 * (1.0 + jnp.tanh(0.7978845608028654 * (x + 0.044715 * (x * x * x)))))


def _compress_kernel(x_ref, w1_ref, pos_ref, w2_ref, w2T_ref, kc_ref, cT_ref, rows_sc, *, nchunk):
    half = CMP_STRIDE * HEAD_DIM
    for l in range(CMP_STRIDE):
        xl = x_ref[0, pl.ds(l, nchunk, stride=CMP_STRIDE), :]
        for p in range(LANES // HEAD_DIM):
            rows_sc[p, :, l * HEAD_DIM:(l + 1) * HEAD_DIM] = xl[:, p * HEAD_DIM:(p + 1) * HEAD_DIM]
    rows_ok = lax.broadcasted_iota(jnp.int32, (nchunk, LANES), 0) < nchunk - 1
    cols_ok = lax.broadcasted_iota(jnp.int32, (HEAD_DIM, nchunk), 1) < nchunk - 1
    for p in range(LANES // HEAD_DIM):
        x = rows_sc[p]
        u = _dot((x + pos_ref[0, :, 0:half]).astype(BF16), w1_ref[0, 0:half, :])
        v = _dot((x + pos_ref[0, :, half:2 * half]).astype(BF16), w1_ref[0, half:2 * half, :])
        act = _gelu_tanh(u + pltpu.roll(v, nchunk - 1, axis=0)).astype(BF16)
        kc_ref[0, 0, p] = jnp.where(rows_ok, _dot(act, w2_ref[0]), 0.0).astype(BF16)
        cT_ref[0, 0, p] = jnp.where(cols_ok, _dot_nt(w2T_ref[0], act), 0.0).astype(BF16)


def _compress(kvc, w1, pos, w2p, w2T):
    bsz, seq, _ = kvc.shape
    nchunk = seq // CMP_STRIDE
    hidden = w2T.shape[2]
    width = CMP_BLOCK * HEAD_DIM
    pair = LANES // HEAD_DIM
    return pl.pallas_call(
        functools.partial(_compress_kernel, nchunk=nchunk),
        out_shape=[
            jax.ShapeDtypeStruct((2, bsz, N_KV_GROUPS, nchunk, LANES), BF16),
            jax.ShapeDtypeStruct((2, bsz, N_KV_GROUPS, HEAD_DIM, nchunk), BF16),
        ],
        grid=(2, bsz, N_KV_GROUPS // pair),
        in_specs=[
            pl.BlockSpec((1, seq, LANES), lambda s, b, p: (b, 0, (N_KV_GROUPS // pair) * s + p)),
            pl.BlockSpec((1, width, hidden), lambda s, b, p: (s, 0, 0)),
            pl.BlockSpec((1, 1, width), lambda s, b, p: (s, 0, 0)),
            pl.BlockSpec((1, hidden, LANES), lambda s, b, p: (s, 0, 0)),
            pl.BlockSpec((1, HEAD_DIM, hidden), lambda s, b, p: (s, 0, 0)),
        ],
        out_specs=[
            pl.BlockSpec((1, 1, pair, nchunk, LANES), lambda s, b, p: (s, b, p, 0, 0)),
            pl.BlockSpec((1, 1, pair, HEAD_DIM, nchunk), lambda s, b, p: (s, b, p, 0, 0)),
        ],
        scratch_shapes=[pltpu.VMEM((pair, nchunk, CMP_STRIDE * HEAD_DIM), F32)],
        compiler_params=pltpu.CompilerParams(dimension_semantics=("parallel", "parallel", "parallel"),
                                             vmem_limit_bytes=VMEM_LIMIT),
        name="compress",
    )(kvc, w1, pos, w2p, w2T)


def _attn_kernel(q_ref, qr_ref, g_ref, sz_ref, kc_ref, vcT_ref, ks_ref, kw_ref, vsT_ref, vwT_ref, y_ref,
                 p_sc, sc_sc, rank_sc, oc_sc, qa_sc, c_sc, w_sc, s_sc, pb_sc, smax_sc, st_sc, acc_sc, wmax_sc,
                 wp_sc, ow_sc, *, tq, n_cmp):
    qi = pl.program_id(2)
    t0 = qi * tq
    neg_inf = -jnp.inf
    tvec = t0 + lax.broadcasted_iota(jnp.int32, (1, tq), 1)

    kc = kc_ref[0, 0, :, 0:HEAD_DIM]
    cmp_end = lax.broadcasted_iota(jnp.int32, (n_cmp, tq), 0) * CMP_STRIDE + (CMP_BLOCK - 1)
    cmask = cmp_end <= tvec
    heads = range(HEADS_PER_GROUP)

    def fold8(x, op, ways=4):
        parts = [None] * ways
        for idx, r in enumerate(range(0, x.shape[0], SUBLANES)):
            slab = x[r:r + SUBLANES, :]
            parts[idx % ways] = slab if parts[idx % ways] is None else op(parts[idx % ways], slab)
        return functools.reduce(op, [p for p in parts if p is not None])

    def key_chunk(ref, c):
        return ref[0, pl.ds(pl.multiple_of(c * KEY_CHUNK, KEY_CHUNK), KEY_CHUNK), :]

    n_sb = n_cmp // 4
    jrow = lax.broadcasted_iota(jnp.int32, (n_sb, tq), 0)
    cur = tvec // SLC_BLOCK
    valid = jrow <= cur

    QA_SEL, QA_DEAD, QA_WIN = 0, 1, 2
    pad_blocks = lambda b: b if n_sb == HEAD_DIM else jnp.concatenate(
        [b, jnp.zeros((HEAD_DIM - n_sb, tq), BF16)], axis=0)

    def set_operand(idx, bias):
        for hh in heads:
            r0 = hh * HEAD_DIM
            qa_sc[idx, hh] = jnp.concatenate([qr_ref[0, r0:r0 + HEAD_DIM, :], pad_blocks(bias)], axis=0)

    win_blocks = WINDOW // SLC_BLOCK
    set_operand(QA_WIN, jnp.where((jrow >= cur - win_blocks) & valid, 0.0, MASK_BIAS).astype(BF16))
    set_operand(QA_DEAD, jnp.full((n_sb, tq), MASK_BIAS, BF16))

    blk_row = lax.broadcasted_iota(jnp.int32, (SLC_BLOCK, LANES), 0)
    blk_lane = lax.broadcasted_iota(jnp.int32, (SLC_BLOCK, LANES), 1)
    edge = [blk_row - blk_lane, blk_row - (blk_lane - SLC_BLOCK)]
    off_diag = [blk_lane >= SLC_BLOCK, blk_lane < SLC_BLOCK]

    def mask_diagonal_blocks(s, first_block, keep):
        width = s.shape[1]
        blocks = [s[r:r + SLC_BLOCK, :] for r in range(0, s.shape[0], SLC_BLOCK)]
        for j in range(width // SLC_BLOCK):
            cols = [blocks[first_block + j][:, h * LANES:(h + 1) * LANES] for h in range(width // LANES)]
            cols[j // 2] = jnp.where(off_diag[j % 2] | keep(edge[j % 2]), cols[j // 2], neg_inf)
            blocks[first_block + j] = jnp.concatenate(cols, axis=1)
        return jnp.concatenate(blocks, axis=0)

    n_win = KEY_CHUNK + HALF
    prev_operand = jnp.where(qi >= 1, QA_WIN, QA_DEAD)
    prev_base = jnp.maximum(qi - 1, 0) * KEY_CHUNK

    def window_pieces(half):
        first = half * HALF
        n_old = KEY_CHUNK - first
        old = (pl.multiple_of(prev_base + first, HALF), n_old)
        new = (pl.multiple_of(qi * KEY_CHUNK, HALF), HALF + first)
        return old, new

    for hh in heads:
        r0 = hh * HEAD_DIM
        c_sc[hh] = jnp.where(cmask, _dot(kc, q_ref[0, r0:r0 + HEAD_DIM, :]), neg_inf)
        for half in range(tq // HALF):
            (old_start, n_old), (new_start, n_new) = window_pieces(half)
            lanes = slice(half * HALF, (half + 1) * HALF)
            s_old = _dot(kw_ref[0, pl.ds(old_start, n_old), :], qa_sc[prev_operand, hh, :, lanes])
            s_new = _dot(kw_ref[0, pl.ds(new_start, n_new), :], qa_sc[QA_WIN, hh, :, lanes])
            s_old = mask_diagonal_blocks(s_old, 0, lambda e: e > 0)
            s_new = mask_diagonal_blocks(s_new, half * (HALF // SLC_BLOCK), lambda e: e <= 0)
            w_sc[hh, half, 0:n_old, :] = s_old
            w_sc[hh, half, n_old:n_win, :] = s_new
            wmax_sc[hh, half] = jnp.maximum(fold8(s_old, jnp.maximum), fold8(s_new, jnp.maximum))
    probs = []
    for hh in heads:
        s = c_sc[hh]
        m = jnp.max(fold8(s, jnp.maximum), axis=0, keepdims=True)
        m = jnp.where(m == neg_inf, 0.0, m)
        e = jnp.exp2(s - m)
        den = jnp.sum(fold8(e, jnp.add), axis=0, keepdims=True)
        probs.append(e * (1.0 / jnp.maximum(den, 1e-30)))
    p_grp = functools.reduce(jnp.add, probs)
    for hh in heads:
        r0 = hh * HEAD_DIM
        oc_sc[r0:r0 + HEAD_DIM, :] = _dot(vcT_ref[0, 0], probs[hh].astype(BF16))

    ratio = SLC_BLOCK // CMP_STRIDE
    imp_cols = []
    for c in range(tq // LANES):
        p_sc[c, 0:SUBLANES, :] = jnp.zeros((SUBLANES, LANES), F32)
        p_sc[c, SUBLANES:SUBLANES + n_cmp, :] = p_grp[:, c * LANES:(c + 1) * LANES]
        tap = lambda o: p_sc[c, pl.ds(SUBLANES + o, n_sb, stride=ratio), :]
        imp_cols.append(tap(-1) + 2.0 * (tap(0) + tap(1) + tap(2)) + tap(3))
    imp = jnp.concatenate(imp_cols, axis=1)

    forced = (jrow == 0) | (valid & (jrow > cur - N_LOCAL))
    score = jnp.where(valid, jnp.where(forced, FORCE_SCORE, imp), neg_inf)
    sc_sc[...] = score

    n_valid = (t0 + tq) // SLC_BLOCK
    n_slabs = n_sb // SUBLANES
    slab_row = lax.broadcasted_iota(jnp.int32, (SUBLANES, tq), 0)
    rank_sc[...] = jnp.zeros((n_sb, tq), jnp.int32)
    for first in range(0, n_sb, RANK_SECTION):
        @pl.when(n_valid > max(N_SELECT, first))
        def _():
            slabs = [sc_sc[k * SUBLANES:(k + 1) * SUBLANES, :] for k in range(n_slabs)]
            counts = [rank_sc[k * SUBLANES:(k + 1) * SUBLANES, :] for k in range(n_slabs)]
            for jp in range(first, first + RANK_SECTION):
                sb = sc_sc[jp:jp + 1, :]
                for k in range(n_slabs):
                    if k * SUBLANES > jp:
                        before = sb >= slabs[k]
                    elif k * SUBLANES + SUBLANES - 1 < jp:
                        before = sb > slabs[k]
                    else:
                        before = (sb > slabs[k]) | ((sb == slabs[k]) & (slab_row > jp % SUBLANES))
                    counts[k] = counts[k] + jnp.where(before, 1, 0)
            for k in range(n_slabs):
                rank_sc[k * SUBLANES:(k + 1) * SUBLANES, :] = counts[k]
    set_operand(QA_SEL, jnp.where((rank_sc[...] < N_SELECT) & valid, 0.0, MASK_BIAS).astype(BF16))

    group = KEY_CHUNK
    n_double = qi // 2

    def values_of(ref, c):
        per = KEY_CHUNK // PROJ_TILE
        return jnp.concatenate([ref[0, per * c + v] for v in range(per)], axis=1)

    ST_MAX, ST_SUM, ST_RESCALE = 0, 1, 2
    rows8 = lambda x: jnp.broadcast_to(x, (SUBLANES, tq))
    never = 1 << 20

    def scores_to(slot, pos, hh, may_be_last):
        s = _dot(key_chunk(ks_ref, pos), qa_sc[QA_SEL, hh])
        if may_be_last:
            causal_slack = jnp.where(pos == qi, 0, never)
            s = mask_diagonal_blocks(s, 0, lambda e: e <= causal_slack)
        s_sc[slot, hh] = s
        smax_sc[slot, hh] = fold8(s, jnp.maximum)

    def accumulate(pos, slot, hh):
        acc_sc[hh] = (acc_sc[hh] * st_sc[ST_RESCALE + slot, hh, 0:1, :]
                      + _dot(values_of(vsT_ref, pos), pb_sc[slot, hh]))

    def softmax_group(slot, hh):
        m_old = st_sc[ST_MAX, hh, 0:1, :]
        m_new = jnp.maximum(m_old, jnp.max(smax_sc[slot, hh], axis=0, keepdims=True))
        a = jnp.exp2(m_old - m_new)
        psum = None
        for r in range(0, group, SOFTMAX_ROWS):
            s = s_sc[slot, hh, r:r + SOFTMAX_ROWS, :]
            p = jnp.exp2(s - m_new)
            pb_sc[slot, hh, r:r + SOFTMAX_ROWS, :] = p.astype(BF16)
            f = fold8(p, jnp.add, ways=2)
            psum = f if psum is None else psum + f
        st_sc[ST_SUM, hh] = a * st_sc[ST_SUM, hh] + psum
        st_sc[ST_MAX, hh] = rows8(m_new)
        st_sc[ST_RESCALE + slot, hh] = rows8(a)

    def stage(slot, pos, next_may_be_last):
        for hh in heads:
            scores_to(1 - slot, pos + 1, hh, next_may_be_last)
            accumulate(jnp.maximum(pos - 1, 0), 1 - slot, hh)
            softmax_group(slot, hh)

    def drain(slot):
        for hh in heads:
            accumulate(jnp.maximum(qi - 1, 0), 1 - slot, hh)
            softmax_group(slot, hh)
            accumulate(qi, slot, hh)

    def double_trip(d, carry):
        stage(0, 2 * d, next_may_be_last=False)
        stage(1, 2 * d + 1, next_may_be_last=True)
        return carry

    def window_values(half):
        pieces = []
        for start, rows in window_pieces(half):
            pieces += [vwT_ref[0, start // PROJ_TILE + v] for v in range(rows // PROJ_TILE)]
        return jnp.concatenate(pieces, axis=1)

    for hh in heads:
        acc_sc[hh] = jnp.zeros((HEAD_DIM, tq), F32)
        pb_sc[1, hh] = jnp.zeros((group, tq), BF16)
        st_sc[ST_MAX, hh] = jnp.full((SUBLANES, tq), neg_inf, F32)
        st_sc[ST_SUM, hh] = jnp.zeros((SUBLANES, tq), F32)
        st_sc[ST_RESCALE + 1, hh] = jnp.ones((SUBLANES, tq), F32)
        scores_to(0, 0, hh, may_be_last=True)
        for half in range(tq // HALF):
            m = jnp.max(wmax_sc[hh, half], axis=0, keepdims=True)
            psum = None
            for r in range(0, n_win, 2 * SOFTMAX_ROWS):
                p = jnp.exp2(w_sc[hh, half, r:r + 2 * SOFTMAX_ROWS, :] - m)
                wp_sc[hh, half, r:r + 2 * SOFTMAX_ROWS, :] = p.astype(BF16)
                f = fold8(p, jnp.add, ways=2)
                psum = f if psum is None else psum + f
            ow_sc[hh, :, half * HALF:(half + 1) * HALF] = (
                _dot(window_values(half), wp_sc[hh, half]) * (1.0 / jnp.sum(psum, axis=0, keepdims=True)))

    lax.fori_loop(0, n_double, double_trip, 0)

    @pl.when(qi % 2 == 0)
    def _():
        drain(0)

    @pl.when(qi % 2 == 1)
    def _():
        stage(0, qi - 1, next_may_be_last=True)
        drain(1)

    for hh in heads:
        r0 = hh * HEAD_DIM
        o_w = ow_sc[hh]
        o_s = acc_sc[hh] * (1.0 / jnp.sum(st_sc[ST_SUM, hh], axis=0, keepdims=True))

        g0 = g_ref[0, N_BRANCH * hh:N_BRANCH * hh + 1, :]
        g1 = g_ref[0, N_BRANCH * hh + 1:N_BRANCH * hh + 2, :]
        g2 = g_ref[0, N_BRANCH * hh + 2:N_BRANCH * hh + 3, :]
        o = g0 * oc_sc[r0:r0 + HEAD_DIM, :] + g1 * o_s + g2 * o_w
        y_ref[0, r0:r0 + HEAD_DIM, :] = (o * sz_ref[0, r0:r0 + HEAD_DIM, :]).astype(BF16)


def _nsa_attn(qT, qrT, gT, szT, kcmp, vcmpT, ks, kw, vsT, vwT, *, tq=KEY_CHUNK):
    bsz, attn_d, seq = qT.shape
    n_cmp = kcmp.shape[2]
    n_win = KEY_CHUNK + HALF
    gd = HEADS_PER_GROUP * HEAD_DIM
    assert tq == KEY_CHUNK == WINDOW and HALF % PROJ_TILE == 0 and vsT.shape[3] == PROJ_TILE
    qspec = pl.BlockSpec((1, gd, tq), lambda b, g, i: (b, g, i))
    kspec = pl.BlockSpec((1, seq, LANES), lambda b, g, i: (b, 0, g))
    vspec = pl.BlockSpec((1, seq // PROJ_TILE, HEAD_DIM, PROJ_TILE), lambda b, g, i: (b, 0, g, 0))
    return pl.pallas_call(
        functools.partial(_attn_kernel, tq=tq, n_cmp=n_cmp),
        out_shape=jax.ShapeDtypeStruct((bsz, attn_d, seq), BF16),
        grid=(bsz, N_KV_GROUPS, seq // tq),
        in_specs=[
            qspec, qspec,
            pl.BlockSpec((1, GATE_ROWS, tq), lambda b, g, i: (b, g, i)),
            qspec,
            pl.BlockSpec((1, 1, n_cmp, LANES), lambda b, g, i: (b, g, 0, 0)),
            pl.BlockSpec((1, 1, HEAD_DIM, n_cmp), lambda b, g, i: (b, g, 0, 0)),
            kspec, kspec, vspec, vspec,
        ],
        out_specs=qspec,
        scratch_shapes=[
            pltpu.VMEM((tq // LANES, SUBLANES + n_cmp, LANES), F32),
            pltpu.VMEM((n_cmp // 4, tq), F32),
            pltpu.VMEM((n_cmp // 4, tq), jnp.int32),
            pltpu.VMEM((gd, tq), F32),
            pltpu.VMEM((3, HEADS_PER_GROUP, 2 * HEAD_DIM, tq), BF16),
            pltpu.VMEM((HEADS_PER_GROUP, n_cmp, tq), F32),
            pltpu.VMEM((HEADS_PER_GROUP, tq // HALF, n_win, HALF), F32),
            pltpu.VMEM((2, HEADS_PER_GROUP, KEY_CHUNK, tq), F32),
            pltpu.VMEM((2, HEADS_PER_GROUP, KEY_CHUNK, tq), BF16),
            pltpu.VMEM((2, HEADS_PER_GROUP, SUBLANES, tq), F32),
            pltpu.VMEM((4, HEADS_PER_GROUP, SUBLANES, tq), F32),
            pltpu.VMEM((HEADS_PER_GROUP, HEAD_DIM, tq), F32),
            pltpu.VMEM((HEADS_PER_GROUP, tq // HALF, SUBLANES, HALF), F32),
            pltpu.VMEM((HEADS_PER_GROUP, tq // HALF, n_win, HALF), BF16),
            pltpu.VMEM((HEADS_PER_GROUP, HEAD_DIM, tq), F32),
        ],
        compiler_params=pltpu.CompilerParams(dimension_semantics=("parallel", "parallel", "arbitrary"),
                                             vmem_limit_bytes=VMEM_LIMIT),
        name="nsa_attn",
    )(qT, qrT, gT, szT, kcmp, vcmpT, ks, kw, vsT, vwT)


def _nsa_out_kernel(y_ref, h_ref, wo_ref, fn_ref, o_ref):
    h2 = h_ref[0] + _dot_tn(y_ref[0], wo_ref[...])
    o_ref[0] = h2 * _inv_rms(h2) * fn_ref[...]


def _nsa_out(yT, h3, wo, final_norm, *, tm=OUT_TILE):
    bsz, seq, d = h3.shape
    attn_d = yT.shape[1]
    return pl.pallas_call(
        _nsa_out_kernel,
        out_shape=jax.ShapeDtypeStruct((bsz, seq, d), F32),
        grid=(bsz, seq // tm),
        in_specs=[
            pl.BlockSpec((1, attn_d, tm), lambda b, i: (b, 0, i)),
            pl.BlockSpec((1, tm, d), lambda b, i: (b, i, 0)),
            pl.BlockSpec((attn_d, d), lambda b, i: (0, 0)),
            pl.BlockSpec((1, d), lambda b, i: (0, 0)),
        ],
        out_specs=pl.BlockSpec((1, tm, d), lambda b, i: (b, i, 0)),
        compiler_params=pltpu.CompilerParams(dimension_semantics=("parallel", "parallel"),
                                             vmem_limit_bytes=VMEM_LIMIT),
        name="nsa_out",
    )(yT, h3, wo, final_norm.reshape(1, d))


def _rope_tables(seq):
    pos = jnp.arange(seq, dtype=F32)
    inv = ROPE_THETA ** (-jnp.arange(0, ROT_DIM, 2, dtype=F32) / ROT_DIM)
    ang = pos[:, None] * inv[None, :]
    cos, sin = jnp.cos(ang), jnp.sin(ang)
    z = lambda n: jnp.zeros((seq, n), F32)
    per_head = lambda parts: jnp.tile(jnp.concatenate(parts, axis=1), (1, LANES // HEAD_DIM))
    cosf = per_head([cos, cos, jnp.ones((seq, HEAD_DIM - ROT_DIM), F32)])
    sina = per_head([-sin, z(HEAD_DIM - N_FREQ)])
    sinb = per_head([z(N_FREQ), sin, z(HEAD_DIM - ROT_DIM)])
    return cos.T, sin.T, cosf, sina, sinb


def kernel(x, a_norm, a_w_in, a_conv_w, a_w_out, kv_norm, w_kv, cmp_pos_k, cmp_w1_k, cmp_w2_k,
           cmp_pos_v, cmp_w1_v, cmp_w2_v, b_norm, b_w_in, b_w_out, final_norm):
    bsz, seq, d = x.shape
    attn_d = N_HEADS * HEAD_DIM
    kv_d = N_KV_GROUPS * HEAD_DIM
    assert b_norm.shape[0] == 1, "one NSA layer reads the shared K/V side"
    assert seq % KEY_CHUNK == 0

    h = x.reshape(bsz * seq, d)
    for layer in range(a_norm.shape[0]):
        h = _conv_layer(h, a_norm[layer], a_w_in[layer].astype(BF16), a_conv_w[layer],
                        a_w_out[layer].astype(BF16), seq=seq)
    h3 = h.reshape(bsz, seq, d)

    w_in = b_w_in[0]
    n_gate = N_HEADS * N_BRANCH
    wg = w_in[:, attn_d:attn_d + n_gate].reshape(d, N_KV_GROUPS, HEADS_PER_GROUP * N_BRANCH)
    wg = jnp.pad(wg, ((0, 0), (0, 0), (0, GATE_ROWS - HEADS_PER_GROUP * N_BRANCH)))
    waT = jnp.concatenate([w_in[:, :attn_d], w_in[:, attn_d + n_gate:], wg.reshape(d, -1)], axis=1).T.astype(BF16)
    wkv = w_kv.reshape(d, 2 * N_BRANCH, N_KV_GROUPS, HEAD_DIM)
    k_c, v_c, k_s, v_s, k_w, v_w = [wkv[:, i] for i in range(2 * N_BRANCH)]
    flat = lambda w: w.reshape(d, kv_d)
    wvT = jnp.concatenate([flat(v_s), flat(v_w)], axis=1).T.astype(BF16)
    wk = jnp.concatenate([flat(k_c), flat(v_c), flat(k_s), flat(k_w)], axis=1).astype(BF16)
    cosT, sinT, cosf, sina, sinb = _rope_tables(seq)

    qT, qrT, szT, gT, kvc, ks, kw, vsT, vwT = _nsa_proj(
        h3, b_norm[0], kv_norm, waT, wvT, wk, cosT, sinT, cosf, sina, sinb)

    w1 = jnp.stack([cmp_w1_k, cmp_w1_v]).astype(BF16)
    pos = jnp.stack([cmp_pos_k.reshape(1, -1), cmp_pos_v.reshape(1, -1)])
    w2 = jnp.stack([cmp_w2_k, cmp_w2_v])
    w2p = jnp.pad(w2, ((0, 0), (0, 0), (0, LANES - HEAD_DIM))).astype(BF16)
    w2T = jnp.swapaxes(w2, 1, 2).astype(BF16)
    cmp_tm, cmp_fm = _compress(kvc, w1, pos, w2p, w2T)

    yT = _nsa_attn(qT, qrT, gT, szT, cmp_tm[0], cmp_fm[1], ks, kw, vsT, vwT)
    return _nsa_out(yT, h3, b_w_out[0].astype(BF16), final_norm)
```

```python
import functools

import jax
import jax.numpy as jnp
from jax import lax
from jax.experimental import pallas as pl
from jax.experimental.pallas import tpu as pltpu

EPS = 1e-6
CONV_WIDTH = 3
N_HEADS = 16
HEAD_DIM = 64
N_KV_GROUPS = 4
HEADS_PER_GROUP = N_HEADS // N_KV_GROUPS
N_BRANCH = 3
ROT_DIM = HEAD_DIM // 4
N_FREQ = ROT_DIM // 2
ROPE_THETA = 500000.0
CMP_BLOCK = 32
CMP_STRIDE = 16
SLC_BLOCK = 64
N_SELECT = 16
N_LOCAL = 2
WINDOW = 512
FORCE_SCORE = 1e4

LANES = 128
SUBLANES = 8
KEY_CHUNK = 512
HALF = KEY_CHUNK // 2
PROJ_TILE = 256
OUT_TILE = 1024
SOFTMAX_ROWS = 32
RANK_SECTION = 8
GATE_ROWS = 16
VW_ROWS = HEAD_DIM + 16
MASK_BIAS = -1e30
LOG2_E = 1.4426950408889634
VMEM_LIMIT = 56 * 1024 * 1024

BF16 = jnp.bfloat16
F32 = jnp.float32
NT_DIMS = (((1,), (1,)), ((), ()))


def _dot(a, b):
    return jnp.dot(a, b, preferred_element_type=F32)


def _dot_nt(a, b):
    return lax.dot_general(a, b, NT_DIMS, preferred_element_type=F32)


def _dot_tn(a, b):
    return lax.dot_general(a, b, (((0,), (0,)), ((), ())), preferred_element_type=F32)


def _sigmoid(x):
    return 1.0 / (1.0 + jnp.exp(-x))


def _inv_rms(x):
    return lax.rsqrt(jnp.mean(x * x, axis=-1, keepdims=True) + EPS)


def _conv_layer_kernel(x_ref, g_ref, win_ref, cw_ref, wout_ref, o_ref, vbuf_ref, *,
                       tm, tiles_per_seq, conv_d, cchunk):
    @pl.when(pl.program_id(0) % tiles_per_seq == 0)
    def _():
        vbuf_ref[0:SUBLANES, :] = jnp.zeros((SUBLANES, conv_d), F32)

    x = x_ref[...]
    hn = (x * _inv_rms(x) * g_ref[...]).astype(BF16)
    acc = jnp.zeros(x.shape, F32)
    for cc in range(conv_d // cchunk):
        cs = cc * cchunk
        b = _dot(hn, win_ref[:, cs:cs + cchunk])
        c = _dot(hn, win_ref[:, conv_d + cs:conv_d + cs + cchunk])
        u = _dot(hn, win_ref[:, 2 * conv_d + cs:2 * conv_d + cs + cchunk])
        z = _dot(hn, win_ref[:, 3 * conv_d + cs:3 * conv_d + cs + cchunk])
        v = c * u
        vbuf_ref[SUBLANES:SUBLANES + tm, cs:cs + cchunk] = v
        v1 = vbuf_ref[SUBLANES - 1:SUBLANES - 1 + tm, cs:cs + cchunk]
        v2 = vbuf_ref[SUBLANES - 2:SUBLANES - 2 + tm, cs:cs + cchunk]
        conv = (cw_ref[0:1, cs:cs + cchunk] * v2 + cw_ref[1:2, cs:cs + cchunk] * v1
                + cw_ref[2:3, cs:cs + cchunk] * v)
        vbuf_ref[0:SUBLANES, cs:cs + cchunk] = v[tm - SUBLANES:tm, :]
        y = b * conv * (z * _sigmoid(z))
        acc = acc + _dot(y.astype(BF16), wout_ref[cs:cs + cchunk, :])
    o_ref[...] = x + acc


def _conv_layer(h, norm_g, w_in, conv_w, w_out, *, seq, tm=512, cchunk=1024):
    t, d = h.shape
    conv_d = conv_w.shape[1]
    tm = min(tm, seq)
    const = lambda shape: pl.BlockSpec(shape, lambda i: (0,) * len(shape), pipeline_mode=pl.Buffered(1))
    return pl.pallas_call(
        functools.partial(_conv_layer_kernel, tm=tm, tiles_per_seq=seq // tm, conv_d=conv_d, cchunk=cchunk),
        out_shape=jax.ShapeDtypeStruct((t, d), F32),
        grid=(t // tm,),
        in_specs=[
            pl.BlockSpec((tm, d), lambda i: (i, 0)),
            const((1, d)),
            const((d, 4 * conv_d)),
            const((CONV_WIDTH, conv_d)),
            const((conv_d, d)),
        ],
        out_specs=pl.BlockSpec((tm, d), lambda i: (i, 0)),
        scratch_shapes=[pltpu.VMEM((SUBLANES + tm, conv_d), F32)],
        compiler_params=pltpu.CompilerParams(dimension_semantics=("arbitrary",), vmem_limit_bytes=VMEM_LIMIT),
        name="conv_layer",
    )(h, norm_g.reshape(1, d), w_in, conv_w, w_out)


def _nsa_proj_kernel(h_ref, bn_ref, kn_ref, waT_ref, wvT_ref, wk_ref, cosT_ref, sinT_ref,
                     cosf_ref, sina_ref, sinb_ref,
                     qT_ref, qrT_ref, szT_ref, gT_ref, kvc_ref, ks_ref, kw_ref, vsT_ref, vwT_ref, *, tm, d):
    attn_d = N_HEADS * HEAD_DIM
    kv_d = N_KV_GROUPS * HEAD_DIM
    h = h_ref[0]
    hr = h * _inv_rms(h)
    hq = (hr * bn_ref[...]).astype(BF16)
    hk = (hr * kn_ref[...]).astype(BF16)

    qT = _dot_nt(waT_ref[0:attn_d, :], hq) * (HEAD_DIM ** -0.5 * LOG2_E)
    cosT = cosT_ref[...]
    sinT = sinT_ref[...]
    for hd in range(N_HEADS):
        r0 = hd * HEAD_DIM
        blk = qT[r0:r0 + HEAD_DIM, :]
        x1 = blk[0:N_FREQ, :]
        x2 = blk[N_FREQ:ROT_DIM, :]
        rot = jnp.concatenate([x1 * cosT - x2 * sinT, x2 * cosT + x1 * sinT, blk[ROT_DIM:, :]], axis=0)
        qT_ref[0, r0:r0 + HEAD_DIM, :] = blk.astype(BF16)
        qrT_ref[0, r0:r0 + HEAD_DIM, :] = rot.astype(BF16)

    zT = _dot_nt(waT_ref[attn_d:2 * attn_d, :], hq)
    szT_ref[0] = zT * _sigmoid(zT)
    gT_ref[0] = _sigmoid(_dot_nt(waT_ref[2 * attn_d:2 * attn_d + N_KV_GROUPS * GATE_ROWS, :], hq))

    vT = _dot_nt(wvT_ref[...], hk)
    vsT_ref[0, 0] = vT[0:kv_d, :].astype(BF16)
    ones_rows = jnp.ones((VW_ROWS - HEAD_DIM, tm), BF16)
    for g in range(N_KV_GROUPS):
        src = kv_d + g * HEAD_DIM
        vwT_ref[0, 0, g * VW_ROWS:g * VW_ROWS + HEAD_DIM, :] = vT[src:src + HEAD_DIM, :].astype(BF16)
        vwT_ref[0, 0, g * VW_ROWS + HEAD_DIM:(g + 1) * VW_ROWS, :] = ones_rows

    kk = _dot(hk, wk_ref[...])
    kvc_ref[0] = kk[:, 0:2 * kv_d]
    cosf = cosf_ref[...]
    sina = sina_ref[...]
    sinb = sinb_ref[...]
    row = lax.broadcasted_iota(jnp.int32, (tm, LANES), 0)
    lane = lax.broadcasted_iota(jnp.int32, (tm, LANES), 1)
    tok_blk = (pl.program_id(1) * tm + row) // SLC_BLOCK
    onehot = jnp.where((lane >= HEAD_DIM) & (lane - HEAD_DIM == tok_blk), 1.0, 0.0)
    lower = lane < HEAD_DIM
    for i, ref in enumerate((ks_ref, kw_ref)):
        for pair in range(N_KV_GROUPS // 2):
            c0 = (2 + i) * kv_d + pair * LANES
            xb = kk[:, c0:c0 + LANES]
            rot = (xb * cosf + pltpu.roll(xb, LANES - N_FREQ, axis=1) * sina
                   + pltpu.roll(xb, N_FREQ, axis=1) * sinb)
            for half, keys in enumerate((rot, pltpu.roll(rot, HEAD_DIM, axis=1))):
                g = 2 * pair + half
                ref[0, :, g * LANES:(g + 1) * LANES] = (jnp.where(lower, keys, 0.0) + onehot).astype(BF16)


def _nsa_proj(h3, b_norm, kv_norm, waT, wvT, wk, cosT, sinT, cosf, sina, sinb, *, tm=PROJ_TILE):
    bsz, seq, d = h3.shape
    attn_d = N_HEADS * HEAD_DIM
    kv_d = N_KV_GROUPS * HEAD_DIM
    kpad = N_KV_GROUPS * LANES
    nt = seq // tm
    const = lambda shape: pl.BlockSpec(shape, lambda b, i: (0,) * len(shape))
    fm = lambda rows: pl.BlockSpec((1, rows, tm), lambda b, i: (b, 0, i))
    tmj = lambda cols: pl.BlockSpec((1, tm, cols), lambda b, i: (b, i, 0))
    out_shape = [
        jax.ShapeDtypeStruct((bsz, attn_d, seq), BF16),
        jax.ShapeDtypeStruct((bsz, attn_d, seq), BF16),
        jax.ShapeDtypeStruct((bsz, attn_d, seq), F32),
        jax.ShapeDtypeStruct((bsz, N_KV_GROUPS * GATE_ROWS, seq), F32),
        jax.ShapeDtypeStruct((bsz, seq, 2 * kv_d), F32),
        jax.ShapeDtypeStruct((bsz, seq, kpad), BF16),
        jax.ShapeDtypeStruct((bsz, seq, kpad), BF16),
        jax.ShapeDtypeStruct((bsz, nt, kv_d, tm), BF16),
        jax.ShapeDtypeStruct((bsz, nt, N_KV_GROUPS * VW_ROWS, tm), BF16),
    ]
    out_specs = [
        fm(attn_d), fm(attn_d), fm(attn_d), fm(N_KV_GROUPS * GATE_ROWS),
        tmj(2 * kv_d), tmj(kpad), tmj(kpad),
        pl.BlockSpec((1, 1, kv_d, tm), lambda b, i: (b, i, 0, 0)),
        pl.BlockSpec((1, 1, N_KV_GROUPS * VW_ROWS, tm), lambda b, i: (b, i, 0, 0)),
    ]
    return pl.pallas_call(
        functools.partial(_nsa_proj_kernel, tm=tm, d=d),
        out_shape=out_shape,
        grid=(bsz, nt),
        in_specs=[
            tmj(d), const((1, d)), const((1, d)),
            const(waT.shape), const(wvT.shape), const(wk.shape),
            pl.BlockSpec((N_FREQ, tm), lambda b, i: (0, i)),
            pl.BlockSpec((N_FREQ, tm), lambda b, i: (0, i)),
            pl.BlockSpec((tm, LANES), lambda b, i: (i, 0)),
            pl.BlockSpec((tm, LANES), lambda b, i: (i, 0)),
            pl.BlockSpec((tm, LANES), lambda b, i: (i, 0)),
        ],
        out_specs=out_specs,
        compiler_params=pltpu.CompilerParams(dimension_semantics=("parallel", "parallel"),
                                             vmem_limit_bytes=VMEM_LIMIT),
        name="nsa_proj",
    )(h3, b_norm.reshape(1, d), kv_norm.reshape(1, d), waT, wvT, wk, cosT, sinT, cosf, sina, sinb)


def _gelu_tanh(x):
    return x * (0.5 * (1.0 + jnp.tanh(0.7978845608028654 * (x + 0.044715 * (x * x * x)))))


def _compress_kernel(x_ref, w1_ref, pos_ref, w2_ref, w2T_ref, kc_ref, cT_ref, rows_sc, *, nchunk):
    half = CMP_STRIDE * HEAD_DIM
    for l in range(CMP_STRIDE):
        xl = x_ref[0, pl.ds(l, nchunk, stride=CMP_STRIDE), :]
        for p in range(LANES // HEAD_DIM):
            rows_sc[p, :, l * HEAD_DIM:(l + 1) * HEAD_DIM] = xl[:, p * HEAD_DIM:(p + 1) * HEAD_DIM]
    rows_ok = lax.broadcasted_iota(jnp.int32, (nchunk, LANES), 0) < nchunk - 1
    cols_ok = lax.broadcasted_iota(jnp.int32, (HEAD_DIM, nchunk), 1) < nchunk - 1
    for p in range(LANES // HEAD_DIM):
        x = rows_sc[p]
        u = _dot((x + pos_ref[0, :, 0:half]).astype(BF16), w1_ref[0, 0:half, :])
        v = _dot((x + pos_ref[0, :, half:2 * half]).astype(BF16), w1_ref[0, half:2 * half, :])
        act = _gelu_tanh(u + pltpu.roll(v, nchunk - 1, axis=0)).astype(BF16)
        kc_ref[0, 0, p] = jnp.where(rows_ok, _dot(act, w2_ref[0]), 0.0).astype(BF16)
        cT_ref[0, 0, p] = jnp.where(cols_ok, _dot_nt(w2T_ref[0], act), 0.0).astype(BF16)


def _compress(kvc, w1, pos, w2p, w2T):
    bsz, seq, _ = kvc.shape
    nchunk = seq // CMP_STRIDE
    hidden = w2T.shape[2]
    width = CMP_BLOCK * HEAD_DIM
    pair = LANES // HEAD_DIM
    return pl.pallas_call(
        functools.partial(_compress_kernel, nchunk=nchunk),
        out_shape=[
            jax.ShapeDtypeStruct((2, bsz, N_KV_GROUPS, nchunk, LANES), BF16),
            jax.ShapeDtypeStruct((2, bsz, N_KV_GROUPS, HEAD_DIM, nchunk), BF16),
        ],
        grid=(2, bsz, N_KV_GROUPS // pair),
        in_specs=[
            pl.BlockSpec((1, seq, LANES), lambda s, b, p: (b, 0, (N_KV_GROUPS // pair) * s + p)),
            pl.BlockSpec((1, width, hidden), lambda s, b, p: (s, 0, 0)),
            pl.BlockSpec((1, 1, width), lambda s, b, p: (s, 0, 0)),
            pl.BlockSpec((1, hidden, LANES), lambda s, b, p: (s, 0, 0)),
            pl.BlockSpec((1, HEAD_DIM, hidden), lambda s, b, p: (s, 0, 0)),
        ],
        out_specs=[
            pl.BlockSpec((1, 1, pair, nchunk, LANES), lambda s, b, p: (s, b, p, 0, 0)),
            pl.BlockSpec((1, 1, pair, HEAD_DIM, nchunk), lambda s, b, p: (s, b, p, 0, 0)),
        ],
        scratch_shapes=[pltpu.VMEM((pair, nchunk, CMP_STRIDE * HEAD_DIM), F32)],
        compiler_params=pltpu.CompilerParams(dimension_semantics=("parallel", "parallel", "parallel"),
                                             vmem_limit_bytes=VMEM_LIMIT),
        name="compress",
    )(kvc, w1, pos, w2p, w2T)


def _attn_kernel(q_ref, qr_ref, g_ref, sz_ref, kc_ref, vcT_ref, ks_ref, kw_ref, vsT_ref, vwT_ref, y_ref,
                 p_sc, sc_sc, rank_sc, oc_sc, qa_sc, c_sc, w_sc, s_sc, pb_sc, smax_sc, st_sc, acc_sc, wmax_sc,
                 wp_sc, ow_sc, *, tq, n_cmp):
    qi = pl.program_id(2)
    t0 = qi * tq
    neg_inf = -jnp.inf
    tvec = t0 + lax.broadcasted_iota(jnp.int32, (1, tq), 1)

    kc = kc_ref[0, 0, :, 0:HEAD_DIM]
    cmp_end = lax.broadcasted_iota(jnp.int32, (n_cmp, tq), 0) * CMP_STRIDE + (CMP_BLOCK - 1)
    cmask = cmp_end <= tvec
    heads = range(HEADS_PER_GROUP)

    def fold8(x, op, ways=4):
        parts = [None] * ways
        for idx, r in enumerate(range(0, x.shape[0], SUBLANES)):
            slab = x[r:r + SUBLANES, :]
            parts[idx % ways] = slab if parts[idx % ways] is None else op(parts[idx % ways], slab)
        return functools.reduce(op, [p for p in parts if p is not None])

    def key_chunk(ref, c):
        return ref[0, pl.ds(pl.multiple_of(c * KEY_CHUNK, KEY_CHUNK), KEY_CHUNK), :]

    n_sb = n_cmp // 4
    jrow = lax.broadcasted_iota(jnp.int32, (n_sb, tq), 0)
    cur = tvec // SLC_BLOCK
    valid = jrow <= cur

    QA_SEL, QA_DEAD, QA_WIN = 0, 1, 2
    pad_blocks = lambda b: b if n_sb == HEAD_DIM else jnp.concatenate(
        [b, jnp.zeros((HEAD_DIM - n_sb, tq), BF16)], axis=0)

    def set_operand(idx, bias):
        for hh in heads:
            r0 = hh * HEAD_DIM
            qa_sc[idx, hh] = jnp.concatenate([qr_ref[0, r0:r0 + HEAD_DIM, :], pad_blocks(bias)], axis=0)

    win_blocks = WINDOW // SLC_BLOCK
    set_operand(QA_WIN, jnp.where((jrow >= cur - win_blocks) & valid, 0.0, MASK_BIAS).astype(BF16))
    set_operand(QA_DEAD, jnp.full((n_sb, tq), MASK_BIAS, BF16))

    blk_row = lax.broadcasted_iota(jnp.int32, (SLC_BLOCK, LANES), 0)
    blk_lane = lax.broadcasted_iota(jnp.int32, (SLC_BLOCK, LANES), 1)
    edge = [blk_row - blk_lane, blk_row - (blk_lane - SLC_BLOCK)]
    off_diag = [blk_lane >= SLC_BLOCK, blk_lane < SLC_BLOCK]

    def mask_diagonal_blocks(s, first_block, keep):
        width = s.shape[1]
        blocks = [s[r:r + SLC_BLOCK, :] for r in range(0, s.shape[0], SLC_BLOCK)]
        for j in range(width // SLC_BLOCK):
            cols = [blocks[first_block + j][:, h * LANES:(h + 1) * LANES] for h in range(width // LANES)]
            cols[j // 2] = jnp.where(off_diag[j % 2] | keep(edge[j % 2]), cols[j // 2], neg_inf)
            blocks[first_block + j] = jnp.concatenate(cols, axis=1)
        return jnp.concatenate(blocks, axis=0)

    n_win = KEY_CHUNK + HALF
    prev_operand = jnp.where(qi >= 1, QA_WIN, QA_DEAD)
    prev_base = jnp.maximum(qi - 1, 0) * KEY_CHUNK

    def window_pieces(half):
        first = half * HALF
        n_old = KEY_CHUNK - first
        old = (pl.multiple_of(prev_base + first, HALF), n_old)
        new = (pl.multiple_of(qi * KEY_CHUNK, HALF), HALF + first)
        return old, new

    for hh in heads:
        r0 = hh * HEAD_DIM
        c_sc[hh] = jnp.where(cmask, _dot(kc, q_ref[0, r0:r0 + HEAD_DIM, :]), neg_inf)
        for half in range(tq // HALF):
            (old_start, n_old), (new_start, n_new) = window_pieces(half)
            lanes = slice(half * HALF, (half + 1) * HALF)
            s_old = _dot(kw_ref[0, pl.ds(old_start, n_old), :], qa_sc[prev_operand, hh, :, lanes])
            s_new = _dot(kw_ref[0, pl.ds(new_start, n_new), :], qa_sc[QA_WIN, hh, :, lanes])
            s_old = mask_diagonal_blocks(s_old, 0, lambda e: e > 0)
            s_new = mask_diagonal_blocks(s_new, half * (HALF // SLC_BLOCK), lambda e: e <= 0)
            w_sc[hh, half, 0:n_old, :] = s_old
            w_sc[hh, half, n_old:n_win, :] = s_new
            wmax_sc[hh, half] = jnp.maximum(fold8(s_old, jnp.maximum), fold8(s_new, jnp.maximum))
    probs = []
    for hh in heads:
        s = c_sc[hh]
        m = jnp.max(fold8(s, jnp.maximum), axis=0, keepdims=True)
        m = jnp.where(m == neg_inf, 0.0, m)
        e = jnp.exp2(s - m)
        den = jnp.sum(fold8(e, jnp.add), axis=0, keepdims=True)
        probs.append(e * (1.0 / jnp.maximum(den, 1e-30)))
    p_grp = functools.reduce(jnp.add, probs)
    for hh in heads:
        r0 = hh * HEAD_DIM
        oc_sc[r0:r0 + HEAD_DIM, :] = _dot(vcT_ref[0, 0], probs[hh].astype(BF16))

    ratio = SLC_BLOCK // CMP_STRIDE
    imp_cols = []
    for c in range(tq // LANES):
        p_sc[c, 0:SUBLANES, :] = jnp.zeros((SUBLANES, LANES), F32)
        p_sc[c, SUBLANES:SUBLANES + n_cmp, :] = p_grp[:, c * LANES:(c + 1) * LANES]
        tap = lambda o: p_sc[c, pl.ds(SUBLANES + o, n_sb, stride=ratio), :]
        imp_cols.append(tap(-1) + 2.0 * (tap(0) + tap(1) + tap(2)) + tap(3))
    imp = jnp.concatenate(imp_cols, axis=1)

    forced = (jrow == 0) | (valid & (jrow > cur - N_LOCAL))
    score = jnp.where(valid, jnp.where(forced, FORCE_SCORE, imp), neg_inf)
    sc_sc[...] = score

    n_valid = (t0 + tq) // SLC_BLOCK
    n_slabs = n_sb // SUBLANES
    slab_row = lax.broadcasted_iota(jnp.int32, (SUBLANES, tq), 0)
    rank_sc[...] = jnp.zeros((n_sb, tq), jnp.int32)
    for first in range(0, n_sb, RANK_SECTION):
        @pl.when(n_valid > max(N_SELECT, first))
        def _():
            slabs = [sc_sc[k * SUBLANES:(k + 1) * SUBLANES, :] for k in range(n_slabs)]
            counts = [rank_sc[k * SUBLANES:(k + 1) * SUBLANES, :] for k in range(n_slabs)]
            for jp in range(first, first + RANK_SECTION):
                sb = sc_sc[jp:jp + 1, :]
                for k in range(n_slabs):
                    if k * SUBLANES > jp:
                        before = sb >= slabs[k]
                    elif k * SUBLANES + SUBLANES - 1 < jp:
                        before = sb > slabs[k]
                    else:
                        before = (sb > slabs[k]) | ((sb == slabs[k]) & (slab_row > jp % SUBLANES))
                    counts[k] = counts[k] + jnp.where(before, 1, 0)
            for k in range(n_slabs):
                rank_sc[k * SUBLANES:(k + 1) * SUBLANES, :] = counts[k]
    set_operand(QA_SEL, jnp.where((rank_sc[...] < N_SELECT) & valid, 0.0, MASK_BIAS).astype(BF16))

    group = KEY_CHUNK
    n_double = qi // 2

    def values_of(ref, c):
        per = KEY_CHUNK // PROJ_TILE
        return jnp.concatenate([ref[0, per * c + v] for v in range(per)], axis=1)

    ST_MAX, ST_SUM, ST_RESCALE = 0, 1, 2
    rows8 = lambda x: jnp.broadcast_to(x, (SUBLANES, tq))
    never = 1 << 20

    def scores_to(slot, pos, hh, may_be_last):
        s = _dot(key_chunk(ks_ref, pos), qa_sc[QA_SEL, hh])
        if may_be_last:
            causal_slack = jnp.where(pos == qi, 0, never)
            s = mask_diagonal_blocks(s, 0, lambda e: e <= causal_slack)
        s_sc[slot, hh] = s
        smax_sc[slot, hh] = fold8(s, jnp.maximum)

    def accumulate(pos, slot, hh):
        acc_sc[hh] = (acc_sc[hh] * st_sc[ST_RESCALE + slot, hh, 0:1, :]
                      + _dot(values_of(vsT_ref, pos), pb_sc[slot, hh]))

    def softmax_group(slot, hh):
        m_old = st_sc[ST_MAX, hh, 0:1, :]
        m_new = jnp.maximum(m_old, jnp.max(smax_sc[slot, hh], axis=0, keepdims=True))
        a = jnp.exp2(m_old - m_new)
        psum = None
        for r in range(0, group, SOFTMAX_ROWS):
            s = s_sc[slot, hh, r:r + SOFTMAX_ROWS, :]
            p = jnp.exp2(s - m_new)
            pb_sc[slot, hh, r:r + SOFTMAX_ROWS, :] = p.astype(BF16)
            f = fold8(p, jnp.add, ways=2)
            psum = f if psum is None else psum + f
        st_sc[ST_SUM, hh] = a * st_sc[ST_SUM, hh] + psum
        st_sc[ST_MAX, hh] = rows8(m_new)
        st_sc[ST_RESCALE + slot, hh] = rows8(a)

    def stage(slot, pos, next_may_be_last):
        for hh in heads:
            scores_to(1 - slot, pos + 1, hh, next_may_be_last)
            accumulate(jnp.maximum(pos - 1, 0), 1 - slot, hh)
            softmax_group(slot, hh)

    def drain(slot):
        for hh in heads:
            accumulate(jnp.maximum(qi - 1, 0), 1 - slot, hh)
            softmax_group(slot, hh)
            accumulate(qi, slot, hh)

    def double_trip(d, carry):
        stage(0, 2 * d, next_may_be_last=False)
        stage(1, 2 * d + 1, next_may_be_last=True)
        return carry

    def window_values(half):
        pieces = []
        for start, rows in window_pieces(half):
            pieces += [vwT_ref[0, start // PROJ_TILE + v] for v in range(rows // PROJ_TILE)]
        return jnp.concatenate(pieces, axis=1)

    for hh in heads:
        acc_sc[hh] = jnp.zeros((HEAD_DIM, tq), F32)
        pb_sc[1, hh] = jnp.zeros((group, tq), BF16)
        st_sc[ST_MAX, hh] = jnp.full((SUBLANES, tq), neg_inf, F32)
        st_sc[ST_SUM, hh] = jnp.zeros((SUBLANES, tq), F32)
        st_sc[ST_RESCALE + 1, hh] = jnp.ones((SUBLANES, tq), F32)
        scores_to(0, 0, hh, may_be_last=True)
        for half in range(tq // HALF):
            m = jnp.max(wmax_sc[hh, half], axis=0, keepdims=True)
            for r in range(0, n_win, 2 * SOFTMAX_ROWS):
                wp_sc[hh, half, r:r + 2 * SOFTMAX_ROWS, :] = jnp.exp2(
                    w_sc[hh, half, r:r + 2 * SOFTMAX_ROWS, :] - m).astype(BF16)
            pv = _dot(window_values(half), wp_sc[hh, half])
            ow_sc[hh, :, half * HALF:(half + 1) * HALF] = pv[0:HEAD_DIM, :] * (1.0 / pv[HEAD_DIM:HEAD_DIM + 1, :])

    lax.fori_loop(0, n_double, double_trip, 0)

    @pl.when(qi % 2 == 0)
    def _():
        drain(0)

    @pl.when(qi % 2 == 1)
    def _():
        stage(0, qi - 1, next_may_be_last=True)
        drain(1)

    for hh in heads:
        r0 = hh * HEAD_DIM
        o_w = ow_sc[hh]
        o_s = acc_sc[hh] * (1.0 / jnp.sum(st_sc[ST_SUM, hh], axis=0, keepdims=True))

        g0 = g_ref[0, N_BRANCH * hh:N_BRANCH * hh + 1, :]
        g1 = g_ref[0, N_BRANCH * hh + 1:N_BRANCH * hh + 2, :]
        g2 = g_ref[0, N_BRANCH * hh + 2:N_BRANCH * hh + 3, :]
        o = g0 * oc_sc[r0:r0 + HEAD_DIM, :] + g1 * o_s + g2 * o_w
        y_ref[0, r0:r0 + HEAD_DIM, :] = (o * sz_ref[0, r0:r0 + HEAD_DIM, :]).astype(BF16)


def _nsa_attn(qT, qrT, gT, szT, kcmp, vcmpT, ks, kw, vsT, vwT, *, tq=KEY_CHUNK):
    bsz, attn_d, seq = qT.shape
    n_cmp = kcmp.shape[2]
    n_win = KEY_CHUNK + HALF
    gd = HEADS_PER_GROUP * HEAD_DIM
    assert tq == KEY_CHUNK == WINDOW and HALF % PROJ_TILE == 0 and vsT.shape[3] == PROJ_TILE
    qspec = pl.BlockSpec((1, gd, tq), lambda b, g, i: (b, g, i))
    kspec = pl.BlockSpec((1, seq, LANES), lambda b, g, i: (b, 0, g))
    vspec = pl.BlockSpec((1, seq // PROJ_TILE, HEAD_DIM, PROJ_TILE), lambda b, g, i: (b, 0, g, 0))
    return pl.pallas_call(
        functools.partial(_attn_kernel, tq=tq, n_cmp=n_cmp),
        out_shape=jax.ShapeDtypeStruct((bsz, attn_d, seq), BF16),
        grid=(bsz, N_KV_GROUPS, seq // tq),
        in_specs=[
            qspec, qspec,
            pl.BlockSpec((1, GATE_ROWS, tq), lambda b, g, i: (b, g, i)),
            qspec,
            pl.BlockSpec((1, 1, n_cmp, LANES), lambda b, g, i: (b, g, 0, 0)),
            pl.BlockSpec((1, 1, HEAD_DIM, n_cmp), lambda b, g, i: (b, g, 0, 0)),
            kspec, kspec, vspec,
            pl.BlockSpec((1, seq // PROJ_TILE, VW_ROWS, PROJ_TILE), lambda b, g, i: (b, 0, g, 0)),
        ],
        out_specs=qspec,
        scratch_shapes=[
            pltpu.VMEM((tq // LANES, SUBLANES + n_cmp, LANES), F32),
            pltpu.VMEM((n_cmp // 4, tq), F32),
            pltpu.VMEM((n_cmp // 4, tq), jnp.int32),
            pltpu.VMEM((gd, tq), F32),
            pltpu.VMEM((3, HEADS_PER_GROUP, 2 * HEAD_DIM, tq), BF16),
            pltpu.VMEM((HEADS_PER_GROUP, n_cmp, tq), F32),
            pltpu.VMEM((HEADS_PER_GROUP, tq // HALF, n_win, HALF), F32),
            pltpu.VMEM((2, HEADS_PER_GROUP, KEY_CHUNK, tq), F32),
            pltpu.VMEM((2, HEADS_PER_GROUP, KEY_CHUNK, tq), BF16),
            pltpu.VMEM((2, HEADS_PER_GROUP, SUBLANES, tq), F32),
            pltpu.VMEM((4, HEADS_PER_GROUP, SUBLANES, tq), F32),
            pltpu.VMEM((HEADS_PER_GROUP, HEAD_DIM, tq), F32),
            pltpu.VMEM((HEADS_PER_GROUP, tq // HALF, SUBLANES, HALF), F32),
            pltpu.VMEM((HEADS_PER_GROUP, tq // HALF, n_win, HALF), BF16),
            pltpu.VMEM((HEADS_PER_GROUP, HEAD_DIM, tq), F32),
        ],
        compiler_params=pltpu.CompilerParams(dimension_semantics=("parallel", "parallel", "arbitrary"),
                                             vmem_limit_bytes=VMEM_LIMIT),
        name="nsa_attn",
    )(qT, qrT, gT, szT, kcmp, vcmpT, ks, kw, vsT, vwT)


def _nsa_out_kernel(y_ref, h_ref, wo_ref, fn_ref, o_ref):
    h2 = h_ref[0] + _dot_tn(y_ref[0], wo_ref[...])
    o_ref[0] = h2 * _inv_rms(h2) * fn_ref[...]


def _nsa_out(yT, h3, wo, final_norm, *, tm=OUT_TILE):
    bsz, seq, d = h3.shape
    attn_d = yT.shape[1]
    return pl.pallas_call(
        _nsa_out_kernel,
        out_shape=jax.ShapeDtypeStruct((bsz, seq, d), F32),
        grid=(bsz, seq // tm),
        in_specs=[
            pl.BlockSpec((1, attn_d, tm), lambda b, i: (b, 0, i)),
            pl.BlockSpec((1, tm, d), lambda b, i: (b, i, 0)),
            pl.BlockSpec((attn_d, d), lambda b, i: (0, 0)),
            pl.BlockSpec((1, d), lambda b, i: (0, 0)),
        ],
        out_specs=pl.BlockSpec((1, tm, d), lambda b, i: (b, i, 0)),
        compiler_params=pltpu.CompilerParams(dimension_semantics=("parallel", "parallel"),
                                             vmem_limit_bytes=VMEM_LIMIT),
        name="nsa_out",
    )(yT, h3, wo, final_norm.reshape(1, d))


def _rope_tables(seq):
    pos = jnp.arange(seq, dtype=F32)
    inv = ROPE_THETA ** (-jnp.arange(0, ROT_DIM, 2, dtype=F32) / ROT_DIM)
    ang = pos[:, None] * inv[None, :]
    cos, sin = jnp.cos(ang), jnp.sin(ang)
    z = lambda n: jnp.zeros((seq, n), F32)
    per_head = lambda parts: jnp.tile(jnp.concatenate(parts, axis=1), (1, LANES // HEAD_DIM))
    cosf = per_head([cos, cos, jnp.ones((seq, HEAD_DIM - ROT_DIM), F32)])
    sina = per_head([-sin, z(HEAD_DIM - N_FREQ)])
    sinb = per_head([z(N_FREQ), sin, z(HEAD_DIM - ROT_DIM)])
    return cos.T, sin.T, cosf, sina, sinb


def kernel(x, a_norm, a_w_in, a_conv_w, a_w_out, kv_norm, w_kv, cmp_pos_k, cmp_w1_k, cmp_w2_k,
           cmp_pos_v, cmp_w1_v, cmp_w2_v, b_norm, b_w_in, b_w_out, final_norm):
    bsz, seq, d = x.shape
    attn_d = N_HEADS * HEAD_DIM
    kv_d = N_KV_GROUPS * HEAD_DIM
    assert b_norm.shape[0] == 1, "one NSA layer reads the shared K/V side"
    assert seq % KEY_CHUNK == 0

    h = x.reshape(bsz * seq, d)
    for layer in range(a_norm.shape[0]):
        h = _conv_layer(h, a_norm[layer], a_w_in[layer].astype(BF16), a_conv_w[layer],
                        a_w_out[layer].astype(BF16), seq=seq)
    h3 = h.reshape(bsz, seq, d)

    w_in = b_w_in[0]
    n_gate = N_HEADS * N_BRANCH
    wg = w_in[:, attn_d:attn_d + n_gate].reshape(d, N_KV_GROUPS, HEADS_PER_GROUP * N_BRANCH)
    wg = jnp.pad(wg, ((0, 0), (0, 0), (0, GATE_ROWS - HEADS_PER_GROUP * N_BRANCH)))
    waT = jnp.concatenate([w_in[:, :attn_d], w_in[:, attn_d + n_gate:], wg.reshape(d, -1)], axis=1).T.astype(BF16)
    wkv = w_kv.reshape(d, 2 * N_BRANCH, N_KV_GROUPS, HEAD_DIM)
    k_c, v_c, k_s, v_s, k_w, v_w = [wkv[:, i] for i in range(2 * N_BRANCH)]
    flat = lambda w: w.reshape(d, kv_d)
    wvT = jnp.concatenate([flat(v_s), flat(v_w)], axis=1).T.astype(BF16)
    wk = jnp.concatenate([flat(k_c), flat(v_c), flat(k_s), flat(k_w)], axis=1).astype(BF16)
    cosT, sinT, cosf, sina, sinb = _rope_tables(seq)

    qT, qrT, szT, gT, kvc, ks, kw, vsT, vwT = _nsa_proj(
        h3, b_norm[0], kv_norm, waT, wvT, wk, cosT, sinT, cosf, sina, sinb)

    w1 = jnp.stack([cmp_w1_k, cmp_w1_v]).astype(BF16)
    pos = jnp.stack([cmp_pos_k.reshape(1, -1), cmp_pos_v.reshape(1, -1)])
    w2 = jnp.stack([cmp_w2_k, cmp_w2_v])
    w2p = jnp.pad(w2, ((0, 0), (0, 0), (0, LANES - HEAD_DIM))).astype(BF16)
    w2T = jnp.swapaxes(w2, 1, 2).astype(BF16)
    cmp_tm, cmp_fm = _compress(kvc, w1, pos, w2p, w2T)

    yT = _nsa_attn(qT, qrT, gT, szT, cmp_tm[0], cmp_fm[1], ks, kw, vsT, vwT)
    return _nsa_out(yT, h3, b_w_out[0].astype(BF16), final_norm)
```

```python
import functools

import jax
import jax.numpy as jnp
from jax import lax
from jax.experimental import pallas as pl
from jax.experimental.pallas import tpu as pltpu

EPS = 1e-6
CONV_WIDTH = 3
N_HEADS = 16
HEAD_DIM = 64
N_KV_GROUPS = 4
HEADS_PER_GROUP = N_HEADS // N_KV_GROUPS
N_BRANCH = 3
ROT_DIM = HEAD_DIM // 4
N_FREQ = ROT_DIM // 2
ROPE_THETA = 500000.0
CMP_BLOCK = 32
CMP_STRIDE = 16
SLC_BLOCK = 64
N_SELECT = 16
N_LOCAL = 2
WINDOW = 512
FORCE_SCORE = 1e4

LANES = 128
SUBLANES = 8
KEY_CHUNK = 512
HALF = KEY_CHUNK // 2
PROJ_TILE = 256
OUT_TILE = 1024
SOFTMAX_ROWS = 32
RANK_SECTION = 8
GATE_ROWS = 16
V_ROWS = HEAD_DIM + 16
MASK_BIAS = -1e30
LOG2_E = 1.4426950408889634
VMEM_LIMIT = 56 * 1024 * 1024

BF16 = jnp.bfloat16
F32 = jnp.float32
NT_DIMS = (((1,), (1,)), ((), ()))


def _dot(a, b):
    return jnp.dot(a, b, preferred_element_type=F32)


def _dot_nt(a, b):
    return lax.dot_general(a, b, NT_DIMS, preferred_element_type=F32)


def _dot_tn(a, b):
    return lax.dot_general(a, b, (((0,), (0,)), ((), ())), preferred_element_type=F32)


def _sigmoid(x):
    return 1.0 / (1.0 + jnp.exp(-x))


def _inv_rms(x):
    return lax.rsqrt(jnp.mean(x * x, axis=-1, keepdims=True) + EPS)


def _conv_layer_kernel(x_ref, g_ref, win_ref, cw_ref, wout_ref, o_ref, vbuf_ref, *,
                       tm, tiles_per_seq, conv_d, cchunk):
    @pl.when(pl.program_id(0) % tiles_per_seq == 0)
    def _():
        vbuf_ref[0:SUBLANES, :] = jnp.zeros((SUBLANES, conv_d), F32)

    x = x_ref[...]
    hn = (x * _inv_rms(x) * g_ref[...]).astype(BF16)
    acc = jnp.zeros(x.shape, F32)
    for cc in range(conv_d // cchunk):
        cs = cc * cchunk
        b = _dot(hn, win_ref[:, cs:cs + cchunk])
        c = _dot(hn, win_ref[:, conv_d + cs:conv_d + cs + cchunk])
        u = _dot(hn, win_ref[:, 2 * conv_d + cs:2 * conv_d + cs + cchunk])
        z = _dot(hn, win_ref[:, 3 * conv_d + cs:3 * conv_d + cs + cchunk])
        v = c * u
        vbuf_ref[SUBLANES:SUBLANES + tm, cs:cs + cchunk] = v
        v1 = vbuf_ref[SUBLANES - 1:SUBLANES - 1 + tm, cs:cs + cchunk]
        v2 = vbuf_ref[SUBLANES - 2:SUBLANES - 2 + tm, cs:cs + cchunk]
        conv = (cw_ref[0:1, cs:cs + cchunk] * v2 + cw_ref[1:2, cs:cs + cchunk] * v1
                + cw_ref[2:3, cs:cs + cchunk] * v)
        vbuf_ref[0:SUBLANES, cs:cs + cchunk] = v[tm - SUBLANES:tm, :]
        y = b * conv * (z * _sigmoid(z))
        acc = acc + _dot(y.astype(BF16), wout_ref[cs:cs + cchunk, :])
    o_ref[...] = x + acc


def _conv_layer(h, norm_g, w_in, conv_w, w_out, *, seq, tm=512, cchunk=1024):
    t, d = h.shape
    conv_d = conv_w.shape[1]
    tm = min(tm, seq)
    const = lambda shape: pl.BlockSpec(shape, lambda i: (0,) * len(shape), pipeline_mode=pl.Buffered(1))
    return pl.pallas_call(
        functools.partial(_conv_layer_kernel, tm=tm, tiles_per_seq=seq // tm, conv_d=conv_d, cchunk=cchunk),
        out_shape=jax.ShapeDtypeStruct((t, d), F32),
        grid=(t // tm,),
        in_specs=[
            pl.BlockSpec((tm, d), lambda i: (i, 0)),
            const((1, d)),
            const((d, 4 * conv_d)),
            const((CONV_WIDTH, conv_d)),
            const((conv_d, d)),
        ],
        out_specs=pl.BlockSpec((tm, d), lambda i: (i, 0)),
        scratch_shapes=[pltpu.VMEM((SUBLANES + tm, conv_d), F32)],
        compiler_params=pltpu.CompilerParams(dimension_semantics=("arbitrary",), vmem_limit_bytes=VMEM_LIMIT),
        name="conv_layer",
    )(h, norm_g.reshape(1, d), w_in, conv_w, w_out)


def _nsa_proj_kernel(h_ref, bn_ref, kn_ref, waT_ref, wvT_ref, wk_ref, cosT_ref, sinT_ref,
                     cosf_ref, sina_ref, sinb_ref,
                     qT_ref, qrT_ref, szT_ref, gT_ref, kvc_ref, ks_ref, kw_ref, vsT_ref, vwT_ref, *, tm, d):
    attn_d = N_HEADS * HEAD_DIM
    kv_d = N_KV_GROUPS * HEAD_DIM
    h = h_ref[0]
    hr = h * _inv_rms(h)
    hq = (hr * bn_ref[...]).astype(BF16)
    hk = (hr * kn_ref[...]).astype(BF16)

    qT = _dot_nt(waT_ref[0:attn_d, :], hq) * (HEAD_DIM ** -0.5 * LOG2_E)
    cosT = cosT_ref[...]
    sinT = sinT_ref[...]
    for hd in range(N_HEADS):
        r0 = hd * HEAD_DIM
        blk = qT[r0:r0 + HEAD_DIM, :]
        x1 = blk[0:N_FREQ, :]
        x2 = blk[N_FREQ:ROT_DIM, :]
        rot = jnp.concatenate([x1 * cosT - x2 * sinT, x2 * cosT + x1 * sinT, blk[ROT_DIM:, :]], axis=0)
        qT_ref[0, r0:r0 + HEAD_DIM, :] = blk.astype(BF16)
        qrT_ref[0, r0:r0 + HEAD_DIM, :] = rot.astype(BF16)

    zT = _dot_nt(waT_ref[attn_d:2 * attn_d, :], hq)
    szT_ref[0] = zT * _sigmoid(zT)
    gT_ref[0] = _sigmoid(_dot_nt(waT_ref[2 * attn_d:2 * attn_d + N_KV_GROUPS * GATE_ROWS, :], hq))

    vT = _dot_nt(wvT_ref[...], hk)
    ones_rows = jnp.ones((V_ROWS - HEAD_DIM, tm), BF16)
    for i, ref in enumerate((vsT_ref, vwT_ref)):
        for g in range(N_KV_GROUPS):
            src = i * kv_d + g * HEAD_DIM
            ref[0, 0, g * V_ROWS:g * V_ROWS + HEAD_DIM, :] = vT[src:src + HEAD_DIM, :].astype(BF16)
            ref[0, 0, g * V_ROWS + HEAD_DIM:(g + 1) * V_ROWS, :] = ones_rows

    kk = _dot(hk, wk_ref[...])
    kvc_ref[0] = kk[:, 0:2 * kv_d]
    cosf = cosf_ref[...]
    sina = sina_ref[...]
    sinb = sinb_ref[...]
    row = lax.broadcasted_iota(jnp.int32, (tm, LANES), 0)
    lane = lax.broadcasted_iota(jnp.int32, (tm, LANES), 1)
    tok_blk = (pl.program_id(1) * tm + row) // SLC_BLOCK
    onehot = jnp.where((lane >= HEAD_DIM) & (lane - HEAD_DIM == tok_blk), 1.0, 0.0)
    lower = lane < HEAD_DIM
    for i, ref in enumerate((ks_ref, kw_ref)):
        for pair in range(N_KV_GROUPS // 2):
            c0 = (2 + i) * kv_d + pair * LANES
            xb = kk[:, c0:c0 + LANES]
            rot = (xb * cosf + pltpu.roll(xb, LANES - N_FREQ, axis=1) * sina
                   + pltpu.roll(xb, N_FREQ, axis=1) * sinb)
            for half, keys in enumerate((rot, pltpu.roll(rot, HEAD_DIM, axis=1))):
                g = 2 * pair + half
                ref[0, :, g * LANES:(g + 1) * LANES] = (jnp.where(lower, keys, 0.0) + onehot).astype(BF16)


def _nsa_proj(h3, b_norm, kv_norm, waT, wvT, wk, cosT, sinT, cosf, sina, sinb, *, tm=PROJ_TILE):
    bsz, seq, d = h3.shape
    attn_d = N_HEADS * HEAD_DIM
    kv_d = N_KV_GROUPS * HEAD_DIM
    kpad = N_KV_GROUPS * LANES
    nt = seq // tm
    const = lambda shape: pl.BlockSpec(shape, lambda b, i: (0,) * len(shape))
    fm = lambda rows: pl.BlockSpec((1, rows, tm), lambda b, i: (b, 0, i))
    tmj = lambda cols: pl.BlockSpec((1, tm, cols), lambda b, i: (b, i, 0))
    out_shape = [
        jax.ShapeDtypeStruct((bsz, attn_d, seq), BF16),
        jax.ShapeDtypeStruct((bsz, attn_d, seq), BF16),
        jax.ShapeDtypeStruct((bsz, attn_d, seq), F32),
        jax.ShapeDtypeStruct((bsz, N_KV_GROUPS * GATE_ROWS, seq), F32),
        jax.ShapeDtypeStruct((bsz, seq, 2 * kv_d), F32),
        jax.ShapeDtypeStruct((bsz, seq, kpad), BF16),
        jax.ShapeDtypeStruct((bsz, seq, kpad), BF16),
        jax.ShapeDtypeStruct((bsz, nt, N_KV_GROUPS * V_ROWS, tm), BF16),
        jax.ShapeDtypeStruct((bsz, nt, N_KV_GROUPS * V_ROWS, tm), BF16),
    ]
    out_specs = [
        fm(attn_d), fm(attn_d), fm(attn_d), fm(N_KV_GROUPS * GATE_ROWS),
        tmj(2 * kv_d), tmj(kpad), tmj(kpad),
        pl.BlockSpec((1, 1, N_KV_GROUPS * V_ROWS, tm), lambda b, i: (b, i, 0, 0)),
        pl.BlockSpec((1, 1, N_KV_GROUPS * V_ROWS, tm), lambda b, i: (b, i, 0, 0)),
    ]
    return pl.pallas_call(
        functools.partial(_nsa_proj_kernel, tm=tm, d=d),
        out_shape=out_shape,
        grid=(bsz, nt),
        in_specs=[
            tmj(d), const((1, d)), const((1, d)),
            const(waT.shape), const(wvT.shape), const(wk.shape),
            pl.BlockSpec((N_FREQ, tm), lambda b, i: (0, i)),
            pl.BlockSpec((N_FREQ, tm), lambda b, i: (0, i)),
            pl.BlockSpec((tm, LANES), lambda b, i: (i, 0)),
            pl.BlockSpec((tm, LANES), lambda b, i: (i, 0)),
            pl.BlockSpec((tm, LANES), lambda b, i: (i, 0)),
        ],
        out_specs=out_specs,
        compiler_params=pltpu.CompilerParams(dimension_semantics=("parallel", "parallel"),
                                             vmem_limit_bytes=VMEM_LIMIT),
        name="nsa_proj",
    )(h3, b_norm.reshape(1, d), kv_norm.reshape(1, d), waT, wvT, wk, cosT, sinT, cosf, sina, sinb)


def _gelu_tanh(x):
    return x * (0.5 * (1.0 + jnp.tanh(0.7978845608028654 * (x + 0.044715 * (x * x * x)))))


def _compress_kernel(x_ref, w1_ref, pos_ref, w2_ref, w2T_ref, kc_ref, cT_ref, rows_sc, *, nchunk):
    half = CMP_STRIDE * HEAD_DIM
    for l in range(CMP_STRIDE):
        xl = x_ref[0, pl.ds(l, nchunk, stride=CMP_STRIDE), :]
        for p in range(LANES // HEAD_DIM):
            rows_sc[p, :, l * HEAD_DIM:(l + 1) * HEAD_DIM] = xl[:, p * HEAD_DIM:(p + 1) * HEAD_DIM]
    rows_ok = lax.broadcasted_iota(jnp.int32, (nchunk, LANES), 0) < nchunk - 1
    cols_ok = lax.broadcasted_iota(jnp.int32, (HEAD_DIM, nchunk), 1) < nchunk - 1
    for p in range(LANES // HEAD_DIM):
        x = rows_sc[p]
        u = _dot((x + pos_ref[0, :, 0:half]).astype(BF16), w1_ref[0, 0:half, :])
        v = _dot((x + pos_ref[0, :, half:2 * half]).astype(BF16), w1_ref[0, half:2 * half, :])
        act = _gelu_tanh(u + pltpu.roll(v, nchunk - 1, axis=0)).astype(BF16)
        kc_ref[0, 0, p] = jnp.where(rows_ok, _dot(act, w2_ref[0]), 0.0).astype(BF16)
        cT_ref[0, 0, p] = jnp.where(cols_ok, _dot_nt(w2T_ref[0], act), 0.0).astype(BF16)


def _compress(kvc, w1, pos, w2p, w2T):
    bsz, seq, _ = kvc.shape
    nchunk = seq // CMP_STRIDE
    hidden = w2T.shape[2]
    width = CMP_BLOCK * HEAD_DIM
    pair = LANES // HEAD_DIM
    return pl.pallas_call(
        functools.partial(_compress_kernel, nchunk=nchunk),
        out_shape=[
            jax.ShapeDtypeStruct((2, bsz, N_KV_GROUPS, nchunk, LANES), BF16),
            jax.ShapeDtypeStruct((2, bsz, N_KV_GROUPS, HEAD_DIM, nchunk), BF16),
        ],
        grid=(2, bsz, N_KV_GROUPS // pair),
        in_specs=[
            pl.BlockSpec((1, seq, LANES), lambda s, b, p: (b, 0, (N_KV_GROUPS // pair) * s + p)),
            pl.BlockSpec((1, width, hidden), lambda s, b, p: (s, 0, 0)),
            pl.BlockSpec((1, 1, width), lambda s, b, p: (s, 0, 0)),
            pl.BlockSpec((1, hidden, LANES), lambda s, b, p: (s, 0, 0)),
            pl.BlockSpec((1, HEAD_DIM, hidden), lambda s, b, p: (s, 0, 0)),
        ],
        out_specs=[
            pl.BlockSpec((1, 1, pair, nchunk, LANES), lambda s, b, p: (s, b, p, 0, 0)),
            pl.BlockSpec((1, 1, pair, HEAD_DIM, nchunk), lambda s, b, p: (s, b, p, 0, 0)),
        ],
        scratch_shapes=[pltpu.VMEM((pair, nchunk, CMP_STRIDE * HEAD_DIM), F32)],
        compiler_params=pltpu.CompilerParams(dimension_semantics=("parallel", "parallel", "parallel"),
                                             vmem_limit_bytes=VMEM_LIMIT),
        name="compress",
    )(kvc, w1, pos, w2p, w2T)


def _attn_kernel(q_ref, qr_ref, g_ref, sz_ref, kc_ref, vcT_ref, ks_ref, kw_ref, vsT_ref, vwT_ref, y_ref,
                 p_sc, sc_sc, rank_sc, oc_sc, qa_sc, c_sc, w_sc, s_sc, pb_sc, smax_sc, st_sc, acc_sc, wmax_sc,
                 wp_sc, ow_sc, *, tq, n_cmp):
    qi = pl.program_id(2)
    t0 = qi * tq
    neg_inf = -jnp.inf
    tvec = t0 + lax.broadcasted_iota(jnp.int32, (1, tq), 1)

    kc = kc_ref[0, 0, :, 0:HEAD_DIM]
    cmp_end = lax.broadcasted_iota(jnp.int32, (n_cmp, tq), 0) * CMP_STRIDE + (CMP_BLOCK - 1)
    cmask = cmp_end <= tvec
    heads = range(HEADS_PER_GROUP)

    def fold8(x, op, ways=4):
        parts = [None] * ways
        for idx, r in enumerate(range(0, x.shape[0], SUBLANES)):
            slab = x[r:r + SUBLANES, :]
            parts[idx % ways] = slab if parts[idx % ways] is None else op(parts[idx % ways], slab)
        return functools.reduce(op, [p for p in parts if p is not None])

    def key_chunk(ref, c):
        return ref[0, pl.ds(pl.multiple_of(c * KEY_CHUNK, KEY_CHUNK), KEY_CHUNK), :]

    n_sb = n_cmp // 4
    jrow = lax.broadcasted_iota(jnp.int32, (n_sb, tq), 0)
    cur = tvec // SLC_BLOCK
    valid = jrow <= cur

    QA_SEL, QA_DEAD, QA_WIN = 0, 1, 2
    pad_blocks = lambda b: b if n_sb == HEAD_DIM else jnp.concatenate(
        [b, jnp.zeros((HEAD_DIM - n_sb, tq), BF16)], axis=0)

    def set_operand(idx, bias):
        for hh in heads:
            r0 = hh * HEAD_DIM
            qa_sc[idx, hh] = jnp.concatenate([qr_ref[0, r0:r0 + HEAD_DIM, :], pad_blocks(bias)], axis=0)

    win_blocks = WINDOW // SLC_BLOCK
    set_operand(QA_WIN, jnp.where((jrow >= cur - win_blocks) & valid, 0.0, MASK_BIAS).astype(BF16))
    set_operand(QA_DEAD, jnp.full((n_sb, tq), MASK_BIAS, BF16))

    blk_row = lax.broadcasted_iota(jnp.int32, (SLC_BLOCK, LANES), 0)
    blk_lane = lax.broadcasted_iota(jnp.int32, (SLC_BLOCK, LANES), 1)
    edge = [blk_row - blk_lane, blk_row - (blk_lane - SLC_BLOCK)]
    off_diag = [blk_lane >= SLC_BLOCK, blk_lane < SLC_BLOCK]

    def mask_diagonal_blocks(s, first_block, keep):
        width = s.shape[1]
        blocks = [s[r:r + SLC_BLOCK, :] for r in range(0, s.shape[0], SLC_BLOCK)]
        for j in range(width // SLC_BLOCK):
            cols = [blocks[first_block + j][:, h * LANES:(h + 1) * LANES] for h in range(width // LANES)]
            cols[j // 2] = jnp.where(off_diag[j % 2] | keep(edge[j % 2]), cols[j // 2], neg_inf)
            blocks[first_block + j] = jnp.concatenate(cols, axis=1)
        return jnp.concatenate(blocks, axis=0)

    n_win = KEY_CHUNK + HALF
    prev_operand = jnp.where(qi >= 1, QA_WIN, QA_DEAD)
    prev_base = jnp.maximum(qi - 1, 0) * KEY_CHUNK

    def window_pieces(half):
        first = half * HALF
        n_old = KEY_CHUNK - first
        old = (pl.multiple_of(prev_base + first, HALF), n_old)
        new = (pl.multiple_of(qi * KEY_CHUNK, HALF), HALF + first)
        return old, new

    for hh in heads:
        r0 = hh * HEAD_DIM
        c_sc[hh] = jnp.where(cmask, _dot(kc, q_ref[0, r0:r0 + HEAD_DIM, :]), neg_inf)
        for half in range(tq // HALF):
            (old_start, n_old), (new_start, n_new) = window_pieces(half)
            lanes = slice(half * HALF, (half + 1) * HALF)
            s_old = _dot(kw_ref[0, pl.ds(old_start, n_old), :], qa_sc[prev_operand, hh, :, lanes])
            s_new = _dot(kw_ref[0, pl.ds(new_start, n_new), :], qa_sc[QA_WIN, hh, :, lanes])
            s_old = mask_diagonal_blocks(s_old, 0, lambda e: e > 0)
            s_new = mask_diagonal_blocks(s_new, half * (HALF // SLC_BLOCK), lambda e: e <= 0)
            w_sc[hh, half, 0:n_old, :] = s_old
            w_sc[hh, half, n_old:n_win, :] = s_new
            wmax_sc[hh, half] = jnp.maximum(fold8(s_old, jnp.maximum), fold8(s_new, jnp.maximum))
    probs = []
    for hh in heads:
        s = c_sc[hh]
        m = jnp.max(fold8(s, jnp.maximum), axis=0, keepdims=True)
        m = jnp.where(m == neg_inf, 0.0, m)
        e = jnp.exp2(s - m)
        den = jnp.sum(fold8(e, jnp.add), axis=0, keepdims=True)
        probs.append(e * (1.0 / jnp.maximum(den, 1e-30)))
    p_grp = functools.reduce(jnp.add, probs)
    for hh in heads:
        r0 = hh * HEAD_DIM
        oc_sc[r0:r0 + HEAD_DIM, :] = _dot(vcT_ref[0, 0], probs[hh].astype(BF16))

    ratio = SLC_BLOCK // CMP_STRIDE
    imp_cols = []
    for c in range(tq // LANES):
        p_sc[c, 0:SUBLANES, :] = jnp.zeros((SUBLANES, LANES), F32)
        p_sc[c, SUBLANES:SUBLANES + n_cmp, :] = p_grp[:, c * LANES:(c + 1) * LANES]
        tap = lambda o: p_sc[c, pl.ds(SUBLANES + o, n_sb, stride=ratio), :]
        imp_cols.append(tap(-1) + 2.0 * (tap(0) + tap(1) + tap(2)) + tap(3))
    imp = jnp.concatenate(imp_cols, axis=1)

    forced = (jrow == 0) | (valid & (jrow > cur - N_LOCAL))
    score = jnp.where(valid, jnp.where(forced, FORCE_SCORE, imp), neg_inf)
    sc_sc[...] = score

    n_valid = (t0 + tq) // SLC_BLOCK
    n_slabs = n_sb // SUBLANES
    slab_row = lax.broadcasted_iota(jnp.int32, (SUBLANES, tq), 0)
    rank_sc[...] = jnp.zeros((n_sb, tq), jnp.int32)
    for first in range(0, n_sb, RANK_SECTION):
        @pl.when(n_valid > max(N_SELECT, first))
        def _():
            slabs = [sc_sc[k * SUBLANES:(k + 1) * SUBLANES, :] for k in range(n_slabs)]
            counts = [rank_sc[k * SUBLANES:(k + 1) * SUBLANES, :] for k in range(n_slabs)]
            for jp in range(first, first + RANK_SECTION):
                sb = sc_sc[jp:jp + 1, :]
                for k in range(n_slabs):
                    if k * SUBLANES > jp:
                        before = sb >= slabs[k]
                    elif k * SUBLANES + SUBLANES - 1 < jp:
                        before = sb > slabs[k]
                    else:
                        before = (sb > slabs[k]) | ((sb == slabs[k]) & (slab_row > jp % SUBLANES))
                    counts[k] = counts[k] + jnp.where(before, 1, 0)
            for k in range(n_slabs):
                rank_sc[k * SUBLANES:(k + 1) * SUBLANES, :] = counts[k]
    set_operand(QA_SEL, jnp.where((rank_sc[...] < N_SELECT) & valid, 0.0, MASK_BIAS).astype(BF16))

    group = KEY_CHUNK
    n_double = qi // 2

    def values_of(ref, c):
        per = KEY_CHUNK // PROJ_TILE
        return jnp.concatenate([ref[0, per * c + v] for v in range(per)], axis=1)

    ST_MAX, ST_SUM, ST_RESCALE, ST_TAIL = 0, 1, 2, 4
    rows8 = lambda x: jnp.broadcast_to(x, (SUBLANES, tq))
    never = 1 << 20

    def scores_to(slot, pos, hh, may_be_last):
        s = _dot(key_chunk(ks_ref, pos), qa_sc[QA_SEL, hh])
        if may_be_last:
            causal_slack = jnp.where(pos == qi, 0, never)
            s = mask_diagonal_blocks(s, 0, lambda e: e <= causal_slack)
        s_sc[slot, hh] = s
        smax_sc[slot, hh] = fold8(s, jnp.maximum)

    def accumulate(pos, slot, hh, with_sum=False):
        vT = values_of(vsT_ref, pos)
        pv = _dot(vT if with_sum else vT[0:HEAD_DIM, :], pb_sc[slot, hh])
        acc_sc[hh] = acc_sc[hh] * st_sc[ST_RESCALE + slot, hh, 0:1, :] + pv[0:HEAD_DIM, :]
        if with_sum:
            st_sc[ST_TAIL, hh] = pv[HEAD_DIM:HEAD_DIM + SUBLANES, :]

    def softmax_group(slot, hh, with_sum=True):
        m_old = st_sc[ST_MAX, hh, 0:1, :]
        m_new = jnp.maximum(m_old, jnp.max(smax_sc[slot, hh], axis=0, keepdims=True))
        a = jnp.exp2(m_old - m_new)
        psum = None
        for r in range(0, group, SOFTMAX_ROWS):
            s = s_sc[slot, hh, r:r + SOFTMAX_ROWS, :]
            p = jnp.exp2(s - m_new)
            pb_sc[slot, hh, r:r + SOFTMAX_ROWS, :] = p.astype(BF16)
            if with_sum:
                f = fold8(p, jnp.add, ways=2)
                psum = f if psum is None else psum + f
        st_sc[ST_SUM, hh] = a * st_sc[ST_SUM, hh] + psum if with_sum else a * st_sc[ST_SUM, hh]
        st_sc[ST_MAX, hh] = rows8(m_new)
        st_sc[ST_RESCALE + slot, hh] = rows8(a)

    def stage(slot, pos, next_may_be_last):
        for hh in heads:
            scores_to(1 - slot, pos + 1, hh, next_may_be_last)
            accumulate(jnp.maximum(pos - 1, 0), 1 - slot, hh)
            softmax_group(slot, hh)

    def drain(slot):
        for hh in heads:
            accumulate(jnp.maximum(qi - 1, 0), 1 - slot, hh)
            softmax_group(slot, hh, with_sum=False)
            accumulate(qi, slot, hh, with_sum=True)

    def double_trip(d, carry):
        stage(0, 2 * d, next_may_be_last=False)
        stage(1, 2 * d + 1, next_may_be_last=True)
        return carry

    def window_values(half):
        pieces = []
        for start, rows in window_pieces(half):
            pieces += [vwT_ref[0, start // PROJ_TILE + v] for v in range(rows // PROJ_TILE)]
        return jnp.concatenate(pieces, axis=1)

    for hh in heads:
        acc_sc[hh] = jnp.zeros((HEAD_DIM, tq), F32)
        pb_sc[1, hh] = jnp.zeros((group, tq), BF16)
        st_sc[ST_MAX, hh] = jnp.full((SUBLANES, tq), neg_inf, F32)
        st_sc[ST_SUM, hh] = jnp.zeros((SUBLANES, tq), F32)
        st_sc[ST_RESCALE + 1, hh] = jnp.ones((SUBLANES, tq), F32)
        scores_to(0, 0, hh, may_be_last=True)
        for half in range(tq // HALF):
            m = jnp.max(wmax_sc[hh, half], axis=0, keepdims=True)
            for r in range(0, n_win, 2 * SOFTMAX_ROWS):
                wp_sc[hh, half, r:r + 2 * SOFTMAX_ROWS, :] = jnp.exp2(
                    w_sc[hh, half, r:r + 2 * SOFTMAX_ROWS, :] - m).astype(BF16)
            pv = _dot(window_values(half), wp_sc[hh, half])
            ow_sc[hh, :, half * HALF:(half + 1) * HALF] = pv[0:HEAD_DIM, :] * (1.0 / pv[HEAD_DIM:HEAD_DIM + 1, :])

    lax.fori_loop(0, n_double, double_trip, 0)

    @pl.when(qi % 2 == 0)
    def _():
        drain(0)

    @pl.when(qi % 2 == 1)
    def _():
        stage(0, qi - 1, next_may_be_last=True)
        drain(1)

    for hh in heads:
        r0 = hh * HEAD_DIM
        o_w = ow_sc[hh]
        denom = jnp.sum(st_sc[ST_SUM, hh], axis=0, keepdims=True) + st_sc[ST_TAIL, hh, 0:1, :]
        o_s = acc_sc[hh] * (1.0 / denom)

        g0 = g_ref[0, N_BRANCH * hh:N_BRANCH * hh + 1, :]
        g1 = g_ref[0, N_BRANCH * hh + 1:N_BRANCH * hh + 2, :]
        g2 = g_ref[0, N_BRANCH * hh + 2:N_BRANCH * hh + 3, :]
        o = g0 * oc_sc[r0:r0 + HEAD_DIM, :] + g1 * o_s + g2 * o_w
        y_ref[0, r0:r0 + HEAD_DIM, :] = (o * sz_ref[0, r0:r0 + HEAD_DIM, :]).astype(BF16)


def _nsa_attn(qT, qrT, gT, szT, kcmp, vcmpT, ks, kw, vsT, vwT, *, tq=KEY_CHUNK):
    bsz, attn_d, seq = qT.shape
    n_cmp = kcmp.shape[2]
    n_win = KEY_CHUNK + HALF
    gd = HEADS_PER_GROUP * HEAD_DIM
    assert tq == KEY_CHUNK == WINDOW and HALF % PROJ_TILE == 0 and vsT.shape[3] == PROJ_TILE
    qspec = pl.BlockSpec((1, gd, tq), lambda b, g, i: (b, g, i))
    kspec = pl.BlockSpec((1, seq, LANES), lambda b, g, i: (b, 0, g))
    vspec = pl.BlockSpec((1, seq // PROJ_TILE, V_ROWS, PROJ_TILE), lambda b, g, i: (b, 0, g, 0))
    return pl.pallas_call(
        functools.partial(_attn_kernel, tq=tq, n_cmp=n_cmp),
        out_shape=jax.ShapeDtypeStruct((bsz, attn_d, seq), BF16),
        grid=(bsz, N_KV_GROUPS, seq // tq),
        in_specs=[
            qspec, qspec,
            pl.BlockSpec((1, GATE_ROWS, tq), lambda b, g, i: (b, g, i)),
            qspec,
            pl.BlockSpec((1, 1, n_cmp, LANES), lambda b, g, i: (b, g, 0, 0)),
            pl.BlockSpec((1, 1, HEAD_DIM, n_cmp), lambda b, g, i: (b, g, 0, 0)),
            kspec, kspec, vspec, vspec,
        ],
        out_specs=qspec,
        scratch_shapes=[
            pltpu.VMEM((tq // LANES, SUBLANES + n_cmp, LANES), F32),
            pltpu.VMEM((n_cmp // 4, tq), F32),
            pltpu.VMEM((n_cmp // 4, tq), jnp.int32),
            pltpu.VMEM((gd, tq), F32),
            pltpu.VMEM((3, HEADS_PER_GROUP, 2 * HEAD_DIM, tq), BF16),
            pltpu.VMEM((HEADS_PER_GROUP, n_cmp, tq), F32),
            pltpu.VMEM((HEADS_PER_GROUP, tq // HALF, n_win, HALF), F32),
            pltpu.VMEM((2, HEADS_PER_GROUP, KEY_CHUNK, tq), F32),
            pltpu.VMEM((2, HEADS_PER_GROUP, KEY_CHUNK, tq), BF16),
            pltpu.VMEM((2, HEADS_PER_GROUP, SUBLANES, tq), F32),
            pltpu.VMEM((5, HEADS_PER_GROUP, SUBLANES, tq), F32),
            pltpu.VMEM((HEADS_PER_GROUP, HEAD_DIM, tq), F32),
            pltpu.VMEM((HEADS_PER_GROUP, tq // HALF, SUBLANES, HALF), F32),
            pltpu.VMEM((HEADS_PER_GROUP, tq // HALF, n_win, HALF), BF16),
            pltpu.VMEM((HEADS_PER_GROUP, HEAD_DIM, tq), F32),
        ],
        compiler_params=pltpu.CompilerParams(dimension_semantics=("parallel", "parallel", "arbitrary"),
                                             vmem_limit_bytes=VMEM_LIMIT),
        name="nsa_attn",
    )(qT, qrT, gT, szT, kcmp, vcmpT, ks, kw, vsT, vwT)


def _nsa_out_kernel(y_ref, h_ref, wo_ref, fn_ref, o_ref):
    h2 = h_ref[0] + _dot_tn(y_ref[0], wo_ref[...])
    o_ref[0] = h2 * _inv_rms(h2) * fn_ref[...]


def _nsa_out(yT, h3, wo, final_norm, *, tm=OUT_TILE):
    bsz, seq, d = h3.shape
    attn_d = yT.shape[1]
    return pl.pallas_call(
        _nsa_out_kernel,
        out_shape=jax.ShapeDtypeStruct((bsz, seq, d), F32),
        grid=(bsz, seq // tm),
        in_specs=[
            pl.BlockSpec((1, attn_d, tm), lambda b, i: (b, 0, i)),
            pl.BlockSpec((1, tm, d), lambda b, i: (b, i, 0)),
            pl.BlockSpec((attn_d, d), lambda b, i: (0, 0)),
            pl.BlockSpec((1, d), lambda b, i: (0, 0)),
        ],
        out_specs=pl.BlockSpec((1, tm, d), lambda b, i: (b, i, 0)),
        compiler_params=pltpu.CompilerParams(dimension_semantics=("parallel", "parallel"),
                                             vmem_limit_bytes=VMEM_LIMIT),
        name="nsa_out",
    )(yT, h3, wo, final_norm.reshape(1, d))


def _rope_tables(seq):
    pos = jnp.arange(seq, dtype=F32)
    inv = ROPE_THETA ** (-jnp.arange(0, ROT_DIM, 2, dtype=F32) / ROT_DIM)
    ang = pos[:, None] * inv[None, :]
    cos, sin = jnp.cos(ang), jnp.sin(ang)
    z = lambda n: jnp.zeros((seq, n), F32)
    per_head = lambda parts: jnp.tile(jnp.concatenate(parts, axis=1), (1, LANES // HEAD_DIM))
    cosf = per_head([cos, cos, jnp.ones((seq, HEAD_DIM - ROT_DIM), F32)])
    sina = per_head([-sin, z(HEAD_DIM - N_FREQ)])
    sinb = per_head([z(N_FREQ), sin, z(HEAD_DIM - ROT_DIM)])
    return cos.T, sin.T, cosf, sina, sinb


def kernel(x, a_norm, a_w_in, a_conv_w, a_w_out, kv_norm, w_kv, cmp_pos_k, cmp_w1_k, cmp_w2_k,
           cmp_pos_v, cmp_w1_v, cmp_w2_v, b_norm, b_w_in, b_w_out, final_norm):
    bsz, seq, d = x.shape
    attn_d = N_HEADS * HEAD_DIM
    kv_d = N_KV_GROUPS * HEAD_DIM
    assert b_norm.shape[0] == 1, "one NSA layer reads the shared K/V side"
    assert seq % KEY_CHUNK == 0

    h = x.reshape(bsz * seq, d)
    for layer in range(a_norm.shape[0]):
        h = _conv_layer(h, a_norm[layer], a_w_in[layer].astype(BF16), a_conv_w[layer],
                        a_w_out[layer].astype(BF16), seq=seq)
    h3 = h.reshape(bsz, seq, d)

    w_in = b_w_in[0]
    n_gate = N_HEADS * N_BRANCH
    wg = w_in[:, attn_d:attn_d + n_gate].reshape(d, N_KV_GROUPS, HEADS_PER_GROUP * N_BRANCH)
    wg = jnp.pad(wg, ((0, 0), (0, 0), (0, GATE_ROWS - HEADS_PER_GROUP * N_BRANCH)))
    waT = jnp.concatenate([w_in[:, :attn_d], w_in[:, attn_d + n_gate:], wg.reshape(d, -1)], axis=1).T.astype(BF16)
    wkv = w_kv.reshape(d, 2 * N_BRANCH, N_KV_GROUPS, HEAD_DIM)
    k_c, v_c, k_s, v_s, k_w, v_w = [wkv[:, i] for i in range(2 * N_BRANCH)]
    flat = lambda w: w.reshape(d, kv_d)
    wvT = jnp.concatenate([flat(v_s), flat(v_w)], axis=1).T.astype(BF16)
    wk = jnp.concatenate([flat(k_c), flat(v_c), flat(k_s), flat(k_w)], axis=1).astype(BF16)
    cosT, sinT, cosf, sina, sinb = _rope_tables(seq)

    qT, qrT, szT, gT, kvc, ks, kw, vsT, vwT = _nsa_proj(
        h3, b_norm[0], kv_norm, waT, wvT, wk, cosT, sinT, cosf, sina, sinb)

    w1 = jnp.stack([cmp_w1_k, cmp_w1_v]).astype(BF16)
    pos = jnp.stack([cmp_pos_k.reshape(1, -1), cmp_pos_v.reshape(1, -1)])
    w2 = jnp.stack([cmp_w2_k, cmp_w2_v])
    w2p = jnp.pad(w2, ((0, 0), (0, 0), (0, LANES - HEAD_DIM))).astype(BF16)
    w2T = jnp.swapaxes(w2, 1, 2).astype(BF16)
    cmp_tm, cmp_fm = _compress(kvc, w1, pos, w2p, w2T)

    yT = _nsa_attn(qT, qrT, gT, szT, cmp_tm[0], cmp_fm[1], ks, kw, vsT, vwT)
    return _nsa_out(yT, h3, b_w_out[0].astype(BF16), final_norm)
```

```python
import functools

import jax
import jax.numpy as jnp
from jax import lax
from jax.experimental import pallas as pl
from jax.experimental.pallas import tpu as pltpu

EPS = 1e-6
CONV_WIDTH = 3
N_HEADS = 16
HEAD_DIM = 64
N_KV_GROUPS = 4
HEADS_PER_GROUP = N_HEADS // N_KV_GROUPS
N_BRANCH = 3
ROT_DIM = HEAD_DIM // 4
N_FREQ = ROT_DIM // 2
ROPE_THETA = 500000.0
CMP_BLOCK = 32
CMP_STRIDE = 16
SLC_BLOCK = 64
N_SELECT = 16
N_LOCAL = 2
WINDOW = 512
FORCE_SCORE = 1e4

LANES = 128
SUBLANES = 8
KEY_CHUNK = 512
HALF = KEY_CHUNK // 2
PROJ_TILE = 256
OUT_TILE = 1024
SOFTMAX_ROWS = 32
RANK_SECTION = 8
GATE_ROWS = 16
V_ROWS = HEAD_DIM + 16
MASK_BIAS = -1e30
LOG2_E = 1.4426950408889634
VMEM_LIMIT = 56 * 1024 * 1024

BF16 = jnp.bfloat16
F32 = jnp.float32
NT_DIMS = (((1,), (1,)), ((), ()))


def _dot(a, b):
    return jnp.dot(a, b, preferred_element_type=F32)


def _dot_nt(a, b):
    return lax.dot_general(a, b, NT_DIMS, preferred_element_type=F32)


def _dot_tn(a, b):
    return lax.dot_general(a, b, (((0,), (0,)), ((), ())), preferred_element_type=F32)


def _sigmoid(x):
    return 1.0 / (1.0 + jnp.exp(-x))


def _inv_rms(x):
    return lax.rsqrt(jnp.mean(x * x, axis=-1, keepdims=True) + EPS)


def _conv_layer_kernel(x_ref, g_ref, win_ref, cw_ref, wout_ref, o_ref, vbuf_ref, *,
                       tm, tiles_per_seq, conv_d, cchunk):
    @pl.when(pl.program_id(0) % tiles_per_seq == 0)
    def _():
        vbuf_ref[0:SUBLANES, :] = jnp.zeros((SUBLANES, conv_d), F32)

    x = x_ref[...]
    hn = (x * _inv_rms(x) * g_ref[...]).astype(BF16)
    acc = jnp.zeros(x.shape, F32)
    for cc in range(conv_d // cchunk):
        cs = cc * cchunk
        b = _dot(hn, win_ref[:, cs:cs + cchunk])
        c = _dot(hn, win_ref[:, conv_d + cs:conv_d + cs + cchunk])
        u = _dot(hn, win_ref[:, 2 * conv_d + cs:2 * conv_d + cs + cchunk])
        z = _dot(hn, win_ref[:, 3 * conv_d + cs:3 * conv_d + cs + cchunk])
        v = c * u
        vbuf_ref[SUBLANES:SUBLANES + tm, cs:cs + cchunk] = v
        v1 = vbuf_ref[SUBLANES - 1:SUBLANES - 1 + tm, cs:cs + cchunk]
        v2 = vbuf_ref[SUBLANES - 2:SUBLANES - 2 + tm, cs:cs + cchunk]
        conv = (cw_ref[0:1, cs:cs + cchunk] * v2 + cw_ref[1:2, cs:cs + cchunk] * v1
                + cw_ref[2:3, cs:cs + cchunk] * v)
        vbuf_ref[0:SUBLANES, cs:cs + cchunk] = v[tm - SUBLANES:tm, :]
        y = b * conv * (z * _sigmoid(z))
        acc = acc + _dot(y.astype(BF16), wout_ref[cs:cs + cchunk, :])
    o_ref[...] = x + acc


def _conv_layer(h, norm_g, w_in, conv_w, w_out, *, seq, tm=512, cchunk=1024):
    t, d = h.shape
    conv_d = conv_w.shape[1]
    tm = min(tm, seq)
    const = lambda shape: pl.BlockSpec(shape, lambda i: (0,) * len(shape), pipeline_mode=pl.Buffered(1))
    return pl.pallas_call(
        functools.partial(_conv_layer_kernel, tm=tm, tiles_per_seq=seq // tm, conv_d=conv_d, cchunk=cchunk),
        out_shape=jax.ShapeDtypeStruct((t, d), F32),
        grid=(t // tm,),
        in_specs=[
            pl.BlockSpec((tm, d), lambda i: (i, 0)),
            const((1, d)),
            const((d, 4 * conv_d)),
            const((CONV_WIDTH, conv_d)),
            const((conv_d, d)),
        ],
        out_specs=pl.BlockSpec((tm, d), lambda i: (i, 0)),
        scratch_shapes=[pltpu.VMEM((SUBLANES + tm, conv_d), F32)],
        compiler_params=pltpu.CompilerParams(dimension_semantics=("arbitrary",), vmem_limit_bytes=VMEM_LIMIT),
        name="conv_layer",
    )(h, norm_g.reshape(1, d), w_in, conv_w, w_out)


def _nsa_proj_kernel(h_ref, bn_ref, kn_ref, waT_ref, wvT_ref, wk_ref, cosT_ref, sinT_ref,
                     cosf_ref, sina_ref, sinb_ref,
                     qT_ref, qrT_ref, szT_ref, gT_ref, kvc_ref, ks_ref, kw_ref, vsT_ref, vwT_ref, *, tm, d):
    attn_d = N_HEADS * HEAD_DIM
    kv_d = N_KV_GROUPS * HEAD_DIM
    h = h_ref[0]
    hr = h * _inv_rms(h)
    hq = (hr * bn_ref[...]).astype(BF16)
    hk = (hr * kn_ref[...]).astype(BF16)

    qT = _dot_nt(waT_ref[0:attn_d, :], hq) * (HEAD_DIM ** -0.5 * LOG2_E)
    cosT = cosT_ref[...]
    sinT = sinT_ref[...]
    for hd in range(N_HEADS):
        r0 = hd * HEAD_DIM
        blk = qT[r0:r0 + HEAD_DIM, :]
        x1 = blk[0:N_FREQ, :]
        x2 = blk[N_FREQ:ROT_DIM, :]
        rot = jnp.concatenate([x1 * cosT - x2 * sinT, x2 * cosT + x1 * sinT, blk[ROT_DIM:, :]], axis=0)
        qT_ref[0, r0:r0 + HEAD_DIM, :] = blk.astype(BF16)
        qrT_ref[0, r0:r0 + HEAD_DIM, :] = rot.astype(BF16)

    zT = _dot_nt(waT_ref[attn_d:2 * attn_d, :], hq)
    szT_ref[0] = zT * _sigmoid(zT)
    gT_ref[0] = _sigmoid(_dot_nt(waT_ref[2 * attn_d:2 * attn_d + N_KV_GROUPS * GATE_ROWS, :], hq))

    vT = _dot_nt(wvT_ref[...], hk)
    ones_rows = jnp.ones((V_ROWS - HEAD_DIM, tm), BF16)
    for i, ref in enumerate((vsT_ref, vwT_ref)):
        for g in range(N_KV_GROUPS):
            src = i * kv_d + g * HEAD_DIM
            ref[0, 0, g * V_ROWS:g * V_ROWS + HEAD_DIM, :] = vT[src:src + HEAD_DIM, :].astype(BF16)
            ref[0, 0, g * V_ROWS + HEAD_DIM:(g + 1) * V_ROWS, :] = ones_rows

    kk = _dot(hk, wk_ref[...])
    kvc_ref[0] = kk[:, 0:2 * kv_d]
    cosf = cosf_ref[...]
    sina = sina_ref[...]
    sinb = sinb_ref[...]
    row = lax.broadcasted_iota(jnp.int32, (tm, LANES), 0)
    lane = lax.broadcasted_iota(jnp.int32, (tm, LANES), 1)
    tok_blk = (pl.program_id(1) * tm + row) // SLC_BLOCK
    onehot = jnp.where((lane >= HEAD_DIM) & (lane - HEAD_DIM == tok_blk), 1.0, 0.0)
    lower = lane < HEAD_DIM
    for i, ref in enumerate((ks_ref, kw_ref)):
        for pair in range(N_KV_GROUPS // 2):
            c0 = (2 + i) * kv_d + pair * LANES
            xb = kk[:, c0:c0 + LANES]
            rot = (xb * cosf + pltpu.roll(xb, LANES - N_FREQ, axis=1) * sina
                   + pltpu.roll(xb, N_FREQ, axis=1) * sinb)
            for half, keys in enumerate((rot, pltpu.roll(rot, HEAD_DIM, axis=1))):
                g = 2 * pair + half
                ref[0, :, g * LANES:(g + 1) * LANES] = (jnp.where(lower, keys, 0.0) + onehot).astype(BF16)


def _nsa_proj(h3, b_norm, kv_norm, waT, wvT, wk, cosT, sinT, cosf, sina, sinb, *, tm=PROJ_TILE):
    bsz, seq, d = h3.shape
    attn_d = N_HEADS * HEAD_DIM
    kv_d = N_KV_GROUPS * HEAD_DIM
    kpad = N_KV_GROUPS * LANES
    nt = seq // tm
    const = lambda shape: pl.BlockSpec(shape, lambda b, i: (0,) * len(shape))
    fm = lambda rows: pl.BlockSpec((1, rows, tm), lambda b, i: (b, 0, i))
    tmj = lambda cols: pl.BlockSpec((1, tm, cols), lambda b, i: (b, i, 0))
    out_shape = [
        jax.ShapeDtypeStruct((bsz, attn_d, seq), BF16),
        jax.ShapeDtypeStruct((bsz, attn_d, seq), BF16),
        jax.ShapeDtypeStruct((bsz, attn_d, seq), F32),
        jax.ShapeDtypeStruct((bsz, N_KV_GROUPS * GATE_ROWS, seq), F32),
        jax.ShapeDtypeStruct((bsz, seq, 2 * kv_d), F32),
        jax.ShapeDtypeStruct((bsz, seq, kpad), BF16),
        jax.ShapeDtypeStruct((bsz, seq, kpad), BF16),
        jax.ShapeDtypeStruct((bsz, nt, N_KV_GROUPS * V_ROWS, tm), BF16),
        jax.ShapeDtypeStruct((bsz, nt, N_KV_GROUPS * V_ROWS, tm), BF16),
    ]
    out_specs = [
        fm(attn_d), fm(attn_d), fm(attn_d), fm(N_KV_GROUPS * GATE_ROWS),
        tmj(2 * kv_d), tmj(kpad), tmj(kpad),
        pl.BlockSpec((1, 1, N_KV_GROUPS * V_ROWS, tm), lambda b, i: (b, i, 0, 0)),
        pl.BlockSpec((1, 1, N_KV_GROUPS * V_ROWS, tm), lambda b, i: (b, i, 0, 0)),
    ]
    return pl.pallas_call(
        functools.partial(_nsa_proj_kernel, tm=tm, d=d),
        out_shape=out_shape,
        grid=(bsz, nt),
        in_specs=[
            tmj(d), const((1, d)), const((1, d)),
            const(waT.shape), const(wvT.shape), const(wk.shape),
            pl.BlockSpec((N_FREQ, tm), lambda b, i: (0, i)),
            pl.BlockSpec((N_FREQ, tm), lambda b, i: (0, i)),
            pl.BlockSpec((tm, LANES), lambda b, i: (i, 0)),
            pl.BlockSpec((tm, LANES), lambda b, i: (i, 0)),
            pl.BlockSpec((tm, LANES), lambda b, i: (i, 0)),
        ],
        out_specs=out_specs,
        compiler_params=pltpu.CompilerParams(dimension_semantics=("parallel", "parallel"),
                                             vmem_limit_bytes=VMEM_LIMIT),
        name="nsa_proj",
    )(h3, b_norm.reshape(1, d), kv_norm.reshape(1, d), waT, wvT, wk, cosT, sinT, cosf, sina, sinb)


def _gelu_tanh(x):
    return x * (0.5 * (1.0 + jnp.tanh(0.7978845608028654 * (x + 0.044715 * (x * x * x)))))


def _compress_kernel(x_ref, w1_ref, pos_ref, w2_ref, w2T_ref, kc_ref, cT_ref, rows_sc, *, nchunk):
    half = CMP_STRIDE * HEAD_DIM
    for l in range(CMP_STRIDE):
        xl = x_ref[0, pl.ds(l, nchunk, stride=CMP_STRIDE), :]
        for p in range(LANES // HEAD_DIM):
            rows_sc[p, :, l * HEAD_DIM:(l + 1) * HEAD_DIM] = xl[:, p * HEAD_DIM:(p + 1) * HEAD_DIM]
    rows_ok = lax.broadcasted_iota(jnp.int32, (nchunk, LANES), 0) < nchunk - 1
    cols_ok = lax.broadcasted_iota(jnp.int32, (HEAD_DIM, nchunk), 1) < nchunk - 1
    for p in range(LANES // HEAD_DIM):
        x = rows_sc[p]
        u = _dot((x + pos_ref[0, :, 0:half]).astype(BF16), w1_ref[0, 0:half, :])
        v = _dot((x + pos_ref[0, :, half:2 * half]).astype(BF16), w1_ref[0, half:2 * half, :])
        act = _gelu_tanh(u + pltpu.roll(v, nchunk - 1, axis=0)).astype(BF16)
        kc_ref[0, 0, p] = jnp.where(rows_ok, _dot(act, w2_ref[0]), 0.0).astype(BF16)
        cT_ref[0, 0, p] = jnp.where(cols_ok, _dot_nt(w2T_ref[0], act), 0.0).astype(BF16)


def _compress(kvc, w1, pos, w2p, w2T):
    bsz, seq, _ = kvc.shape
    nchunk = seq // CMP_STRIDE
    hidden = w2T.shape[2]
    width = CMP_BLOCK * HEAD_DIM
    pair = LANES // HEAD_DIM
    return pl.pallas_call(
        functools.partial(_compress_kernel, nchunk=nchunk),
        out_shape=[
            jax.ShapeDtypeStruct((2, bsz, N_KV_GROUPS, nchunk, LANES), BF16),
            jax.ShapeDtypeStruct((2, bsz, N_KV_GROUPS, HEAD_DIM, nchunk), BF16),
        ],
        grid=(2, bsz, N_KV_GROUPS // pair),
        in_specs=[
            pl.BlockSpec((1, seq, LANES), lambda s, b, p: (b, 0, (N_KV_GROUPS // pair) * s + p)),
            pl.BlockSpec((1, width, hidden), lambda s, b, p: (s, 0, 0)),
            pl.BlockSpec((1, 1, width), lambda s, b, p: (s, 0, 0)),
            pl.BlockSpec((1, hidden, LANES), lambda s, b, p: (s, 0, 0)),
            pl.BlockSpec((1, HEAD_DIM, hidden), lambda s, b, p: (s, 0, 0)),
        ],
        out_specs=[
            pl.BlockSpec((1, 1, pair, nchunk, LANES), lambda s, b, p: (s, b, p, 0, 0)),
            pl.BlockSpec((1, 1, pair, HEAD_DIM, nchunk), lambda s, b, p: (s, b, p, 0, 0)),
        ],
        scratch_shapes=[pltpu.VMEM((pair, nchunk, CMP_STRIDE * HEAD_DIM), F32)],
        compiler_params=pltpu.CompilerParams(dimension_semantics=("parallel", "parallel", "parallel"),
                                             vmem_limit_bytes=VMEM_LIMIT),
        name="compress",
    )(kvc, w1, pos, w2p, w2T)


def _attn_kernel(q_ref, qr_ref, g_ref, sz_ref, kc_ref, vcT_ref, ks_ref, kw_ref, vsT_ref, vwT_ref, y_ref,
                 p_sc, sc_sc, rank_sc, oc_sc, qa_sc, c_sc, w_sc, s_sc, pb_sc, smax_sc, st_sc, acc_sc, wmax_sc,
                 wp_sc, ow_sc, *, tq, n_cmp):
    qi = pl.program_id(2)
    t0 = qi * tq
    neg_inf = -jnp.inf
    tvec = t0 + lax.broadcasted_iota(jnp.int32, (1, tq), 1)

    kc = kc_ref[0, 0, :, 0:HEAD_DIM]
    cmp_end = lax.broadcasted_iota(jnp.int32, (n_cmp, tq), 0) * CMP_STRIDE + (CMP_BLOCK - 1)
    cmask = cmp_end <= tvec
    heads = range(HEADS_PER_GROUP)

    def fold8(x, op, ways=4):
        parts = [None] * ways
        for idx, r in enumerate(range(0, x.shape[0], SUBLANES)):
            slab = x[r:r + SUBLANES, :]
            parts[idx % ways] = slab if parts[idx % ways] is None else op(parts[idx % ways], slab)
        return functools.reduce(op, [p for p in parts if p is not None])

    def key_chunk(ref, c):
        return ref[0, pl.ds(pl.multiple_of(c * KEY_CHUNK, KEY_CHUNK), KEY_CHUNK), :]

    n_sb = n_cmp // 4
    jrow = lax.broadcasted_iota(jnp.int32, (n_sb, tq), 0)
    cur = tvec // SLC_BLOCK
    valid = jrow <= cur

    QA_SEL, QA_DEAD, QA_WIN = 0, 1, 2
    pad_blocks = lambda b: b if n_sb == HEAD_DIM else jnp.concatenate(
        [b, jnp.zeros((HEAD_DIM - n_sb, tq), BF16)], axis=0)

    def set_operand(idx, bias):
        for hh in heads:
            r0 = hh * HEAD_DIM
            qa_sc[idx, hh] = jnp.concatenate([qr_ref[0, r0:r0 + HEAD_DIM, :], pad_blocks(bias)], axis=0)

    win_blocks = WINDOW // SLC_BLOCK
    set_operand(QA_WIN, jnp.where((jrow >= cur - win_blocks) & valid, 0.0, MASK_BIAS).astype(BF16))
    set_operand(QA_DEAD, jnp.full((n_sb, tq), MASK_BIAS, BF16))

    blk_row = lax.broadcasted_iota(jnp.int32, (SLC_BLOCK, LANES), 0)
    blk_lane = lax.broadcasted_iota(jnp.int32, (SLC_BLOCK, LANES), 1)
    edge = [blk_row - blk_lane, blk_row - (blk_lane - SLC_BLOCK)]
    off_diag = [blk_lane >= SLC_BLOCK, blk_lane < SLC_BLOCK]

    def mask_diagonal_blocks(s, first_block, keep):
        width = s.shape[1]
        blocks = [s[r:r + SLC_BLOCK, :] for r in range(0, s.shape[0], SLC_BLOCK)]
        for j in range(width // SLC_BLOCK):
            cols = [blocks[first_block + j][:, h * LANES:(h + 1) * LANES] for h in range(width // LANES)]
            cols[j // 2] = jnp.where(off_diag[j % 2] | keep(edge[j % 2]), cols[j // 2], neg_inf)
            blocks[first_block + j] = jnp.concatenate(cols, axis=1)
        return jnp.concatenate(blocks, axis=0)

    n_win = KEY_CHUNK + HALF
    prev_operand = jnp.where(qi >= 1, QA_WIN, QA_DEAD)
    prev_base = jnp.maximum(qi - 1, 0) * KEY_CHUNK

    def window_pieces(half):
        first = half * HALF
        n_old = KEY_CHUNK - first
        old = (pl.multiple_of(prev_base + first, HALF), n_old)
        new = (pl.multiple_of(qi * KEY_CHUNK, HALF), HALF + first)
        return old, new

    for hh in heads:
        r0 = hh * HEAD_DIM
        c_sc[hh] = jnp.where(cmask, _dot(kc, q_ref[0, r0:r0 + HEAD_DIM, :]), neg_inf)
        for half in range(tq // HALF):
            (old_start, n_old), (new_start, n_new) = window_pieces(half)
            lanes = slice(half * HALF, (half + 1) * HALF)
            s_old = _dot(kw_ref[0, pl.ds(old_start, n_old), :], qa_sc[prev_operand, hh, :, lanes])
            s_new = _dot(kw_ref[0, pl.ds(new_start, n_new), :], qa_sc[QA_WIN, hh, :, lanes])
            s_old = mask_diagonal_blocks(s_old, 0, lambda e: e > 0)
            s_new = mask_diagonal_blocks(s_new, half * (HALF // SLC_BLOCK), lambda e: e <= 0)
            w_sc[hh, half, 0:n_old, :] = s_old
            w_sc[hh, half, n_old:n_win, :] = s_new
            wmax_sc[hh, half] = jnp.maximum(fold8(s_old, jnp.maximum), fold8(s_new, jnp.maximum))
    probs = []
    for hh in heads:
        s = c_sc[hh]
        m = jnp.max(fold8(s, jnp.maximum), axis=0, keepdims=True)
        m = jnp.where(m == neg_inf, 0.0, m)
        e = jnp.exp2(s - m)
        den = jnp.sum(fold8(e, jnp.add), axis=0, keepdims=True)
        probs.append(e * (1.0 / jnp.maximum(den, 1e-30)))
    p_grp = functools.reduce(jnp.add, probs)
    for hh in heads:
        r0 = hh * HEAD_DIM
        oc_sc[r0:r0 + HEAD_DIM, :] = _dot(vcT_ref[0, 0], probs[hh].astype(BF16))

    ratio = SLC_BLOCK // CMP_STRIDE
    imp_cols = []
    for c in range(tq // LANES):
        p_sc[c, 0:SUBLANES, :] = jnp.zeros((SUBLANES, LANES), F32)
        p_sc[c, SUBLANES:SUBLANES + n_cmp, :] = p_grp[:, c * LANES:(c + 1) * LANES]
        tap = lambda o: p_sc[c, pl.ds(SUBLANES + o, n_sb, stride=ratio), :]
        imp_cols.append(tap(-1) + 2.0 * (tap(0) + tap(1) + tap(2)) + tap(3))
    imp = jnp.concatenate(imp_cols, axis=1)

    forced = (jrow == 0) | (valid & (jrow > cur - N_LOCAL))
    score = jnp.where(valid, jnp.where(forced, FORCE_SCORE, imp), neg_inf)
    sc_sc[...] = score

    n_valid = (t0 + tq) // SLC_BLOCK
    n_slabs = n_sb // SUBLANES
    slab_row = lax.broadcasted_iota(jnp.int32, (SUBLANES, tq), 0)
    rank_sc[...] = jnp.zeros((n_sb, tq), jnp.int32)
    for first in range(0, n_sb, RANK_SECTION):
        @pl.when(n_valid > max(N_SELECT, first))
        def _():
            slabs = [sc_sc[k * SUBLANES:(k + 1) * SUBLANES, :] for k in range(n_slabs)]
            counts = [rank_sc[k * SUBLANES:(k + 1) * SUBLANES, :] for k in range(n_slabs)]
            for jp in range(first, first + RANK_SECTION):
                sb = sc_sc[jp:jp + 1, :]
                for k in range(n_slabs):
                    if k * SUBLANES > jp:
                        before = sb >= slabs[k]
                    elif k * SUBLANES + SUBLANES - 1 < jp:
                        before = sb > slabs[k]
                    else:
                        before = (sb > slabs[k]) | ((sb == slabs[k]) & (slab_row > jp % SUBLANES))
                    counts[k] = counts[k] + jnp.where(before, 1, 0)
            for k in range(n_slabs):
                rank_sc[k * SUBLANES:(k + 1) * SUBLANES, :] = counts[k]
    set_operand(QA_SEL, jnp.where((rank_sc[...] < N_SELECT) & valid, 0.0, MASK_BIAS).astype(BF16))

    group = KEY_CHUNK
    n_double = qi // 2

    def values_of(ref, c):
        per = KEY_CHUNK // PROJ_TILE
        return jnp.concatenate([ref[0, per * c + v] for v in range(per)], axis=1)

    ST_MAX, ST_SUM, ST_RESCALE, ST_TAIL = 0, 1, 2, 4
    rows8 = lambda x: jnp.broadcast_to(x, (SUBLANES, tq))
    never = 1 << 20

    def scores_to(slot, pos, hh, may_be_last):
        s = _dot(key_chunk(ks_ref, pos), qa_sc[QA_SEL, hh])
        if may_be_last:
            causal_slack = jnp.where(pos == qi, 0, never)
            s = mask_diagonal_blocks(s, 0, lambda e: e <= causal_slack)
        s_sc[slot, hh] = s
        smax_sc[slot, hh] = fold8(s, jnp.maximum)

    def accumulate(pos, slot, hh, tail=None):
        vT = values_of(vsT_ref, pos)
        pv = _dot(vT if tail else vT[0:HEAD_DIM, :], pb_sc[slot, hh])
        rescale = st_sc[ST_RESCALE + slot, hh, 0:1, :]
        acc_sc[hh] = acc_sc[hh] * rescale + pv[0:HEAD_DIM, :]
        if tail == "set":
            st_sc[ST_TAIL, hh] = pv[HEAD_DIM:HEAD_DIM + SUBLANES, :]
        elif tail == "add":
            st_sc[ST_TAIL, hh] = st_sc[ST_TAIL, hh] * rescale + pv[HEAD_DIM:HEAD_DIM + SUBLANES, :]

    def softmax_group(slot, hh, with_sum=True):
        m_old = st_sc[ST_MAX, hh, 0:1, :]
        m_new = jnp.maximum(m_old, jnp.max(smax_sc[slot, hh], axis=0, keepdims=True))
        a = jnp.exp2(m_old - m_new)
        psum = None
        for r in range(0, group, SOFTMAX_ROWS):
            s = s_sc[slot, hh, r:r + SOFTMAX_ROWS, :]
            p = jnp.exp2(s - m_new)
            pb_sc[slot, hh, r:r + SOFTMAX_ROWS, :] = p.astype(BF16)
            if with_sum:
                f = fold8(p, jnp.add, ways=2)
                psum = f if psum is None else psum + f
        st_sc[ST_SUM, hh] = a * st_sc[ST_SUM, hh] + psum if with_sum else a * st_sc[ST_SUM, hh]
        st_sc[ST_MAX, hh] = rows8(m_new)
        st_sc[ST_RESCALE + slot, hh] = rows8(a)

    def stage(slot, pos, next_may_be_last, with_sum=True):
        for hh in heads:
            scores_to(1 - slot, pos + 1, hh, next_may_be_last)
            accumulate(jnp.maximum(pos - 1, 0), 1 - slot, hh)
            softmax_group(slot, hh, with_sum)

    def drain(slot, previous_in_tail):
        for hh in heads:
            accumulate(jnp.maximum(qi - 1, 0), 1 - slot, hh, "set" if previous_in_tail else None)
            softmax_group(slot, hh, with_sum=False)
            accumulate(qi, slot, hh, "add" if previous_in_tail else "set")

    def double_trip(d, carry):
        stage(0, 2 * d, next_may_be_last=False)
        stage(1, 2 * d + 1, next_may_be_last=True)
        return carry

    def window_values(half):
        pieces = []
        for start, rows in window_pieces(half):
            pieces += [vwT_ref[0, start // PROJ_TILE + v] for v in range(rows // PROJ_TILE)]
        return jnp.concatenate(pieces, axis=1)

    for hh in heads:
        acc_sc[hh] = jnp.zeros((HEAD_DIM, tq), F32)
        pb_sc[1, hh] = jnp.zeros((group, tq), BF16)
        st_sc[ST_MAX, hh] = jnp.full((SUBLANES, tq), neg_inf, F32)
        st_sc[ST_SUM, hh] = jnp.zeros((SUBLANES, tq), F32)
        st_sc[ST_RESCALE + 1, hh] = jnp.ones((SUBLANES, tq), F32)
        scores_to(0, 0, hh, may_be_last=True)
        for half in range(tq // HALF):
            m = jnp.max(wmax_sc[hh, half], axis=0, keepdims=True)
            for r in range(0, n_win, 2 * SOFTMAX_ROWS):
                wp_sc[hh, half, r:r + 2 * SOFTMAX_ROWS, :] = jnp.exp2(
                    w_sc[hh, half, r:r + 2 * SOFTMAX_ROWS, :] - m).astype(BF16)
            pv = _dot(window_values(half), wp_sc[hh, half])
            ow_sc[hh, :, half * HALF:(half + 1) * HALF] = pv[0:HEAD_DIM, :] * (1.0 / pv[HEAD_DIM:HEAD_DIM + 1, :])

    lax.fori_loop(0, n_double, double_trip, 0)

    @pl.when(qi % 2 == 0)
    def _():
        drain(0, previous_in_tail=False)

    @pl.when(qi % 2 == 1)
    def _():
        stage(0, qi - 1, next_may_be_last=True, with_sum=False)
        drain(1, previous_in_tail=True)

    for hh in heads:
        r0 = hh * HEAD_DIM
        o_w = ow_sc[hh]
        denom = jnp.sum(st_sc[ST_SUM, hh], axis=0, keepdims=True) + st_sc[ST_TAIL, hh, 0:1, :]
        o_s = acc_sc[hh] * (1.0 / denom)

        g0 = g_ref[0, N_BRANCH * hh:N_BRANCH * hh + 1, :]
        g1 = g_ref[0, N_BRANCH * hh + 1:N_BRANCH * hh + 2, :]
        g2 = g_ref[0, N_BRANCH * hh + 2:N_BRANCH * hh + 3, :]
        o = g0 * oc_sc[r0:r0 + HEAD_DIM, :] + g1 * o_s + g2 * o_w
        y_ref[0, r0:r0 + HEAD_DIM, :] = (o * sz_ref[0, r0:r0 + HEAD_DIM, :]).astype(BF16)


def _nsa_attn(qT, qrT, gT, szT, kcmp, vcmpT, ks, kw, vsT, vwT, *, tq=KEY_CHUNK):
    bsz, attn_d, seq = qT.shape
    n_cmp = kcmp.shape[2]
    n_win = KEY_CHUNK + HALF
    gd = HEADS_PER_GROUP * HEAD_DIM
    assert tq == KEY_CHUNK == WINDOW and HALF % PROJ_TILE == 0 and vsT.shape[3] == PROJ_TILE
    qspec = pl.BlockSpec((1, gd, tq), lambda b, g, i: (b, g, i))
    kspec = pl.BlockSpec((1, seq, LANES), lambda b, g, i: (b, 0, g))
    vspec = pl.BlockSpec((1, seq // PROJ_TILE, V_ROWS, PROJ_TILE), lambda b, g, i: (b, 0, g, 0))
    return pl.pallas_call(
        functools.partial(_attn_kernel, tq=tq, n_cmp=n_cmp),
        out_shape=jax.ShapeDtypeStruct((bsz, attn_d, seq), BF16),
        grid=(bsz, N_KV_GROUPS, seq // tq),
        in_specs=[
            qspec, qspec,
            pl.BlockSpec((1, GATE_ROWS, tq), lambda b, g, i: (b, g, i)),
            qspec,
            pl.BlockSpec((1, 1, n_cmp, LANES), lambda b, g, i: (b, g, 0, 0)),
            pl.BlockSpec((1, 1, HEAD_DIM, n_cmp), lambda b, g, i: (b, g, 0, 0)),
            kspec, kspec, vspec, vspec,
        ],
        out_specs=qspec,
        scratch_shapes=[
            pltpu.VMEM((tq // LANES, SUBLANES + n_cmp, LANES), F32),
            pltpu.VMEM((n_cmp // 4, tq), F32),
            pltpu.VMEM((n_cmp // 4, tq), jnp.int32),
            pltpu.VMEM((gd, tq), F32),
            pltpu.VMEM((3, HEADS_PER_GROUP, 2 * HEAD_DIM, tq), BF16),
            pltpu.VMEM((HEADS_PER_GROUP, n_cmp, tq), F32),
            pltpu.VMEM((HEADS_PER_GROUP, tq // HALF, n_win, HALF), F32),
            pltpu.VMEM((2, HEADS_PER_GROUP, KEY_CHUNK, tq), F32),
            pltpu.VMEM((2, HEADS_PER_GROUP, KEY_CHUNK, tq), BF16),
            pltpu.VMEM((2, HEADS_PER_GROUP, SUBLANES, tq), F32),
            pltpu.VMEM((5, HEADS_PER_GROUP, SUBLANES, tq), F32),
            pltpu.VMEM((HEADS_PER_GROUP, HEAD_DIM, tq), F32),
            pltpu.VMEM((HEADS_PER_GROUP, tq // HALF, SUBLANES, HALF), F32),
            pltpu.VMEM((HEADS_PER_GROUP, tq // HALF, n_win, HALF), BF16),
            pltpu.VMEM((HEADS_PER_GROUP, HEAD_DIM, tq), F32),
        ],
        compiler_params=pltpu.CompilerParams(dimension_semantics=("parallel", "parallel", "arbitrary"),
                                             vmem_limit_bytes=VMEM_LIMIT),
        name="nsa_attn",
    )(qT, qrT, gT, szT, kcmp, vcmpT, ks, kw, vsT, vwT)


def _nsa_out_kernel(y_ref, h_ref, wo_ref, fn_ref, o_ref):
    h2 = h_ref[0] + _dot_tn(y_ref[0], wo_ref[...])
    o_ref[0] = h2 * _inv_rms(h2) * fn_ref[...]


def _nsa_out(yT, h3, wo, final_norm, *, tm=OUT_TILE):
    bsz, seq, d = h3.shape
    attn_d = yT.shape[1]
    return pl.pallas_call(
        _nsa_out_kernel,
        out_shape=jax.ShapeDtypeStruct((bsz, seq, d), F32),
        grid=(bsz, seq // tm),
        in_specs=[
            pl.BlockSpec((1, attn_d, tm), lambda b, i: (b, 0, i)),
            pl.BlockSpec((1, tm, d), lambda b, i: (b, i, 0)),
            pl.BlockSpec((attn_d, d), lambda b, i: (0, 0)),
            pl.BlockSpec((1, d), lambda b, i: (0, 0)),
        ],
        out_specs=pl.BlockSpec((1, tm, d), lambda b, i: (b, i, 0)),
        compiler_params=pltpu.CompilerParams(dimension_semantics=("parallel", "parallel"),
                                             vmem_limit_bytes=VMEM_LIMIT),
        name="nsa_out",
    )(yT, h3, wo, final_norm.reshape(1, d))


def _rope_tables(seq):
    pos = jnp.arange(seq, dtype=F32)
    inv = ROPE_THETA ** (-jnp.arange(0, ROT_DIM, 2, dtype=F32) / ROT_DIM)
    ang = pos[:, None] * inv[None, :]
    cos, sin = jnp.cos(ang), jnp.sin(ang)
    z = lambda n: jnp.zeros((seq, n), F32)
    per_head = lambda parts: jnp.tile(jnp.concatenate(parts, axis=1), (1, LANES // HEAD_DIM))
    cosf = per_head([cos, cos, jnp.ones((seq, HEAD_DIM - ROT_DIM), F32)])
    sina = per_head([-sin, z(HEAD_DIM - N_FREQ)])
    sinb = per_head([z(N_FREQ), sin, z(HEAD_DIM - ROT_DIM)])
    return cos.T, sin.T, cosf, sina, sinb


def kernel(x, a_norm, a_w_in, a_conv_w, a_w_out, kv_norm, w_kv, cmp_pos_k, cmp_w1_k, cmp_w2_k,
           cmp_pos_v, cmp_w1_v, cmp_w2_v, b_norm, b_w_in, b_w_out, final_norm):
    bsz, seq, d = x.shape
    attn_d = N_HEADS * HEAD_DIM
    kv_d = N_KV_GROUPS * HEAD_DIM
    assert b_norm.shape[0] == 1, "one NSA layer reads the shared K/V side"
    assert seq % KEY_CHUNK == 0

    h = x.reshape(bsz * seq, d)
    for layer in range(a_norm.shape[0]):
        h = _conv_layer(h, a_norm[layer], a_w_in[layer].astype(BF16), a_conv_w[layer],
                        a_w_out[layer].astype(BF16), seq=seq)
    h3 = h.reshape(bsz, seq, d)

    w_in = b_w_in[0]
    n_gate = N_HEADS * N_BRANCH
    wg = w_in[:, attn_d:attn_d + n_gate].reshape(d, N_KV_GROUPS, HEADS_PER_GROUP * N_BRANCH)
    wg = jnp.pad(wg, ((0, 0), (0, 0), (0, GATE_ROWS - HEADS_PER_GROUP * N_BRANCH)))
    waT = jnp.concatenate([w_in[:, :attn_d], w_in[:, attn_d + n_gate:], wg.reshape(d, -1)], axis=1).T.astype(BF16)
    wkv = w_kv.reshape(d, 2 * N_BRANCH, N_KV_GROUPS, HEAD_DIM)
    k_c, v_c, k_s, v_s, k_w, v_w = [wkv[:, i] for i in range(2 * N_BRANCH)]
    flat = lambda w: w.reshape(d, kv_d)
    wvT = jnp.concatenate([flat(v_s), flat(v_w)], axis=1).T.astype(BF16)
    wk = jnp.concatenate([flat(k_c), flat(v_c), flat(k_s), flat(k_w)], axis=1).astype(BF16)
    cosT, sinT, cosf, sina, sinb = _rope_tables(seq)

    qT, qrT, szT, gT, kvc, ks, kw, vsT, vwT = _nsa_proj(
        h3, b_norm[0], kv_norm, waT, wvT, wk, cosT, sinT, cosf, sina, sinb)

    w1 = jnp.stack([cmp_w1_k, cmp_w1_v]).astype(BF16)
    pos = jnp.stack([cmp_pos_k.reshape(1, -1), cmp_pos_v.reshape(1, -1)])
    w2 = jnp.stack([cmp_w2_k, cmp_w2_v])
    w2p = jnp.pad(w2, ((0, 0), (0, 0), (0, LANES - HEAD_DIM))).astype(BF16)
    w2T = jnp.swapaxes(w2, 1, 2).astype(BF16)
    cmp_tm, cmp_fm = _compress(kvc, w1, pos, w2p, w2T)

    yT = _nsa_attn(qT, qrT, gT, szT, cmp_tm[0], cmp_fm[1], ks, kw, vsT, vwT)
    return _nsa_out(yT, h3, b_w_out[0].astype(BF16), final_norm)
```

```python
import functools

import jax
import jax.numpy as jnp
from jax import lax
from jax.experimental import pallas as pl
from jax.experimental.pallas import tpu as pltpu

EPS = 1e-6
CONV_WIDTH = 3
N_HEADS = 16
HEAD_DIM = 64
N_KV_GROUPS = 4
HEADS_PER_GROUP = N_HEADS // N_KV_GROUPS
N_BRANCH = 3
ROT_DIM = HEAD_DIM // 4
N_FREQ = ROT_DIM // 2
ROPE_THETA = 500000.0
CMP_BLOCK = 32
CMP_STRIDE = 16
SLC_BLOCK = 64
N_SELECT = 16
N_LOCAL = 2
WINDOW = 512
FORCE_SCORE = 1e4

LANES = 128
SUBLANES = 8
KEY_CHUNK = 512
HALF = KEY_CHUNK // 2
PROJ_TILE = 256
OUT_TILE = 1024
SOFTMAX_ROWS = 32
RANK_SECTION = 8
GATE_ROWS = 16
V_ROWS = HEAD_DIM + 16
MASK_BIAS = -1e30
LOG2_E = 1.4426950408889634
VMEM_LIMIT = 56 * 1024 * 1024

BF16 = jnp.bfloat16
F32 = jnp.float32
NT_DIMS = (((1,), (1,)), ((), ()))


def _dot(a, b):
    return jnp.dot(a, b, preferred_element_type=F32)


def _dot_nt(a, b):
    return lax.dot_general(a, b, NT_DIMS, preferred_element_type=F32)


def _dot_tn(a, b):
    return lax.dot_general(a, b, (((0,), (0,)), ((), ())), preferred_element_type=F32)


def _sigmoid(x):
    return 1.0 / (1.0 + jnp.exp(-x))


def _inv_rms(x):
    return lax.rsqrt(jnp.mean(x * x, axis=-1, keepdims=True) + EPS)


def _conv_layer_kernel(x_ref, g_ref, win_ref, cw_ref, wout_ref, o_ref, vbuf_ref, *,
                       tm, tiles_per_seq, conv_d, cchunk):
    @pl.when(pl.program_id(0) % tiles_per_seq == 0)
    def _():
        vbuf_ref[0:SUBLANES, :] = jnp.zeros((SUBLANES, conv_d), F32)

    x = x_ref[...]
    hn = (x * _inv_rms(x) * g_ref[...]).astype(BF16)
    acc = jnp.zeros(x.shape, F32)
    for cc in range(conv_d // cchunk):
        cs = cc * cchunk
        b = _dot(hn, win_ref[:, cs:cs + cchunk])
        c = _dot(hn, win_ref[:, conv_d + cs:conv_d + cs + cchunk])
        u = _dot(hn, win_ref[:, 2 * conv_d + cs:2 * conv_d + cs + cchunk])
        z = _dot(hn, win_ref[:, 3 * conv_d + cs:3 * conv_d + cs + cchunk])
        v = c * u
        vbuf_ref[SUBLANES:SUBLANES + tm, cs:cs + cchunk] = v
        v1 = vbuf_ref[SUBLANES - 1:SUBLANES - 1 + tm, cs:cs + cchunk]
        v2 = vbuf_ref[SUBLANES - 2:SUBLANES - 2 + tm, cs:cs + cchunk]
        conv = (cw_ref[0:1, cs:cs + cchunk] * v2 + cw_ref[1:2, cs:cs + cchunk] * v1
                + cw_ref[2:3, cs:cs + cchunk] * v)
        vbuf_ref[0:SUBLANES, cs:cs + cchunk] = v[tm - SUBLANES:tm, :]
        y = b * conv * (z * _sigmoid(z))
        acc = acc + _dot(y.astype(BF16), wout_ref[cs:cs + cchunk, :])
    o_ref[...] = x + acc


def _conv_layer(h, norm_g, w_in, conv_w, w_out, *, seq, tm=512, cchunk=1024):
    t, d = h.shape
    conv_d = conv_w.shape[1]
    tm = min(tm, seq)
    const = lambda shape: pl.BlockSpec(shape, lambda i: (0,) * len(shape), pipeline_mode=pl.Buffered(1))
    return pl.pallas_call(
        functools.partial(_conv_layer_kernel, tm=tm, tiles_per_seq=seq // tm, conv_d=conv_d, cchunk=cchunk),
        out_shape=jax.ShapeDtypeStruct((t, d), F32),
        grid=(t // tm,),
        in_specs=[
            pl.BlockSpec((tm, d), lambda i: (i, 0)),
            const((1, d)),
            const((d, 4 * conv_d)),
            const((CONV_WIDTH, conv_d)),
            const((conv_d, d)),
        ],
        out_specs=pl.BlockSpec((tm, d), lambda i: (i, 0)),
        scratch_shapes=[pltpu.VMEM((SUBLANES + tm, conv_d), F32)],
        compiler_params=pltpu.CompilerParams(dimension_semantics=("arbitrary",), vmem_limit_bytes=VMEM_LIMIT,
                                             allow_input_fusion=[False, False, True, False, True]),
        name="conv_layer",
    )(h, norm_g.reshape(1, d), w_in, conv_w, w_out)


def _nsa_proj_kernel(h_ref, bn_ref, kn_ref, waT_ref, wvT_ref, wk_ref, cosT_ref, sinT_ref,
                     cosf_ref, sina_ref, sinb_ref,
                     qT_ref, qrT_ref, szT_ref, gT_ref, kvc_ref, ks_ref, kw_ref, vsT_ref, vwT_ref, *, tm, d):
    attn_d = N_HEADS * HEAD_DIM
    kv_d = N_KV_GROUPS * HEAD_DIM
    h = h_ref[0]
    hr = h * _inv_rms(h)
    hq = (hr * bn_ref[...]).astype(BF16)
    hk = (hr * kn_ref[...]).astype(BF16)

    qT = _dot_nt(waT_ref[0:attn_d, :], hq) * (HEAD_DIM ** -0.5 * LOG2_E)
    cosT = cosT_ref[...]
    sinT = sinT_ref[...]
    for hd in range(N_HEADS):
        r0 = hd * HEAD_DIM
        blk = qT[r0:r0 + HEAD_DIM, :]
        x1 = blk[0:N_FREQ, :]
        x2 = blk[N_FREQ:ROT_DIM, :]
        rot = jnp.concatenate([x1 * cosT - x2 * sinT, x2 * cosT + x1 * sinT, blk[ROT_DIM:, :]], axis=0)
        qT_ref[0, r0:r0 + HEAD_DIM, :] = blk.astype(BF16)
        qrT_ref[0, r0:r0 + HEAD_DIM, :] = rot.astype(BF16)

    zT = _dot_nt(waT_ref[attn_d:2 * attn_d, :], hq)
    szT_ref[0] = zT * _sigmoid(zT)
    gT_ref[0] = _sigmoid(_dot_nt(waT_ref[2 * attn_d:2 * attn_d + N_KV_GROUPS * GATE_ROWS, :], hq))

    vT = _dot_nt(wvT_ref[...], hk)
    ones_rows = jnp.ones((V_ROWS - HEAD_DIM, tm), BF16)
    for i, ref in enumerate((vsT_ref, vwT_ref)):
        for g in range(N_KV_GROUPS):
            src = i * kv_d + g * HEAD_DIM
            ref[0, 0, g * V_ROWS:g * V_ROWS + HEAD_DIM, :] = vT[src:src + HEAD_DIM, :].astype(BF16)
            ref[0, 0, g * V_ROWS + HEAD_DIM:(g + 1) * V_ROWS, :] = ones_rows

    kk = _dot(hk, wk_ref[...])
    kvc_ref[0] = kk[:, 0:2 * kv_d]
    cosf = cosf_ref[...]
    sina = sina_ref[...]
    sinb = sinb_ref[...]
    row = lax.broadcasted_iota(jnp.int32, (tm, LANES), 0)
    lane = lax.broadcasted_iota(jnp.int32, (tm, LANES), 1)
    tok_blk = (pl.program_id(1) * tm + row) // SLC_BLOCK
    onehot = jnp.where((lane >= HEAD_DIM) & (lane - HEAD_DIM == tok_blk), 1.0, 0.0)
    lower = lane < HEAD_DIM
    for i, ref in enumerate((ks_ref, kw_ref)):
        for pair in range(N_KV_GROUPS // 2):
            c0 = (2 + i) * kv_d + pair * LANES
            xb = kk[:, c0:c0 + LANES]
            rot = (xb * cosf + pltpu.roll(xb, LANES - N_FREQ, axis=1) * sina
                   + pltpu.roll(xb, N_FREQ, axis=1) * sinb)
            for half, keys in enumerate((rot, pltpu.roll(rot, HEAD_DIM, axis=1))):
                g = 2 * pair + half
                ref[0, :, g * LANES:(g + 1) * LANES] = (jnp.where(lower, keys, 0.0) + onehot).astype(BF16)


def _nsa_proj(h3, b_norm, kv_norm, waT, wvT, wk, cosT, sinT, cosf, sina, sinb, *, tm=PROJ_TILE):
    bsz, seq, d = h3.shape
    attn_d = N_HEADS * HEAD_DIM
    kv_d = N_KV_GROUPS * HEAD_DIM
    kpad = N_KV_GROUPS * LANES
    nt = seq // tm
    const = lambda shape: pl.BlockSpec(shape, lambda b, i: (0,) * len(shape))
    fm = lambda rows: pl.BlockSpec((1, rows, tm), lambda b, i: (b, 0, i))
    tmj = lambda cols: pl.BlockSpec((1, tm, cols), lambda b, i: (b, i, 0))
    out_shape = [
        jax.ShapeDtypeStruct((bsz, attn_d, seq), BF16),
        jax.ShapeDtypeStruct((bsz, attn_d, seq), BF16),
        jax.ShapeDtypeStruct((bsz, attn_d, seq), F32),
        jax.ShapeDtypeStruct((bsz, N_KV_GROUPS * GATE_ROWS, seq), F32),
        jax.ShapeDtypeStruct((bsz, seq, 2 * kv_d), F32),
        jax.ShapeDtypeStruct((bsz, seq, kpad), BF16),
        jax.ShapeDtypeStruct((bsz, seq, kpad), BF16),
        jax.ShapeDtypeStruct((bsz, nt, N_KV_GROUPS * V_ROWS, tm), BF16),
        jax.ShapeDtypeStruct((bsz, nt, N_KV_GROUPS * V_ROWS, tm), BF16),
    ]
    out_specs = [
        fm(attn_d), fm(attn_d), fm(attn_d), fm(N_KV_GROUPS * GATE_ROWS),
        tmj(2 * kv_d), tmj(kpad), tmj(kpad),
        pl.BlockSpec((1, 1, N_KV_GROUPS * V_ROWS, tm), lambda b, i: (b, i, 0, 0)),
        pl.BlockSpec((1, 1, N_KV_GROUPS * V_ROWS, tm), lambda b, i: (b, i, 0, 0)),
    ]
    return pl.pallas_call(
        functools.partial(_nsa_proj_kernel, tm=tm, d=d),
        out_shape=out_shape,
        grid=(bsz, nt),
        in_specs=[
            tmj(d), const((1, d)), const((1, d)),
            const(waT.shape), const(wvT.shape), const(wk.shape),
            pl.BlockSpec((N_FREQ, tm), lambda b, i: (0, i)),
            pl.BlockSpec((N_FREQ, tm), lambda b, i: (0, i)),
            pl.BlockSpec((tm, LANES), lambda b, i: (i, 0)),
            pl.BlockSpec((tm, LANES), lambda b, i: (i, 0)),
            pl.BlockSpec((tm, LANES), lambda b, i: (i, 0)),
        ],
        out_specs=out_specs,
        compiler_params=pltpu.CompilerParams(dimension_semantics=("parallel", "parallel"),
                                             vmem_limit_bytes=VMEM_LIMIT),
        name="nsa_proj",
    )(h3, b_norm.reshape(1, d), kv_norm.reshape(1, d), waT, wvT, wk, cosT, sinT, cosf, sina, sinb)


def _gelu_tanh(x):
    return x * (0.5 * (1.0 + jnp.tanh(0.7978845608028654 * (x + 0.044715 * (x * x * x)))))


def _compress_kernel(x_ref, w1_ref, pos_ref, w2_ref, w2T_ref, kc_ref, cT_ref, rows_sc, *, nchunk):
    half = CMP_STRIDE * HEAD_DIM
    for l in range(CMP_STRIDE):
        xl = x_ref[0, pl.ds(l, nchunk, stride=CMP_STRIDE), :]
        for p in range(LANES // HEAD_DIM):
            rows_sc[p, :, l * HEAD_DIM:(l + 1) * HEAD_DIM] = xl[:, p * HEAD_DIM:(p + 1) * HEAD_DIM]
    rows_ok = lax.broadcasted_iota(jnp.int32, (nchunk, LANES), 0) < nchunk - 1
    cols_ok = lax.broadcasted_iota(jnp.int32, (HEAD_DIM, nchunk), 1) < nchunk - 1
    for p in range(LANES // HEAD_DIM):
        x = rows_sc[p]
        u = _dot((x + pos_ref[0, :, 0:half]).astype(BF16), w1_ref[0, 0:half, :])
        v = _dot((x + pos_ref[0, :, half:2 * half]).astype(BF16), w1_ref[0, half:2 * half, :])
        act = _gelu_tanh(u + pltpu.roll(v, nchunk - 1, axis=0)).astype(BF16)
        kc_ref[0, 0, p] = jnp.where(rows_ok, _dot(act, w2_ref[0]), 0.0).astype(BF16)
        cT_ref[0, 0, p] = jnp.where(cols_ok, _dot_nt(w2T_ref[0], act), 0.0).astype(BF16)


def _compress(kvc, w1, pos, w2p, w2T):
    bsz, seq, _ = kvc.shape
    nchunk = seq // CMP_STRIDE
    hidden = w2T.shape[2]
    width = CMP_BLOCK * HEAD_DIM
    pair = LANES // HEAD_DIM
    return pl.pallas_call(
        functools.partial(_compress_kernel, nchunk=nchunk),
        out_shape=[
            jax.ShapeDtypeStruct((2, bsz, N_KV_GROUPS, nchunk, LANES), BF16),
            jax.ShapeDtypeStruct((2, bsz, N_KV_GROUPS, HEAD_DIM, nchunk), BF16),
        ],
        grid=(2, bsz, N_KV_GROUPS // pair),
        in_specs=[
            pl.BlockSpec((1, seq, LANES), lambda s, b, p: (b, 0, (N_KV_GROUPS // pair) * s + p)),
            pl.BlockSpec((1, width, hidden), lambda s, b, p: (s, 0, 0)),
            pl.BlockSpec((1, 1, width), lambda s, b, p: (s, 0, 0)),
            pl.BlockSpec((1, hidden, LANES), lambda s, b, p: (s, 0, 0)),
            pl.BlockSpec((1, HEAD_DIM, hidden), lambda s, b, p: (s, 0, 0)),
        ],
        out_specs=[
            pl.BlockSpec((1, 1, pair, nchunk, LANES), lambda s, b, p: (s, b, p, 0, 0)),
            pl.BlockSpec((1, 1, pair, HEAD_DIM, nchunk), lambda s, b, p: (s, b, p, 0, 0)),
        ],
        scratch_shapes=[pltpu.VMEM((pair, nchunk, CMP_STRIDE * HEAD_DIM), F32)],
        compiler_params=pltpu.CompilerParams(dimension_semantics=("parallel", "parallel", "parallel"),
                                             vmem_limit_bytes=VMEM_LIMIT),
        name="compress",
    )(kvc, w1, pos, w2p, w2T)


def _attn_kernel(q_ref, qr_ref, g_ref, sz_ref, kc_ref, vcT_ref, ks_ref, kw_ref, vsT_ref, vwT_ref, y_ref,
                 p_sc, sc_sc, rank_sc, oc_sc, qa_sc, c_sc, w_sc, s_sc, pb_sc, smax_sc, st_sc, acc_sc, wmax_sc,
                 wp_sc, ow_sc, *, tq, n_cmp):
    qi = pl.program_id(2)
    t0 = qi * tq
    neg_inf = -jnp.inf
    tvec = t0 + lax.broadcasted_iota(jnp.int32, (1, tq), 1)

    kc = kc_ref[0, 0, :, 0:HEAD_DIM]
    cmp_end = lax.broadcasted_iota(jnp.int32, (n_cmp, tq), 0) * CMP_STRIDE + (CMP_BLOCK - 1)
    cmask = cmp_end <= tvec
    heads = range(HEADS_PER_GROUP)

    def fold8(x, op, ways=4):
        parts = [None] * ways
        for idx, r in enumerate(range(0, x.shape[0], SUBLANES)):
            slab = x[r:r + SUBLANES, :]
            parts[idx % ways] = slab if parts[idx % ways] is None else op(parts[idx % ways], slab)
        return functools.reduce(op, [p for p in parts if p is not None])

    def key_chunk(ref, c):
        return ref[0, pl.ds(pl.multiple_of(c * KEY_CHUNK, KEY_CHUNK), KEY_CHUNK), :]

    n_sb = n_cmp // 4
    jrow = lax.broadcasted_iota(jnp.int32, (n_sb, tq), 0)
    cur = tvec // SLC_BLOCK
    valid = jrow <= cur

    QA_SEL, QA_DEAD, QA_WIN = 0, 1, 2
    pad_blocks = lambda b: b if n_sb == HEAD_DIM else jnp.concatenate(
        [b, jnp.zeros((HEAD_DIM - n_sb, tq), BF16)], axis=0)

    def set_operand(idx, bias):
        for hh in heads:
            r0 = hh * HEAD_DIM
            qa_sc[idx, hh] = jnp.concatenate([qr_ref[0, r0:r0 + HEAD_DIM, :], pad_blocks(bias)], axis=0)

    win_blocks = WINDOW // SLC_BLOCK
    set_operand(QA_WIN, jnp.where((jrow >= cur - win_blocks) & valid, 0.0, MASK_BIAS).astype(BF16))
    set_operand(QA_DEAD, jnp.full((n_sb, tq), MASK_BIAS, BF16))

    blk_row = lax.broadcasted_iota(jnp.int32, (SLC_BLOCK, LANES), 0)
    blk_lane = lax.broadcasted_iota(jnp.int32, (SLC_BLOCK, LANES), 1)
    edge = [blk_row - blk_lane, blk_row - (blk_lane - SLC_BLOCK)]
    off_diag = [blk_lane >= SLC_BLOCK, blk_lane < SLC_BLOCK]

    def mask_diagonal_blocks(s, first_block, keep):
        width = s.shape[1]
        blocks = [s[r:r + SLC_BLOCK, :] for r in range(0, s.shape[0], SLC_BLOCK)]
        for j in range(width // SLC_BLOCK):
            cols = [blocks[first_block + j][:, h * LANES:(h + 1) * LANES] for h in range(width // LANES)]
            cols[j // 2] = jnp.where(off_diag[j % 2] | keep(edge[j % 2]), cols[j // 2], neg_inf)
            blocks[first_block + j] = jnp.concatenate(cols, axis=1)
        return jnp.concatenate(blocks, axis=0)

    n_win = KEY_CHUNK + HALF
    prev_operand = jnp.where(qi >= 1, QA_WIN, QA_DEAD)
    prev_base = jnp.maximum(qi - 1, 0) * KEY_CHUNK

    def window_pieces(half):
        first = half * HALF
        n_old = KEY_CHUNK - first
        old = (pl.multiple_of(prev_base + first, HALF), n_old)
        new = (pl.multiple_of(qi * KEY_CHUNK, HALF), HALF + first)
        return old, new

    for hh in heads:
        r0 = hh * HEAD_DIM
        c_sc[hh] = jnp.where(cmask, _dot(kc, q_ref[0, r0:r0 + HEAD_DIM, :]), neg_inf)
        for half in range(tq // HALF):
            (old_start, n_old), (new_start, n_new) = window_pieces(half)
            lanes = slice(half * HALF, (half + 1) * HALF)
            s_old = _dot(kw_ref[0, pl.ds(old_start, n_old), :], qa_sc[prev_operand, hh, :, lanes])
            s_new = _dot(kw_ref[0, pl.ds(new_start, n_new), :], qa_sc[QA_WIN, hh, :, lanes])
            s_old = mask_diagonal_blocks(s_old, 0, lambda e: e > 0)
            s_new = mask_diagonal_blocks(s_new, half * (HALF // SLC_BLOCK), lambda e: e <= 0)
            w_sc[hh, half, 0:n_old, :] = s_old
            w_sc[hh, half, n_old:n_win, :] = s_new
            wmax_sc[hh, half] = jnp.maximum(fold8(s_old, jnp.maximum), fold8(s_new, jnp.maximum))
    probs = []
    for hh in heads:
        s = c_sc[hh]
        m = jnp.max(fold8(s, jnp.maximum), axis=0, keepdims=True)
        m = jnp.where(m == neg_inf, 0.0, m)
        e = jnp.exp2(s - m)
        den = jnp.sum(fold8(e, jnp.add), axis=0, keepdims=True)
        probs.append(e * (1.0 / jnp.maximum(den, 1e-30)))
    p_grp = functools.reduce(jnp.add, probs)
    for hh in heads:
        r0 = hh * HEAD_DIM
        oc_sc[r0:r0 + HEAD_DIM, :] = _dot(vcT_ref[0, 0], probs[hh].astype(BF16))

    ratio = SLC_BLOCK // CMP_STRIDE
    imp_cols = []
    for c in range(tq // LANES):
        p_sc[c, 0:SUBLANES, :] = jnp.zeros((SUBLANES, LANES), F32)
        p_sc[c, SUBLANES:SUBLANES + n_cmp, :] = p_grp[:, c * LANES:(c + 1) * LANES]
        tap = lambda o: p_sc[c, pl.ds(SUBLANES + o, n_sb, stride=ratio), :]
        imp_cols.append(tap(-1) + 2.0 * (tap(0) + tap(1) + tap(2)) + tap(3))
    imp = jnp.concatenate(imp_cols, axis=1)

    forced = (jrow == 0) | (valid & (jrow > cur - N_LOCAL))
    score = jnp.where(valid, jnp.where(forced, FORCE_SCORE, imp), neg_inf)
    sc_sc[...] = score

    n_valid = (t0 + tq) // SLC_BLOCK
    n_slabs = n_sb // SUBLANES
    slab_row = lax.broadcasted_iota(jnp.int32, (SUBLANES, tq), 0)
    rank_sc[...] = jnp.zeros((n_sb, tq), jnp.int32)
    for first in range(0, n_sb, RANK_SECTION):
        @pl.when(n_valid > max(N_SELECT, first))
        def _():
            slabs = [sc_sc[k * SUBLANES:(k + 1) * SUBLANES, :] for k in range(n_slabs)]
            counts = [rank_sc[k * SUBLANES:(k + 1) * SUBLANES, :] for k in range(n_slabs)]
            for jp in range(first, first + RANK_SECTION):
                sb = sc_sc[jp:jp + 1, :]
                for k in range(n_slabs):
                    if k * SUBLANES > jp:
                        before = sb >= slabs[k]
                    elif k * SUBLANES + SUBLANES - 1 < jp:
                        before = sb > slabs[k]
                    else:
                        before = (sb > slabs[k]) | ((sb == slabs[k]) & (slab_row > jp % SUBLANES))
                    counts[k] = counts[k] + jnp.where(before, 1, 0)
            for k in range(n_slabs):
                rank_sc[k * SUBLANES:(k + 1) * SUBLANES, :] = counts[k]
    set_operand(QA_SEL, jnp.where((rank_sc[...] < N_SELECT) & valid, 0.0, MASK_BIAS).astype(BF16))

    group = KEY_CHUNK
    n_double = qi // 2

    def values_of(ref, c):
        per = KEY_CHUNK // PROJ_TILE
        return jnp.concatenate([ref[0, per * c + v] for v in range(per)], axis=1)

    ST_MAX, ST_SUM, ST_RESCALE, ST_TAIL = 0, 1, 2, 4
    rows8 = lambda x: jnp.broadcast_to(x, (SUBLANES, tq))
    never = 1 << 20

    def scores_to(slot, pos, hh, may_be_last):
        s = _dot(key_chunk(ks_ref, pos), qa_sc[QA_SEL, hh])
        if may_be_last:
            causal_slack = jnp.where(pos == qi, 0, never)
            s = mask_diagonal_blocks(s, 0, lambda e: e <= causal_slack)
        s_sc[slot, hh] = s
        smax_sc[slot, hh] = fold8(s, jnp.maximum)

    def accumulate(pos, slot, hh, tail=None):
        vT = values_of(vsT_ref, pos)
        pv = _dot(vT if tail else vT[0:HEAD_DIM, :], pb_sc[slot, hh])
        rescale = st_sc[ST_RESCALE + slot, hh, 0:1, :]
        acc_sc[hh] = acc_sc[hh] * rescale + pv[0:HEAD_DIM, :]
        if tail == "set":
            st_sc[ST_TAIL, hh] = pv[HEAD_DIM:HEAD_DIM + SUBLANES, :]
        elif tail == "add":
            st_sc[ST_TAIL, hh] = st_sc[ST_TAIL, hh] * rescale + pv[HEAD_DIM:HEAD_DIM + SUBLANES, :]

    def softmax_group(slot, hh, with_sum=True):
        m_old = st_sc[ST_MAX, hh, 0:1, :]
        m_new = jnp.maximum(m_old, jnp.max(smax_sc[slot, hh], axis=0, keepdims=True))
        a = jnp.exp2(m_old - m_new)
        psum = None
        for r in range(0, group, SOFTMAX_ROWS):
            s = s_sc[slot, hh, r:r + SOFTMAX_ROWS, :]
            p = jnp.exp2(s - m_new)
            pb_sc[slot, hh, r:r + SOFTMAX_ROWS, :] = p.astype(BF16)
            if with_sum:
                f = fold8(p, jnp.add, ways=2)
                psum = f if psum is None else psum + f
        st_sc[ST_SUM, hh] = a * st_sc[ST_SUM, hh] + psum if with_sum else a * st_sc[ST_SUM, hh]
        st_sc[ST_MAX, hh] = rows8(m_new)
        st_sc[ST_RESCALE + slot, hh] = rows8(a)

    def stage(slot, pos, next_may_be_last, with_sum=True):
        for hh in heads:
            scores_to(1 - slot, pos + 1, hh, next_may_be_last)
            accumulate(jnp.maximum(pos - 1, 0), 1 - slot, hh)
            softmax_group(slot, hh, with_sum)

    def drain(slot, previous_in_tail):
        for hh in heads:
            accumulate(jnp.maximum(qi - 1, 0), 1 - slot, hh, "set" if previous_in_tail else None)
            softmax_group(slot, hh, with_sum=False)
            accumulate(qi, slot, hh, "add" if previous_in_tail else "set")

    def double_trip(d, carry):
        stage(0, 2 * d, next_may_be_last=False)
        stage(1, 2 * d + 1, next_may_be_last=True)
        return carry

    def window_values(half):
        pieces = []
        for start, rows in window_pieces(half):
            pieces += [vwT_ref[0, start // PROJ_TILE + v] for v in range(rows // PROJ_TILE)]
        return jnp.concatenate(pieces, axis=1)

    for hh in heads:
        acc_sc[hh] = jnp.zeros((HEAD_DIM, tq), F32)
        pb_sc[1, hh] = jnp.zeros((group, tq), BF16)
        st_sc[ST_MAX, hh] = jnp.full((SUBLANES, tq), neg_inf, F32)
        st_sc[ST_SUM, hh] = jnp.zeros((SUBLANES, tq), F32)
        st_sc[ST_RESCALE + 1, hh] = jnp.ones((SUBLANES, tq), F32)
        scores_to(0, 0, hh, may_be_last=True)
        for half in range(tq // HALF):
            m = jnp.max(wmax_sc[hh, half], axis=0, keepdims=True)
            for r in range(0, n_win, 2 * SOFTMAX_ROWS):
                wp_sc[hh, half, r:r + 2 * SOFTMAX_ROWS, :] = jnp.exp2(
                    w_sc[hh, half, r:r + 2 * SOFTMAX_ROWS, :] - m).astype(BF16)
            pv = _dot(window_values(half), wp_sc[hh, half])
            ow_sc[hh, :, half * HALF:(half + 1) * HALF] = pv[0:HEAD_DIM, :] * (1.0 / pv[HEAD_DIM:HEAD_DIM + 1, :])

    lax.fori_loop(0, n_double, double_trip, 0)

    @pl.when(qi % 2 == 0)
    def _():
        drain(0, previous_in_tail=False)

    @pl.when(qi % 2 == 1)
    def _():
        stage(0, qi - 1, next_may_be_last=True, with_sum=False)
        drain(1, previous_in_tail=True)

    for hh in heads:
        r0 = hh * HEAD_DIM
        o_w = ow_sc[hh]
        denom = jnp.sum(st_sc[ST_SUM, hh], axis=0, keepdims=True) + st_sc[ST_TAIL, hh, 0:1, :]
        o_s = acc_sc[hh] * (1.0 / denom)

        g0 = g_ref[0, N_BRANCH * hh:N_BRANCH * hh + 1, :]
        g1 = g_ref[0, N_BRANCH * hh + 1:N_BRANCH * hh + 2, :]
        g2 = g_ref[0, N_BRANCH * hh + 2:N_BRANCH * hh + 3, :]
        o = g0 * oc_sc[r0:r0 + HEAD_DIM, :] + g1 * o_s + g2 * o_w
        y_ref[0, r0:r0 + HEAD_DIM, :] = (o * sz_ref[0, r0:r0 + HEAD_DIM, :]).astype(BF16)


def _nsa_attn(qT, qrT, gT, szT, kcmp, vcmpT, ks, kw, vsT, vwT, *, tq=KEY_CHUNK):
    bsz, attn_d, seq = qT.shape
    n_cmp = kcmp.shape[2]
    n_win = KEY_CHUNK + HALF
    gd = HEADS_PER_GROUP * HEAD_DIM
    assert tq == KEY_CHUNK == WINDOW and HALF % PROJ_TILE == 0 and vsT.shape[3] == PROJ_TILE
    qspec = pl.BlockSpec((1, gd, tq), lambda b, g, i: (b, g, i))
    kspec = pl.BlockSpec((1, seq, LANES), lambda b, g, i: (b, 0, g))
    vspec = pl.BlockSpec((1, seq // PROJ_TILE, V_ROWS, PROJ_TILE), lambda b, g, i: (b, 0, g, 0))
    return pl.pallas_call(
        functools.partial(_attn_kernel, tq=tq, n_cmp=n_cmp),
        out_shape=jax.ShapeDtypeStruct((bsz, attn_d, seq), BF16),
        grid=(bsz, N_KV_GROUPS, seq // tq),
        in_specs=[
            qspec, qspec,
            pl.BlockSpec((1, GATE_ROWS, tq), lambda b, g, i: (b, g, i)),
            qspec,
            pl.BlockSpec((1, 1, n_cmp, LANES), lambda b, g, i: (b, g, 0, 0)),
            pl.BlockSpec((1, 1, HEAD_DIM, n_cmp), lambda b, g, i: (b, g, 0, 0)),
            kspec, kspec, vspec, vspec,
        ],
        out_specs=qspec,
        scratch_shapes=[
            pltpu.VMEM((tq // LANES, SUBLANES + n_cmp, LANES), F32),
            pltpu.VMEM((n_cmp // 4, tq), F32),
            pltpu.VMEM((n_cmp // 4, tq), jnp.int32),
            pltpu.VMEM((gd, tq), F32),
            pltpu.VMEM((3, HEADS_PER_GROUP, 2 * HEAD_DIM, tq), BF16),
            pltpu.VMEM((HEADS_PER_GROUP, n_cmp, tq), F32),
            pltpu.VMEM((HEADS_PER_GROUP, tq // HALF, n_win, HALF), F32),
            pltpu.VMEM((2, HEADS_PER_GROUP, KEY_CHUNK, tq), F32),
            pltpu.VMEM((2, HEADS_PER_GROUP, KEY_CHUNK, tq), BF16),
            pltpu.VMEM((2, HEADS_PER_GROUP, SUBLANES, tq), F32),
            pltpu.VMEM((5, HEADS_PER_GROUP, SUBLANES, tq), F32),
            pltpu.VMEM((HEADS_PER_GROUP, HEAD_DIM, tq), F32),
            pltpu.VMEM((HEADS_PER_GROUP, tq // HALF, SUBLANES, HALF), F32),
            pltpu.VMEM((HEADS_PER_GROUP, tq // HALF, n_win, HALF), BF16),
            pltpu.VMEM((HEADS_PER_GROUP, HEAD_DIM, tq), F32),
        ],
        compiler_params=pltpu.CompilerParams(dimension_semantics=("parallel", "parallel", "arbitrary"),
                                             vmem_limit_bytes=VMEM_LIMIT),
        name="nsa_attn",
    )(qT, qrT, gT, szT, kcmp, vcmpT, ks, kw, vsT, vwT)


def _nsa_out_kernel(y_ref, h_ref, wo_ref, fn_ref, o_ref):
    h2 = h_ref[0] + _dot_tn(y_ref[0], wo_ref[...])
    o_ref[0] = h2 * _inv_rms(h2) * fn_ref[...]


def _nsa_out(yT, h3, wo, final_norm, *, tm=OUT_TILE):
    bsz, seq, d = h3.shape
    attn_d = yT.shape[1]
    return pl.pallas_call(
        _nsa_out_kernel,
        out_shape=jax.ShapeDtypeStruct((bsz, seq, d), F32),
        grid=(bsz, seq // tm),
        in_specs=[
            pl.BlockSpec((1, attn_d, tm), lambda b, i: (b, 0, i)),
            pl.BlockSpec((1, tm, d), lambda b, i: (b, i, 0)),
            pl.BlockSpec((attn_d, d), lambda b, i: (0, 0)),
            pl.BlockSpec((1, d), lambda b, i: (0, 0)),
        ],
        out_specs=pl.BlockSpec((1, tm, d), lambda b, i: (b, i, 0)),
        compiler_params=pltpu.CompilerParams(dimension_semantics=("parallel", "parallel"),
                                             vmem_limit_bytes=VMEM_LIMIT),
        name="nsa_out",
    )(yT, h3, wo, final_norm.reshape(1, d))


def _rope_tables(seq):
    pos = jnp.arange(seq, dtype=F32)
    inv = ROPE_THETA ** (-jnp.arange(0, ROT_DIM, 2, dtype=F32) / ROT_DIM)
    ang = pos[:, None] * inv[None, :]
    cos, sin = jnp.cos(ang), jnp.sin(ang)
    z = lambda n: jnp.zeros((seq, n), F32)
    per_head = lambda parts: jnp.tile(jnp.concatenate(parts, axis=1), (1, LANES // HEAD_DIM))
    cosf = per_head([cos, cos, jnp.ones((seq, HEAD_DIM - ROT_DIM), F32)])
    sina = per_head([-sin, z(HEAD_DIM - N_FREQ)])
    sinb = per_head([z(N_FREQ), sin, z(HEAD_DIM - ROT_DIM)])
    return cos.T, sin.T, cosf, sina, sinb


def kernel(x, a_norm, a_w_in, a_conv_w, a_w_out, kv_norm, w_kv, cmp_pos_k, cmp_w1_k, cmp_w2_k,
           cmp_pos_v, cmp_w1_v, cmp_w2_v, b_norm, b_w_in, b_w_out, final_norm):
    bsz, seq, d = x.shape
    attn_d = N_HEADS * HEAD_DIM
    kv_d = N_KV_GROUPS * HEAD_DIM
    assert b_norm.shape[0] == 1, "one NSA layer reads the shared K/V side"
    assert seq % KEY_CHUNK == 0

    h = x.reshape(bsz * seq, d)
    for layer in range(a_norm.shape[0]):
        h = _conv_layer(h, a_norm[layer], a_w_in[layer].astype(BF16), a_conv_w[layer],
                        a_w_out[layer].astype(BF16), seq=seq)
    h3 = h.reshape(bsz, seq, d)

    w_in = b_w_in[0]
    n_gate = N_HEADS * N_BRANCH
    wg = w_in[:, attn_d:attn_d + n_gate].reshape(d, N_KV_GROUPS, HEADS_PER_GROUP * N_BRANCH)
    wg = jnp.pad(wg, ((0, 0), (0, 0), (0, GATE_ROWS - HEADS_PER_GROUP * N_BRANCH)))
    waT = jnp.concatenate([w_in[:, :attn_d], w_in[:, attn_d + n_gate:], wg.reshape(d, -1)], axis=1).T.astype(BF16)
    wkv = w_kv.reshape(d, 2 * N_BRANCH, N_KV_GROUPS, HEAD_DIM)
    k_c, v_c, k_s, v_s, k_w, v_w = [wkv[:, i] for i in range(2 * N_BRANCH)]
    flat = lambda w: w.reshape(d, kv_d)
    wvT = jnp.concatenate([flat(v_s), flat(v_w)], axis=1).T.astype(BF16)
    wk = jnp.concatenate([flat(k_c), flat(v_c), flat(k_s), flat(k_w)], axis=1).astype(BF16)
    cosT, sinT, cosf, sina, sinb = _rope_tables(seq)

    qT, qrT, szT, gT, kvc, ks, kw, vsT, vwT = _nsa_proj(
        h3, b_norm[0], kv_norm, waT, wvT, wk, cosT, sinT, cosf, sina, sinb)

    w1 = jnp.stack([cmp_w1_k, cmp_w1_v]).astype(BF16)
    pos = jnp.stack([cmp_pos_k.reshape(1, -1), cmp_pos_v.reshape(1, -1)])
    w2 = jnp.stack([cmp_w2_k, cmp_w2_v])
    w2p = jnp.pad(w2, ((0, 0), (0, 0), (0, LANES - HEAD_DIM))).astype(BF16)
    w2T = jnp.swapaxes(w2, 1, 2).astype(BF16)
    cmp_tm, cmp_fm = _compress(kvc, w1, pos, w2p, w2T)

    yT = _nsa_attn(qT, qrT, gT, szT, cmp_tm[0], cmp_fm[1], ks, kw, vsT, vwT)
    return _nsa_out(yT, h3, b_w_out[0].astype(BF16), final_norm)
```
